```python
import math
import jax, jax.numpy as jnp
from jax import lax
import numpy as np

D_MODEL = 2048
BATCH = 1
SEQ = 8192
DEPTH = 1

HEAD_DIM = 64
HEADS_PER_GROUP = 8
DILATED_GROUPS = ((128, 1), (512, 4), (2048, 16))
N_DGROUPS = len(DILATED_GROUPS)
N_ATTN_HEADS = N_DGROUPS * HEADS_PER_GROUP
ATTN_QKV = N_ATTN_HEADS * HEAD_DIM
ATTN_OUT = HEADS_PER_GROUP * HEAD_DIM
ATTN_BLOCK = 128

SSM_WIDTH = D_MODEL // 2
SSM_CH_PER_GROUP = 16
SSM_GROUPS = SSM_WIDTH // SSM_CH_PER_GROUP
SSM_STATE = 64

N_EXPERT_GROUPS = 4
EXPERTS_PER_GROUP = 8
N_EXPERTS = N_EXPERT_GROUPS * EXPERTS_PER_GROUP
TOP_K_IN_GROUP = 2
EXPERT_FF = D_MODEL // 4
MOE_BLOCK = 128

NORM_EPS = 1e-6
IN_COLS = 3 * ATTN_QKV + SSM_WIDTH + 2 * D_MODEL

kernel_name = "hybrid_dilated_attn_s5_hmoe_block"


def rms_norm(x, gain):
    xf = x.astype(jnp.float32)
    xf = xf * lax.rsqrt(jnp.mean(xf * xf, axis=-1, keepdims=True) + NORM_EPS)
    return (xf * gain.astype(jnp.float32)).astype(x.dtype)


def alibi_slopes(n_heads):
    return jnp.asarray([2.0 ** (-8.0 * (j + 1) / n_heads) for j in range(n_heads)], jnp.float32)


def dilated_window_attention(q, k, v, window, dilation, slopes):
    B_, S, H, Dh = q.shape
    L = ATTN_BLOCK
    n_back = window // dilation
    span = dilation * L
    Sp = -(-S // span) * span
    NB = Sp // span

    def to_blocks(t):
        t = jnp.pad(t, ((0, 0), (0, Sp - S), (0, 0), (0, 0)))
        t = t.reshape(B_, Sp // dilation, dilation, H, Dh).transpose(0, 2, 1, 3, 4)
        return t.reshape(B_, dilation, NB, L, H, Dh)

    def with_prev(t):
        prev = jnp.pad(t, ((0, 0), (0, 0), (1, 0), (0, 0), (0, 0), (0, 0)))[:, :, :-1]
        return jnp.concatenate([prev, t], axis=3)

    def from_blocks(t):
        tail = t.shape[4:]
        t = t.reshape((B_, dilation, Sp // dilation) + tail)
        t = jnp.swapaxes(t, 1, 2).reshape((B_, Sp) + tail)
        return t[:, :S]

    qb, kb, vb = to_blocks(q), to_blocks(k), to_blocks(v)
    kk, vv = with_prev(kb), with_prev(vb)

    qi = jnp.arange(L)[:, None]
    kj = jnp.arange(2 * L)[None, :]
    steps = qi + L - kj
    band = (steps >= 0) & (steps <= n_back)
    first_block = (jnp.arange(NB) == 0)[:, None, None]
    valid = band[None] & ~(first_block & (kj < L)[None])

    s = jnp.einsum('brnqhd,brnkhd->brnhqk', qb, kk).astype(jnp.float32) * (HEAD_DIM ** -0.5)
    bias = -slopes[:, None, None] * (dilation * steps).astype(jnp.float32)[None]
    s = s + bias[None, None, None]
    s = jnp.where(valid[None, None, :, None], s, -jnp.inf)
    m = jnp.max(s, axis=-1, keepdims=True)
    p = jnp.exp(s - m)
    denom = jnp.sum(p, axis=-1)
    o = jnp.einsum('brnhqk,brnkhd->brnqhd', p.astype(vv.dtype), vv).astype(jnp.float32)
    denom_t = jnp.swapaxes(denom, 3, 4)
    o = (o / denom_t[..., None]).astype(q.dtype)
    lse = jnp.swapaxes(m[..., 0] + jnp.log(denom), 3, 4)
    return from_blocks(o), from_blocks(lse)


def dilated_mixture_attention(q, k, v, slopes):
    B_, S, _ = q.shape

    def heads(t):
        return t.reshape(B_, S, N_DGROUPS, HEADS_PER_GROUP, HEAD_DIM)

    qh, kh, vh = heads(q), heads(k), heads(v)
    outs, lses = [], []
    for g, (window, dilation) in enumerate(DILATED_GROUPS):
        o, l = dilated_window_attention(qh[:, :, g], kh[:, :, g], vh[:, :, g], window, dilation, slopes)
        outs.append(o)
        lses.append(l)
    w = jax.nn.softmax(jnp.stack(lses, axis=2), axis=2)
    o = jnp.einsum('bsgh,bsghd->bshd', w.astype(q.dtype), jnp.stack(outs, axis=2))
    return o.reshape(B_, S, ATTN_OUT)


def s5_ssm(u, a_re, a_im, log_step, b_re, b_im, c_re, c_im, d_skip):
    f32 = jnp.float32
    u = u.astype(f32)
    a_re, a_im = a_re.astype(f32), a_im.astype(f32)
    B_, S, _ = u.shape
    ug = u.reshape(B_, S, SSM_GROUPS, SSM_CH_PER_GROUP)
    step = jnp.exp(log_step.astype(f32))[:, None]
    mag = jnp.exp(a_re * step)
    ang = a_im * step
    abar_re, abar_im = mag * jnp.cos(ang), mag * jnp.sin(ang)
    nr, ni = abar_re - 1.0, abar_im
    den = a_re * a_re + a_im * a_im
    f_re = (nr * a_re + ni * a_im) / den
    f_im = (ni * a_re - nr * a_im) / den
    b_re, b_im = b_re.astype(f32), b_im.astype(f32)
    bbar_re = f_re[..., None] * b_re - f_im[..., None] * b_im
    bbar_im = f_re[..., None] * b_im + f_im[..., None] * b_re
    bu_re = jnp.einsum('gph,bsgh->bsgp', bbar_re, ug)
    bu_im = jnp.einsum('gph,bsgh->bsgp', bbar_im, ug)
    ar = jnp.broadcast_to(abar_re, bu_re.shape)
    ai = jnp.broadcast_to(abar_im, bu_re.shape)

    def combine(e1, e2):
        a1r, a1i, b1r, b1i = e1
        a2r, a2i, b2r, b2i = e2
        return (a2r * a1r - a2i * a1i, a2r * a1i + a2i * a1r,
                a2r * b1r - a2i * b1i + b2r, a2r * b1i + a2i * b1r + b2i)

    _, _, xr, xi = lax.associative_scan(combine, (ar, ai, bu_re, bu_im), axis=1)
    y = (jnp.einsum('ghp,bsgp->bsgh', c_re.astype(f32), xr)
         - jnp.einsum('ghp,bsgp->bsgh', c_im.astype(f32), xi))
    return y.reshape(B_, S, SSM_WIDTH) + d_skip.astype(f32) * u


def hierarchical_moe(h, w_rg, b_rg, w_re, b_re, w_gate, w_up, w_down):
    B_, S, D = h.shape
    xt = h.reshape(-1, D)
    T = xt.shape[0]
    K = TOP_K_IN_GROUP
    g_logits = (xt @ w_rg).astype(jnp.float32) + b_rg.astype(jnp.float32)
    g_probs = jax.nn.softmax(g_logits, axis=-1)
    g_p, g_idx = lax.top_k(g_probs, 1)
    e_logits = ((xt @ w_re).astype(jnp.float32) + b_re.astype(jnp.float32)).reshape(T, N_EXPERT_GROUPS, EXPERTS_PER_GROUP)
    in_group = jnp.take_along_axis(e_logits, g_idx[:, :, None], axis=1)[:, 0]
    top_l, top_j = lax.top_k(in_group, K)
    weights = g_p * jax.nn.softmax(top_l, axis=-1)
    expert_idx = g_idx * EXPERTS_PER_GROUP + top_j

    A = T * K
    flat_e = expert_idx.reshape(A)
    flat_tok = jnp.repeat(jnp.arange(T, dtype=jnp.int32), K)
    order = jnp.argsort(flat_e)
    sorted_e = flat_e[order]
    counts = jnp.bincount(flat_e, length=N_EXPERTS)
    padded = (counts + MOE_BLOCK - 1) // MOE_BLOCK * MOE_BLOCK
    pad_end = jnp.cumsum(padded)
    pad_start = pad_end - padded
    start = jnp.cumsum(counts) - counts
    dest_sorted = pad_start[sorted_e] + jnp.arange(A) - start[sorted_e]
    dest = jnp.zeros((A,), jnp.int32).at[order].set(dest_sorted.astype(jnp.int32))
    n_rows = (-(-A // MOE_BLOCK) + N_EXPERTS) * MOE_BLOCK
    n_blocks = n_rows // MOE_BLOCK
    row_token = jnp.zeros((n_rows,), jnp.int32).at[dest].set(flat_tok)
    row_valid = jnp.zeros((n_rows,), bool).at[dest].set(True)
    block_expert = jnp.minimum(
        jnp.searchsorted(pad_end, jnp.arange(n_blocks) * MOE_BLOCK, side='right'), N_EXPERTS - 1)
    xs = jnp.where(row_valid[:, None], xt[row_token], 0).reshape(n_blocks, MOE_BLOCK, D)

    def expert_block(args):
        xb, e = args
        hid = jax.nn.silu(xb @ w_gate[e]) * (xb @ w_up[e])
        return hid @ w_down[e]

    ys = lax.map(expert_block, (xs, block_expert)).reshape(n_rows, D)
    y_assign = ys[dest].reshape(T, K, D)
    out = jnp.einsum('tk,tkd->td', weights.astype(y_assign.dtype), y_assign)
    return out.reshape(B_, S, D)


def setup_inputs(seed: int = 0) -> dict:
    key = jax.random.key(seed)
    ks = jax.random.split(key, 28)
    f32 = jnp.float32
    D, L_, G, P, H = D_MODEL, DEPTH, SSM_GROUPS, SSM_STATE, SSM_CH_PER_GROUP

    def nrm(k, shape, scale):
        return jax.random.normal(k, shape, f32) * scale

    return {
        "x": nrm(ks[0], (BATCH, SEQ, D), 1.0),
        "norm_mix": 1.0 + nrm(ks[1], (L_, D), 0.01),
        "w_in": nrm(ks[2], (L_, D, IN_COLS), D ** -0.5),
        "b_gate": nrm(ks[3], (L_, 2 * D), 0.1),
        "ssm_a_re": -0.5 + nrm(ks[4], (L_, G, P), 0.01),
        "ssm_a_im": math.pi * jnp.arange(P, dtype=f32) + nrm(ks[5], (L_, G, P), 0.01),
        "ssm_log_step": jax.random.uniform(ks[6], (L_, G), f32, math.log(1e-3), math.log(1e-1)),
        "ssm_b_re": nrm(ks[7], (L_, G, P, H), (2 * H) ** -0.5),
        "ssm_b_im": nrm(ks[8], (L_, G, P, H), (2 * H) ** -0.5),
        "ssm_c_re": nrm(ks[9], (L_, G, H, P), P ** -0.5),
        "ssm_c_im": nrm(ks[10], (L_, G, H, P), P ** -0.5),
        "ssm_d": nrm(ks[11], (L_, SSM_WIDTH), 1.0),
        "w_glu": nrm(ks[12], (L_, SSM_WIDTH, SSM_WIDTH), SSM_WIDTH ** -0.5),
        "b_glu": nrm(ks[13], (L_, SSM_WIDTH), 0.01),
        "w_up_attn": nrm(ks[14], (L_, ATTN_OUT, D), ATTN_OUT ** -0.5),
        "w_up_ssm": nrm(ks[15], (L_, SSM_WIDTH, D), SSM_WIDTH ** -0.5),
        "w_out": nrm(ks[16], (L_, D, D), D ** -0.5),
        "norm_ffn": 1.0 + nrm(ks[17], (L_, D), 0.01),
        "w_router_group": nrm(ks[18], (L_, D, N_EXPERT_GROUPS), D ** -0.5),
        "b_router_group": nrm(ks[19], (L_, N_EXPERT_GROUPS), 0.01),
        "w_router_expert": nrm(ks[20], (L_, D, N_EXPERTS), D ** -0.5),
        "b_router_expert": nrm(ks[21], (L_, N_EXPERTS), 0.01),
        "w_expert_gate": nrm(ks[22], (L_, N_EXPERTS, D, EXPERT_FF), D ** -0.5),
        "w_expert_up": nrm(ks[23], (L_, N_EXPERTS, D, EXPERT_FF), D ** -0.5),
        "w_expert_down": nrm(ks[24], (L_, N_EXPERTS, EXPERT_FF, D), EXPERT_FF ** -0.5),
        "norm_final": 1.0 + nrm(ks[25], (D,), 0.01),
    }


def reference(x, norm_mix, w_in, b_gate, ssm_a_re, ssm_a_im, ssm_log_step, ssm_b_re, ssm_b_im,
              ssm_c_re, ssm_c_im, ssm_d, w_glu, b_glu, w_up_attn, w_up_ssm, w_out, norm_ffn,
              w_router_group, b_router_group, w_router_expert, b_router_expert,
              w_expert_gate, w_expert_up, w_expert_down, norm_final):
    slopes = alibi_slopes(HEADS_PER_GROUP)
    splits = [ATTN_QKV, 2 * ATTN_QKV, 3 * ATTN_QKV, 3 * ATTN_QKV + SSM_WIDTH]
    for layer in range(DEPTH):
        h = rms_norm(x, norm_mix[layer])
        proj = h @ w_in[layer]
        q, k, v, u, gate_logits = jnp.split(proj, splits, axis=-1)
        attn = dilated_mixture_attention(q, k, v, slopes)
        y = s5_ssm(u, ssm_a_re[layer], ssm_a_im[layer], ssm_log_step[layer], ssm_b_re[layer],
                   ssm_b_im[layer], ssm_c_re[layer], ssm_c_im[layer], ssm_d[layer])
        yg = jax.nn.gelu(y)
        ssm = (yg * jax.nn.sigmoid(yg @ w_glu[layer].astype(jnp.float32) + b_glu[layer].astype(jnp.float32))).astype(x.dtype)
        gates = jax.nn.sigmoid(gate_logits + b_gate[layer])
        g_attn, g_ssm = jnp.split(gates, 2, axis=-1)
        merged = g_attn * (attn @ w_up_attn[layer]) + g_ssm * (ssm @ w_up_ssm[layer])
        x = x + merged @ w_out[layer]
        h = rms_norm(x, norm_ffn[layer])
        x = x + hierarchical_moe(h, w_router_group[layer], b_router_group[layer], w_router_expert[layer],
                                 b_router_expert[layer], w_expert_gate[layer], w_expert_up[layer],
                                 w_expert_down[layer])
    return rms_norm(x, norm_final)
```

```python
import functools
import math

import jax
import jax.numpy as jnp
from jax import lax
from jax.experimental import pallas as pl
from jax.experimental.pallas import tpu as pltpu

F32 = jnp.float32
BF16 = jnp.bfloat16

T = 8192
D = 2048
R = 16
NI = T // R
HEAD_DIM = 64
N_HEAD_SLOTS = 8
DILATIONS = (1, 4, 16)
ATTN_BLOCK = 128
QKV_COLS = 1536
ATTN_OUT = 512
SSM_W = 1024
SSM_STATE = 64
SSM_CH = 16
IN_COLS = 3 * QKV_COLS + SSM_W + 2 * D
N_EXPERTS = 32
N_EGROUPS = 4
EXPERTS_PER_GROUP = 8
TOP_K = 2
EXPERT_FF = 512
NORM_EPS = 1e-6
LANES = 128
VMEM_LIMIT = 48 * 1024 * 1024

MOE_BLOCK = 256
N_ASSIGN = T * TOP_K
MOE_BLOCKS = N_ASSIGN // MOE_BLOCK + N_EXPERTS
MOE_ROWS = MOE_BLOCKS * MOE_BLOCK


def _cparams(sem):
    return pltpu.CompilerParams(dimension_semantics=sem, vmem_limit_bytes=VMEM_LIMIT)


def _norm_permute_kernel(x_ref, g_ref, h_ref):
    g = g_ref[...]
    for r in range(R):
        xr = x_ref[:, r, :]
        ms = jnp.mean(xr * xr, axis=-1, keepdims=True)
        h_ref[r] = (xr * lax.rsqrt(ms + NORM_EPS) * g).astype(BF16)


def norm_permute(x3, gain):
    tb = 32
    return pl.pallas_call(
        _norm_permute_kernel,
        out_shape=jax.ShapeDtypeStruct((R, NI, D), BF16),
        grid=(NI // tb,),
        in_specs=[pl.BlockSpec((tb, R, D), lambda i: (i, 0, 0)),
                  pl.BlockSpec((1, D), lambda i: (0, 0))],
        out_specs=pl.BlockSpec((R, tb, D), lambda i: (0, i, 0)),
        compiler_params=_cparams(("arbitrary",)),
        name="norm_permute",
    )(x3, gain)


def _proj_kernel(a_ref, w_ref, o_ref):
    o_ref[...] = jnp.dot(a_ref[...], w_ref[...].astype(BF16), preferred_element_type=F32)


def _proj_gate_kernel(a_ref, w_ref, b_ref, o_ref):
    acc = jnp.dot(a_ref[...], w_ref[...].astype(BF16), preferred_element_type=F32)
    o_ref[...] = jax.nn.sigmoid(acc + b_ref[...])


def proj(h, w_in, col_off, n_cols, bias=None, name="proj"):
    tm, tn = 1024, 512
    off = col_off // tn
    in_specs = [pl.BlockSpec((tm, D), lambda i, j: (i, 0)),
                pl.BlockSpec((D, tn), lambda i, j: (0, j + off))]
    args = [h, w_in]
    kern = _proj_kernel
    if bias is not None:
        in_specs.append(pl.BlockSpec((1, tn), lambda i, j: (0, j)))
        args.append(bias)
        kern = _proj_gate_kernel
    return pl.pallas_call(
        kern,
        out_shape=jax.ShapeDtypeStruct((T, n_cols), F32),
        grid=(T // tm, n_cols // tn),
        in_specs=in_specs,
        out_specs=pl.BlockSpec((tm, tn), lambda i, j: (i, j)),
        compiler_params=_cparams(("arbitrary", "arbitrary")),
        name=name,
    )(*args)


def _seq_index_maps(d):
    nseg = R // d
    qlen = ATTN_BLOCK // nseg
    return nseg, qlen


def _bias_matrices(d, hp):
    nseg, qlen = _seq_index_maps(d)
    klen = 2 * qlen
    row = lax.broadcasted_iota(jnp.int32, (2 * ATTN_BLOCK, 2 * ATTN_BLOCK), 0)
    col = lax.broadcasted_iota(jnp.int32, (2 * ATTN_BLOCK, 2 * ATTN_BLOCK), 1)
    rho = row % ATTN_BLOCK
    jq = (rho % qlen) * nseg + rho // qlen
    jk = ((col % klen) - qlen) * nseg + col // klen
    steps = jq - jk
    valid = (steps >= 0) & (steps <= ATTN_BLOCK)
    head = 2 * hp + row // ATTN_BLOCK
    slope = lax.bitcast_convert_type((127 - (head + 1)) << 23, F32)
    bias = -slope * (d * steps).astype(F32)
    neg = jnp.float32(-jnp.inf)
    return jnp.where(valid, bias, neg), jnp.where(valid & (jk >= 0), bias, neg)


def _attend_pair(q, k, v, bias):
    lane = lax.broadcasted_iota(jnp.int32, (ATTN_BLOCK, LANES), 1)
    first = lane < HEAD_DIM
    zero = jnp.zeros_like(q)
    q2 = jnp.concatenate([jnp.where(first, q, zero), jnp.where(first, zero, q)], axis=0).astype(BF16)
    s = lax.dot_general(q2, k.astype(BF16), (((1,), (1,)), ((), ())), preferred_element_type=F32)
    s = s + bias
    m = jnp.max(s, axis=-1, keepdims=True)
    p = jnp.exp(s - m)
    l = jnp.sum(p, axis=-1, keepdims=True)
    o2 = jnp.dot(p.astype(BF16), v.astype(BF16), preferred_element_type=F32)
    o2 = o2 / l
    lse = m + jnp.log(l)
    o = jnp.where(first, o2[:ATTN_BLOCK], o2[ATTN_BLOCK:])
    lse_b = jnp.where(first, lse[:ATTN_BLOCK], lse[ATTN_BLOCK:])
    return o, lse_b


def _attn_kernel(q_ref, kp_ref, kc_ref, vp_ref, vc_ref, o_ref, kbuf, vbuf, obuf, lbuf, bias_ref):
    hp = pl.program_id(0)
    it = pl.program_id(1)
    g = pl.program_id(2)
    scale = HEAD_DIM ** -0.5

    kbuf[:, :ATTN_BLOCK, :] = kp_ref[...]
    kbuf[:, ATTN_BLOCK:, :] = kc_ref[...]
    vbuf[:, :ATTN_BLOCK, :] = vp_ref[...]
    vbuf[:, ATTN_BLOCK:, :] = vc_ref[...]

    for gi, d in enumerate(DILATIONS):
        nseg, qlen = _seq_index_maps(d)
        klen = 2 * qlen
        nblk = ATTN_BLOCK // qlen

        @pl.when(g == gi)
        def _(gi=gi, d=d, nseg=nseg, qlen=qlen, klen=klen, nblk=nblk):
            b_reg, b_first = _bias_matrices(d, hp)
            bias_ref[0] = b_reg
            bias_ref[1] = b_first

            def block(idx, carry):
                rd = idx // nblk
                bb = idx % nblk
                q0 = pl.multiple_of(bb * qlen, qlen)
                k0 = pl.multiple_of(ATTN_BLOCK + bb * qlen - qlen, qlen)
                qs, ks, vs = [], [], []
                for m_ in range(nseg):
                    rr = rd + d * m_
                    qs.append(q_ref[rr, pl.ds(q0, qlen), :])
                    ks.append(kbuf[rr, pl.ds(k0, klen), :])
                    vs.append(vbuf[rr, pl.ds(k0, klen), :])
                q = jnp.concatenate(qs, axis=0) * scale
                k = jnp.concatenate(ks, axis=0)
                v = jnp.concatenate(vs, axis=0)
                is_first = jnp.logical_and(it == 0, bb == 0)
                bias = bias_ref[jnp.where(is_first, 1, 0)]
                o, lse = _attend_pair(q, k, v, bias)
                for m_ in range(nseg):
                    rr = rd + d * m_
                    obuf[gi, rr, pl.ds(q0, qlen), :] = o[m_ * qlen:(m_ + 1) * qlen]
                    lbuf[gi, rr, pl.ds(q0, qlen), :] = lse[m_ * qlen:(m_ + 1) * qlen]
                return carry

            lax.fori_loop(0, d * nblk, block, 0)

    @pl.when(g == len(DILATIONS) - 1)
    def _():
        for r in range(R):
            l0, l1, l2 = lbuf[0, r], lbuf[1, r], lbuf[2, r]
            mx = jnp.maximum(jnp.maximum(l0, l1), l2)
            e0, e1, e2 = jnp.exp(l0 - mx), jnp.exp(l1 - mx), jnp.exp(l2 - mx)
            den = e0 + e1 + e2
            num = e0 * obuf[0, r] + e1 * obuf[1, r] + e2 * obuf[2, r]
            o_ref[r] = (num / den).astype(BF16)


def attention(qkv3):
    n_hp = N_HEAD_SLOTS // 2
    n_it = NI // ATTN_BLOCK
    ng = len(DILATIONS)
    cb = QKV_COLS // LANES

    def cur(base):
        return pl.BlockSpec((R, ATTN_BLOCK, LANES), lambda hp, it, g: (0, it, base + g * n_hp + hp))

    def prev(base):
        return pl.BlockSpec((R, ATTN_BLOCK, LANES),
                            lambda hp, it, g: (0, jnp.maximum(it - 1, 0), base + g * n_hp + hp))

    return pl.pallas_call(
        _attn_kernel,
        out_shape=jax.ShapeDtypeStruct((R, NI, ATTN_OUT), BF16),
        grid=(n_hp, n_it, ng),
        in_specs=[cur(0), prev(cb), cur(cb), prev(2 * cb), cur(2 * cb)],
        out_specs=pl.BlockSpec((R, ATTN_BLOCK, LANES), lambda hp, it, g: (0, it, hp)),
        scratch_shapes=[pltpu.VMEM((R, 2 * ATTN_BLOCK, LANES), F32),
                        pltpu.VMEM((R, 2 * ATTN_BLOCK, LANES), F32),
                        pltpu.VMEM((ng, R, ATTN_BLOCK, LANES), F32),
                        pltpu.VMEM((ng, R, ATTN_BLOCK, LANES), F32),
                        pltpu.VMEM((2, 2 * ATTN_BLOCK, 2 * ATTN_BLOCK), F32)],
        compiler_params=_cparams(("arbitrary", "arbitrary", "arbitrary")),
        name="dilated_attention",
    )(qkv3, qkv3, qkv3, qkv3, qkv3)


SSM_SLAB = 256
SSM_SLABS = SSM_W // SSM_SLAB
SLAB_STATES = SSM_SLAB // SSM_CH * SSM_STATE
SSM_TI = 128
SSM_MM_CHUNK = 4


def _ssm_kernel(u_ref, wb_ref, wc_ref, a_ref, pw_ref, dsk_ref, o_ref, s_ref, zs_ref, zc_ref):
    ic = pl.program_id(1)
    ns = SLAB_STATES

    @pl.when(ic == 0)
    def _():
        zc_ref[...] = jnp.zeros_like(zc_ref)

    wb = wb_ref[0]
    for c in range(R // SSM_MM_CHUNK):
        uc = u_ref[c * SSM_MM_CHUNK:(c + 1) * SSM_MM_CHUNK].reshape(SSM_MM_CHUNK * SSM_TI, SSM_SLAB)
        bu = jnp.dot(uc.astype(BF16), wb, preferred_element_type=F32)
        s_ref[c * SSM_MM_CHUNK:(c + 1) * SSM_MM_CHUNK] = bu.reshape(SSM_MM_CHUNK, SSM_TI, 2 * ns)

    ar = a_ref[0, :, :ns]
    ai = a_ref[0, :, ns:]

    def local_tile(t, carry):
        rows = pl.ds(pl.multiple_of(t * 8, 8), 8)
        pr = s_ref[0, rows, :ns]
        pi = s_ref[0, rows, ns:]
        for r in range(1, R):
            nr = s_ref[r, rows, :ns] + (ar * pr - ai * pi)
            ni = s_ref[r, rows, ns:] + (ar * pi + ai * pr)
            s_ref[r, rows, :ns] = nr
            s_ref[r, rows, ns:] = ni
            pr, pi = nr, ni
        return carry

    lax.fori_loop(0, SSM_TI // 8, local_tile, 0)

    a16r = pw_ref[0, R - 1:R, :ns]
    a16i = pw_ref[0, R - 1:R, ns:]

    def zstep(i, z):
        zs_ref[pl.ds(i, 1), :] = z
        e = s_ref[R - 1, pl.ds(i, 1), :]
        zr, zi = z[:, :ns], z[:, ns:]
        nz = jnp.concatenate([a16r * zr - a16i * zi, a16r * zi + a16i * zr], axis=-1)
        return nz + e

    zc_ref[...] = lax.fori_loop(0, SSM_TI, zstep, zc_ref[...])

    def fix_tile(t, carry):
        rows = pl.ds(pl.multiple_of(t * 8, 8), 8)
        zr = zs_ref[rows, :ns]
        zi = zs_ref[rows, ns:]
        for r in range(R):
            pr = pw_ref[0, r:r + 1, :ns]
            pi = pw_ref[0, r:r + 1, ns:]
            s_ref[r, rows, :ns] = s_ref[r, rows, :ns] + (pr * zr - pi * zi)
            s_ref[r, rows, ns:] = s_ref[r, rows, ns:] + (pr * zi + pi * zr)
        return carry

    lax.fori_loop(0, SSM_TI // 8, fix_tile, 0)

    wc = wc_ref[0]
    dsk = dsk_ref[...]
    for c in range(R // SSM_MM_CHUNK):
        xs = s_ref[c * SSM_MM_CHUNK:(c + 1) * SSM_MM_CHUNK].reshape(SSM_MM_CHUNK * SSM_TI, 2 * ns)
        y = jnp.dot(xs.astype(BF16), wc, preferred_element_type=F32)
        y = y.reshape(SSM_MM_CHUNK, SSM_TI, SSM_SLAB) + dsk * u_ref[c * SSM_MM_CHUNK:(c + 1) * SSM_MM_CHUNK]
        o_ref[c * SSM_MM_CHUNK:(c + 1) * SSM_MM_CHUNK] = jax.nn.gelu(y)


def ssm_scan(u3, wb, wc, avec, pows, dskip):
    ns2 = 2 * SLAB_STATES
    return pl.pallas_call(
        _ssm_kernel,
        out_shape=jax.ShapeDtypeStruct((R, NI, SSM_W), F32),
        grid=(SSM_SLABS, NI // SSM_TI),
        in_specs=[pl.BlockSpec((R, SSM_TI, SSM_SLAB), lambda kb, ic: (0, ic, kb)),
                  pl.BlockSpec((1, SSM_SLAB, ns2), lambda kb, ic: (kb, 0, 0)),
                  pl.BlockSpec((1, ns2, SSM_SLAB), lambda kb, ic: (kb, 0, 0)),
                  pl.BlockSpec((1, 1, ns2), lambda kb, ic: (kb, 0, 0)),
                  pl.BlockSpec((1, R, ns2), lambda kb, ic: (kb, 0, 0)),
                  pl.BlockSpec((1, SSM_SLAB), lambda kb, ic: (0, kb))],
        out_specs=pl.BlockSpec((R, SSM_TI, SSM_SLAB), lambda kb, ic: (0, ic, kb)),
        scratch_shapes=[pltpu.VMEM((R, SSM_TI, ns2), F32),
                        pltpu.VMEM((SSM_TI, ns2), F32),
                        pltpu.VMEM((1, ns2), F32)],
        compiler_params=_cparams(("arbitrary", "arbitrary")),
        name="s5_ssm",
    )(u3, wb, wc, avec, pows, dskip)


def ssm_params(a_re, a_im, log_step, b_re, b_im, c_re, c_im):
    G, P, H = SSM_W // SSM_CH, SSM_STATE, SSM_CH
    gs = SSM_SLAB // SSM_CH
    step = jnp.exp(log_step.astype(F32))[:, None]
    mag = jnp.exp(a_re * step)
    ang = a_im * step
    abar_re, abar_im = mag * jnp.cos(ang), mag * jnp.sin(ang)
    nr, ni = abar_re - 1.0, abar_im
    den = a_re * a_re + a_im * a_im
    f_re = (nr * a_re + ni * a_im) / den
    f_im = (ni * a_re - nr * a_im) / den
    bbar_re = f_re[..., None] * b_re - f_im[..., None] * b_im
    bbar_im = f_re[..., None] * b_im + f_im[..., None] * b_re
    eye = jnp.eye(gs, dtype=F32)

    def in_mat(b):
        b4 = b.reshape(SSM_SLABS, gs, P, H)
        m = jnp.einsum('kgph,gj->kghjp', b4, eye)
        return m.reshape(SSM_SLABS, gs * H, gs * P)

    def out_mat(c):
        c4 = c.reshape(SSM_SLABS, gs, H, P)
        m = jnp.einsum('kghp,gj->kgpjh', c4, eye)
        return m.reshape(SSM_SLABS, gs * P, gs * H)

    wb = jnp.concatenate([in_mat(bbar_re), in_mat(bbar_im)], axis=-1).astype(BF16)
    wc = jnp.concatenate([out_mat(c_re.astype(F32)), -out_mat(c_im.astype(F32))], axis=1).astype(BF16)

    def slab_vec(v):
        return v.reshape(SSM_SLABS, gs * P)

    avec = jnp.concatenate([slab_vec(abar_re), slab_vec(abar_im)], axis=-1)[:, None, :]
    pr, pi = abar_re, abar_im
    prs, pis = [pr], [pi]
    for _ in range(R - 1):
        pr, pi = pr * abar_re - pi * abar_im, pr * abar_im + pi * abar_re
        prs.append(pr)
        pis.append(pi)
    pows = jnp.concatenate([jnp.stack([slab_vec(p) for p in prs], axis=1),
                            jnp.stack([slab_vec(p) for p in pis], axis=1)], axis=-1)
    return wb, wc, avec, pows


def _glu_kernel(a_ref, w_ref, b_ref, yb_ref, o_ref, abf_ref):
    @pl.when(pl.program_id(1) == 0)
    def _():
        abf_ref[...] = a_ref[...].astype(BF16)

    acc = jnp.dot(abf_ref[...], w_ref[...].astype(BF16), preferred_element_type=F32)
    o_ref[...] = (yb_ref[...] * jax.nn.sigmoid(acc + b_ref[...])).astype(BF16)


def glu(yg, w_glu, b_glu):
    tm, tn = 1024, 512
    return pl.pallas_call(
        _glu_kernel,
        out_shape=jax.ShapeDtypeStruct((T, SSM_W), BF16),
        grid=(T // tm, SSM_W // tn),
        in_specs=[pl.BlockSpec((tm, SSM_W), lambda i, j: (i, 0)),
                  pl.BlockSpec((SSM_W, tn), lambda i, j: (0, j)),
                  pl.BlockSpec((1, tn), lambda i, j: (0, j)),
                  pl.BlockSpec((tm, tn), lambda i, j: (i, j))],
        out_specs=pl.BlockSpec((tm, tn), lambda i, j: (i, j)),
        scratch_shapes=[pltpu.VMEM((tm, SSM_W), BF16)],
        compiler_params=_cparams(("arbitrary", "arbitrary")),
        name="ssm_glu",
    )(yg, w_glu, b_glu, yg)


def _merge_kernel(at_ref, ss_ref, wa_ref, ws_ref, ga_ref, gs_ref, o_ref):
    a = jnp.dot(at_ref[...], wa_ref[...].astype(BF16), preferred_element_type=F32)
    s = jnp.dot(ss_ref[...], ws_ref[...].astype(BF16), preferred_element_type=F32)
    o_ref[...] = (ga_ref[...] * a + gs_ref[...] * s).astype(BF16)


def merge(attn, ssm, w_up_attn, w_up_ssm, gates):
    tm, tn = 1024, 512
    nj = D // tn
    return pl.pallas_call(
        _merge_kernel,
        out_shape=jax.ShapeDtypeStruct((T, D), BF16),
        grid=(T // tm, nj),
        in_specs=[pl.BlockSpec((tm, ATTN_OUT), lambda i, j: (i, 0)),
                  pl.BlockSpec((tm, SSM_W), lambda i, j: (i, 0)),
                  pl.BlockSpec((ATTN_OUT, tn), lambda i, j: (0, j)),
                  pl.BlockSpec((SSM_W, tn), lambda i, j: (0, j)),
                  pl.BlockSpec((tm, tn), lambda i, j: (i, j)),
                  pl.BlockSpec((tm, tn), lambda i, j: (i, j + nj))],
        out_specs=pl.BlockSpec((tm, tn), lambda i, j: (i, j)),
        compiler_params=_cparams(("arbitrary", "arbitrary")),
        name="branch_merge",
    )(attn, ssm, w_up_attn, w_up_ssm, gates, gates)


def _outproj_kernel(m_ref, w_ref, x_ref, o_ref):
    tb = m_ref.shape[1]
    a = m_ref[...].reshape(R * tb, D)
    acc = jnp.dot(a, w_ref[...].astype(BF16), preferred_element_type=F32)
    for r in range(R):
        o_ref[r] = acc[r * tb:(r + 1) * tb] + x_ref[:, r, :]


def out_proj(merged3, w_out, x3):
    tb, tn = 64, 512
    return pl.pallas_call(
        _outproj_kernel,
        out_shape=jax.ShapeDtypeStruct((R, NI, D), F32),
        grid=(NI // tb, D // tn),
        in_specs=[pl.BlockSpec((R, tb, D), lambda i, j: (0, i, 0)),
                  pl.BlockSpec((D, tn), lambda i, j: (0, j)),
                  pl.BlockSpec((tb, R, tn), lambda i, j: (i, 0, j))],
        out_specs=pl.BlockSpec((R, tb, tn), lambda i, j: (0, i, j)),
        compiler_params=_cparams(("arbitrary", "arbitrary")),
        name="out_proj_residual",
    )(merged3, w_out, x3)


def _router_kernel(x_ref, g_ref, w_ref, b_ref, h_ref, id_ref, wt_ref):
    x = x_ref[...]
    ms = jnp.mean(x * x, axis=-1, keepdims=True)
    z = x * lax.rsqrt(ms + NORM_EPS) * g_ref[...]
    h_ref[...] = z
    zh = z.astype(BF16)
    zl = (z - zh.astype(F32)).astype(BF16)
    w = w_ref[...]
    wh = w.astype(BF16)
    wl = (w - wh.astype(F32)).astype(BF16)
    logits = (jnp.dot(zh, wh, preferred_element_type=F32) + jnp.dot(zl, wh, preferred_element_type=F32)
              + jnp.dot(zh, wl, preferred_element_type=F32)) + b_ref[...]
    lane = lax.broadcasted_iota(jnp.int32, logits.shape, 1)
    lanef = lane.astype(F32)
    neg = jnp.float32(-jnp.inf)
    big = jnp.float32(1e9)
    gl = jnp.where(lane < N_EGROUPS, logits, neg)
    gmax = jnp.max(gl, axis=-1, keepdims=True)
    gidx = jnp.min(jnp.where(gl == gmax, lanef, big), axis=-1, keepdims=True)
    pg = 1.0 / jnp.sum(jnp.exp(gl - gmax), axis=-1, keepdims=True)
    lo = N_EGROUPS + EXPERTS_PER_GROUP * gidx
    el = jnp.where((lanef >= lo) & (lanef < lo + EXPERTS_PER_GROUP), logits, neg)
    t1 = jnp.max(el, axis=-1, keepdims=True)
    j1 = jnp.min(jnp.where(el == t1, lanef, big), axis=-1, keepdims=True)
    el2 = jnp.where(lanef == j1, neg, el)
    t2 = jnp.max(el2, axis=-1, keepdims=True)
    j2 = jnp.min(jnp.where(el2 == t2, lanef, big), axis=-1, keepdims=True)
    e21 = jnp.exp(t2 - t1)
    w1 = pg / (1.0 + e21)
    w2 = pg * e21 / (1.0 + e21)
    e1 = (j1 - N_EGROUPS).astype(jnp.int32)
    e2 = (j2 - N_EGROUPS).astype(jnp.int32)
    id_ref[...] = jnp.where(lane == 0, e1, jnp.where(lane == 1, e2, 0))
    wt_ref[...] = jnp.where(lane == 0, w1, jnp.where(lane == 1, w2, 0.0))


def router(x1, gain, w_r, b_r):
    tm = 256
    return pl.pallas_call(
        _router_kernel,
        out_shape=(jax.ShapeDtypeStruct((T, D), F32),
                   jax.ShapeDtypeStruct((T, LANES), jnp.int32),
                   jax.ShapeDtypeStruct((T, LANES), F32)),
        grid=(T // tm,),
        in_specs=[pl.BlockSpec((tm, D), lambda i: (i, 0)),
                  pl.BlockSpec((1, D), lambda i: (0, 0)),
                  pl.BlockSpec((D, LANES), lambda i: (0, 0)),
                  pl.BlockSpec((1, LANES), lambda i: (0, 0))],
        out_specs=(pl.BlockSpec((tm, D), lambda i: (i, 0)),
                   pl.BlockSpec((tm, LANES), lambda i: (i, 0)),
                   pl.BlockSpec((tm, LANES), lambda i: (i, 0))),
        compiler_params=_cparams(("arbitrary",)),
        name="ffn_norm_router",
    )(x1, gain, w_r, b_r)


def _row_copy(src_hbm, row, dst, j, sem):
    return pltpu.make_async_copy(src_hbm.at[pl.ds(row, 1)], dst.at[pl.ds(j, 1)], sem)


def _expert_kernel(be_ref, rt_ref, nu_ref, h_hbm, wg_ref, wu_ref, wd_ref, y_ref,
                   xbuf, sem, wg_bf, wu_bf, wd_bf):
    b = pl.program_id(0)
    n_used = nu_ref[0]
    slot = b % 2

    def issue(blk, s):
        def body(j, c):
            _row_copy(h_hbm, rt_ref[blk * MOE_BLOCK + j], xbuf.at[s], j, sem.at[s]).start()
            return c
        lax.fori_loop(0, MOE_BLOCK, body, 0)

    @pl.when(jnp.logical_and(b == 0, n_used > 0))
    def _():
        issue(0, 0)

    @pl.when(b + 1 < n_used)
    def _():
        issue(b + 1, 1 - slot)

    @pl.when(b < n_used)
    def _():
        def wbody(j, c):
            _row_copy(h_hbm, 0, xbuf.at[slot], j, sem.at[slot]).wait()
            return c
        lax.fori_loop(0, MOE_BLOCK, wbody, 0)

        changed = jnp.logical_or(b == 0, be_ref[b] != be_ref[jnp.maximum(b - 1, 0)])

        @pl.when(changed)
        def _():
            wg_bf[...] = wg_ref[...].astype(BF16)
            wu_bf[...] = wu_ref[...].astype(BF16)
            wd_bf[...] = wd_ref[...].astype(BF16)

        x = xbuf[slot].astype(BF16)
        gate = jnp.dot(x, wg_bf[...], preferred_element_type=F32)
        up = jnp.dot(x, wu_bf[...], preferred_element_type=F32)
        hid = (jax.nn.silu(gate) * up).astype(BF16)
        y_ref[...] = jnp.dot(hid, wd_bf[...], preferred_element_type=F32)

    @pl.when(b >= n_used)
    def _():
        y_ref[...] = jnp.zeros_like(y_ref)


def experts(block_expert, row_token, n_used, h2, w_gate, w_up, w_down):
    grid_spec = pltpu.PrefetchScalarGridSpec(
        num_scalar_prefetch=3,
        grid=(MOE_BLOCKS,),
        in_specs=[pl.BlockSpec(memory_space=pl.ANY),
                  pl.BlockSpec((None, D, EXPERT_FF), lambda b, be, rt, nu: (be[b], 0, 0)),
                  pl.BlockSpec((None, D, EXPERT_FF), lambda b, be, rt, nu: (be[b], 0, 0)),
                  pl.BlockSpec((None, EXPERT_FF, D), lambda b, be, rt, nu: (be[b], 0, 0))],
        out_specs=pl.BlockSpec((MOE_BLOCK, D), lambda b, be, rt, nu: (b, 0)),
        scratch_shapes=[pltpu.VMEM((2, MOE_BLOCK, D), F32),
                        pltpu.SemaphoreType.DMA((2,)),
                        pltpu.VMEM((D, EXPERT_FF), BF16),
                        pltpu.VMEM((D, EXPERT_FF), BF16),
                        pltpu.VMEM((EXPERT_FF, D), BF16)],
    )
    return pl.pallas_call(
        _expert_kernel,
        out_shape=jax.ShapeDtypeStruct((MOE_ROWS, D), F32),
        grid_spec=grid_spec,
        compiler_params=_cparams(("arbitrary",)),
        name="moe_experts",
    )(block_expert, row_token, n_used, h2, w_gate, w_up, w_down)


COMBINE_TB = 16


def _combine_kernel(dest_ref, ys_hbm, x_ref, wt_ref, g_ref, o_ref, ybuf, sem):
    s = pl.program_id(0)
    ns = pl.num_programs(0)
    slot = s % 2
    tb = COMBINE_TB
    rows = R * tb

    def issue(step, sl):
        def body(n, c):
            r = n // tb
            il = n % tb
            tok = r * NI + step * tb + il
            for k in range(TOP_K):
                _row_copy(ys_hbm, dest_ref[tok * TOP_K + k], ybuf.at[sl, k], n, sem.at[sl]).start()
            return c
        lax.fori_loop(0, rows, body, 0)

    @pl.when(s == 0)
    def _():
        issue(0, 0)

    @pl.when(s + 1 < ns)
    def _():
        issue(s + 1, 1 - slot)

    def wbody(n, c):
        for k in range(TOP_K):
            _row_copy(ys_hbm, 0, ybuf.at[slot, k], n, sem.at[slot]).wait()
        return c
    lax.fori_loop(0, rows, wbody, 0)

    g = g_ref[...]
    for r in range(R):
        w = wt_ref[r]
        y0 = ybuf[slot, 0, r * tb:(r + 1) * tb, :]
        y1 = ybuf[slot, 1, r * tb:(r + 1) * tb, :]
        z = x_ref[r] + (w[:, 0:1] * y0 + w[:, 1:2] * y1)
        ms = jnp.mean(z * z, axis=-1, keepdims=True)
        o_ref[:, r, :] = z * lax.rsqrt(ms + NORM_EPS) * g


def combine(dest, ys, x1_3, wts3, gain):
    tb = COMBINE_TB
    grid_spec = pltpu.PrefetchScalarGridSpec(
        num_scalar_prefetch=1,
        grid=(NI // tb,),
        in_specs=[pl.BlockSpec(memory_space=pl.ANY),
                  pl.BlockSpec((R, tb, D), lambda s, dst: (0, s, 0)),
                  pl.BlockSpec((R, tb, LANES), lambda s, dst: (0, s, 0)),
                  pl.BlockSpec((1, D), lambda s, dst: (0, 0))],
        out_specs=pl.BlockSpec((tb, R, D), lambda s, dst: (s, 0, 0)),
        scratch_shapes=[pltpu.VMEM((2, TOP_K, R * tb, D), F32),
                        pltpu.SemaphoreType.DMA((2,))],
    )
    return pl.pallas_call(
        _combine_kernel,
        out_shape=jax.ShapeDtypeStruct((NI, R, D), F32),
        grid_spec=grid_spec,
        compiler_params=_cparams(("arbitrary",)),
        name="moe_combine_final_norm",
    )(dest, ys, x1_3, wts3, gain)


def dispatch_plan(ids):
    flat_e = ids.reshape(N_ASSIGN)
    onehot = (flat_e[:, None] == jnp.arange(N_EXPERTS, dtype=jnp.int32)[None, :]).astype(jnp.int32)
    csum = jnp.cumsum(onehot, axis=0)
    rank = jnp.sum(csum * onehot, axis=1) - 1
    counts = csum[-1]
    padded = (counts + MOE_BLOCK - 1) // MOE_BLOCK * MOE_BLOCK
    pad_end = jnp.cumsum(padded)
    pad_start = pad_end - padded
    dest = (pad_start[flat_e] + rank).astype(jnp.int32)
    flat_tok = jnp.arange(N_ASSIGN, dtype=jnp.int32) // TOP_K
    row_token = jnp.zeros((MOE_ROWS,), jnp.int32).at[dest].set(flat_tok)
    block_start = jnp.arange(MOE_BLOCKS, dtype=jnp.int32) * MOE_BLOCK
    block_expert = jnp.minimum(jnp.searchsorted(pad_end, block_start, side='right'),
                               N_EXPERTS - 1).astype(jnp.int32)
    n_used = (pad_end[-1] // MOE_BLOCK).astype(jnp.int32).reshape(1)
    return block_expert, row_token, n_used, dest


def kernel(x, norm_mix, w_in, b_gate, ssm_a_re, ssm_a_im, ssm_log_step, ssm_b_re, ssm_b_im, ssm_c_re, ssm_c_im, ssm_d, w_glu, b_glu, w_up_attn, w_up_ssm, w_out, norm_ffn, w_router_group, b_router_group, w_router_expert, b_router_expert, w_expert_gate, w_expert_up, w_expert_down, norm_final):
    x3 = x.reshape(NI, R, D)
    h = norm_permute(x3, norm_mix.reshape(1, D)).reshape(T, D)
    w_in_l = w_in.reshape(D, IN_COLS)
    qkv = proj(h, w_in_l, 0, 3 * QKV_COLS, name="proj_qkv")
    u = proj(h, w_in_l, 3 * QKV_COLS, SSM_W, name="proj_ssm_in")
    gates = proj(h, w_in_l, 3 * QKV_COLS + SSM_W, 2 * D, bias=b_gate.reshape(1, 2 * D), name="proj_gates")

    attn = attention(qkv.reshape(R, NI, 3 * QKV_COLS)).reshape(T, ATTN_OUT)

    G = SSM_W // SSM_CH
    wb, wc, avec, pows = ssm_params(
        ssm_a_re.reshape(G, SSM_STATE).astype(F32), ssm_a_im.reshape(G, SSM_STATE).astype(F32),
        ssm_log_step.reshape(G),
        ssm_b_re.reshape(G, SSM_STATE, SSM_CH).astype(F32), ssm_b_im.reshape(G, SSM_STATE, SSM_CH).astype(F32),
        ssm_c_re.reshape(G, SSM_CH, SSM_STATE), ssm_c_im.reshape(G, SSM_CH, SSM_STATE))
    yg = ssm_scan(u.reshape(R, NI, SSM_W), wb, wc, avec, pows, ssm_d.reshape(1, SSM_W).astype(F32))
    ssm = glu(yg.reshape(T, SSM_W), w_glu.reshape(SSM_W, SSM_W), b_glu.reshape(1, SSM_W))

    merged = merge(attn, ssm, w_up_attn.reshape(ATTN_OUT, D), w_up_ssm.reshape(SSM_W, D), gates)
    x1 = out_proj(merged.reshape(R, NI, D), w_out.reshape(D, D), x3)

    w_r = jnp.concatenate([w_router_group.reshape(D, N_EGROUPS), w_router_expert.reshape(D, N_EXPERTS),
                           jnp.zeros((D, LANES - N_EGROUPS - N_EXPERTS), F32)], axis=1)
    b_r = jnp.concatenate([b_router_group.reshape(1, N_EGROUPS), b_router_expert.reshape(1, N_EXPERTS),
                           jnp.zeros((1, LANES - N_EGROUPS - N_EXPERTS), F32)], axis=1)
    h2, ids, wts = router(x1.reshape(T, D), norm_ffn.reshape(1, D), w_r, b_r)

    block_expert, row_token, n_used, dest = dispatch_plan(ids[:, :TOP_K])
    ys = experts(block_expert, row_token, n_used, h2,
                 w_expert_gate.reshape(N_EXPERTS, D, EXPERT_FF), w_expert_up.reshape(N_EXPERTS, D, EXPERT_FF),
                 w_expert_down.reshape(N_EXPERTS, EXPERT_FF, D))
    out3 = combine(dest, ys, x1, wts.reshape(R, NI, LANES), norm_final.reshape(1, D))
    return out3.reshape(1, T, D)
```

```python
import functools
import math

import jax
import jax.numpy as jnp
from jax import lax
from jax.experimental import pallas as pl
from jax.experimental.pallas import tpu as pltpu

F32 = jnp.float32
BF16 = jnp.bfloat16

T = 8192
D = 2048
R = 16
NI = T // R
HEAD_DIM = 64
N_HEAD_SLOTS = 8
DILATIONS = (1, 4, 16)
ATTN_BLOCK = 128
QKV_COLS = 1536
ATTN_OUT = 512
SSM_W = 1024
SSM_STATE = 64
SSM_CH = 16
IN_COLS = 3 * QKV_COLS + SSM_W + 2 * D
N_EXPERTS = 32
N_EGROUPS = 4
EXPERTS_PER_GROUP = 8
TOP_K = 2
EXPERT_FF = 512
NORM_EPS = 1e-6
LANES = 128
VMEM_LIMIT = 48 * 1024 * 1024

MOE_BLOCK = 256
N_ASSIGN = T * TOP_K
MOE_BLOCKS = N_ASSIGN // MOE_BLOCK + N_EXPERTS
MOE_ROWS = MOE_BLOCKS * MOE_BLOCK


def _cparams(sem):
    return pltpu.CompilerParams(dimension_semantics=sem, vmem_limit_bytes=VMEM_LIMIT)


def _norm_permute_kernel(xa_ref, xb_ref, g_ref, h_ref):
    g = g_ref[...]
    tb = xa_ref.shape[0]

    def tile(t, c):
        rows = pl.ds(pl.multiple_of(t * 16, 16), 16)
        for half, ref in enumerate((xa_ref, xb_ref)):
            for s_ in range(8):
                xr = ref[rows, s_, :]
                ms = jnp.mean(xr * xr, axis=-1, keepdims=True)
                h_ref[half * 8 + s_, rows, :] = (xr * lax.rsqrt(ms + NORM_EPS) * g).astype(BF16)
        return c

    lax.fori_loop(0, tb // 16, tile, 0)


def norm_permute(x4, gain):
    tb = 32
    return pl.pallas_call(
        _norm_permute_kernel,
        out_shape=jax.ShapeDtypeStruct((R, NI, D), BF16),
        grid=(NI // tb,),
        in_specs=[pl.BlockSpec((tb, None, 8, D), lambda i: (i, 0, 0, 0)),
                  pl.BlockSpec((tb, None, 8, D), lambda i: (i, 1, 0, 0)),
                  pl.BlockSpec((1, D), lambda i: (0, 0))],
        out_specs=pl.BlockSpec((R, tb, D), lambda i: (0, i, 0)),
        compiler_params=_cparams(("arbitrary",)),
        name="norm_permute",
    )(x4, x4, gain)


def _proj_kernel(a_ref, w_ref, o_ref):
    o_ref[...] = jnp.dot(a_ref[...], w_ref[...].astype(BF16), preferred_element_type=F32)


def _proj_gate_kernel(a_ref, w_ref, b_ref, o_ref):
    acc = jnp.dot(a_ref[...], w_ref[...].astype(BF16), preferred_element_type=F32)
    o_ref[...] = jax.nn.sigmoid(acc + b_ref[...])


def proj(h, w_in, col_off, n_cols, bias=None, name="proj"):
    tm, tn = 1024, 512
    off = col_off // tn
    in_specs = [pl.BlockSpec((tm, D), lambda i, j: (i, 0)),
                pl.BlockSpec((D, tn), lambda i, j: (0, j + off))]
    args = [h, w_in]
    kern = _proj_kernel
    if bias is not None:
        in_specs.append(pl.BlockSpec((1, tn), lambda i, j: (0, j)))
        args.append(bias)
        kern = _proj_gate_kernel
    return pl.pallas_call(
        kern,
        out_shape=jax.ShapeDtypeStruct((T, n_cols), F32),
        grid=(T // tm, n_cols // tn),
        in_specs=in_specs,
        out_specs=pl.BlockSpec((tm, tn), lambda i, j: (i, j)),
        compiler_params=_cparams(("arbitrary", "arbitrary")),
        name=name,
    )(*args)


def _seq_index_maps(d):
    nseg = R // d
    qlen = ATTN_BLOCK // nseg
    return nseg, qlen


def _bias_matrices(d, hp):
    nseg, qlen = _seq_index_maps(d)
    klen = 2 * qlen
    row = lax.broadcasted_iota(jnp.int32, (2 * ATTN_BLOCK, 2 * ATTN_BLOCK), 0)
    col = lax.broadcasted_iota(jnp.int32, (2 * ATTN_BLOCK, 2 * ATTN_BLOCK), 1)
    rho = row % ATTN_BLOCK
    jq = (rho % qlen) * nseg + rho // qlen
    jk = ((col % klen) - qlen) * nseg + col // klen
    steps = jq - jk
    valid = (steps >= 0) & (steps <= ATTN_BLOCK)
    head = 2 * hp + row // ATTN_BLOCK
    slope = lax.bitcast_convert_type((127 - (head + 1)) << 23, F32)
    bias = -slope * (d * steps).astype(F32)
    neg = jnp.float32(-jnp.inf)
    return jnp.where(valid, bias, neg), jnp.where(valid & (jk >= 0), bias, neg)


def _attend_pair(q, k, v, bias):
    lane = lax.broadcasted_iota(jnp.int32, (ATTN_BLOCK, LANES), 1)
    first = lane < HEAD_DIM
    zero = jnp.zeros_like(q)
    q2 = jnp.concatenate([jnp.where(first, q, zero), jnp.where(first, zero, q)], axis=0).astype(BF16)
    s = lax.dot_general(q2, k.astype(BF16), (((1,), (1,)), ((), ())), preferred_element_type=F32)
    s = s + bias
    m = jnp.max(s, axis=-1, keepdims=True)
    p = jnp.exp(s - m)
    l = jnp.sum(p, axis=-1, keepdims=True)
    o2 = jnp.dot(p.astype(BF16), v.astype(BF16), preferred_element_type=F32)
    o2 = o2 / l
    lse = m + jnp.log(l)
    o = jnp.where(first, o2[:ATTN_BLOCK], o2[ATTN_BLOCK:])
    lse_b = jnp.where(first, lse[:ATTN_BLOCK], lse[ATTN_BLOCK:])
    return o, lse_b


def _attn_kernel(q_ref, kp_ref, kc_ref, vp_ref, vc_ref, o_ref, kbuf, vbuf, obuf, lbuf, bias_ref):
    hp = pl.program_id(0)
    it = pl.program_id(1)
    g = pl.program_id(2)
    scale = HEAD_DIM ** -0.5

    kbuf[:, :ATTN_BLOCK, :] = kp_ref[...]
    kbuf[:, ATTN_BLOCK:, :] = kc_ref[...]
    vbuf[:, :ATTN_BLOCK, :] = vp_ref[...]
    vbuf[:, ATTN_BLOCK:, :] = vc_ref[...]

    for gi, d in enumerate(DILATIONS):
        nseg, qlen = _seq_index_maps(d)
        klen = 2 * qlen
        nblk = ATTN_BLOCK // qlen

        @pl.when(g == gi)
        def _(gi=gi, d=d, nseg=nseg, qlen=qlen, klen=klen, nblk=nblk):
            b_reg, b_first = _bias_matrices(d, hp)
            bias_ref[0] = b_reg
            bias_ref[1] = b_first

            def block(idx, carry):
                rd = idx // nblk
                bb = idx % nblk
                q0 = pl.multiple_of(bb * qlen, qlen)
                k0 = pl.multiple_of(ATTN_BLOCK + bb * qlen - qlen, qlen)
                qs, ks, vs = [], [], []
                for m_ in range(nseg):
                    rr = rd + d * m_
                    qs.append(q_ref[rr, pl.ds(q0, qlen), :])
                    ks.append(kbuf[rr, pl.ds(k0, klen), :])
                    vs.append(vbuf[rr, pl.ds(k0, klen), :])
                q = jnp.concatenate(qs, axis=0) * scale
                k = jnp.concatenate(ks, axis=0)
                v = jnp.concatenate(vs, axis=0)
                is_first = jnp.logical_and(it == 0, bb == 0)
                bias = bias_ref[jnp.where(is_first, 1, 0)]
                o, lse = _attend_pair(q, k, v, bias)
                for m_ in range(nseg):
                    rr = rd + d * m_
                    obuf[gi, rr, pl.ds(q0, qlen), :] = o[m_ * qlen:(m_ + 1) * qlen]
                    lbuf[gi, rr, pl.ds(q0, qlen), :] = lse[m_ * qlen:(m_ + 1) * qlen]
                return carry

            lax.fori_loop(0, d * nblk, block, 0, unroll=4)

    @pl.when(g == len(DILATIONS) - 1)
    def _():
        for r in range(R):
            l0, l1, l2 = lbuf[0, r], lbuf[1, r], lbuf[2, r]
            mx = jnp.maximum(jnp.maximum(l0, l1), l2)
            e0, e1, e2 = jnp.exp(l0 - mx), jnp.exp(l1 - mx), jnp.exp(l2 - mx)
            den = e0 + e1 + e2
            num = e0 * obuf[0, r] + e1 * obuf[1, r] + e2 * obuf[2, r]
            o_ref[r] = (num / den).astype(BF16)


def attention(qkv3):
    n_hp = N_HEAD_SLOTS // 2
    n_it = NI // ATTN_BLOCK
    ng = len(DILATIONS)
    cb = QKV_COLS // LANES

    def cur(base):
        return pl.BlockSpec((R, ATTN_BLOCK, LANES), lambda hp, it, g: (0, it, base + g * n_hp + hp))

    def prev(base):
        return pl.BlockSpec((R, ATTN_BLOCK, LANES),
                            lambda hp, it, g: (0, jnp.maximum(it - 1, 0), base + g * n_hp + hp))

    return pl.pallas_call(
        _attn_kernel,
        out_shape=jax.ShapeDtypeStruct((R, NI, ATTN_OUT), BF16),
        grid=(n_hp, n_it, ng),
        in_specs=[cur(0), prev(cb), cur(cb), prev(2 * cb), cur(2 * cb)],
        out_specs=pl.BlockSpec((R, ATTN_BLOCK, LANES), lambda hp, it, g: (0, it, hp)),
        scratch_shapes=[pltpu.VMEM((R, 2 * ATTN_BLOCK, LANES), F32),
                        pltpu.VMEM((R, 2 * ATTN_BLOCK, LANES), F32),
                        pltpu.VMEM((ng, R, ATTN_BLOCK, LANES), F32),
                        pltpu.VMEM((ng, R, ATTN_BLOCK, LANES), F32),
                        pltpu.VMEM((2, 2 * ATTN_BLOCK, 2 * ATTN_BLOCK), F32)],
        compiler_params=_cparams(("arbitrary", "arbitrary", "arbitrary")),
        name="dilated_attention",
    )(qkv3, qkv3, qkv3, qkv3, qkv3)


SSM_SLAB = 256
SSM_SLABS = SSM_W // SSM_SLAB
SLAB_STATES = SSM_SLAB // SSM_CH * SSM_STATE
SSM_TI = 128
SSM_MM_CHUNK = 4


def _ssm_kernel(u_ref, wb_ref, wc_ref, a_ref, pw_ref, dsk_ref, o_ref, s_ref, zs_ref, zc_ref):
    ic = pl.program_id(1)
    ns = SLAB_STATES

    @pl.when(ic == 0)
    def _():
        zc_ref[...] = jnp.zeros_like(zc_ref)

    wb = wb_ref[0]
    for c in range(R // SSM_MM_CHUNK):
        uc = u_ref[c * SSM_MM_CHUNK:(c + 1) * SSM_MM_CHUNK].reshape(SSM_MM_CHUNK * SSM_TI, SSM_SLAB)
        bu = jnp.dot(uc.astype(BF16), wb, preferred_element_type=F32)
        s_ref[c * SSM_MM_CHUNK:(c + 1) * SSM_MM_CHUNK] = bu.reshape(SSM_MM_CHUNK, SSM_TI, 2 * ns)

    ar = a_ref[0, :, :ns]
    ai = a_ref[0, :, ns:]

    def local_tile(t, carry):
        rows = pl.ds(pl.multiple_of(t * 8, 8), 8)
        pr = s_ref[0, rows, :ns]
        pi = s_ref[0, rows, ns:]
        for r in range(1, R):
            nr = s_ref[r, rows, :ns] + (ar * pr - ai * pi)
            ni = s_ref[r, rows, ns:] + (ar * pi + ai * pr)
            s_ref[r, rows, :ns] = nr
            s_ref[r, rows, ns:] = ni
            pr, pi = nr, ni
        return carry

    lax.fori_loop(0, SSM_TI // 8, local_tile, 0)

    a16r = pw_ref[0, R - 1:R, :ns]
    a16i = pw_ref[0, R - 1:R, ns:]

    def zstep(i, z):
        zs_ref[pl.ds(i, 1), :] = z
        e = s_ref[R - 1, pl.ds(i, 1), :]
        zr, zi = z[:, :ns], z[:, ns:]
        nz = jnp.concatenate([a16r * zr - a16i * zi, a16r * zi + a16i * zr], axis=-1)
        return nz + e

    zc_ref[...] = lax.fori_loop(0, SSM_TI, zstep, zc_ref[...])

    def fix_tile(t, carry):
        rows = pl.ds(pl.multiple_of(t * 8, 8), 8)
        zr = zs_ref[rows, :ns]
        zi = zs_ref[rows, ns:]
        for r in range(R):
            pr = pw_ref[0, r:r + 1, :ns]
            pi = pw_ref[0, r:r + 1, ns:]
            s_ref[r, rows, :ns] = s_ref[r, rows, :ns] + (pr * zr - pi * zi)
            s_ref[r, rows, ns:] = s_ref[r, rows, ns:] + (pr * zi + pi * zr)
        return carry

    lax.fori_loop(0, SSM_TI // 8, fix_tile, 0)

    wc = wc_ref[0]
    dsk = dsk_ref[...]
    for c in range(R // SSM_MM_CHUNK):
        xs = s_ref[c * SSM_MM_CHUNK:(c + 1) * SSM_MM_CHUNK].reshape(SSM_MM_CHUNK * SSM_TI, 2 * ns)
        y = jnp.dot(xs.astype(BF16), wc, preferred_element_type=F32)
        y = y.reshape(SSM_MM_CHUNK, SSM_TI, SSM_SLAB) + dsk * u_ref[c * SSM_MM_CHUNK:(c + 1) * SSM_MM_CHUNK]
        o_ref[c * SSM_MM_CHUNK:(c + 1) * SSM_MM_CHUNK] = jax.nn.gelu(y)


def ssm_scan(u3, wb, wc, avec, pows, dskip):
    ns2 = 2 * SLAB_STATES
    return pl.pallas_call(
        _ssm_kernel,
        out_shape=jax.ShapeDtypeStruct((R, NI, SSM_W), F32),
        grid=(SSM_SLABS, NI // SSM_TI),
        in_specs=[pl.BlockSpec((R, SSM_TI, SSM_SLAB), lambda kb, ic: (0, ic, kb)),
                  pl.BlockSpec((1, SSM_SLAB, ns2), lambda kb, ic: (kb, 0, 0)),
                  pl.BlockSpec((1, ns2, SSM_SLAB), lambda kb, ic: (kb, 0, 0)),
                  pl.BlockSpec((1, 1, ns2), lambda kb, ic: (kb, 0, 0)),
                  pl.BlockSpec((1, R, ns2), lambda kb, ic: (kb, 0, 0)),
                  pl.BlockSpec((1, SSM_SLAB), lambda kb, ic: (0, kb))],
        out_specs=pl.BlockSpec((R, SSM_TI, SSM_SLAB), lambda kb, ic: (0, ic, kb)),
        scratch_shapes=[pltpu.VMEM((R, SSM_TI, ns2), F32),
                        pltpu.VMEM((SSM_TI, ns2), F32),
                        pltpu.VMEM((1, ns2), F32)],
        compiler_params=_cparams(("arbitrary", "arbitrary")),
        name="s5_ssm",
    )(u3, wb, wc, avec, pows, dskip)


def ssm_params(a_re, a_im, log_step, b_re, b_im, c_re, c_im):
    G, P, H = SSM_W // SSM_CH, SSM_STATE, SSM_CH
    gs = SSM_SLAB // SSM_CH
    step = jnp.exp(log_step.astype(F32))[:, None]
    mag = jnp.exp(a_re * step)
    ang = a_im * step
    abar_re, abar_im = mag * jnp.cos(ang), mag * jnp.sin(ang)
    nr, ni = abar_re - 1.0, abar_im
    den = a_re * a_re + a_im * a_im
    f_re = (nr * a_re + ni * a_im) / den
    f_im = (ni * a_re - nr * a_im) / den
    bbar_re = f_re[..., None] * b_re - f_im[..., None] * b_im
    bbar_im = f_re[..., None] * b_im + f_im[..., None] * b_re
    eye = jnp.eye(gs, dtype=F32)

    def in_mat(b):
        b4 = b.reshape(SSM_SLABS, gs, P, H)
        m = jnp.einsum('kgph,gj->kghjp', b4, eye)
        return m.reshape(SSM_SLABS, gs * H, gs * P)

    def out_mat(c):
        c4 = c.reshape(SSM_SLABS, gs, H, P)
        m = jnp.einsum('kghp,gj->kgpjh', c4, eye)
        return m.reshape(SSM_SLABS, gs * P, gs * H)

    wb = jnp.concatenate([in_mat(bbar_re), in_mat(bbar_im)], axis=-1).astype(BF16)
    wc = jnp.concatenate([out_mat(c_re.astype(F32)), -out_mat(c_im.astype(F32))], axis=1).astype(BF16)

    def slab_vec(v):
        return v.reshape(SSM_SLABS, gs * P)

    avec = jnp.concatenate([slab_vec(abar_re), slab_vec(abar_im)], axis=-1)[:, None, :]
    pr, pi = abar_re, abar_im
    prs, pis = [pr], [pi]
    for _ in range(R - 1):
        pr, pi = pr * abar_re - pi * abar_im, pr * abar_im + pi * abar_re
        prs.append(pr)
        pis.append(pi)
    pows = jnp.concatenate([jnp.stack([slab_vec(p) for p in prs], axis=1),
                            jnp.stack([slab_vec(p) for p in pis], axis=1)], axis=-1)
    return wb, wc, avec, pows


def _glu_kernel(a_ref, w_ref, b_ref, yb_ref, o_ref, abf_ref):
    @pl.when(pl.program_id(1) == 0)
    def _():
        abf_ref[...] = a_ref[...].astype(BF16)

    acc = jnp.dot(abf_ref[...], w_ref[...].astype(BF16), preferred_element_type=F32)
    o_ref[...] = (yb_ref[...] * jax.nn.sigmoid(acc + b_ref[...])).astype(BF16)


def glu(yg, w_glu, b_glu):
    tm, tn = 1024, 512
    return pl.pallas_call(
        _glu_kernel,
        out_shape=jax.ShapeDtypeStruct((T, SSM_W), BF16),
        grid=(T // tm, SSM_W // tn),
        in_specs=[pl.BlockSpec((tm, SSM_W), lambda i, j: (i, 0)),
                  pl.BlockSpec((SSM_W, tn), lambda i, j: (0, j)),
                  pl.BlockSpec((1, tn), lambda i, j: (0, j)),
                  pl.BlockSpec((tm, tn), lambda i, j: (i, j))],
        out_specs=pl.BlockSpec((tm, tn), lambda i, j: (i, j)),
        scratch_shapes=[pltpu.VMEM((tm, SSM_W), BF16)],
        compiler_params=_cparams(("arbitrary", "arbitrary")),
        name="ssm_glu",
    )(yg, w_glu, b_glu, yg)


def _merge_kernel(at_ref, ss_ref, wa_ref, ws_ref, ga_ref, gs_ref, o_ref):
    a = jnp.dot(at_ref[...], wa_ref[...].astype(BF16), preferred_element_type=F32)
    s = jnp.dot(ss_ref[...], ws_ref[...].astype(BF16), preferred_element_type=F32)
    o_ref[...] = (ga_ref[...] * a + gs_ref[...] * s).astype(BF16)


def merge(attn, ssm, w_up_attn, w_up_ssm, gates):
    tm, tn = 1024, 512
    nj = D // tn
    return pl.pallas_call(
        _merge_kernel,
        out_shape=jax.ShapeDtypeStruct((T, D), BF16),
        grid=(T // tm, nj),
        in_specs=[pl.BlockSpec((tm, ATTN_OUT), lambda i, j: (i, 0)),
                  pl.BlockSpec((tm, SSM_W), lambda i, j: (i, 0)),
                  pl.BlockSpec((ATTN_OUT, tn), lambda i, j: (0, j)),
                  pl.BlockSpec((SSM_W, tn), lambda i, j: (0, j)),
                  pl.BlockSpec((tm, tn), lambda i, j: (i, j)),
                  pl.BlockSpec((tm, tn), lambda i, j: (i, j + nj))],
        out_specs=pl.BlockSpec((tm, tn), lambda i, j: (i, j)),
        compiler_params=_cparams(("arbitrary", "arbitrary")),
        name="branch_merge",
    )(attn, ssm, w_up_attn, w_up_ssm, gates, gates)


def _outproj_kernel(m_ref, w_ref, x_ref, o_ref):
    tb = m_ref.shape[1]
    a = m_ref[...].reshape(R * tb, D)
    acc = jnp.dot(a, w_ref[...].astype(BF16), preferred_element_type=F32)
    for r in range(R):
        o_ref[r] = acc[r * tb:(r + 1) * tb] + x_ref[:, r, :]


def out_proj(merged3, w_out, x3):
    tb, tn = 64, 512
    return pl.pallas_call(
        _outproj_kernel,
        out_shape=jax.ShapeDtypeStruct((R, NI, D), F32),
        grid=(NI // tb, D // tn),
        in_specs=[pl.BlockSpec((R, tb, D), lambda i, j: (0, i, 0)),
                  pl.BlockSpec((D, tn), lambda i, j: (0, j)),
                  pl.BlockSpec((tb, R, tn), lambda i, j: (i, 0, j))],
        out_specs=pl.BlockSpec((R, tb, tn), lambda i, j: (0, i, j)),
        compiler_params=_cparams(("arbitrary", "arbitrary")),
        name="out_proj_residual",
    )(merged3, w_out, x3)


PACK_ROWS = 8
HALF_D = D // 2


def _pack_bf16_pairs(z):
    zb = lax.bitcast_convert_type(z, jnp.uint32)
    rnd = zb + jnp.uint32(0x7FFF) + ((zb >> 16) & jnp.uint32(1))
    top = rnd & jnp.uint32(0xFFFF0000)
    return top[:, HALF_D:] | (top[:, :HALF_D] >> 16)


def _unpack_bf16_pairs(x_ref):
    lo, hi = [], []
    for c in range(PACK_ROWS):
        w = x_ref[:, c, :]
        lo.append(lax.bitcast_convert_type(w << 16, F32))
        hi.append(lax.bitcast_convert_type(w & jnp.uint32(0xFFFF0000), F32))
    return jnp.concatenate(lo + hi, axis=1).astype(BF16)


def _router_kernel(x_ref, g_ref, w_ref, b_ref, h_ref, id_ref, wt_ref, cnt_ref, carry_ref):
    step = pl.program_id(0)

    @pl.when(step == 0)
    def _():
        carry_ref[...] = jnp.zeros_like(carry_ref)

    x = x_ref[...]
    ms = jnp.mean(x * x, axis=-1, keepdims=True)
    z = x * lax.rsqrt(ms + NORM_EPS) * g_ref[...]
    word = _pack_bf16_pairs(z)
    for c in range(PACK_ROWS):
        h_ref[:, c, :] = word[:, c * LANES:(c + 1) * LANES]
    zh = z.astype(BF16)
    zl = (z - zh.astype(F32)).astype(BF16)
    w = w_ref[...]
    wh = w.astype(BF16)
    wl = (w - wh.astype(F32)).astype(BF16)
    logits = (jnp.dot(zh, wh, preferred_element_type=F32) + jnp.dot(zl, wh, preferred_element_type=F32)
              + jnp.dot(zh, wl, preferred_element_type=F32)) + b_ref[...]
    lane = lax.broadcasted_iota(jnp.int32, logits.shape, 1)
    lanef = lane.astype(F32)
    neg = jnp.float32(-jnp.inf)
    big = jnp.float32(1e9)
    gl = jnp.where(lane < N_EGROUPS, logits, neg)
    gmax = jnp.max(gl, axis=-1, keepdims=True)
    gidx = jnp.min(jnp.where(gl == gmax, lanef, big), axis=-1, keepdims=True)
    pg = 1.0 / jnp.sum(jnp.exp(gl - gmax), axis=-1, keepdims=True)
    lo = N_EGROUPS + EXPERTS_PER_GROUP * gidx
    el = jnp.where((lanef >= lo) & (lanef < lo + EXPERTS_PER_GROUP), logits, neg)
    t1 = jnp.max(el, axis=-1, keepdims=True)
    j1 = jnp.min(jnp.where(el == t1, lanef, big), axis=-1, keepdims=True)
    el2 = jnp.where(lanef == j1, neg, el)
    t2 = jnp.max(el2, axis=-1, keepdims=True)
    j2 = jnp.min(jnp.where(el2 == t2, lanef, big), axis=-1, keepdims=True)
    e21 = jnp.exp(t2 - t1)
    w1 = pg / (1.0 + e21)
    w2 = pg * e21 / (1.0 + e21)
    e1f = j1 - N_EGROUPS
    e2f = j2 - N_EGROUPS

    tm = x.shape[0]
    oh1 = (lanef == e1f).astype(F32)
    oh2 = (lanef == e2f).astype(F32)
    ri = lax.broadcasted_iota(jnp.int32, (tm, tm), 0)
    ci = lax.broadcasted_iota(jnp.int32, (tm, tm), 1)
    before = (ci < ri).astype(BF16)
    p1 = jnp.dot(before, oh1.astype(BF16), preferred_element_type=F32)
    p2 = jnp.dot(before, oh2.astype(BF16), preferred_element_type=F32)
    carry = carry_ref[...]
    c1 = jnp.sum(oh1, axis=0, keepdims=True)
    c2 = jnp.sum(oh2, axis=0, keepdims=True)
    rank1 = jnp.sum(oh1 * (carry + p1), axis=-1, keepdims=True)
    rank2 = jnp.sum(oh2 * (carry + c1 + p2), axis=-1, keepdims=True)
    carry = carry + c1 + c2
    carry_ref[...] = carry
    cnt_ref[...] = jnp.broadcast_to(carry, cnt_ref.shape).astype(jnp.int32)

    ids = jnp.where(lane == 0, e1f, jnp.where(lane == 1, e2f, jnp.where(lane == 2, rank1, jnp.where(lane == 3, rank2, 0.0))))
    id_ref[...] = ids.astype(jnp.int32)
    wt_ref[...] = jnp.where(lane == 0, w1, jnp.where(lane == 1, w2, 0.0))


def router(x1, gain, w_r, b_r):
    tm = 256
    return pl.pallas_call(
        _router_kernel,
        out_shape=(jax.ShapeDtypeStruct((T, PACK_ROWS, LANES), jnp.uint32),
                   jax.ShapeDtypeStruct((T, LANES), jnp.int32),
                   jax.ShapeDtypeStruct((T, LANES), F32),
                   jax.ShapeDtypeStruct((8, LANES), jnp.int32)),
        grid=(T // tm,),
        in_specs=[pl.BlockSpec((tm, D), lambda i: (i, 0)),
                  pl.BlockSpec((1, D), lambda i: (0, 0)),
                  pl.BlockSpec((D, LANES), lambda i: (0, 0)),
                  pl.BlockSpec((1, LANES), lambda i: (0, 0))],
        out_specs=(pl.BlockSpec((tm, PACK_ROWS, LANES), lambda i: (i, 0, 0)),
                   pl.BlockSpec((tm, LANES), lambda i: (i, 0)),
                   pl.BlockSpec((tm, LANES), lambda i: (i, 0)),
                   pl.BlockSpec((8, LANES), lambda i: (0, 0))),
        scratch_shapes=[pltpu.VMEM((1, LANES), F32)],
        compiler_params=_cparams(("arbitrary",)),
        name="ffn_norm_router",
    )(x1, gain, w_r, b_r)


DISPATCH_CHUNK = 512
N_ZERO_FILLS = 2 * N_EXPERTS
YS_ROWS = D // LANES


def _dispatch_kernel(dest_ref, zs_ref, h_hbm, xs_hbm, zbuf, zsem, sem):
    zbuf[...] = jnp.zeros_like(zbuf)

    def zero_copy(e):
        return pltpu.make_async_copy(zbuf, xs_hbm.at[pl.ds(jnp.maximum(zs_ref[e], 0), MOE_BLOCK)], zsem.at[0])

    def zstart(e, c):
        @pl.when(zs_ref[e] >= 0)
        def _():
            zero_copy(e).start()
        return c

    def zwait(e, c):
        @pl.when(zs_ref[e] >= 0)
        def _():
            zero_copy(e).wait()
        return c

    lax.fori_loop(0, N_ZERO_FILLS, zstart, 0)
    lax.fori_loop(0, N_ZERO_FILLS, zwait, 0)

    def row_copy(a, s):
        return pltpu.make_async_copy(h_hbm.at[a // TOP_K], xs_hbm.at[dest_ref[a]], sem.at[s])

    n_chunks = N_ASSIGN // DISPATCH_CHUNK

    def issue(chunk, s):
        def body(n, c):
            row_copy(chunk * DISPATCH_CHUNK + n, s).start()
            return c
        lax.fori_loop(0, DISPATCH_CHUNK, body, 0, unroll=8)

    def drain(chunk, s):
        def body(n, c):
            row_copy(chunk * DISPATCH_CHUNK + n, s).wait()
            return c
        lax.fori_loop(0, DISPATCH_CHUNK, body, 0, unroll=8)

    issue(0, 0)

    def chunk_body(chunk, c):
        s = chunk % 2
        issue(chunk, s)
        drain(chunk - 1, 1 - s)
        return c

    lax.fori_loop(1, n_chunks, chunk_body, 0)
    drain(n_chunks - 1, (n_chunks - 1) % 2)


def dispatch(dest, zero_start, hpk):
    grid_spec = pltpu.PrefetchScalarGridSpec(
        num_scalar_prefetch=2,
        grid=(1,),
        in_specs=[pl.BlockSpec(memory_space=pl.ANY)],
        out_specs=pl.BlockSpec(memory_space=pl.ANY),
        scratch_shapes=[pltpu.VMEM((MOE_BLOCK, PACK_ROWS, LANES), jnp.uint32),
                        pltpu.SemaphoreType.DMA((1,)),
                        pltpu.SemaphoreType.DMA((2,))],
    )
    return pl.pallas_call(
        _dispatch_kernel,
        out_shape=jax.ShapeDtypeStruct((MOE_ROWS, PACK_ROWS, LANES), jnp.uint32),
        grid_spec=grid_spec,
        compiler_params=_cparams(("arbitrary",)),
        name="moe_dispatch",
    )(dest, zero_start, hpk)


def _expert_kernel(be_ref, nu_ref, x_ref, wg_ref, wu_ref, wd_ref, y_ref, wg_bf, wu_bf, wd_bf):
    b = pl.program_id(0)

    @pl.when(b < nu_ref[0])
    def _():
        changed = jnp.logical_or(b == 0, be_ref[b] != be_ref[jnp.maximum(b - 1, 0)])

        @pl.when(changed)
        def _():
            wg_bf[...] = wg_ref[...].astype(BF16)
            wu_bf[...] = wu_ref[...].astype(BF16)
            wd_bf[...] = wd_ref[...].astype(BF16)

        x = _unpack_bf16_pairs(x_ref)
        gate = jnp.dot(x, wg_bf[...], preferred_element_type=F32)
        up = jnp.dot(x, wu_bf[...], preferred_element_type=F32)
        hid = (jax.nn.silu(gate) * up).astype(BF16)
        y = jnp.dot(hid, wd_bf[...], preferred_element_type=F32)
        for c in range(YS_ROWS):
            y_ref[:, c, :] = y[:, c * LANES:(c + 1) * LANES]

    @pl.when(b >= nu_ref[0])
    def _():
        y_ref[...] = jnp.zeros_like(y_ref)


def experts(block_expert, n_used, xs, w_gate, w_up, w_down):
    def blk(b, be, nu):
        return jnp.minimum(b, nu[0] - 1)

    grid_spec = pltpu.PrefetchScalarGridSpec(
        num_scalar_prefetch=2,
        grid=(MOE_BLOCKS,),
        in_specs=[pl.BlockSpec((MOE_BLOCK, PACK_ROWS, LANES), lambda b, be, nu: (blk(b, be, nu), 0, 0)),
                  pl.BlockSpec((None, D, EXPERT_FF), lambda b, be, nu: (be[b], 0, 0)),
                  pl.BlockSpec((None, D, EXPERT_FF), lambda b, be, nu: (be[b], 0, 0)),
                  pl.BlockSpec((None, EXPERT_FF, D), lambda b, be, nu: (be[b], 0, 0))],
        out_specs=pl.BlockSpec((MOE_BLOCK, YS_ROWS, LANES), lambda b, be, nu: (b, 0, 0)),
        scratch_shapes=[pltpu.VMEM((D, EXPERT_FF), BF16),
                        pltpu.VMEM((D, EXPERT_FF), BF16),
                        pltpu.VMEM((EXPERT_FF, D), BF16)],
    )
    return pl.pallas_call(
        _expert_kernel,
        out_shape=jax.ShapeDtypeStruct((MOE_ROWS, YS_ROWS, LANES), F32),
        grid_spec=grid_spec,
        compiler_params=_cparams(("arbitrary",)),
        name="moe_experts",
    )(block_expert, n_used, xs, w_gate, w_up, w_down)


COMBINE_TB = 16


def _combine_kernel(dest_ref, ys_hbm, x_ref, wt_ref, g_ref, o_ref, ybuf, sem):
    s = pl.program_id(0)
    ns = pl.num_programs(0)
    slot = s % 2
    tb = COMBINE_TB
    rows = R * tb

    def row_copy(step, n, k, sl):
        tok = (n // tb) * NI + step * tb + n % tb
        return pltpu.make_async_copy(ys_hbm.at[dest_ref[tok * TOP_K + k]], ybuf.at[sl, k, n], sem.at[sl])

    def issue(step, sl):
        def body(n, c):
            for k in range(TOP_K):
                row_copy(step, n, k, sl).start()
            return c
        lax.fori_loop(0, rows, body, 0, unroll=4)

    @pl.when(s == 0)
    def _():
        issue(0, 0)

    @pl.when(s + 1 < ns)
    def _():
        issue(s + 1, 1 - slot)

    def wbody(n, c):
        for k in range(TOP_K):
            row_copy(s, n, k, slot).wait()
        return c
    lax.fori_loop(0, rows, wbody, 0, unroll=4)

    g = g_ref[...]
    for r in range(R):
        w = wt_ref[r]
        y0 = jnp.concatenate([ybuf[slot, 0, r * tb:(r + 1) * tb, c, :] for c in range(YS_ROWS)], axis=1)
        y1 = jnp.concatenate([ybuf[slot, 1, r * tb:(r + 1) * tb, c, :] for c in range(YS_ROWS)], axis=1)
        z = x_ref[r] + (w[:, 0:1] * y0 + w[:, 1:2] * y1)
        ms = jnp.mean(z * z, axis=-1, keepdims=True)
        o_ref[:, r, :] = z * lax.rsqrt(ms + NORM_EPS) * g


def combine(dest, ys, x1_3, wts3, gain):
    tb = COMBINE_TB
    grid_spec = pltpu.PrefetchScalarGridSpec(
        num_scalar_prefetch=1,
        grid=(NI // tb,),
        in_specs=[pl.BlockSpec(memory_space=pl.ANY),
                  pl.BlockSpec((R, tb, D), lambda s, dst: (0, s, 0)),
                  pl.BlockSpec((R, tb, LANES), lambda s, dst: (0, s, 0)),
                  pl.BlockSpec((1, D), lambda s, dst: (0, 0))],
        out_specs=pl.BlockSpec((tb, R, D), lambda s, dst: (s, 0, 0)),
        scratch_shapes=[pltpu.VMEM((2, TOP_K, R * tb, YS_ROWS, LANES), F32),
                        pltpu.SemaphoreType.DMA((2,))],
    )
    return pl.pallas_call(
        _combine_kernel,
        out_shape=jax.ShapeDtypeStruct((NI, R, D), F32),
        grid_spec=grid_spec,
        compiler_params=_cparams(("arbitrary",)),
        name="moe_combine_final_norm",
    )(dest, ys, x1_3, wts3, gain)


def dispatch_plan(ids, counts):
    experts_ = jnp.arange(N_EXPERTS, dtype=jnp.int32)
    padded = (counts + MOE_BLOCK - 1) // MOE_BLOCK * MOE_BLOCK
    pad_end = jnp.cumsum(padded)
    pad_start = pad_end - padded
    e = ids[:, :TOP_K]
    start_of = jnp.sum(jnp.where(e[:, :, None] == experts_[None, None, :], pad_start[None, None, :], 0), axis=-1)
    dest = (start_of + ids[:, TOP_K:2 * TOP_K]).reshape(N_ASSIGN).astype(jnp.int32)
    n_used = pad_end[-1] // MOE_BLOCK
    block_start = jnp.minimum(jnp.arange(MOE_BLOCKS, dtype=jnp.int32), n_used - 1) * MOE_BLOCK
    block_expert = jnp.sum((block_start[:, None] >= pad_end[None, :]).astype(jnp.int32), axis=1)
    block_expert = jnp.minimum(block_expert, N_EXPERTS - 1).astype(jnp.int32)
    tail = n_used + experts_
    zero_start = jnp.concatenate([jnp.where(counts > 0, pad_end - MOE_BLOCK, -1),
                                  jnp.where(tail < MOE_BLOCKS, tail * MOE_BLOCK, -1)]).astype(jnp.int32)
    return block_expert, n_used.astype(jnp.int32).reshape(1), dest, zero_start


def kernel(x, norm_mix, w_in, b_gate, ssm_a_re, ssm_a_im, ssm_log_step, ssm_b_re, ssm_b_im, ssm_c_re, ssm_c_im, ssm_d, w_glu, b_glu, w_up_attn, w_up_ssm, w_out, norm_ffn, w_router_group, b_router_group, w_router_expert, b_router_expert, w_expert_gate, w_expert_up, w_expert_down, norm_final):
    x3 = x.reshape(NI, R, D)
    h = norm_permute(x.reshape(NI, 2, 8, D), norm_mix.reshape(1, D)).reshape(T, D)
    w_in_l = w_in.reshape(D, IN_COLS)
    qkv = proj(h, w_in_l, 0, 3 * QKV_COLS, name="proj_qkv")
    u = proj(h, w_in_l, 3 * QKV_COLS, SSM_W, name="proj_ssm_in")
    gates = proj(h, w_in_l, 3 * QKV_COLS + SSM_W, 2 * D, bias=b_gate.reshape(1, 2 * D), name="proj_gates")

    attn = attention(qkv.reshape(R, NI, 3 * QKV_COLS)).reshape(T, ATTN_OUT)

    G = SSM_W // SSM_CH
    wb, wc, avec, pows = ssm_params(
        ssm_a_re.reshape(G, SSM_STATE).astype(F32), ssm_a_im.reshape(G, SSM_STATE).astype(F32),
        ssm_log_step.reshape(G),
        ssm_b_re.reshape(G, SSM_STATE, SSM_CH).astype(F32), ssm_b_im.reshape(G, SSM_STATE, SSM_CH).astype(F32),
        ssm_c_re.reshape(G, SSM_CH, SSM_STATE), ssm_c_im.reshape(G, SSM_CH, SSM_STATE))
    yg = ssm_scan(u.reshape(R, NI, SSM_W), wb, wc, avec, pows, ssm_d.reshape(1, SSM_W).astype(F32))
    ssm = glu(yg.reshape(T, SSM_W), w_glu.reshape(SSM_W, SSM_W), b_glu.reshape(1, SSM_W))

    merged = merge(attn, ssm, w_up_attn.reshape(ATTN_OUT, D), w_up_ssm.reshape(SSM_W, D), gates)
    x1 = out_proj(merged.reshape(R, NI, D), w_out.reshape(D, D), x3)

    w_r = jnp.concatenate([w_router_group.reshape(D, N_EGROUPS), w_router_expert.reshape(D, N_EXPERTS),
                           jnp.zeros((D, LANES - N_EGROUPS - N_EXPERTS), F32)], axis=1)
    b_r = jnp.concatenate([b_router_group.reshape(1, N_EGROUPS), b_router_expert.reshape(1, N_EXPERTS),
                           jnp.zeros((1, LANES - N_EGROUPS - N_EXPERTS), F32)], axis=1)
    hpk, ids, wts, counts = router(x1.reshape(T, D), norm_ffn.reshape(1, D), w_r, b_r)

    block_expert, n_used, dest, zero_start = dispatch_plan(ids[:, :2 * TOP_K], counts[0, :N_EXPERTS])
    xs = dispatch(dest, zero_start, hpk)
    ys = experts(block_expert, n_used, xs,
                 w_expert_gate.reshape(N_EXPERTS, D, EXPERT_FF), w_expert_up.reshape(N_EXPERTS, D, EXPERT_FF),
                 w_expert_down.reshape(N_EXPERTS, EXPERT_FF, D))
    out3 = combine(dest, ys, x1, wts.reshape(R, NI, LANES), norm_final.reshape(1, D))
    return out3.reshape(1, T, D)
```

```python
import functools
import math

import jax
import jax.numpy as jnp
from jax import lax
from jax.experimental import pallas as pl
from jax.experimental.pallas import tpu as pltpu

F32 = jnp.float32
BF16 = jnp.bfloat16

T = 8192
D = 2048
R = 16
NI = T // R
HEAD_DIM = 64
N_HEAD_SLOTS = 8
DILATIONS = (1, 4, 16)
ATTN_BLOCK = 128
QKV_COLS = 1536
ATTN_OUT = 512
SSM_W = 1024
SSM_STATE = 64
SSM_CH = 16
IN_COLS = 3 * QKV_COLS + SSM_W + 2 * D
N_EXPERTS = 32
N_EGROUPS = 4
EXPERTS_PER_GROUP = 8
TOP_K = 2
EXPERT_FF = 512
NORM_EPS = 1e-6
LANES = 128
VMEM_LIMIT = 48 * 1024 * 1024

MOE_BLOCK = 256
N_ASSIGN = T * TOP_K
MOE_BLOCKS = N_ASSIGN // MOE_BLOCK + N_EXPERTS
MOE_ROWS = MOE_BLOCKS * MOE_BLOCK


def _cparams(sem):
    return pltpu.CompilerParams(dimension_semantics=sem, vmem_limit_bytes=VMEM_LIMIT)


N_SLABS = D // LANES
NORM_TB = 32
NORM_CHUNK = 64


def _norm_permute_kernel(x_ref, g_ref, h_ref, slab):
    g = g_ref[...]

    def chunk(t, c):
        rows = pl.ds(pl.multiple_of(t * NORM_CHUNK, NORM_CHUNK), NORM_CHUNK)
        x = x_ref[rows, :]
        ms = jnp.mean(x * x, axis=-1, keepdims=True)
        hn = x * lax.rsqrt(ms + NORM_EPS) * g
        for s_ in range(N_SLABS):
            slab[s_, rows, :] = hn[:, s_ * LANES:(s_ + 1) * LANES]
        return c

    lax.fori_loop(0, NORM_TB * R // NORM_CHUNK, chunk, 0)
    for r in range(R):
        pieces = [slab[s_, pl.ds(r, NORM_TB, stride=R), :] for s_ in range(N_SLABS)]
        h_ref[r] = jnp.concatenate(pieces, axis=1).astype(BF16)


def norm_permute(x2, gain):
    return pl.pallas_call(
        _norm_permute_kernel,
        out_shape=jax.ShapeDtypeStruct((R, NI, D), BF16),
        grid=(NI // NORM_TB,),
        in_specs=[pl.BlockSpec((NORM_TB * R, D), lambda i: (i, 0)),
                  pl.BlockSpec((1, D), lambda i: (0, 0))],
        out_specs=pl.BlockSpec((R, NORM_TB, D), lambda i: (0, i, 0)),
        scratch_shapes=[pltpu.VMEM((N_SLABS, NORM_TB * R, LANES), F32)],
        compiler_params=_cparams(("arbitrary",)),
        name="norm_permute",
    )(x2, gain)


def _proj_kernel(a_ref, w_ref, o_ref):
    o_ref[...] = jnp.dot(a_ref[...], w_ref[...].astype(BF16), preferred_element_type=F32)


def _proj_gate_kernel(a_ref, w_ref, b_ref, o_ref):
    acc = jnp.dot(a_ref[...], w_ref[...].astype(BF16), preferred_element_type=F32)
    o_ref[...] = jax.nn.sigmoid(acc + b_ref[...])


def proj(h, w_in, col_off, n_cols, bias=None, name="proj"):
    tm, tn = 1024, 512
    off = col_off // tn
    in_specs = [pl.BlockSpec((tm, D), lambda i, j: (i, 0)),
                pl.BlockSpec((D, tn), lambda i, j: (0, j + off))]
    args = [h, w_in]
    kern = _proj_kernel
    if bias is not None:
        in_specs.append(pl.BlockSpec((1, tn), lambda i, j: (0, j)))
        args.append(bias)
        kern = _proj_gate_kernel
    return pl.pallas_call(
        kern,
        out_shape=jax.ShapeDtypeStruct((T, n_cols), F32),
        grid=(T // tm, n_cols // tn),
        in_specs=in_specs,
        out_specs=pl.BlockSpec((tm, tn), lambda i, j: (i, j)),
        compiler_params=_cparams(("arbitrary", "arbitrary")),
        name=name,
    )(*args)


def _seq_index_maps(d):
    nseg = R // d
    qlen = ATTN_BLOCK // nseg
    return nseg, qlen


def _bias_matrices(d, hp):
    nseg, qlen = _seq_index_maps(d)
    klen = 2 * qlen
    row = lax.broadcasted_iota(jnp.int32, (2 * ATTN_BLOCK, 2 * ATTN_BLOCK), 0)
    col = lax.broadcasted_iota(jnp.int32, (2 * ATTN_BLOCK, 2 * ATTN_BLOCK), 1)
    rho = row % ATTN_BLOCK
    jq = (rho % qlen) * nseg + rho // qlen
    jk = ((col % klen) - qlen) * nseg + col // klen
    steps = jq - jk
    valid = (steps >= 0) & (steps <= ATTN_BLOCK)
    head = 2 * hp + row // ATTN_BLOCK
    slope = lax.bitcast_convert_type((127 - (head + 1)) << 23, F32)
    bias = -slope * (d * steps).astype(F32)
    neg = jnp.float32(-jnp.inf)
    return jnp.where(valid, bias, neg), jnp.where(valid & (jk >= 0), bias, neg)


def _attend_pair(q, k, v, bias):
    lane = lax.broadcasted_iota(jnp.int32, (ATTN_BLOCK, LANES), 1)
    first = lane < HEAD_DIM
    zero = jnp.zeros_like(q)
    q2 = jnp.concatenate([jnp.where(first, q, zero), jnp.where(first, zero, q)], axis=0).astype(BF16)
    s = lax.dot_general(q2, k.astype(BF16), (((1,), (1,)), ((), ())), preferred_element_type=F32)
    s = s + bias
    m = jnp.max(s, axis=-1, keepdims=True)
    p = jnp.exp(s - m)
    l = jnp.sum(p, axis=-1, keepdims=True)
    o2 = jnp.dot(p.astype(BF16), v.astype(BF16), preferred_element_type=F32)
    o2 = o2 / l
    lse = m + jnp.log(l)
    o = jnp.where(first, o2[:ATTN_BLOCK], o2[ATTN_BLOCK:])
    lse_b = jnp.where(first, lse[:ATTN_BLOCK], lse[ATTN_BLOCK:])
    return o, lse_b


def _attn_kernel(q_ref, kp_ref, kc_ref, vp_ref, vc_ref, o_ref, kbuf, vbuf, obuf, lbuf, bias_ref):
    hp = pl.program_id(0)
    it = pl.program_id(1)
    g = pl.program_id(2)
    scale = HEAD_DIM ** -0.5

    kbuf[:, :ATTN_BLOCK, :] = kp_ref[...]
    kbuf[:, ATTN_BLOCK:, :] = kc_ref[...]
    vbuf[:, :ATTN_BLOCK, :] = vp_ref[...]
    vbuf[:, ATTN_BLOCK:, :] = vc_ref[...]

    for gi, d in enumerate(DILATIONS):
        nseg, qlen = _seq_index_maps(d)
        klen = 2 * qlen
        nblk = ATTN_BLOCK // qlen

        @pl.when(g == gi)
        def _(gi=gi, d=d, nseg=nseg, qlen=qlen, klen=klen, nblk=nblk):
            b_reg, b_first = _bias_matrices(d, hp)
            bias_ref[0] = b_reg
            bias_ref[1] = b_first

            def block(idx, carry):
                rd = idx // nblk
                bb = idx % nblk
                q0 = pl.multiple_of(bb * qlen, qlen)
                k0 = pl.multiple_of(ATTN_BLOCK + bb * qlen - qlen, qlen)
                qs, ks, vs = [], [], []
                for m_ in range(nseg):
                    rr = rd + d * m_
                    qs.append(q_ref[rr, pl.ds(q0, qlen), :])
                    ks.append(kbuf[rr, pl.ds(k0, klen), :])
                    vs.append(vbuf[rr, pl.ds(k0, klen), :])
                q = jnp.concatenate(qs, axis=0) * scale
                k = jnp.concatenate(ks, axis=0)
                v = jnp.concatenate(vs, axis=0)
                is_first = jnp.logical_and(it == 0, bb == 0)
                bias = bias_ref[jnp.where(is_first, 1, 0)]
                o, lse = _attend_pair(q, k, v, bias)
                for m_ in range(nseg):
                    rr = rd + d * m_
                    obuf[gi, rr, pl.ds(q0, qlen), :] = o[m_ * qlen:(m_ + 1) * qlen]
                    lbuf[gi, rr, pl.ds(q0, qlen), :] = lse[m_ * qlen:(m_ + 1) * qlen]
                return carry

            lax.fori_loop(0, d * nblk, block, 0, unroll=4)

    @pl.when(g == len(DILATIONS) - 1)
    def _():
        for r in range(R):
            l0, l1, l2 = lbuf[0, r], lbuf[1, r], lbuf[2, r]
            mx = jnp.maximum(jnp.maximum(l0, l1), l2)
            e0, e1, e2 = jnp.exp(l0 - mx), jnp.exp(l1 - mx), jnp.exp(l2 - mx)
            den = e0 + e1 + e2
            num = e0 * obuf[0, r] + e1 * obuf[1, r] + e2 * obuf[2, r]
            o_ref[r] = (num / den).astype(BF16)


def attention(qkv3):
    n_hp = N_HEAD_SLOTS // 2
    n_it = NI // ATTN_BLOCK
    ng = len(DILATIONS)
    cb = QKV_COLS // LANES

    def cur(base):
        return pl.BlockSpec((R, ATTN_BLOCK, LANES), lambda hp, it, g: (0, it, base + g * n_hp + hp))

    def prev(base):
        return pl.BlockSpec((R, ATTN_BLOCK, LANES),
                            lambda hp, it, g: (0, jnp.maximum(it - 1, 0), base + g * n_hp + hp))

    return pl.pallas_call(
        _attn_kernel,
        out_shape=jax.ShapeDtypeStruct((R, NI, ATTN_OUT), BF16),
        grid=(n_hp, n_it, ng),
        in_specs=[cur(0), prev(cb), cur(cb), prev(2 * cb), cur(2 * cb)],
        out_specs=pl.BlockSpec((R, ATTN_BLOCK, LANES), lambda hp, it, g: (0, it, hp)),
        scratch_shapes=[pltpu.VMEM((R, 2 * ATTN_BLOCK, LANES), F32),
                        pltpu.VMEM((R, 2 * ATTN_BLOCK, LANES), F32),
                        pltpu.VMEM((ng, R, ATTN_BLOCK, LANES), F32),
                        pltpu.VMEM((ng, R, ATTN_BLOCK, LANES), F32),
                        pltpu.VMEM((2, 2 * ATTN_BLOCK, 2 * ATTN_BLOCK), F32)],
        compiler_params=_cparams(("arbitrary", "arbitrary", "arbitrary")),
        name="dilated_attention",
    )(qkv3, qkv3, qkv3, qkv3, qkv3)


SSM_SLAB = 256
SSM_SLABS = SSM_W // SSM_SLAB
SLAB_STATES = SSM_SLAB // SSM_CH * SSM_STATE
SSM_TI = 128
SSM_MM_CHUNK = 4


def _ssm_kernel(u_ref, wb_ref, wc_ref, a_ref, pw_ref, dsk_ref, o_ref, s_ref, zs_ref, zc_ref):
    ic = pl.program_id(1)
    ns = SLAB_STATES

    @pl.when(ic == 0)
    def _():
        zc_ref[...] = jnp.zeros_like(zc_ref)

    wb = wb_ref[0]
    for c in range(R // SSM_MM_CHUNK):
        uc = u_ref[c * SSM_MM_CHUNK:(c + 1) * SSM_MM_CHUNK].reshape(SSM_MM_CHUNK * SSM_TI, SSM_SLAB)
        bu = jnp.dot(uc.astype(BF16), wb, preferred_element_type=F32)
        s_ref[c * SSM_MM_CHUNK:(c + 1) * SSM_MM_CHUNK] = bu.reshape(SSM_MM_CHUNK, SSM_TI, 2 * ns)

    ar = a_ref[0, :, :ns]
    ai = a_ref[0, :, ns:]

    def local_tile(t, carry):
        rows = pl.ds(pl.multiple_of(t * 8, 8), 8)
        pr = s_ref[0, rows, :ns]
        pi = s_ref[0, rows, ns:]
        for r in range(1, R):
            nr = s_ref[r, rows, :ns] + (ar * pr - ai * pi)
            ni = s_ref[r, rows, ns:] + (ar * pi + ai * pr)
            s_ref[r, rows, :ns] = nr
            s_ref[r, rows, ns:] = ni
            pr, pi = nr, ni
        return carry

    lax.fori_loop(0, SSM_TI // 8, local_tile, 0)

    a16r = pw_ref[0, R - 1:R, :ns]
    a16i = pw_ref[0, R - 1:R, ns:]

    def zstep(i, z):
        zs_ref[pl.ds(i, 1), :] = z
        e = s_ref[R - 1, pl.ds(i, 1), :]
        zr, zi = z[:, :ns], z[:, ns:]
        nz = jnp.concatenate([a16r * zr - a16i * zi, a16r * zi + a16i * zr], axis=-1)
        return nz + e

    zc_ref[...] = lax.fori_loop(0, SSM_TI, zstep, zc_ref[...])

    def fix_tile(t, carry):
        rows = pl.ds(pl.multiple_of(t * 8, 8), 8)
        zr = zs_ref[rows, :ns]
        zi = zs_ref[rows, ns:]
        for r in range(R):
            pr = pw_ref[0, r:r + 1, :ns]
            pi = pw_ref[0, r:r + 1, ns:]
            s_ref[r, rows, :ns] = s_ref[r, rows, :ns] + (pr * zr - pi * zi)
            s_ref[r, rows, ns:] = s_ref[r, rows, ns:] + (pr * zi + pi * zr)
        return carry

    lax.fori_loop(0, SSM_TI // 8, fix_tile, 0)

    wc = wc_ref[0]
    dsk = dsk_ref[...]
    for c in range(R // SSM_MM_CHUNK):
        xs = s_ref[c * SSM_MM_CHUNK:(c + 1) * SSM_MM_CHUNK].reshape(SSM_MM_CHUNK * SSM_TI, 2 * ns)
        y = jnp.dot(xs.astype(BF16), wc, preferred_element_type=F32)
        y = y.reshape(SSM_MM_CHUNK, SSM_TI, SSM_SLAB) + dsk * u_ref[c * SSM_MM_CHUNK:(c + 1) * SSM_MM_CHUNK]
        o_ref[c * SSM_MM_CHUNK:(c + 1) * SSM_MM_CHUNK] = jax.nn.gelu(y)


def ssm_scan(u3, wb, wc, avec, pows, dskip):
    ns2 = 2 * SLAB_STATES
    return pl.pallas_call(
        _ssm_kernel,
        out_shape=jax.ShapeDtypeStruct((R, NI, SSM_W), F32),
        grid=(SSM_SLABS, NI // SSM_TI),
        in_specs=[pl.BlockSpec((R, SSM_TI, SSM_SLAB), lambda kb, ic: (0, ic, kb)),
                  pl.BlockSpec((1, SSM_SLAB, ns2), lambda kb, ic: (kb, 0, 0)),
                  pl.BlockSpec((1, ns2, SSM_SLAB), lambda kb, ic: (kb, 0, 0)),
                  pl.BlockSpec((1, 1, ns2), lambda kb, ic: (kb, 0, 0)),
                  pl.BlockSpec((1, R, ns2), lambda kb, ic: (kb, 0, 0)),
                  pl.BlockSpec((1, SSM_SLAB), lambda kb, ic: (0, kb))],
        out_specs=pl.BlockSpec((R, SSM_TI, SSM_SLAB), lambda kb, ic: (0, ic, kb)),
        scratch_shapes=[pltpu.VMEM((R, SSM_TI, ns2), F32),
                        pltpu.VMEM((SSM_TI, ns2), F32),
                        pltpu.VMEM((1, ns2), F32)],
        compiler_params=_cparams(("arbitrary", "arbitrary")),
        name="s5_ssm",
    )(u3, wb, wc, avec, pows, dskip)


def ssm_params(a_re, a_im, log_step, b_re, b_im, c_re, c_im):
    G, P, H = SSM_W // SSM_CH, SSM_STATE, SSM_CH
    gs = SSM_SLAB // SSM_CH
    step = jnp.exp(log_step.astype(F32))[:, None]
    mag = jnp.exp(a_re * step)
    ang = a_im * step
    abar_re, abar_im = mag * jnp.cos(ang), mag * jnp.sin(ang)
    nr, ni = abar_re - 1.0, abar_im
    den = a_re * a_re + a_im * a_im
    f_re = (nr * a_re + ni * a_im) / den
    f_im = (ni * a_re - nr * a_im) / den
    bbar_re = f_re[..., None] * b_re - f_im[..., None] * b_im
    bbar_im = f_re[..., None] * b_im + f_im[..., None] * b_re
    eye = jnp.eye(gs, dtype=F32)

    def in_mat(b):
        b4 = b.reshape(SSM_SLABS, gs, P, H)
        m = jnp.einsum('kgph,gj->kghjp', b4, eye)
        return m.reshape(SSM_SLABS, gs * H, gs * P)

    def out_mat(c):
        c4 = c.reshape(SSM_SLABS, gs, H, P)
        m = jnp.einsum('kghp,gj->kgpjh', c4, eye)
        return m.reshape(SSM_SLABS, gs * P, gs * H)

    wb = jnp.concatenate([in_mat(bbar_re), in_mat(bbar_im)], axis=-1).astype(BF16)
    wc = jnp.concatenate([out_mat(c_re.astype(F32)), -out_mat(c_im.astype(F32))], axis=1).astype(BF16)

    def slab_vec(v):
        return v.reshape(SSM_SLABS, gs * P)

    avec = jnp.concatenate([slab_vec(abar_re), slab_vec(abar_im)], axis=-1)[:, None, :]
    pr, pi = abar_re, abar_im
    prs, pis = [pr], [pi]
    for _ in range(R - 1):
        pr, pi = pr * abar_re - pi * abar_im, pr * abar_im + pi * abar_re
        prs.append(pr)
        pis.append(pi)
    pows = jnp.concatenate([jnp.stack([slab_vec(p) for p in prs], axis=1),
                            jnp.stack([slab_vec(p) for p in pis], axis=1)], axis=-1)
    return wb, wc, avec, pows


def _glu_kernel(a_ref, w_ref, b_ref, yb_ref, o_ref, abf_ref):
    @pl.when(pl.program_id(1) == 0)
    def _():
        abf_ref[...] = a_ref[...].astype(BF16)

    acc = jnp.dot(abf_ref[...], w_ref[...].astype(BF16), preferred_element_type=F32)
    o_ref[...] = (yb_ref[...] * jax.nn.sigmoid(acc + b_ref[...])).astype(BF16)


def glu(yg, w_glu, b_glu):
    tm, tn = 1024, 512
    return pl.pallas_call(
        _glu_kernel,
        out_shape=jax.ShapeDtypeStruct((T, SSM_W), BF16),
        grid=(T // tm, SSM_W // tn),
        in_specs=[pl.BlockSpec((tm, SSM_W), lambda i, j: (i, 0)),
                  pl.BlockSpec((SSM_W, tn), lambda i, j: (0, j)),
                  pl.BlockSpec((1, tn), lambda i, j: (0, j)),
                  pl.BlockSpec((tm, tn), lambda i, j: (i, j))],
        out_specs=pl.BlockSpec((tm, tn), lambda i, j: (i, j)),
        scratch_shapes=[pltpu.VMEM((tm, SSM_W), BF16)],
        compiler_params=_cparams(("arbitrary", "arbitrary")),
        name="ssm_glu",
    )(yg, w_glu, b_glu, yg)


def _merge_kernel(at_ref, ss_ref, wa_ref, ws_ref, ga_ref, gs_ref, o_ref):
    a = jnp.dot(at_ref[...], wa_ref[...].astype(BF16), preferred_element_type=F32)
    s = jnp.dot(ss_ref[...], ws_ref[...].astype(BF16), preferred_element_type=F32)
    o_ref[...] = (ga_ref[...] * a + gs_ref[...] * s).astype(BF16)


def merge(attn, ssm, w_up_attn, w_up_ssm, gates):
    tm, tn = 1024, 512
    nj = D // tn
    return pl.pallas_call(
        _merge_kernel,
        out_shape=jax.ShapeDtypeStruct((T, D), BF16),
        grid=(T // tm, nj),
        in_specs=[pl.BlockSpec((tm, ATTN_OUT), lambda i, j: (i, 0)),
                  pl.BlockSpec((tm, SSM_W), lambda i, j: (i, 0)),
                  pl.BlockSpec((ATTN_OUT, tn), lambda i, j: (0, j)),
                  pl.BlockSpec((SSM_W, tn), lambda i, j: (0, j)),
                  pl.BlockSpec((tm, tn), lambda i, j: (i, j)),
                  pl.BlockSpec((tm, tn), lambda i, j: (i, j + nj))],
        out_specs=pl.BlockSpec((tm, tn), lambda i, j: (i, j)),
        compiler_params=_cparams(("arbitrary", "arbitrary")),
        name="branch_merge",
    )(attn, ssm, w_up_attn, w_up_ssm, gates, gates)


def _outproj_kernel(m_ref, w_ref, x_ref, o_ref, slab):
    tb = m_ref.shape[1]
    tn = w_ref.shape[1]
    a = m_ref[...].reshape(R * tb, D)
    acc = jnp.dot(a, w_ref[...].astype(BF16), preferred_element_type=F32)
    for s_ in range(tn // LANES):
        for r in range(R):
            slab[s_, pl.ds(r, tb, stride=R), :] = acc[r * tb:(r + 1) * tb, s_ * LANES:(s_ + 1) * LANES]
    for s_ in range(tn // LANES):
        o_ref[:, s_ * LANES:(s_ + 1) * LANES] = slab[s_] + x_ref[:, s_ * LANES:(s_ + 1) * LANES]


def out_proj(merged3, w_out, x2):
    tb, tn = 64, 512
    return pl.pallas_call(
        _outproj_kernel,
        out_shape=jax.ShapeDtypeStruct((T, D), F32),
        grid=(NI // tb, D // tn),
        in_specs=[pl.BlockSpec((R, tb, D), lambda i, j: (0, i, 0)),
                  pl.BlockSpec((D, tn), lambda i, j: (0, j)),
                  pl.BlockSpec((tb * R, tn), lambda i, j: (i, j))],
        out_specs=pl.BlockSpec((tb * R, tn), lambda i, j: (i, j)),
        scratch_shapes=[pltpu.VMEM((tn // LANES, tb * R, LANES), F32)],
        compiler_params=_cparams(("arbitrary", "arbitrary")),
        name="out_proj_residual",
    )(merged3, w_out, x2)


PACK_ROWS = 8
HALF_D = D // 2


def _pack_bf16_pairs(z):
    zb = lax.bitcast_convert_type(z, jnp.uint32)
    rnd = zb + jnp.uint32(0x7FFF) + ((zb >> 16) & jnp.uint32(1))
    top = rnd & jnp.uint32(0xFFFF0000)
    return top[:, HALF_D:] | (top[:, :HALF_D] >> 16)


def _unpack_bf16_pairs(x_ref, n):
    lo, hi = [], []
    for c in range(PACK_ROWS):
        w = x_ref[pl.ds(c, n, stride=PACK_ROWS), :]
        lo.append(lax.bitcast_convert_type(w << 16, F32))
        hi.append(lax.bitcast_convert_type(w & jnp.uint32(0xFFFF0000), F32))
    return jnp.concatenate(lo + hi, axis=1).astype(BF16)


def _router_kernel(x_ref, g_ref, w_ref, b_ref, h_ref, id_ref, wt_ref, cnt_ref, carry_ref):
    step = pl.program_id(0)

    @pl.when(step == 0)
    def _():
        carry_ref[...] = jnp.zeros_like(carry_ref)

    x = x_ref[...]
    ms = jnp.mean(x * x, axis=-1, keepdims=True)
    z = x * lax.rsqrt(ms + NORM_EPS) * g_ref[...]
    word = _pack_bf16_pairs(z)
    for c in range(PACK_ROWS):
        h_ref[pl.ds(c, x.shape[0], stride=PACK_ROWS), :] = word[:, c * LANES:(c + 1) * LANES]
    zh = z.astype(BF16)
    zl = (z - zh.astype(F32)).astype(BF16)
    w = w_ref[...]
    wh = w.astype(BF16)
    wl = (w - wh.astype(F32)).astype(BF16)
    logits = (jnp.dot(zh, wh, preferred_element_type=F32) + jnp.dot(zl, wh, preferred_element_type=F32)
              + jnp.dot(zh, wl, preferred_element_type=F32)) + b_ref[...]
    lane = lax.broadcasted_iota(jnp.int32, logits.shape, 1)
    lanef = lane.astype(F32)
    neg = jnp.float32(-jnp.inf)
    big = jnp.float32(1e9)
    gl = jnp.where(lane < N_EGROUPS, logits, neg)
    gmax = jnp.max(gl, axis=-1, keepdims=True)
    gidx = jnp.min(jnp.where(gl == gmax, lanef, big), axis=-1, keepdims=True)
    pg = 1.0 / jnp.sum(jnp.exp(gl - gmax), axis=-1, keepdims=True)
    lo = N_EGROUPS + EXPERTS_PER_GROUP * gidx
    el = jnp.where((lanef >= lo) & (lanef < lo + EXPERTS_PER_GROUP), logits, neg)
    t1 = jnp.max(el, axis=-1, keepdims=True)
    j1 = jnp.min(jnp.where(el == t1, lanef, big), axis=-1, keepdims=True)
    el2 = jnp.where(lanef == j1, neg, el)
    t2 = jnp.max(el2, axis=-1, keepdims=True)
    j2 = jnp.min(jnp.where(el2 == t2, lanef, big), axis=-1, keepdims=True)
    e21 = jnp.exp(t2 - t1)
    w1 = pg / (1.0 + e21)
    w2 = pg * e21 / (1.0 + e21)
    e1f = j1 - N_EGROUPS
    e2f = j2 - N_EGROUPS

    tm = x.shape[0]
    oh1 = (lanef == e1f).astype(F32)
    oh2 = (lanef == e2f).astype(F32)
    ri = lax.broadcasted_iota(jnp.int32, (tm, tm), 0)
    ci = lax.broadcasted_iota(jnp.int32, (tm, tm), 1)
    before = (ci < ri).astype(BF16)
    p1 = jnp.dot(before, oh1.astype(BF16), preferred_element_type=F32)
    p2 = jnp.dot(before, oh2.astype(BF16), preferred_element_type=F32)
    carry = carry_ref[...]
    c1 = jnp.sum(oh1, axis=0, keepdims=True)
    c2 = jnp.sum(oh2, axis=0, keepdims=True)
    rank1 = jnp.sum(oh1 * (carry + p1), axis=-1, keepdims=True)
    rank2 = jnp.sum(oh2 * (carry + c1 + p2), axis=-1, keepdims=True)
    carry = carry + c1 + c2
    carry_ref[...] = carry
    cnt_ref[...] = jnp.broadcast_to(carry, cnt_ref.shape).astype(jnp.int32)

    ids = jnp.where(lane == 0, e1f, jnp.where(lane == 1, e2f, jnp.where(lane == 2, rank1, jnp.where(lane == 3, rank2, 0.0))))
    id_ref[...] = ids.astype(jnp.int32)
    wt_ref[...] = jnp.where(lane == 0, w1, jnp.where(lane == 1, w2, 0.0))


def router(x1, gain, w_r, b_r):
    tm = 256
    return pl.pallas_call(
        _router_kernel,
        out_shape=(jax.ShapeDtypeStruct((T * PACK_ROWS, LANES), jnp.uint32),
                   jax.ShapeDtypeStruct((T, LANES), jnp.int32),
                   jax.ShapeDtypeStruct((T, LANES), F32),
                   jax.ShapeDtypeStruct((8, LANES), jnp.int32)),
        grid=(T // tm,),
        in_specs=[pl.BlockSpec((tm, D), lambda i: (i, 0)),
                  pl.BlockSpec((1, D), lambda i: (0, 0)),
                  pl.BlockSpec((D, LANES), lambda i: (0, 0)),
                  pl.BlockSpec((1, LANES), lambda i: (0, 0))],
        out_specs=(pl.BlockSpec((tm * PACK_ROWS, LANES), lambda i: (i, 0)),
                   pl.BlockSpec((tm, LANES), lambda i: (i, 0)),
                   pl.BlockSpec((tm, LANES), lambda i: (i, 0)),
                   pl.BlockSpec((8, LANES), lambda i: (0, 0))),
        scratch_shapes=[pltpu.VMEM((1, LANES), F32)],
        compiler_params=_cparams(("arbitrary",)),
        name="ffn_norm_router",
    )(x1, gain, w_r, b_r)


DISPATCH_TB = 256
N_ZERO_FILLS = 2 * N_EXPERTS
YS_ROWS = D // LANES


def _dispatch_kernel(dest_ref, zs_ref, h_ref, xs_hbm, zbuf, zsem, sem):
    step = pl.program_id(0)

    @pl.when(step == 0)
    def _():
        zbuf[...] = jnp.zeros_like(zbuf)

        def zero_copy(e):
            start = pl.multiple_of(jnp.maximum(zs_ref[e], 0) * PACK_ROWS, PACK_ROWS)
            return pltpu.make_async_copy(zbuf, xs_hbm.at[pl.ds(start, MOE_BLOCK * PACK_ROWS)], zsem.at[0])

        def zstart(e, c):
            @pl.when(zs_ref[e] >= 0)
            def _():
                zero_copy(e).start()
            return c

        def zwait(e, c):
            @pl.when(zs_ref[e] >= 0)
            def _():
                zero_copy(e).wait()
            return c

        lax.fori_loop(0, N_ZERO_FILLS, zstart, 0)
        lax.fori_loop(0, N_ZERO_FILLS, zwait, 0)

    def row_copy(n, k):
        a = (step * DISPATCH_TB + n) * TOP_K + k
        src = h_ref.at[pl.ds(pl.multiple_of(n * PACK_ROWS, PACK_ROWS), PACK_ROWS)]
        dst = xs_hbm.at[pl.ds(pl.multiple_of(dest_ref[a] * PACK_ROWS, PACK_ROWS), PACK_ROWS)]
        return pltpu.make_async_copy(src, dst, sem.at[0])

    def issue(n, c):
        for k in range(TOP_K):
            row_copy(n, k).start()
        return c

    def drain(n, c):
        for k in range(TOP_K):
            row_copy(n, k).wait()
        return c

    lax.fori_loop(0, DISPATCH_TB, issue, 0, unroll=8)
    lax.fori_loop(0, DISPATCH_TB, drain, 0, unroll=8)


def dispatch(dest, zero_start, hpk):
    grid_spec = pltpu.PrefetchScalarGridSpec(
        num_scalar_prefetch=2,
        grid=(T // DISPATCH_TB,),
        in_specs=[pl.BlockSpec((DISPATCH_TB * PACK_ROWS, LANES), lambda i, dst, zs: (i, 0))],
        out_specs=pl.BlockSpec(memory_space=pl.ANY),
        scratch_shapes=[pltpu.VMEM((MOE_BLOCK * PACK_ROWS, LANES), jnp.uint32),
                        pltpu.SemaphoreType.DMA((1,)),
                        pltpu.SemaphoreType.DMA((1,))],
    )
    return pl.pallas_call(
        _dispatch_kernel,
        out_shape=jax.ShapeDtypeStruct((MOE_ROWS * PACK_ROWS, LANES), jnp.uint32),
        grid_spec=grid_spec,
        compiler_params=_cparams(("arbitrary",)),
        name="moe_dispatch",
    )(dest, zero_start, hpk)


def _expert_kernel(be_ref, nu_ref, x_ref, wg_ref, wu_ref, wd_ref, y_ref, wg_bf, wu_bf, wd_bf):
    b = pl.program_id(0)

    @pl.when(b < nu_ref[0])
    def _():
        changed = jnp.logical_or(b == 0, be_ref[b] != be_ref[jnp.maximum(b - 1, 0)])

        @pl.when(changed)
        def _():
            wg_bf[...] = wg_ref[...].astype(BF16)
            wu_bf[...] = wu_ref[...].astype(BF16)
            wd_bf[...] = wd_ref[...].astype(BF16)

        x = _unpack_bf16_pairs(x_ref, MOE_BLOCK)
        gate = jnp.dot(x, wg_bf[...], preferred_element_type=F32)
        up = jnp.dot(x, wu_bf[...], preferred_element_type=F32)
        hid = (jax.nn.silu(gate) * up).astype(BF16)
        y = jnp.dot(hid, wd_bf[...], preferred_element_type=F32)
        for c in range(YS_ROWS):
            y_ref[pl.ds(c, MOE_BLOCK, stride=YS_ROWS), :] = y[:, c * LANES:(c + 1) * LANES]

    @pl.when(b >= nu_ref[0])
    def _():
        y_ref[...] = jnp.zeros_like(y_ref)


def experts(block_expert, n_used, xs, w_gate, w_up, w_down):
    def blk(b, be, nu):
        return jnp.minimum(b, nu[0] - 1)

    grid_spec = pltpu.PrefetchScalarGridSpec(
        num_scalar_prefetch=2,
        grid=(MOE_BLOCKS,),
        in_specs=[pl.BlockSpec((MOE_BLOCK * PACK_ROWS, LANES), lambda b, be, nu: (blk(b, be, nu), 0)),
                  pl.BlockSpec((None, D, EXPERT_FF), lambda b, be, nu: (be[b], 0, 0)),
                  pl.BlockSpec((None, D, EXPERT_FF), lambda b, be, nu: (be[b], 0, 0)),
                  pl.BlockSpec((None, EXPERT_FF, D), lambda b, be, nu: (be[b], 0, 0))],
        out_specs=pl.BlockSpec((MOE_BLOCK * YS_ROWS, LANES), lambda b, be, nu: (b, 0)),
        scratch_shapes=[pltpu.VMEM((D, EXPERT_FF), BF16),
                        pltpu.VMEM((D, EXPERT_FF), BF16),
                        pltpu.VMEM((EXPERT_FF, D), BF16)],
    )
    return pl.pallas_call(
        _expert_kernel,
        out_shape=jax.ShapeDtypeStruct((MOE_ROWS * YS_ROWS, LANES), F32),
        grid_spec=grid_spec,
        compiler_params=_cparams(("arbitrary",)),
        name="moe_experts",
    )(block_expert, n_used, xs, w_gate, w_up, w_down)


COMBINE_ROWS = 256
COMBINE_SUB = 16


def _combine_kernel(dest_ref, ys_hbm, x_ref, wt_ref, g_ref, o_ref, ybuf, sem):
    s = pl.program_id(0)
    ns = pl.num_programs(0)
    slot = s % 2
    tb = COMBINE_SUB
    rows = COMBINE_ROWS

    def row_copy(step, n, k, sl):
        tok = step * rows + n
        src = ys_hbm.at[pl.ds(pl.multiple_of(dest_ref[tok * TOP_K + k] * YS_ROWS, YS_ROWS), YS_ROWS)]
        dst = ybuf.at[sl * TOP_K + k, pl.ds(pl.multiple_of(n * YS_ROWS, YS_ROWS), YS_ROWS)]
        return pltpu.make_async_copy(src, dst, sem.at[sl])

    def issue(step, sl):
        def body(n, c):
            for k in range(TOP_K):
                row_copy(step, n, k, sl).start()
            return c
        lax.fori_loop(0, rows, body, 0, unroll=8)

    @pl.when(s == 0)
    def _():
        issue(0, 0)

    @pl.when(s + 1 < ns)
    def _():
        issue(s + 1, 1 - slot)

    def wbody(n, c):
        for k in range(TOP_K):
            row_copy(s, n, k, slot).wait()
        return c
    lax.fori_loop(0, rows, wbody, 0, unroll=8)

    g = g_ref[...]
    y0_ref = ybuf.at[slot * TOP_K]
    y1_ref = ybuf.at[slot * TOP_K + 1]
    for b in range(rows // tb):
        sub = slice(b * tb, (b + 1) * tb)
        w = wt_ref[sub, :]
        y0 = jnp.concatenate([y0_ref[pl.ds(b * tb * YS_ROWS + c, tb, stride=YS_ROWS), :] for c in range(YS_ROWS)], axis=1)
        y1 = jnp.concatenate([y1_ref[pl.ds(b * tb * YS_ROWS + c, tb, stride=YS_ROWS), :] for c in range(YS_ROWS)], axis=1)
        z = x_ref[sub, :] + (w[:, 0:1] * y0 + w[:, 1:2] * y1)
        ms = jnp.mean(z * z, axis=-1, keepdims=True)
        o_ref[sub, :] = z * lax.rsqrt(ms + NORM_EPS) * g


def combine(dest, ys, x1, wts, gain):
    rows = COMBINE_ROWS
    grid_spec = pltpu.PrefetchScalarGridSpec(
        num_scalar_prefetch=1,
        grid=(T // rows,),
        in_specs=[pl.BlockSpec(memory_space=pl.ANY),
                  pl.BlockSpec((rows, D), lambda s, dst: (s, 0)),
                  pl.BlockSpec((rows, LANES), lambda s, dst: (s, 0)),
                  pl.BlockSpec((1, D), lambda s, dst: (0, 0))],
        out_specs=pl.BlockSpec((rows, D), lambda s, dst: (s, 0)),
        scratch_shapes=[pltpu.VMEM((2 * TOP_K, rows * YS_ROWS, LANES), F32),
                        pltpu.SemaphoreType.DMA((2,))],
    )
    return pl.pallas_call(
        _combine_kernel,
        out_shape=jax.ShapeDtypeStruct((T, D), F32),
        grid_spec=grid_spec,
        compiler_params=_cparams(("arbitrary",)),
        name="moe_combine_final_norm",
    )(dest, ys, x1, wts, gain)


def dispatch_plan(ids, counts):
    experts_ = jnp.arange(N_EXPERTS, dtype=jnp.int32)
    padded = (counts + MOE_BLOCK - 1) // MOE_BLOCK * MOE_BLOCK
    pad_end = jnp.cumsum(padded)
    pad_start = pad_end - padded
    e = ids[:, :TOP_K]
    start_of = jnp.sum(jnp.where(e[:, :, None] == experts_[None, None, :], pad_start[None, None, :], 0), axis=-1)
    dest = (start_of + ids[:, TOP_K:2 * TOP_K]).reshape(N_ASSIGN).astype(jnp.int32)
    n_used = pad_end[-1] // MOE_BLOCK
    block_start = jnp.minimum(jnp.arange(MOE_BLOCKS, dtype=jnp.int32), n_used - 1) * MOE_BLOCK
    block_expert = jnp.sum((block_start[:, None] >= pad_end[None, :]).astype(jnp.int32), axis=1)
    block_expert = jnp.minimum(block_expert, N_EXPERTS - 1).astype(jnp.int32)
    tail = n_used + experts_
    zero_start = jnp.concatenate([jnp.where(counts > 0, pad_end - MOE_BLOCK, -1),
                                  jnp.where(tail < MOE_BLOCKS, tail * MOE_BLOCK, -1)]).astype(jnp.int32)
    return block_expert, n_used.astype(jnp.int32).reshape(1), dest, zero_start


def kernel(x, norm_mix, w_in, b_gate, ssm_a_re, ssm_a_im, ssm_log_step, ssm_b_re, ssm_b_im, ssm_c_re, ssm_c_im, ssm_d, w_glu, b_glu, w_up_attn, w_up_ssm, w_out, norm_ffn, w_router_group, b_router_group, w_router_expert, b_router_expert, w_expert_gate, w_expert_up, w_expert_down, norm_final):
    x2 = x.reshape(T, D)
    h = norm_permute(x2, norm_mix.reshape(1, D)).reshape(T, D)
    w_in_l = w_in.reshape(D, IN_COLS)
    qkv = proj(h, w_in_l, 0, 3 * QKV_COLS, name="proj_qkv")
    u = proj(h, w_in_l, 3 * QKV_COLS, SSM_W, name="proj_ssm_in")
    gates = proj(h, w_in_l, 3 * QKV_COLS + SSM_W, 2 * D, bias=b_gate.reshape(1, 2 * D), name="proj_gates")

    attn = attention(qkv.reshape(R, NI, 3 * QKV_COLS)).reshape(T, ATTN_OUT)

    G = SSM_W // SSM_CH
    wb, wc, avec, pows = ssm_params(
        ssm_a_re.reshape(G, SSM_STATE).astype(F32), ssm_a_im.reshape(G, SSM_STATE).astype(F32),
        ssm_log_step.reshape(G),
        ssm_b_re.reshape(G, SSM_STATE, SSM_CH).astype(F32), ssm_b_im.reshape(G, SSM_STATE, SSM_CH).astype(F32),
        ssm_c_re.reshape(G, SSM_CH, SSM_STATE), ssm_c_im.reshape(G, SSM_CH, SSM_STATE))
    yg = ssm_scan(u.reshape(R, NI, SSM_W), wb, wc, avec, pows, ssm_d.reshape(1, SSM_W).astype(F32))
    ssm = glu(yg.reshape(T, SSM_W), w_glu.reshape(SSM_W, SSM_W), b_glu.reshape(1, SSM_W))

    merged = merge(attn, ssm, w_up_attn.reshape(ATTN_OUT, D), w_up_ssm.reshape(SSM_W, D), gates)
    x1 = out_proj(merged.reshape(R, NI, D), w_out.reshape(D, D), x2)

    w_r = jnp.concatenate([w_router_group.reshape(D, N_EGROUPS), w_router_expert.reshape(D, N_EXPERTS),
                           jnp.zeros((D, LANES - N_EGROUPS - N_EXPERTS), F32)], axis=1)
    b_r = jnp.concatenate([b_router_group.reshape(1, N_EGROUPS), b_router_expert.reshape(1, N_EXPERTS),
                           jnp.zeros((1, LANES - N_EGROUPS - N_EXPERTS), F32)], axis=1)
    hpk, ids, wts, counts = router(x1, norm_ffn.reshape(1, D), w_r, b_r)

    block_expert, n_used, dest, zero_start = dispatch_plan(ids[:, :2 * TOP_K], counts[0, :N_EXPERTS])
    xs = dispatch(dest, zero_start, hpk)
    ys = experts(block_expert, n_used, xs,
                 w_expert_gate.reshape(N_EXPERTS, D, EXPERT_FF), w_expert_up.reshape(N_EXPERTS, D, EXPERT_FF),
                 w_expert_down.reshape(N_EXPERTS, EXPERT_FF, D))
    out = combine(dest, ys, x1, wts, norm_final.reshape(1, D))
    return out.reshape(1, T, D)
```

```python
import functools
import math

import jax
import jax.numpy as jnp
from jax import lax
from jax.experimental import pallas as pl
from jax.experimental.pallas import tpu as pltpu

F32 = jnp.float32
BF16 = jnp.bfloat16

T = 8192
D = 2048
R = 16
NI = T // R
HEAD_DIM = 64
N_HEAD_SLOTS = 8
DILATIONS = (1, 4, 16)
ATTN_BLOCK = 128
QKV_COLS = 1536
ATTN_OUT = 512
SSM_W = 1024
SSM_STATE = 64
SSM_CH = 16
IN_COLS = 3 * QKV_COLS + SSM_W + 2 * D
N_EXPERTS = 32
N_EGROUPS = 4
EXPERTS_PER_GROUP = 8
TOP_K = 2
EXPERT_FF = 512
NORM_EPS = 1e-6
LANES = 128
VMEM_LIMIT = 48 * 1024 * 1024

MOE_BLOCK = 256
N_ASSIGN = T * TOP_K
MOE_BLOCKS = N_ASSIGN // MOE_BLOCK + N_EXPERTS
MOE_ROWS = MOE_BLOCKS * MOE_BLOCK


def _cparams(sem):
    return pltpu.CompilerParams(dimension_semantics=sem, vmem_limit_bytes=VMEM_LIMIT)


N_SLABS = D // LANES
NORM_TB = 32
NORM_CHUNK = 64


def _norm_permute_kernel(x_ref, g_ref, h_ref, slab):
    g = g_ref[...]

    def chunk(t, c):
        rows = pl.ds(pl.multiple_of(t * NORM_CHUNK, NORM_CHUNK), NORM_CHUNK)
        x = x_ref[rows, :]
        ms = jnp.mean(x * x, axis=-1, keepdims=True)
        hn = x * lax.rsqrt(ms + NORM_EPS) * g
        for s_ in range(N_SLABS):
            slab[s_, rows, :] = hn[:, s_ * LANES:(s_ + 1) * LANES]
        return c

    lax.fori_loop(0, NORM_TB * R // NORM_CHUNK, chunk, 0)
    for r in range(R):
        pieces = [slab[s_, pl.ds(r, NORM_TB, stride=R), :] for s_ in range(N_SLABS)]
        h_ref[r] = jnp.concatenate(pieces, axis=1).astype(BF16)


def norm_permute(x2, gain):
    return pl.pallas_call(
        _norm_permute_kernel,
        out_shape=jax.ShapeDtypeStruct((R, NI, D), BF16),
        grid=(NI // NORM_TB,),
        in_specs=[pl.BlockSpec((NORM_TB * R, D), lambda i: (i, 0)),
                  pl.BlockSpec((1, D), lambda i: (0, 0))],
        out_specs=pl.BlockSpec((R, NORM_TB, D), lambda i: (0, i, 0)),
        scratch_shapes=[pltpu.VMEM((N_SLABS, NORM_TB * R, LANES), F32)],
        compiler_params=_cparams(("arbitrary",)),
        name="norm_permute",
    )(x2, gain)


MXU_N = 256


def _cast_weight_once(w_ref, wbf_ref):
    @pl.when(pl.program_id(1) == 0)
    def _():
        wbf_ref[...] = w_ref[...].astype(BF16)


def _proj_kernel(a_ref, w_ref, o_ref, wbf_ref):
    _cast_weight_once(w_ref, wbf_ref)
    a = a_ref[...]
    for c in range(w_ref.shape[1] // MXU_N):
        cols = slice(c * MXU_N, (c + 1) * MXU_N)
        o_ref[:, cols] = jnp.dot(a, wbf_ref[:, cols], preferred_element_type=F32)


def _proj_gate_kernel(a_ref, w_ref, b_ref, o_ref, wbf_ref):
    _cast_weight_once(w_ref, wbf_ref)
    a = a_ref[...]
    for c in range(w_ref.shape[1] // MXU_N):
        cols = slice(c * MXU_N, (c + 1) * MXU_N)
        acc = jnp.dot(a, wbf_ref[:, cols], preferred_element_type=F32)
        o_ref[:, cols] = jax.nn.sigmoid(acc + b_ref[:, cols])


def proj(h, w_in, col_off, n_cols, bias=None, name="proj"):
    tm, tn = 1024, 512
    off = col_off // tn
    in_specs = [pl.BlockSpec((tm, D), lambda j, i: (i, 0)),
                pl.BlockSpec((D, tn), lambda j, i: (0, j + off))]
    args = [h, w_in]
    kern = _proj_kernel
    if bias is not None:
        in_specs.append(pl.BlockSpec((1, tn), lambda j, i: (0, j)))
        args.append(bias)
        kern = _proj_gate_kernel
    return pl.pallas_call(
        kern,
        out_shape=jax.ShapeDtypeStruct((T, n_cols), F32),
        grid=(n_cols // tn, T // tm),
        in_specs=in_specs,
        out_specs=pl.BlockSpec((tm, tn), lambda j, i: (i, j)),
        scratch_shapes=[pltpu.VMEM((D, tn), BF16)],
        compiler_params=_cparams(("arbitrary", "arbitrary")),
        name=name,
    )(*args)


def _seq_index_maps(d):
    nseg = R // d
    qlen = ATTN_BLOCK // nseg
    return nseg, qlen


def _bias_matrices(d, hp):
    nseg, qlen = _seq_index_maps(d)
    klen = 2 * qlen
    row = lax.broadcasted_iota(jnp.int32, (2 * ATTN_BLOCK, 2 * ATTN_BLOCK), 0)
    col = lax.broadcasted_iota(jnp.int32, (2 * ATTN_BLOCK, 2 * ATTN_BLOCK), 1)
    rho = row % ATTN_BLOCK
    jq = (rho % qlen) * nseg + rho // qlen
    jk = ((col % klen) - qlen) * nseg + col // klen
    steps = jq - jk
    valid = (steps >= 0) & (steps <= ATTN_BLOCK)
    head = 2 * hp + row // ATTN_BLOCK
    slope = lax.bitcast_convert_type((127 - (head + 1)) << 23, F32)
    bias = -slope * (d * steps).astype(F32)
    neg = jnp.float32(-jnp.inf)
    return jnp.where(valid, bias, neg), jnp.where(valid & (jk >= 0), bias, neg)


def _attend_pair(q, k, v, bias):
    lane = lax.broadcasted_iota(jnp.int32, (ATTN_BLOCK, LANES), 1)
    first = lane < HEAD_DIM
    zero = jnp.zeros_like(q)
    q2 = jnp.concatenate([jnp.where(first, q, zero), jnp.where(first, zero, q)], axis=0).astype(BF16)
    s = lax.dot_general(q2, k.astype(BF16), (((1,), (1,)), ((), ())), preferred_element_type=F32)
    s = s + bias
    m = jnp.max(s, axis=-1, keepdims=True)
    p = jnp.exp(s - m)
    l = jnp.sum(p, axis=-1, keepdims=True)
    o2 = jnp.dot(p.astype(BF16), v.astype(BF16), preferred_element_type=F32)
    o2 = o2 / l
    lse = m + jnp.log(l)
    o = jnp.where(first, o2[:ATTN_BLOCK], o2[ATTN_BLOCK:])
    lse_b = jnp.where(first, lse[:ATTN_BLOCK], lse[ATTN_BLOCK:])
    return o, lse_b


def _attn_kernel(q_ref, kp_ref, kc_ref, vp_ref, vc_ref, o_ref, kbuf, vbuf, obuf, lbuf, bias_ref):
    hp = pl.program_id(0)
    it = pl.program_id(1)
    g = pl.program_id(2)
    scale = HEAD_DIM ** -0.5

    kbuf[:, :ATTN_BLOCK, :] = kp_ref[...]
    kbuf[:, ATTN_BLOCK:, :] = kc_ref[...]
    vbuf[:, :ATTN_BLOCK, :] = vp_ref[...]
    vbuf[:, ATTN_BLOCK:, :] = vc_ref[...]

    for gi, d in enumerate(DILATIONS):
        nseg, qlen = _seq_index_maps(d)
        klen = 2 * qlen
        nblk = ATTN_BLOCK // qlen

        @pl.when(g == gi)
        def _(gi=gi, d=d, nseg=nseg, qlen=qlen, klen=klen, nblk=nblk):
            @pl.when(it == 0)
            def _():
                b_reg, b_first = _bias_matrices(d, hp)
                bias_ref[gi, 0] = b_reg
                bias_ref[gi, 1] = b_first

            def block(idx, carry):
                rd = idx // nblk
                bb = idx % nblk
                q0 = pl.multiple_of(bb * qlen, qlen)
                k0 = pl.multiple_of(ATTN_BLOCK + bb * qlen - qlen, qlen)
                qs, ks, vs = [], [], []
                for m_ in range(nseg):
                    rr = rd + d * m_
                    qs.append(q_ref[rr, pl.ds(q0, qlen), :])
                    ks.append(kbuf[rr, pl.ds(k0, klen), :])
                    vs.append(vbuf[rr, pl.ds(k0, klen), :])
                q = jnp.concatenate(qs, axis=0) * scale
                k = jnp.concatenate(ks, axis=0)
                v = jnp.concatenate(vs, axis=0)
                is_first = jnp.logical_and(it == 0, bb == 0)
                bias = bias_ref[gi, jnp.where(is_first, 1, 0)]
                o, lse = _attend_pair(q, k, v, bias)
                for m_ in range(nseg):
                    rr = rd + d * m_
                    obuf[gi, rr, pl.ds(q0, qlen), :] = o[m_ * qlen:(m_ + 1) * qlen]
                    lbuf[gi, rr, pl.ds(q0, qlen), :] = lse[m_ * qlen:(m_ + 1) * qlen]
                return carry

            lax.fori_loop(0, d * nblk, block, 0, unroll=8)

    @pl.when(g == len(DILATIONS) - 1)
    def _():
        for r in range(R):
            l0, l1, l2 = lbuf[0, r], lbuf[1, r], lbuf[2, r]
            mx = jnp.maximum(jnp.maximum(l0, l1), l2)
            e0, e1, e2 = jnp.exp(l0 - mx), jnp.exp(l1 - mx), jnp.exp(l2 - mx)
            den = e0 + e1 + e2
            num = e0 * obuf[0, r] + e1 * obuf[1, r] + e2 * obuf[2, r]
            o_ref[r] = (num / den).astype(BF16)


def attention(qkv3):
    n_hp = N_HEAD_SLOTS // 2
    n_it = NI // ATTN_BLOCK
    ng = len(DILATIONS)
    cb = QKV_COLS // LANES

    def cur(base):
        return pl.BlockSpec((R, ATTN_BLOCK, LANES), lambda hp, it, g: (0, it, base + g * n_hp + hp))

    def prev(base):
        return pl.BlockSpec((R, ATTN_BLOCK, LANES),
                            lambda hp, it, g: (0, jnp.maximum(it - 1, 0), base + g * n_hp + hp))

    return pl.pallas_call(
        _attn_kernel,
        out_shape=jax.ShapeDtypeStruct((R, NI, ATTN_OUT), BF16),
        grid=(n_hp, n_it, ng),
        in_specs=[cur(0), prev(cb), cur(cb), prev(2 * cb), cur(2 * cb)],
        out_specs=pl.BlockSpec((R, ATTN_BLOCK, LANES), lambda hp, it, g: (0, it, hp)),
        scratch_shapes=[pltpu.VMEM((R, 2 * ATTN_BLOCK, LANES), F32),
                        pltpu.VMEM((R, 2 * ATTN_BLOCK, LANES), F32),
                        pltpu.VMEM((ng, R, ATTN_BLOCK, LANES), F32),
                        pltpu.VMEM((ng, R, ATTN_BLOCK, LANES), F32),
                        pltpu.VMEM((ng, 2, 2 * ATTN_BLOCK, 2 * ATTN_BLOCK), F32)],
        compiler_params=_cparams(("arbitrary", "arbitrary", "arbitrary")),
        name="dilated_attention",
    )(qkv3, qkv3, qkv3, qkv3, qkv3)


SSM_SLAB = 256
SSM_SLABS = SSM_W // SSM_SLAB
SLAB_STATES = SSM_SLAB // SSM_CH * SSM_STATE
SSM_TI = 128
SSM_MM_CHUNK = 4


def _ssm_kernel(u_ref, wb_ref, wc_ref, a_ref, pw_ref, dsk_ref, o_ref, s_ref, zs_ref, zc_ref):
    ic = pl.program_id(1)
    ns = SLAB_STATES

    @pl.when(ic == 0)
    def _():
        zc_ref[...] = jnp.zeros_like(zc_ref)

    wb = wb_ref[0]
    for c in range(R // SSM_MM_CHUNK):
        uc = u_ref[c * SSM_MM_CHUNK:(c + 1) * SSM_MM_CHUNK].reshape(SSM_MM_CHUNK * SSM_TI, SSM_SLAB)
        bu = jnp.dot(uc.astype(BF16), wb, preferred_element_type=F32)
        s_ref[c * SSM_MM_CHUNK:(c + 1) * SSM_MM_CHUNK] = bu.reshape(SSM_MM_CHUNK, SSM_TI, 2 * ns)

    ar = a_ref[0, :, :ns]
    ai = a_ref[0, :, ns:]

    def local_tile(t, carry):
        rows = pl.ds(pl.multiple_of(t * 8, 8), 8)
        pr = s_ref[0, rows, :ns]
        pi = s_ref[0, rows, ns:]
        for r in range(1, R):
            nr = s_ref[r, rows, :ns] + (ar * pr - ai * pi)
            ni = s_ref[r, rows, ns:] + (ar * pi + ai * pr)
            s_ref[r, rows, :ns] = nr
            s_ref[r, rows, ns:] = ni
            pr, pi = nr, ni
        return carry

    lax.fori_loop(0, SSM_TI // 8, local_tile, 0)

    a16r = pw_ref[0, R - 1:R, :ns]
    a16i = pw_ref[0, R - 1:R, ns:]

    def zstep(i, z):
        zs_ref[pl.ds(i, 1), :] = z
        e = s_ref[R - 1, pl.ds(i, 1), :]
        zr, zi = z[:, :ns], z[:, ns:]
        nz = jnp.concatenate([a16r * zr - a16i * zi, a16r * zi + a16i * zr], axis=-1)
        return nz + e

    zc_ref[...] = lax.fori_loop(0, SSM_TI, zstep, zc_ref[...])

    def fix_tile(t, carry):
        rows = pl.ds(pl.multiple_of(t * 8, 8), 8)
        zr = zs_ref[rows, :ns]
        zi = zs_ref[rows, ns:]
        for r in range(R):
            pr = pw_ref[0, r:r + 1, :ns]
            pi = pw_ref[0, r:r + 1, ns:]
            s_ref[r, rows, :ns] = s_ref[r, rows, :ns] + (pr * zr - pi * zi)
            s_ref[r, rows, ns:] = s_ref[r, rows, ns:] + (pr * zi + pi * zr)
        return carry

    lax.fori_loop(0, SSM_TI // 8, fix_tile, 0)

    wc = wc_ref[0]
    dsk = dsk_ref[...]
    for c in range(R // SSM_MM_CHUNK):
        xs = s_ref[c * SSM_MM_CHUNK:(c + 1) * SSM_MM_CHUNK].reshape(SSM_MM_CHUNK * SSM_TI, 2 * ns)
        y = jnp.dot(xs.astype(BF16), wc, preferred_element_type=F32)
        y = y.reshape(SSM_MM_CHUNK, SSM_TI, SSM_SLAB) + dsk * u_ref[c * SSM_MM_CHUNK:(c + 1) * SSM_MM_CHUNK]
        o_ref[c * SSM_MM_CHUNK:(c + 1) * SSM_MM_CHUNK] = jax.nn.gelu(y)


def ssm_scan(u3, wb, wc, avec, pows, dskip):
    ns2 = 2 * SLAB_STATES
    return pl.pallas_call(
        _ssm_kernel,
        out_shape=jax.ShapeDtypeStruct((R, NI, SSM_W), F32),
        grid=(SSM_SLABS, NI // SSM_TI),
        in_specs=[pl.BlockSpec((R, SSM_TI, SSM_SLAB), lambda kb, ic: (0, ic, kb)),
                  pl.BlockSpec((1, SSM_SLAB, ns2), lambda kb, ic: (kb, 0, 0)),
                  pl.BlockSpec((1, ns2, SSM_SLAB), lambda kb, ic: (kb, 0, 0)),
                  pl.BlockSpec((1, 1, ns2), lambda kb, ic: (kb, 0, 0)),
                  pl.BlockSpec((1, R, ns2), lambda kb, ic: (kb, 0, 0)),
                  pl.BlockSpec((1, SSM_SLAB), lambda kb, ic: (0, kb))],
        out_specs=pl.BlockSpec((R, SSM_TI, SSM_SLAB), lambda kb, ic: (0, ic, kb)),
        scratch_shapes=[pltpu.VMEM((R, SSM_TI, ns2), F32),
                        pltpu.VMEM((SSM_TI, ns2), F32),
                        pltpu.VMEM((1, ns2), F32)],
        compiler_params=_cparams(("arbitrary", "arbitrary")),
        name="s5_ssm",
    )(u3, wb, wc, avec, pows, dskip)


def ssm_params(a_re, a_im, log_step, b_re, b_im, c_re, c_im):
    G, P, H = SSM_W // SSM_CH, SSM_STATE, SSM_CH
    gs = SSM_SLAB // SSM_CH
    step = jnp.exp(log_step.astype(F32))[:, None]
    mag = jnp.exp(a_re * step)
    ang = a_im * step
    abar_re, abar_im = mag * jnp.cos(ang), mag * jnp.sin(ang)
    nr, ni = abar_re - 1.0, abar_im
    den = a_re * a_re + a_im * a_im
    f_re = (nr * a_re + ni * a_im) / den
    f_im = (ni * a_re - nr * a_im) / den
    bbar_re = f_re[..., None] * b_re - f_im[..., None] * b_im
    bbar_im = f_re[..., None] * b_im + f_im[..., None] * b_re
    eye = jnp.eye(gs, dtype=F32)

    def in_mat(b):
        b4 = b.reshape(SSM_SLABS, gs, P, H)
        m = jnp.einsum('kgph,gj->kghjp', b4, eye)
        return m.reshape(SSM_SLABS, gs * H, gs * P)

    def out_mat(c):
        c4 = c.reshape(SSM_SLABS, gs, H, P)
        m = jnp.einsum('kghp,gj->kgpjh', c4, eye)
        return m.reshape(SSM_SLABS, gs * P, gs * H)

    wb = jnp.concatenate([in_mat(bbar_re), in_mat(bbar_im)], axis=-1).astype(BF16)
    wc = jnp.concatenate([out_mat(c_re.astype(F32)), -out_mat(c_im.astype(F32))], axis=1).astype(BF16)

    def slab_vec(v):
        return v.reshape(SSM_SLABS, gs * P)

    avec = jnp.concatenate([slab_vec(abar_re), slab_vec(abar_im)], axis=-1)[:, None, :]
    pr, pi = abar_re, abar_im
    prs, pis = [pr], [pi]
    for _ in range(R - 1):
        pr, pi = pr * abar_re - pi * abar_im, pr * abar_im + pi * abar_re
        prs.append(pr)
        pis.append(pi)
    pows = jnp.concatenate([jnp.stack([slab_vec(p) for p in prs], axis=1),
                            jnp.stack([slab_vec(p) for p in pis], axis=1)], axis=-1)
    return wb, wc, avec, pows


def _glu_kernel(a_ref, w_ref, b_ref, o_ref, wbf_ref):
    @pl.when(pl.program_id(0) == 0)
    def _():
        wbf_ref[...] = w_ref[...].astype(BF16)

    a = a_ref[...].astype(BF16)
    for c in range(SSM_W // MXU_N):
        cols = slice(c * MXU_N, (c + 1) * MXU_N)
        acc = jnp.dot(a, wbf_ref[:, cols], preferred_element_type=F32)
        o_ref[:, cols] = (a_ref[:, cols] * jax.nn.sigmoid(acc + b_ref[:, cols])).astype(BF16)


def glu(yg, w_glu, b_glu):
    tm = 512
    return pl.pallas_call(
        _glu_kernel,
        out_shape=jax.ShapeDtypeStruct((T, SSM_W), BF16),
        grid=(T // tm,),
        in_specs=[pl.BlockSpec((tm, SSM_W), lambda i: (i, 0)),
                  pl.BlockSpec((SSM_W, SSM_W), lambda i: (0, 0)),
                  pl.BlockSpec((1, SSM_W), lambda i: (0, 0))],
        out_specs=pl.BlockSpec((tm, SSM_W), lambda i: (i, 0)),
        scratch_shapes=[pltpu.VMEM((SSM_W, SSM_W), BF16)],
        compiler_params=_cparams(("arbitrary",)),
        name="ssm_glu",
    )(yg, w_glu, b_glu)


def _merge_kernel(at_ref, ss_ref, wa_ref, ws_ref, ga_ref, gs_ref, o_ref, wa_bf, ws_bf):
    _cast_weight_once(wa_ref, wa_bf)
    _cast_weight_once(ws_ref, ws_bf)
    at = at_ref[...]
    ss = ss_ref[...]
    for c in range(wa_ref.shape[1] // MXU_N):
        cols = slice(c * MXU_N, (c + 1) * MXU_N)
        a = jnp.dot(at, wa_bf[:, cols], preferred_element_type=F32)
        s = jnp.dot(ss, ws_bf[:, cols], preferred_element_type=F32)
        o_ref[:, cols] = (ga_ref[:, cols] * a + gs_ref[:, cols] * s).astype(BF16)


def merge(attn, ssm, w_up_attn, w_up_ssm, gates):
    tm, tn = 1024, 512
    nj = D // tn
    return pl.pallas_call(
        _merge_kernel,
        out_shape=jax.ShapeDtypeStruct((T, D), BF16),
        grid=(nj, T // tm),
        in_specs=[pl.BlockSpec((tm, ATTN_OUT), lambda j, i: (i, 0)),
                  pl.BlockSpec((tm, SSM_W), lambda j, i: (i, 0)),
                  pl.BlockSpec((ATTN_OUT, tn), lambda j, i: (0, j)),
                  pl.BlockSpec((SSM_W, tn), lambda j, i: (0, j)),
                  pl.BlockSpec((tm, tn), lambda j, i: (i, j)),
                  pl.BlockSpec((tm, tn), lambda j, i: (i, j + nj))],
        out_specs=pl.BlockSpec((tm, tn), lambda j, i: (i, j)),
        scratch_shapes=[pltpu.VMEM((ATTN_OUT, tn), BF16), pltpu.VMEM((SSM_W, tn), BF16)],
        compiler_params=_cparams(("arbitrary", "arbitrary")),
        name="branch_merge",
    )(attn, ssm, w_up_attn, w_up_ssm, gates, gates)


def _outproj_kernel(m_ref, w_ref, x_ref, o_ref, slab, wbf_ref):
    _cast_weight_once(w_ref, wbf_ref)
    tb = m_ref.shape[1]
    a = m_ref[...].reshape(R * tb, D)
    per = MXU_N // LANES
    for c in range(w_ref.shape[1] // MXU_N):
        acc = jnp.dot(a, wbf_ref[:, c * MXU_N:(c + 1) * MXU_N], preferred_element_type=F32)
        for s_ in range(per):
            for r in range(R):
                slab[c * per + s_, pl.ds(r, tb, stride=R), :] = acc[r * tb:(r + 1) * tb, s_ * LANES:(s_ + 1) * LANES]
    for s_ in range(w_ref.shape[1] // LANES):
        o_ref[:, s_ * LANES:(s_ + 1) * LANES] = slab[s_] + x_ref[:, s_ * LANES:(s_ + 1) * LANES]


def out_proj(merged3, w_out, x2):
    tb, tn = 64, 512
    return pl.pallas_call(
        _outproj_kernel,
        out_shape=jax.ShapeDtypeStruct((T, D), F32),
        grid=(D // tn, NI // tb),
        in_specs=[pl.BlockSpec((R, tb, D), lambda j, i: (0, i, 0)),
                  pl.BlockSpec((D, tn), lambda j, i: (0, j)),
                  pl.BlockSpec((tb * R, tn), lambda j, i: (i, j))],
        out_specs=pl.BlockSpec((tb * R, tn), lambda j, i: (i, j)),
        scratch_shapes=[pltpu.VMEM((tn // LANES, tb * R, LANES), F32),
                        pltpu.VMEM((D, tn), BF16)],
        compiler_params=_cparams(("arbitrary", "arbitrary")),
        name="out_proj_residual",
    )(merged3, w_out, x2)


PACK_ROWS = 8
HALF_D = D // 2


def _pack_bf16_pairs(z):
    zb = lax.bitcast_convert_type(z, jnp.uint32)
    rnd = zb + jnp.uint32(0x7FFF) + ((zb >> 16) & jnp.uint32(1))
    top = rnd & jnp.uint32(0xFFFF0000)
    return top[:, HALF_D:] | (top[:, :HALF_D] >> 16)


def _unpack_bf16_pairs(x_ref, n):
    lo, hi = [], []
    for c in range(PACK_ROWS):
        w = x_ref[pl.ds(c, n, stride=PACK_ROWS), :]
        lo.append(lax.bitcast_convert_type(w << 16, F32))
        hi.append(lax.bitcast_convert_type(w & jnp.uint32(0xFFFF0000), F32))
    return jnp.concatenate(lo + hi, axis=1).astype(BF16)


def _router_kernel(x_ref, g_ref, w_ref, b_ref, h_ref, id_ref, wt_ref, cnt_ref, carry_ref):
    step = pl.program_id(0)

    @pl.when(step == 0)
    def _():
        carry_ref[...] = jnp.zeros_like(carry_ref)

    x = x_ref[...]
    ms = jnp.mean(x * x, axis=-1, keepdims=True)
    z = x * lax.rsqrt(ms + NORM_EPS) * g_ref[...]
    word = _pack_bf16_pairs(z)
    for c in range(PACK_ROWS):
        h_ref[pl.ds(c, x.shape[0], stride=PACK_ROWS), :] = word[:, c * LANES:(c + 1) * LANES]
    zh = z.astype(BF16)
    zl = (z - zh.astype(F32)).astype(BF16)
    w = w_ref[...]
    wh = w.astype(BF16)
    wl = (w - wh.astype(F32)).astype(BF16)
    logits = (jnp.dot(zh, wh, preferred_element_type=F32) + jnp.dot(zl, wh, preferred_element_type=F32)
              + jnp.dot(zh, wl, preferred_element_type=F32)) + b_ref[...]
    lane = lax.broadcasted_iota(jnp.int32, logits.shape, 1)
    lanef = lane.astype(F32)
    neg = jnp.float32(-jnp.inf)
    big = jnp.float32(1e9)
    gl = jnp.where(lane < N_EGROUPS, logits, neg)
    gmax = jnp.max(gl, axis=-1, keepdims=True)
    gidx = jnp.min(jnp.where(gl == gmax, lanef, big), axis=-1, keepdims=True)
    pg = 1.0 / jnp.sum(jnp.exp(gl - gmax), axis=-1, keepdims=True)
    lo = N_EGROUPS + EXPERTS_PER_GROUP * gidx
    el = jnp.where((lanef >= lo) & (lanef < lo + EXPERTS_PER_GROUP), logits, neg)
    t1 = jnp.max(el, axis=-1, keepdims=True)
    j1 = jnp.min(jnp.where(el == t1, lanef, big), axis=-1, keepdims=True)
    el2 = jnp.where(lanef == j1, neg, el)
    t2 = jnp.max(el2, axis=-1, keepdims=True)
    j2 = jnp.min(jnp.where(el2 == t2, lanef, big), axis=-1, keepdims=True)
    e21 = jnp.exp(t2 - t1)
    w1 = pg / (1.0 + e21)
    w2 = pg * e21 / (1.0 + e21)
    e1f = j1 - N_EGROUPS
    e2f = j2 - N_EGROUPS

    tm = x.shape[0]
    oh1 = (lanef == e1f).astype(F32)
    oh2 = (lanef == e2f).astype(F32)
    ri = lax.broadcasted_iota(jnp.int32, (tm, tm), 0)
    ci = lax.broadcasted_iota(jnp.int32, (tm, tm), 1)
    before = (ci < ri).astype(BF16)
    p1 = jnp.dot(before, oh1.astype(BF16), preferred_element_type=F32)
    p2 = jnp.dot(before, oh2.astype(BF16), preferred_element_type=F32)
    carry = carry_ref[...]
    c1 = jnp.sum(oh1, axis=0, keepdims=True)
    c2 = jnp.sum(oh2, axis=0, keepdims=True)
    rank1 = jnp.sum(oh1 * (carry + p1), axis=-1, keepdims=True)
    rank2 = jnp.sum(oh2 * (carry + c1 + p2), axis=-1, keepdims=True)
    carry = carry + c1 + c2
    carry_ref[...] = carry
    cnt_ref[...] = jnp.broadcast_to(carry, cnt_ref.shape).astype(jnp.int32)

    ids = jnp.where(lane == 0, e1f, jnp.where(lane == 1, e2f, jnp.where(lane == 2, rank1, jnp.where(lane == 3, rank2, 0.0))))
    id_ref[...] = ids.astype(jnp.int32)
    wt_ref[...] = jnp.where(lane == 0, w1, jnp.where(lane == 1, w2, 0.0))


def router(x1, gain, w_r, b_r):
    tm = 256
    return pl.pallas_call(
        _router_kernel,
        out_shape=(jax.ShapeDtypeStruct((T * PACK_ROWS, LANES), jnp.uint32),
                   jax.ShapeDtypeStruct((T, LANES), jnp.int32),
                   jax.ShapeDtypeStruct((T, LANES), F32),
                   jax.ShapeDtypeStruct((8, LANES), jnp.int32)),
        grid=(T // tm,),
        in_specs=[pl.BlockSpec((tm, D), lambda i: (i, 0)),
                  pl.BlockSpec((1, D), lambda i: (0, 0)),
                  pl.BlockSpec((D, LANES), lambda i: (0, 0)),
                  pl.BlockSpec((1, LANES), lambda i: (0, 0))],
        out_specs=(pl.BlockSpec((tm * PACK_ROWS, LANES), lambda i: (i, 0)),
                   pl.BlockSpec((tm, LANES), lambda i: (i, 0)),
                   pl.BlockSpec((tm, LANES), lambda i: (i, 0)),
                   pl.BlockSpec((8, LANES), lambda i: (0, 0))),
        scratch_shapes=[pltpu.VMEM((1, LANES), F32)],
        compiler_params=_cparams(("arbitrary",)),
        name="ffn_norm_router",
    )(x1, gain, w_r, b_r)


DISPATCH_TB = 256
N_ZERO_FILLS = 2 * N_EXPERTS
YS_ROWS = D // LANES


def _dispatch_kernel(dest_ref, zs_ref, h_ref, xs_hbm, zbuf, zsem, sem):
    step = pl.program_id(0)

    @pl.when(step == 0)
    def _():
        zbuf[...] = jnp.zeros_like(zbuf)

        def zero_copy(e):
            start = pl.multiple_of(jnp.maximum(zs_ref[e], 0) * PACK_ROWS, PACK_ROWS)
            return pltpu.make_async_copy(zbuf, xs_hbm.at[pl.ds(start, MOE_BLOCK * PACK_ROWS)], zsem.at[0])

        def zstart(e, c):
            @pl.when(zs_ref[e] >= 0)
            def _():
                zero_copy(e).start()
            return c

        def zwait(e, c):
            @pl.when(zs_ref[e] >= 0)
            def _():
                zero_copy(e).wait()
            return c

        lax.fori_loop(0, N_ZERO_FILLS, zstart, 0)
        lax.fori_loop(0, N_ZERO_FILLS, zwait, 0)

    def row_copy(n, k):
        a = (step * DISPATCH_TB + n) * TOP_K + k
        src = h_ref.at[pl.ds(pl.multiple_of(n * PACK_ROWS, PACK_ROWS), PACK_ROWS)]
        dst = xs_hbm.at[pl.ds(pl.multiple_of(dest_ref[a] * PACK_ROWS, PACK_ROWS), PACK_ROWS)]
        return pltpu.make_async_copy(src, dst, sem.at[0])

    def issue(n, c):
        for k in range(TOP_K):
            row_copy(n, k).start(priority=k)
        return c

    def drain(n, c):
        for k in range(TOP_K):
            row_copy(n, k).wait()
        return c

    lax.fori_loop(0, DISPATCH_TB, issue, 0, unroll=8)
    lax.fori_loop(0, DISPATCH_TB, drain, 0, unroll=8)


def dispatch(dest, zero_start, hpk):
    grid_spec = pltpu.PrefetchScalarGridSpec(
        num_scalar_prefetch=2,
        grid=(T // DISPATCH_TB,),
        in_specs=[pl.BlockSpec((DISPATCH_TB * PACK_ROWS, LANES), lambda i, dst, zs: (i, 0))],
        out_specs=pl.BlockSpec(memory_space=pl.ANY),
        scratch_shapes=[pltpu.VMEM((MOE_BLOCK * PACK_ROWS, LANES), jnp.uint32),
                        pltpu.SemaphoreType.DMA((1,)),
                        pltpu.SemaphoreType.DMA((1,))],
    )
    return pl.pallas_call(
        _dispatch_kernel,
        out_shape=jax.ShapeDtypeStruct((MOE_ROWS * PACK_ROWS, LANES), jnp.uint32),
        grid_spec=grid_spec,
        compiler_params=_cparams(("arbitrary",)),
        name="moe_dispatch",
    )(dest, zero_start, hpk)


def _expert_kernel(be_ref, nu_ref, ord_ref, seq_ref, x_ref, wg_hbm, wu_hbm, wd_hbm, y_ref,
                   wg_st, wu_st, wd_st, wg_bf, wu_bf, wd_bf, sem):
    b = pl.program_id(0)

    def weight_copies(e, slot):
        return (pltpu.make_async_copy(wg_hbm.at[e], wg_st.at[slot], sem.at[slot, 0]),
                pltpu.make_async_copy(wu_hbm.at[e], wu_st.at[slot], sem.at[slot, 1]),
                pltpu.make_async_copy(wd_hbm.at[e], wd_st.at[slot], sem.at[slot, 2]))

    def start_fetch(n, slot):
        @pl.when(seq_ref[n] >= 0)
        def _():
            for cp in weight_copies(seq_ref[n], slot):
                cp.start()

    @pl.when(b < nu_ref[0])
    def _():
        n = ord_ref[b]
        slot = n % 2
        changed = jnp.logical_or(b == 0, be_ref[b] != be_ref[jnp.maximum(b - 1, 0)])

        @pl.when(b == 0)
        def _():
            start_fetch(0, 0)
            start_fetch(1, 1)

        @pl.when(changed)
        def _():
            cg, cu, cd = weight_copies(be_ref[b], slot)
            cg.wait()
            wg_bf[...] = wg_st[slot].astype(BF16)
            cu.wait()
            wu_bf[...] = wu_st[slot].astype(BF16)
            cd.wait()
            wd_bf[...] = wd_st[slot].astype(BF16)
            start_fetch(n + 2, slot)

        x = _unpack_bf16_pairs(x_ref, MOE_BLOCK)
        gate = jnp.dot(x, wg_bf[...], preferred_element_type=F32)
        up = jnp.dot(x, wu_bf[...], preferred_element_type=F32)
        hid = (jax.nn.silu(gate) * up).astype(BF16)
        y = jnp.dot(hid, wd_bf[...], preferred_element_type=F32)
        for c in range(YS_ROWS):
            y_ref[pl.ds(c, MOE_BLOCK, stride=YS_ROWS), :] = y[:, c * LANES:(c + 1) * LANES]

    @pl.when(b >= nu_ref[0])
    def _():
        y_ref[...] = jnp.zeros_like(y_ref)


def experts(block_expert, n_used, block_ord, expert_seq, xs, w_gate, w_up, w_down):
    def blk(b, be, nu, od, sq):
        return jnp.minimum(b, nu[0] - 1)

    grid_spec = pltpu.PrefetchScalarGridSpec(
        num_scalar_prefetch=4,
        grid=(MOE_BLOCKS,),
        in_specs=[pl.BlockSpec((MOE_BLOCK * PACK_ROWS, LANES), lambda b, be, nu, od, sq: (blk(b, be, nu, od, sq), 0)),
                  pl.BlockSpec(memory_space=pl.ANY),
                  pl.BlockSpec(memory_space=pl.ANY),
                  pl.BlockSpec(memory_space=pl.ANY)],
        out_specs=pl.BlockSpec((MOE_BLOCK * YS_ROWS, LANES), lambda b, be, nu, od, sq: (b, 0)),
        scratch_shapes=[pltpu.VMEM((2, D, EXPERT_FF), F32),
                        pltpu.VMEM((2, D, EXPERT_FF), F32),
                        pltpu.VMEM((2, EXPERT_FF, D), F32),
                        pltpu.VMEM((D, EXPERT_FF), BF16),
                        pltpu.VMEM((D, EXPERT_FF), BF16),
                        pltpu.VMEM((EXPERT_FF, D), BF16),
                        pltpu.SemaphoreType.DMA((2, 3))],
    )
    return pl.pallas_call(
        _expert_kernel,
        out_shape=jax.ShapeDtypeStruct((MOE_ROWS * YS_ROWS, LANES), F32),
        grid_spec=grid_spec,
        compiler_params=_cparams(("arbitrary",)),
        name="moe_experts",
    )(block_expert, n_used, block_ord, expert_seq, xs, w_gate, w_up, w_down)


COMBINE_ROWS = 256
COMBINE_SUB = 16


def _combine_kernel(dest_ref, ys_hbm, x_ref, wt_ref, g_ref, o_ref, ybuf, sem):
    s = pl.program_id(0)
    ns = pl.num_programs(0)
    slot = s % 2
    tb = COMBINE_SUB
    rows = COMBINE_ROWS

    def row_copy(step, n, k, sl):
        tok = step * rows + n
        src = ys_hbm.at[pl.ds(pl.multiple_of(dest_ref[tok * TOP_K + k] * YS_ROWS, YS_ROWS), YS_ROWS)]
        dst = ybuf.at[sl * TOP_K + k, pl.ds(pl.multiple_of(n * YS_ROWS, YS_ROWS), YS_ROWS)]
        return pltpu.make_async_copy(src, dst, sem.at[sl])

    def issue(step, sl):
        def body(n, c):
            for k in range(TOP_K):
                row_copy(step, n, k, sl).start(priority=k)
            return c
        lax.fori_loop(0, rows, body, 0, unroll=8)

    @pl.when(s == 0)
    def _():
        issue(0, 0)

    @pl.when(s + 1 < ns)
    def _():
        issue(s + 1, 1 - slot)

    def wbody(n, c):
        for k in range(TOP_K):
            row_copy(s, n, k, slot).wait()
        return c
    lax.fori_loop(0, rows, wbody, 0, unroll=8)

    g = g_ref[...]
    y0_ref = ybuf.at[slot * TOP_K]
    y1_ref = ybuf.at[slot * TOP_K + 1]
    for b in range(rows // tb):
        sub = slice(b * tb, (b + 1) * tb)
        w = wt_ref[sub, :]
        y0 = jnp.concatenate([y0_ref[pl.ds(b * tb * YS_ROWS + c, tb, stride=YS_ROWS), :] for c in range(YS_ROWS)], axis=1)
        y1 = jnp.concatenate([y1_ref[pl.ds(b * tb * YS_ROWS + c, tb, stride=YS_ROWS), :] for c in range(YS_ROWS)], axis=1)
        z = x_ref[sub, :] + (w[:, 0:1] * y0 + w[:, 1:2] * y1)
        ms = jnp.mean(z * z, axis=-1, keepdims=True)
        o_ref[sub, :] = z * lax.rsqrt(ms + NORM_EPS) * g


def combine(dest, ys, x1, wts, gain):
    rows = COMBINE_ROWS
    grid_spec = pltpu.PrefetchScalarGridSpec(
        num_scalar_prefetch=1,
        grid=(T // rows,),
        in_specs=[pl.BlockSpec(memory_space=pl.ANY),
                  pl.BlockSpec((rows, D), lambda s, dst: (s, 0)),
                  pl.BlockSpec((rows, LANES), lambda s, dst: (s, 0)),
                  pl.BlockSpec((1, D), lambda s, dst: (0, 0))],
        out_specs=pl.BlockSpec((rows, D), lambda s, dst: (s, 0)),
        scratch_shapes=[pltpu.VMEM((2 * TOP_K, rows * YS_ROWS, LANES), F32),
                        pltpu.SemaphoreType.DMA((2,))],
    )
    return pl.pallas_call(
        _combine_kernel,
        out_shape=jax.ShapeDtypeStruct((T, D), F32),
        grid_spec=grid_spec,
        compiler_params=_cparams(("arbitrary",)),
        name="moe_combine_final_norm",
    )(dest, ys, x1, wts, gain)


def dispatch_plan(ids, counts):
    experts_ = jnp.arange(N_EXPERTS, dtype=jnp.int32)
    padded = (counts + MOE_BLOCK - 1) // MOE_BLOCK * MOE_BLOCK
    pad_end = jnp.cumsum(padded)
    pad_start = pad_end - padded
    e = ids[:, :TOP_K]
    start_of = jnp.sum(jnp.where(e[:, :, None] == experts_[None, None, :], pad_start[None, None, :], 0), axis=-1)
    dest = (start_of + ids[:, TOP_K:2 * TOP_K]).reshape(N_ASSIGN).astype(jnp.int32)
    n_used = pad_end[-1] // MOE_BLOCK
    block_start = jnp.minimum(jnp.arange(MOE_BLOCKS, dtype=jnp.int32), n_used - 1) * MOE_BLOCK
    block_expert = jnp.sum((block_start[:, None] >= pad_end[None, :]).astype(jnp.int32), axis=1)
    block_expert = jnp.minimum(block_expert, N_EXPERTS - 1).astype(jnp.int32)
    tail = n_used + experts_
    zero_start = jnp.concatenate([jnp.where(counts > 0, pad_end - MOE_BLOCK, -1),
                                  jnp.where(tail < MOE_BLOCKS, tail * MOE_BLOCK, -1)]).astype(jnp.int32)
    present = counts > 0
    expert_ord = jnp.cumsum(present.astype(jnp.int32)) - 1
    slots = jnp.arange(N_EXPERTS + 2, dtype=jnp.int32)
    hit = present[None, :] & (expert_ord[None, :] == slots[:, None])
    expert_seq = jnp.where(jnp.any(hit, axis=1), jnp.sum(jnp.where(hit, experts_[None, :], 0), axis=1), -1)
    block_ord = jnp.sum(jnp.where(block_expert[:, None] == experts_[None, :], expert_ord[None, :], 0), axis=1)
    return (block_expert, n_used.astype(jnp.int32).reshape(1), dest, zero_start,
            block_ord.astype(jnp.int32), expert_seq.astype(jnp.int32))


def kernel(x, norm_mix, w_in, b_gate, ssm_a_re, ssm_a_im, ssm_log_step, ssm_b_re, ssm_b_im, ssm_c_re, ssm_c_im, ssm_d, w_glu, b_glu, w_up_attn, w_up_ssm, w_out, norm_ffn, w_router_group, b_router_group, w_router_expert, b_router_expert, w_expert_gate, w_expert_up, w_expert_down, norm_final):
    x2 = x.reshape(T, D)
    h = norm_permute(x2, norm_mix.reshape(1, D)).reshape(T, D)
    w_in_l = w_in.reshape(D, IN_COLS)
    qkv = proj(h, w_in_l, 0, 3 * QKV_COLS, name="proj_qkv")
    u = proj(h, w_in_l, 3 * QKV_COLS, SSM_W, name="proj_ssm_in")
    gates = proj(h, w_in_l, 3 * QKV_COLS + SSM_W, 2 * D, bias=b_gate.reshape(1, 2 * D), name="proj_gates")

    attn = attention(qkv.reshape(R, NI, 3 * QKV_COLS)).reshape(T, ATTN_OUT)

    G = SSM_W // SSM_CH
    wb, wc, avec, pows = ssm_params(
        ssm_a_re.reshape(G, SSM_STATE).astype(F32), ssm_a_im.reshape(G, SSM_STATE).astype(F32),
        ssm_log_step.reshape(G),
        ssm_b_re.reshape(G, SSM_STATE, SSM_CH).astype(F32), ssm_b_im.reshape(G, SSM_STATE, SSM_CH).astype(F32),
        ssm_c_re.reshape(G, SSM_CH, SSM_STATE), ssm_c_im.reshape(G, SSM_CH, SSM_STATE))
    yg = ssm_scan(u.reshape(R, NI, SSM_W), wb, wc, avec, pows, ssm_d.reshape(1, SSM_W).astype(F32))
    ssm = glu(yg.reshape(T, SSM_W), w_glu.reshape(SSM_W, SSM_W), b_glu.reshape(1, SSM_W))

    merged = merge(attn, ssm, w_up_attn.reshape(ATTN_OUT, D), w_up_ssm.reshape(SSM_W, D), gates)
    x1 = out_proj(merged.reshape(R, NI, D), w_out.reshape(D, D), x2)

    w_r = jnp.concatenate([w_router_group.reshape(D, N_EGROUPS), w_router_expert.reshape(D, N_EXPERTS),
                           jnp.zeros((D, LANES - N_EGROUPS - N_EXPERTS), F32)], axis=1)
    b_r = jnp.concatenate([b_router_group.reshape(1, N_EGROUPS), b_router_expert.reshape(1, N_EXPERTS),
                           jnp.zeros((1, LANES - N_EGROUPS - N_EXPERTS), F32)], axis=1)
    hpk, ids, wts, counts = router(x1, norm_ffn.reshape(1, D), w_r, b_r)

    block_expert, n_used, dest, zero_start, block_ord, expert_seq = dispatch_plan(ids[:, :2 * TOP_K], counts[0, :N_EXPERTS])
    xs = dispatch(dest, zero_start, hpk)
    ys = experts(block_expert, n_used, block_ord, expert_seq, xs,
                 w_expert_gate.reshape(N_EXPERTS, D, EXPERT_FF), w_expert_up.reshape(N_EXPERTS, D, EXPERT_FF),
                 w_expert_down.reshape(N_EXPERTS, EXPERT_FF, D))
    out = combine(dest, ys, x1, wts, norm_final.reshape(1, D))
    return out.reshape(1, T, D)
```

```python
import functools
import math

import jax
import jax.numpy as jnp
from jax import lax
from jax.experimental import pallas as pl
from jax.experimental.pallas import tpu as pltpu

F32 = jnp.float32
BF16 = jnp.bfloat16

T = 8192
D = 2048
R = 16
NI = T // R
HEAD_DIM = 64
N_HEAD_SLOTS = 8
DILATIONS = (1, 4, 16)
ATTN_BLOCK = 128
QKV_COLS = 1536
ATTN_OUT = 512
SSM_W = 1024
SSM_STATE = 64
SSM_CH = 16
IN_COLS = 3 * QKV_COLS + SSM_W + 2 * D
N_EXPERTS = 32
N_EGROUPS = 4
EXPERTS_PER_GROUP = 8
TOP_K = 2
EXPERT_FF = 512
NORM_EPS = 1e-6
LANES = 128
VMEM_LIMIT = 48 * 1024 * 1024

MOE_BLOCK = 256
N_ASSIGN = T * TOP_K
MOE_BLOCKS = N_ASSIGN // MOE_BLOCK + N_EXPERTS
MOE_ROWS = MOE_BLOCKS * MOE_BLOCK


def _cparams(sem):
    return pltpu.CompilerParams(dimension_semantics=sem, vmem_limit_bytes=VMEM_LIMIT)


N_SLABS = D // LANES
NORM_TB = 32
NORM_CHUNK = 64


def _norm_permute_kernel(x_ref, g_ref, h_ref, slab):
    g = g_ref[...]

    def chunk(t, c):
        rows = pl.ds(pl.multiple_of(t * NORM_CHUNK, NORM_CHUNK), NORM_CHUNK)
        x = x_ref[rows, :]
        ms = jnp.mean(x * x, axis=-1, keepdims=True)
        hn = x * lax.rsqrt(ms + NORM_EPS) * g
        for s_ in range(N_SLABS):
            slab[s_, rows, :] = hn[:, s_ * LANES:(s_ + 1) * LANES]
        return c

    lax.fori_loop(0, NORM_TB * R // NORM_CHUNK, chunk, 0)
    for r in range(R):
        pieces = [slab[s_, pl.ds(r, NORM_TB, stride=R), :] for s_ in range(N_SLABS)]
        h_ref[r] = jnp.concatenate(pieces, axis=1).astype(BF16)


def norm_permute(x2, gain):
    return pl.pallas_call(
        _norm_permute_kernel,
        out_shape=jax.ShapeDtypeStruct((R, NI, D), BF16),
        grid=(NI // NORM_TB,),
        in_specs=[pl.BlockSpec((NORM_TB * R, D), lambda i: (i, 0)),
                  pl.BlockSpec((1, D), lambda i: (0, 0))],
        out_specs=pl.BlockSpec((R, NORM_TB, D), lambda i: (0, i, 0)),
        scratch_shapes=[pltpu.VMEM((N_SLABS, NORM_TB * R, LANES), F32)],
        compiler_params=_cparams(("arbitrary",)),
        name="norm_permute",
    )(x2, gain)


MXU_N = 256


def _cast_weight_once(w_ref, wbf_ref):
    @pl.when(pl.program_id(1) == 0)
    def _():
        wbf_ref[...] = w_ref[...].astype(BF16)


def _proj_kernel(a_ref, w_ref, o_ref, wbf_ref):
    _cast_weight_once(w_ref, wbf_ref)
    a = a_ref[...]
    for c in range(w_ref.shape[1] // MXU_N):
        cols = slice(c * MXU_N, (c + 1) * MXU_N)
        o_ref[:, cols] = jnp.dot(a, wbf_ref[:, cols], preferred_element_type=F32)


def proj(h, w_in, col_off, n_cols, tn, name="proj"):
    tm = 1024
    off = col_off // tn
    return pl.pallas_call(
        _proj_kernel,
        out_shape=jax.ShapeDtypeStruct((T, n_cols), F32),
        grid=(n_cols // tn, T // tm),
        in_specs=[pl.BlockSpec((tm, D), lambda j, i: (i, 0)),
                  pl.BlockSpec((D, tn), lambda j, i: (0, j + off), pipeline_mode=pl.Buffered(1))],
        out_specs=pl.BlockSpec((tm, tn), lambda j, i: (i, j)),
        scratch_shapes=[pltpu.VMEM((D, tn), BF16)],
        compiler_params=_cparams(("arbitrary", "arbitrary")),
        name=name,
    )(h, w_in)


def _seq_index_maps(d):
    nseg = R // d
    qlen = ATTN_BLOCK // nseg
    return nseg, qlen


def _bias_matrices(d, hp):
    nseg, qlen = _seq_index_maps(d)
    klen = 2 * qlen
    row = lax.broadcasted_iota(jnp.int32, (2 * ATTN_BLOCK, 2 * ATTN_BLOCK), 0)
    col = lax.broadcasted_iota(jnp.int32, (2 * ATTN_BLOCK, 2 * ATTN_BLOCK), 1)
    rho = row % ATTN_BLOCK
    jq = (rho % qlen) * nseg + rho // qlen
    jk = ((col % klen) - qlen) * nseg + col // klen
    steps = jq - jk
    valid = (steps >= 0) & (steps <= ATTN_BLOCK)
    head = 2 * hp + row // ATTN_BLOCK
    slope = lax.bitcast_convert_type((127 - (head + 1)) << 23, F32)
    bias = -slope * (d * steps).astype(F32)
    neg = jnp.float32(-jnp.inf)
    return jnp.where(valid, bias, neg), jnp.where(valid & (jk >= 0), bias, neg)


def _attend_pair(q, k, v, bias):
    lane = lax.broadcasted_iota(jnp.int32, (ATTN_BLOCK, LANES), 1)
    first = lane < HEAD_DIM
    zero = jnp.zeros_like(q)
    q2 = jnp.concatenate([jnp.where(first, q, zero), jnp.where(first, zero, q)], axis=0).astype(BF16)
    s = lax.dot_general(q2, k.astype(BF16), (((1,), (1,)), ((), ())), preferred_element_type=F32)
    s = s + bias
    m = jnp.max(s, axis=-1, keepdims=True)
    p = jnp.exp(s - m)
    l = jnp.sum(p, axis=-1, keepdims=True)
    o2 = jnp.dot(p.astype(BF16), v.astype(BF16), preferred_element_type=F32)
    o2 = o2 / l
    lse = m + jnp.log(l)
    o = jnp.where(first, o2[:ATTN_BLOCK], o2[ATTN_BLOCK:])
    lse_b = jnp.where(first, lse[:ATTN_BLOCK], lse[ATTN_BLOCK:])
    return o, lse_b


def _attn_kernel(q_ref, kp_ref, kc_ref, vp_ref, vc_ref, o_ref, kbuf, vbuf, obuf, lbuf, bias_ref):
    hp = pl.program_id(0)
    it = pl.program_id(1)
    g = pl.program_id(2)
    scale = HEAD_DIM ** -0.5

    kbuf[:, :ATTN_BLOCK, :] = kp_ref[...]
    kbuf[:, ATTN_BLOCK:, :] = kc_ref[...]
    vbuf[:, :ATTN_BLOCK, :] = vp_ref[...]
    vbuf[:, ATTN_BLOCK:, :] = vc_ref[...]

    for gi, d in enumerate(DILATIONS):
        nseg, qlen = _seq_index_maps(d)
        klen = 2 * qlen
        nblk = ATTN_BLOCK // qlen

        @pl.when(g == gi)
        def _(gi=gi, d=d, nseg=nseg, qlen=qlen, klen=klen, nblk=nblk):
            @pl.when(it == 0)
            def _():
                b_reg, b_first = _bias_matrices(d, hp)
                bias_ref[gi, 0] = b_reg
                bias_ref[gi, 1] = b_first

            def block(idx, carry):
                rd = idx // nblk
                bb = idx % nblk
                q0 = pl.multiple_of(bb * qlen, qlen)
                k0 = pl.multiple_of(ATTN_BLOCK + bb * qlen - qlen, qlen)
                qs, ks, vs = [], [], []
                for m_ in range(nseg):
                    rr = rd + d * m_
                    qs.append(q_ref[rr, pl.ds(q0, qlen), :])
                    ks.append(kbuf[rr, pl.ds(k0, klen), :])
                    vs.append(vbuf[rr, pl.ds(k0, klen), :])
                q = jnp.concatenate(qs, axis=0) * scale
                k = jnp.concatenate(ks, axis=0)
                v = jnp.concatenate(vs, axis=0)
                is_first = jnp.logical_and(it == 0, bb == 0)
                bias = bias_ref[gi, jnp.where(is_first, 1, 0)]
                o, lse = _attend_pair(q, k, v, bias)
                for m_ in range(nseg):
                    rr = rd + d * m_
                    obuf[gi, rr, pl.ds(q0, qlen), :] = o[m_ * qlen:(m_ + 1) * qlen]
                    lbuf[gi, rr, pl.ds(q0, qlen), :] = lse[m_ * qlen:(m_ + 1) * qlen]
                return carry

            lax.fori_loop(0, d * nblk, block, 0, unroll=8)

    @pl.when(g == len(DILATIONS) - 1)
    def _():
        for r in range(R):
            l0, l1, l2 = lbuf[0, r], lbuf[1, r], lbuf[2, r]
            mx = jnp.maximum(jnp.maximum(l0, l1), l2)
            e0, e1, e2 = jnp.exp(l0 - mx), jnp.exp(l1 - mx), jnp.exp(l2 - mx)
            den = e0 + e1 + e2
            num = e0 * obuf[0, r] + e1 * obuf[1, r] + e2 * obuf[2, r]
            o_ref[r] = (num / den).astype(BF16)


def attention(qkv3):
    n_hp = N_HEAD_SLOTS // 2
    n_it = NI // ATTN_BLOCK
    ng = len(DILATIONS)
    cb = QKV_COLS // LANES

    def cur(base):
        return pl.BlockSpec((R, ATTN_BLOCK, LANES), lambda hp, it, g: (0, it, base + g * n_hp + hp))

    def prev(base):
        return pl.BlockSpec((R, ATTN_BLOCK, LANES),
                            lambda hp, it, g: (0, jnp.maximum(it - 1, 0), base + g * n_hp + hp))

    return pl.pallas_call(
        _attn_kernel,
        out_shape=jax.ShapeDtypeStruct((R, NI, ATTN_OUT), BF16),
        grid=(n_hp, n_it, ng),
        in_specs=[cur(0), prev(cb), cur(cb), prev(2 * cb), cur(2 * cb)],
        out_specs=pl.BlockSpec((R, ATTN_BLOCK, LANES), lambda hp, it, g: (0, it, hp)),
        scratch_shapes=[pltpu.VMEM((R, 2 * ATTN_BLOCK, LANES), F32),
                        pltpu.VMEM((R, 2 * ATTN_BLOCK, LANES), F32),
                        pltpu.VMEM((ng, R, ATTN_BLOCK, LANES), F32),
                        pltpu.VMEM((ng, R, ATTN_BLOCK, LANES), F32),
                        pltpu.VMEM((ng, 2, 2 * ATTN_BLOCK, 2 * ATTN_BLOCK), F32)],
        compiler_params=_cparams(("arbitrary", "arbitrary", "arbitrary")),
        name="dilated_attention",
    )(qkv3, qkv3, qkv3, qkv3, qkv3)


SSM_SLAB = 256
SSM_SLABS = SSM_W // SSM_SLAB
SLAB_STATES = SSM_SLAB // SSM_CH * SSM_STATE
SSM_TI = 128
SSM_MM_CHUNK = 4


def _ssm_kernel(u_ref, wb_ref, wc_ref, a_ref, pw_ref, dsk_ref, o_ref, s_ref, zs_ref, zc_ref):
    ic = pl.program_id(1)
    ns = SLAB_STATES

    @pl.when(ic == 0)
    def _():
        zc_ref[...] = jnp.zeros_like(zc_ref)

    n_chunks = R // SSM_MM_CHUNK
    n_tiles = SSM_TI // 8
    wb = wb_ref[0]
    wc = wc_ref[0]
    dsk = dsk_ref[...]
    arb = jnp.broadcast_to(a_ref[0, :, :ns], (8, ns))
    aib = jnp.broadcast_to(a_ref[0, :, ns:], (8, ns))


    def bu_chunk(c):
        lo = c * SSM_MM_CHUNK
        uc = u_ref[lo:lo + SSM_MM_CHUNK].reshape(SSM_MM_CHUNK * SSM_TI, SSM_SLAB)
        bu = jnp.dot(uc.astype(BF16), wb, preferred_element_type=F32)
        s_ref[lo:lo + SSM_MM_CHUNK] = bu.reshape(SSM_MM_CHUNK, SSM_TI, 2 * ns)

    def local_chunk(c):
        lo = c * SSM_MM_CHUNK
        first = max(lo, 1)
        for t in range(n_tiles):
            rows = slice(t * 8, (t + 1) * 8)
            pr = s_ref[first - 1, rows, :ns]
            pi = s_ref[first - 1, rows, ns:]
            for r in range(first, lo + SSM_MM_CHUNK):
                nr = s_ref[r, rows, :ns] + (arb * pr - aib * pi)
                ni = s_ref[r, rows, ns:] + (arb * pi + aib * pr)
                s_ref[r, rows, :ns] = nr
                s_ref[r, rows, ns:] = ni
                pr, pi = nr, ni

    bu_chunk(0)
    for c in range(n_chunks):
        if c + 1 < n_chunks:
            bu_chunk(c + 1)
        local_chunk(c)

    a16r = pw_ref[0, R - 1:R, :ns]
    a16i = pw_ref[0, R - 1:R, ns:]

    def zstep(i, z):
        zs_ref[pl.ds(i, 1), :] = z
        e = s_ref[R - 1, pl.ds(i, 1), :]
        zr, zi = z[:, :ns], z[:, ns:]
        nz = jnp.concatenate([a16r * zr - a16i * zi, a16r * zi + a16i * zr], axis=-1)
        return nz + e

    zc_ref[...] = lax.fori_loop(0, SSM_TI, zstep, zc_ref[...])

    def fix_chunk(c):
        lo = c * SSM_MM_CHUNK
        for r in range(lo, lo + SSM_MM_CHUNK):
            prb = jnp.broadcast_to(pw_ref[0, r:r + 1, :ns], (8, ns))
            pib = jnp.broadcast_to(pw_ref[0, r:r + 1, ns:], (8, ns))
            for t in range(n_tiles):
                rows = slice(t * 8, (t + 1) * 8)
                zr = zs_ref[rows, :ns]
                zi = zs_ref[rows, ns:]
                s_ref[r, rows, :ns] = s_ref[r, rows, :ns] + (prb * zr - pib * zi)
                s_ref[r, rows, ns:] = s_ref[r, rows, ns:] + (prb * zi + pib * zr)

    def out_chunk(c):
        lo = c * SSM_MM_CHUNK
        xs = s_ref[lo:lo + SSM_MM_CHUNK].reshape(SSM_MM_CHUNK * SSM_TI, 2 * ns)
        y = jnp.dot(xs.astype(BF16), wc, preferred_element_type=F32)
        y = y.reshape(SSM_MM_CHUNK, SSM_TI, SSM_SLAB) + dsk * u_ref[lo:lo + SSM_MM_CHUNK]
        o_ref[lo:lo + SSM_MM_CHUNK] = jax.nn.gelu(y)

    fix_chunk(0)
    for c in range(n_chunks):
        if c + 1 < n_chunks:
            fix_chunk(c + 1)
        out_chunk(c)


def ssm_scan(u3, wb, wc, avec, pows, dskip):
    ns2 = 2 * SLAB_STATES
    return pl.pallas_call(
        _ssm_kernel,
        out_shape=jax.ShapeDtypeStruct((R, NI, SSM_W), F32),
        grid=(SSM_SLABS, NI // SSM_TI),
        in_specs=[pl.BlockSpec((R, SSM_TI, SSM_SLAB), lambda kb, ic: (0, ic, kb)),
                  pl.BlockSpec((1, SSM_SLAB, ns2), lambda kb, ic: (kb, 0, 0)),
                  pl.BlockSpec((1, ns2, SSM_SLAB), lambda kb, ic: (kb, 0, 0)),
                  pl.BlockSpec((1, 1, ns2), lambda kb, ic: (kb, 0, 0)),
                  pl.BlockSpec((1, R, ns2), lambda kb, ic: (kb, 0, 0)),
                  pl.BlockSpec((1, SSM_SLAB), lambda kb, ic: (0, kb))],
        out_specs=pl.BlockSpec((R, SSM_TI, SSM_SLAB), lambda kb, ic: (0, ic, kb)),
        scratch_shapes=[pltpu.VMEM((R, SSM_TI, ns2), F32),
                        pltpu.VMEM((SSM_TI, ns2), F32),
                        pltpu.VMEM((1, ns2), F32)],
        compiler_params=_cparams(("arbitrary", "arbitrary")),
        name="s5_ssm",
    )(u3, wb, wc, avec, pows, dskip)


def ssm_params(a_re, a_im, log_step, b_re, b_im, c_re, c_im):
    G, P, H = SSM_W // SSM_CH, SSM_STATE, SSM_CH
    gs = SSM_SLAB // SSM_CH
    step = jnp.exp(log_step.astype(F32))[:, None]
    mag = jnp.exp(a_re * step)
    ang = a_im * step
    abar_re, abar_im = mag * jnp.cos(ang), mag * jnp.sin(ang)
    nr, ni = abar_re - 1.0, abar_im
    den = a_re * a_re + a_im * a_im
    f_re = (nr * a_re + ni * a_im) / den
    f_im = (ni * a_re - nr * a_im) / den
    bbar_re = f_re[..., None] * b_re - f_im[..., None] * b_im
    bbar_im = f_re[..., None] * b_im + f_im[..., None] * b_re
    eye = jnp.eye(gs, dtype=F32)

    def in_mat(b):
        b4 = b.reshape(SSM_SLABS, gs, P, H)
        m = jnp.einsum('kgph,gj->kghjp', b4, eye)
        return m.reshape(SSM_SLABS, gs * H, gs * P)

    def out_mat(c):
        c4 = c.reshape(SSM_SLABS, gs, H, P)
        m = jnp.einsum('kghp,gj->kgpjh', c4, eye)
        return m.reshape(SSM_SLABS, gs * P, gs * H)

    wb = jnp.concatenate([in_mat(bbar_re), in_mat(bbar_im)], axis=-1).astype(BF16)
    wc = jnp.concatenate([out_mat(c_re.astype(F32)), -out_mat(c_im.astype(F32))], axis=1).astype(BF16)

    def slab_vec(v):
        return v.reshape(SSM_SLABS, gs * P)

    avec = jnp.concatenate([slab_vec(abar_re), slab_vec(abar_im)], axis=-1)[:, None, :]
    pr, pi = abar_re, abar_im
    prs, pis = [pr], [pi]
    for _ in range(R - 1):
        pr, pi = pr * abar_re - pi * abar_im, pr * abar_im + pi * abar_re
        prs.append(pr)
        pis.append(pi)
    pows = jnp.concatenate([jnp.stack([slab_vec(p) for p in prs], axis=1),
                            jnp.stack([slab_vec(p) for p in pis], axis=1)], axis=-1)
    return wb, wc, avec, pows


def _glu_kernel(a_ref, w_ref, b_ref, o_ref, wbf_ref):
    @pl.when(pl.program_id(0) == 0)
    def _():
        wbf_ref[...] = w_ref[...].astype(BF16)

    a = a_ref[...].astype(BF16)
    for c in range(SSM_W // MXU_N):
        cols = slice(c * MXU_N, (c + 1) * MXU_N)
        acc = jnp.dot(a, wbf_ref[:, cols], preferred_element_type=F32)
        o_ref[:, cols] = (a_ref[:, cols] * jax.nn.sigmoid(acc + b_ref[:, cols])).astype(BF16)


def glu(yg, w_glu, b_glu):
    tm = 512
    return pl.pallas_call(
        _glu_kernel,
        out_shape=jax.ShapeDtypeStruct((T, SSM_W), BF16),
        grid=(T // tm,),
        in_specs=[pl.BlockSpec((tm, SSM_W), lambda i: (i, 0)),
                  pl.BlockSpec((SSM_W, SSM_W), lambda i: (0, 0)),
                  pl.BlockSpec((1, SSM_W), lambda i: (0, 0))],
        out_specs=pl.BlockSpec((tm, SSM_W), lambda i: (i, 0)),
        scratch_shapes=[pltpu.VMEM((SSM_W, SSM_W), BF16)],
        compiler_params=_cparams(("arbitrary",)),
        name="ssm_glu",
    )(yg, w_glu, b_glu)


def _merge_kernel(h_ref, at_ref, ss_ref, wga_ref, wgs_ref, ba_ref, bs_ref, wa_ref, ws_ref, o_ref,
                  wga_bf, wgs_bf, wa_bf, ws_bf):
    _cast_weight_once(wga_ref, wga_bf)
    _cast_weight_once(wgs_ref, wgs_bf)
    _cast_weight_once(wa_ref, wa_bf)
    _cast_weight_once(ws_ref, ws_bf)
    h = h_ref[...]
    at = at_ref[...]
    ss = ss_ref[...]
    for c in range(wa_ref.shape[1] // MXU_N):
        cols = slice(c * MXU_N, (c + 1) * MXU_N)
        ga = jax.nn.sigmoid(jnp.dot(h, wga_bf[:, cols], preferred_element_type=F32) + ba_ref[:, cols])
        a = jnp.dot(at, wa_bf[:, cols], preferred_element_type=F32)
        gs = jax.nn.sigmoid(jnp.dot(h, wgs_bf[:, cols], preferred_element_type=F32) + bs_ref[:, cols])
        s = jnp.dot(ss, ws_bf[:, cols], preferred_element_type=F32)
        o_ref[:, cols] = (ga * a + gs * s).astype(BF16)


def merge(h, attn, ssm, w_in, b_gate, w_up_attn, w_up_ssm):
    tm, tn = 1024, 512
    nj = D // tn
    off_a = (3 * QKV_COLS + SSM_W) // tn
    once = pl.Buffered(1)
    return pl.pallas_call(
        _merge_kernel,
        out_shape=jax.ShapeDtypeStruct((T, D), BF16),
        grid=(nj, T // tm),
        in_specs=[pl.BlockSpec((tm, D), lambda j, i: (i, 0)),
                  pl.BlockSpec((tm, ATTN_OUT), lambda j, i: (i, 0)),
                  pl.BlockSpec((tm, SSM_W), lambda j, i: (i, 0)),
                  pl.BlockSpec((D, tn), lambda j, i: (0, j + off_a), pipeline_mode=once),
                  pl.BlockSpec((D, tn), lambda j, i: (0, j + off_a + nj), pipeline_mode=once),
                  pl.BlockSpec((1, tn), lambda j, i: (0, j)),
                  pl.BlockSpec((1, tn), lambda j, i: (0, j + nj)),
                  pl.BlockSpec((ATTN_OUT, tn), lambda j, i: (0, j), pipeline_mode=once),
                  pl.BlockSpec((SSM_W, tn), lambda j, i: (0, j), pipeline_mode=once)],
        out_specs=pl.BlockSpec((tm, tn), lambda j, i: (i, j)),
        scratch_shapes=[pltpu.VMEM((D, tn), BF16), pltpu.VMEM((D, tn), BF16),
                        pltpu.VMEM((ATTN_OUT, tn), BF16), pltpu.VMEM((SSM_W, tn), BF16)],
        compiler_params=_cparams(("arbitrary", "arbitrary")),
        name="gates_branch_merge",
    )(h, attn, ssm, w_in, w_in, b_gate, b_gate, w_up_attn, w_up_ssm)


def _outproj_kernel(m_ref, w_ref, x_ref, o_ref, slab, wbf_ref):
    _cast_weight_once(w_ref, wbf_ref)
    tb = m_ref.shape[1]
    a = m_ref[...].reshape(R * tb, D)
    per = MXU_N // LANES
    for c in range(w_ref.shape[1] // MXU_N):
        acc = jnp.dot(a, wbf_ref[:, c * MXU_N:(c + 1) * MXU_N], preferred_element_type=F32)
        for s_ in range(per):
            for r in range(R):
                slab[c * per + s_, pl.ds(r, tb, stride=R), :] = acc[r * tb:(r + 1) * tb, s_ * LANES:(s_ + 1) * LANES]
    for s_ in range(w_ref.shape[1] // LANES):
        o_ref[:, s_ * LANES:(s_ + 1) * LANES] = slab[s_] + x_ref[:, s_ * LANES:(s_ + 1) * LANES]


def out_proj(merged3, w_out, x2):
    tb, tn = 64, 512
    return pl.pallas_call(
        _outproj_kernel,
        out_shape=jax.ShapeDtypeStruct((T, D), F32),
        grid=(D // tn, NI // tb),
        in_specs=[pl.BlockSpec((R, tb, D), lambda j, i: (0, i, 0)),
                  pl.BlockSpec((D, tn), lambda j, i: (0, j)),
                  pl.BlockSpec((tb * R, tn), lambda j, i: (i, j))],
        out_specs=pl.BlockSpec((tb * R, tn), lambda j, i: (i, j)),
        scratch_shapes=[pltpu.VMEM((tn // LANES, tb * R, LANES), F32),
                        pltpu.VMEM((D, tn), BF16)],
        compiler_params=_cparams(("arbitrary", "arbitrary")),
        name="out_proj_residual",
    )(merged3, w_out, x2)


PACK_ROWS = 8
HALF_D = D // 2


def _pack_bf16_pairs(z):
    zb = lax.bitcast_convert_type(z, jnp.uint32)
    rnd = zb + jnp.uint32(0x7FFF) + ((zb >> 16) & jnp.uint32(1))
    top = rnd & jnp.uint32(0xFFFF0000)
    return top[:, HALF_D:] | (top[:, :HALF_D] >> 16)


def _unpack_bf16_pairs(x_ref, n):
    lo, hi = [], []
    for c in range(PACK_ROWS):
        w = x_ref[pl.ds(c, n, stride=PACK_ROWS), :]
        lo.append(lax.bitcast_convert_type(w << 16, F32))
        hi.append(lax.bitcast_convert_type(w & jnp.uint32(0xFFFF0000), F32))
    return jnp.concatenate(lo + hi, axis=1).astype(BF16)


def _router_kernel(x_ref, g_ref, w_ref, b_ref, h_ref, id_ref, wt_ref, cnt_ref, carry_ref):
    step = pl.program_id(0)

    @pl.when(step == 0)
    def _():
        carry_ref[...] = jnp.zeros_like(carry_ref)

    x = x_ref[...]
    ms = jnp.mean(x * x, axis=-1, keepdims=True)
    z = x * lax.rsqrt(ms + NORM_EPS) * g_ref[...]
    word = _pack_bf16_pairs(z)
    for c in range(PACK_ROWS):
        h_ref[pl.ds(c, x.shape[0], stride=PACK_ROWS), :] = word[:, c * LANES:(c + 1) * LANES]
    zh = z.astype(BF16)
    zl = (z - zh.astype(F32)).astype(BF16)
    w = w_ref[...]
    wh = w.astype(BF16)
    wl = (w - wh.astype(F32)).astype(BF16)
    logits = (jnp.dot(zh, wh, preferred_element_type=F32) + jnp.dot(zl, wh, preferred_element_type=F32)
              + jnp.dot(zh, wl, preferred_element_type=F32)) + b_ref[...]
    lane = lax.broadcasted_iota(jnp.int32, logits.shape, 1)
    lanef = lane.astype(F32)
    neg = jnp.float32(-jnp.inf)
    big = jnp.float32(1e9)
    gl = jnp.where(lane < N_EGROUPS, logits, neg)
    gmax = jnp.max(gl, axis=-1, keepdims=True)
    gidx = jnp.min(jnp.where(gl == gmax, lanef, big), axis=-1, keepdims=True)
    pg = 1.0 / jnp.sum(jnp.exp(gl - gmax), axis=-1, keepdims=True)
    lo = N_EGROUPS + EXPERTS_PER_GROUP * gidx
    el = jnp.where((lanef >= lo) & (lanef < lo + EXPERTS_PER_GROUP), logits, neg)
    t1 = jnp.max(el, axis=-1, keepdims=True)
    j1 = jnp.min(jnp.where(el == t1, lanef, big), axis=-1, keepdims=True)
    el2 = jnp.where(lanef == j1, neg, el)
    t2 = jnp.max(el2, axis=-1, keepdims=True)
    j2 = jnp.min(jnp.where(el2 == t2, lanef, big), axis=-1, keepdims=True)
    e21 = jnp.exp(t2 - t1)
    w1 = pg / (1.0 + e21)
    w2 = pg * e21 / (1.0 + e21)
    e1f = j1 - N_EGROUPS
    e2f = j2 - N_EGROUPS

    tm = x.shape[0]
    oh1 = (lanef == e1f).astype(F32)
    oh2 = (lanef == e2f).astype(F32)
    ri = lax.broadcasted_iota(jnp.int32, (tm, tm), 0)
    ci = lax.broadcasted_iota(jnp.int32, (tm, tm), 1)
    before = (ci < ri).astype(BF16)
    p1 = jnp.dot(before, oh1.astype(BF16), preferred_element_type=F32)
    p2 = jnp.dot(before, oh2.astype(BF16), preferred_element_type=F32)
    carry = carry_ref[...]
    c1 = jnp.sum(oh1, axis=0, keepdims=True)
    c2 = jnp.sum(oh2, axis=0, keepdims=True)
    rank1 = jnp.sum(oh1 * (carry + p1), axis=-1, keepdims=True)
    rank2 = jnp.sum(oh2 * (carry + c1 + p2), axis=-1, keepdims=True)
    carry = carry + c1 + c2
    carry_ref[...] = carry
    cnt_ref[...] = jnp.broadcast_to(carry, cnt_ref.shape).astype(jnp.int32)

    ids = jnp.where(lane == 0, e1f, jnp.where(lane == 1, e2f, jnp.where(lane == 2, rank1, jnp.where(lane == 3, rank2, 0.0))))
    id_ref[...] = ids.astype(jnp.int32)
    wt_ref[...] = jnp.where(lane == 0, w1, jnp.where(lane == 1, w2, 0.0))


def router(x1, gain, w_r, b_r):
    tm = 256
    return pl.pallas_call(
        _router_kernel,
        out_shape=(jax.ShapeDtypeStruct((T * PACK_ROWS, LANES), jnp.uint32),
                   jax.ShapeDtypeStruct((T, LANES), jnp.int32),
                   jax.ShapeDtypeStruct((T, LANES), F32),
                   jax.ShapeDtypeStruct((8, LANES), jnp.int32)),
        grid=(T // tm,),
        in_specs=[pl.BlockSpec((tm, D), lambda i: (i, 0)),
                  pl.BlockSpec((1, D), lambda i: (0, 0)),
                  pl.BlockSpec((D, LANES), lambda i: (0, 0)),
                  pl.BlockSpec((1, LANES), lambda i: (0, 0))],
        out_specs=(pl.BlockSpec((tm * PACK_ROWS, LANES), lambda i: (i, 0)),
                   pl.BlockSpec((tm, LANES), lambda i: (i, 0)),
                   pl.BlockSpec((tm, LANES), lambda i: (i, 0)),
                   pl.BlockSpec((8, LANES), lambda i: (0, 0))),
        scratch_shapes=[pltpu.VMEM((1, LANES), F32)],
        compiler_params=_cparams(("arbitrary",)),
        name="ffn_norm_router",
    )(x1, gain, w_r, b_r)


DISPATCH_TB = 256
N_ZERO_FILLS = 2 * N_EXPERTS
YS_ROWS = D // LANES


def _dispatch_kernel(dest_ref, zs_ref, h_ref, xs_hbm, zbuf, zsem, sem):
    step = pl.program_id(0)

    @pl.when(step == 0)
    def _():
        zbuf[...] = jnp.zeros_like(zbuf)

        def zero_copy(e):
            start = pl.multiple_of(jnp.maximum(zs_ref[e], 0) * PACK_ROWS, PACK_ROWS)
            return pltpu.make_async_copy(zbuf, xs_hbm.at[pl.ds(start, MOE_BLOCK * PACK_ROWS)], zsem.at[0])

        def zstart(e, c):
            @pl.when(zs_ref[e] >= 0)
            def _():
                zero_copy(e).start()
            return c

        def zwait(e, c):
            @pl.when(zs_ref[e] >= 0)
            def _():
                zero_copy(e).wait()
            return c

        lax.fori_loop(0, N_ZERO_FILLS, zstart, 0)
        lax.fori_loop(0, N_ZERO_FILLS, zwait, 0)

    def row_copy(n, k):
        a = (step * DISPATCH_TB + n) * TOP_K + k
        src = h_ref.at[pl.ds(pl.multiple_of(n * PACK_ROWS, PACK_ROWS), PACK_ROWS)]
        dst = xs_hbm.at[pl.ds(pl.multiple_of(dest_ref[a] * PACK_ROWS, PACK_ROWS), PACK_ROWS)]
        return pltpu.make_async_copy(src, dst, sem.at[0])

    def issue(n, c):
        for k in range(TOP_K):
            row_copy(n, k).start(priority=k)
        return c

    def drain(n, c):
        for k in range(TOP_K):
            row_copy(n, k).wait()
        return c

    lax.fori_loop(0, DISPATCH_TB, issue, 0, unroll=8)
    lax.fori_loop(0, DISPATCH_TB, drain, 0, unroll=8)


def dispatch(dest, zero_start, hpk):
    grid_spec = pltpu.PrefetchScalarGridSpec(
        num_scalar_prefetch=2,
        grid=(T // DISPATCH_TB,),
        in_specs=[pl.BlockSpec((DISPATCH_TB * PACK_ROWS, LANES), lambda i, dst, zs: (i, 0))],
        out_specs=pl.BlockSpec(memory_space=pl.ANY),
        scratch_shapes=[pltpu.VMEM((MOE_BLOCK * PACK_ROWS, LANES), jnp.uint32),
                        pltpu.SemaphoreType.DMA((1,)),
                        pltpu.SemaphoreType.DMA((1,))],
    )
    return pl.pallas_call(
        _dispatch_kernel,
        out_shape=jax.ShapeDtypeStruct((MOE_ROWS * PACK_ROWS, LANES), jnp.uint32),
        grid_spec=grid_spec,
        compiler_params=_cparams(("arbitrary",)),
        name="moe_dispatch",
    )(dest, zero_start, hpk)


def _expert_kernel(be_ref, nu_ref, ord_ref, seq_ref, x_ref, wg_hbm, wu_hbm, wd_hbm, y_ref,
                   wg_st, wu_st, wd_st, wg_bf, wu_bf, wd_bf, sem):
    b = pl.program_id(0)

    def weight_copies(e, slot):
        return (pltpu.make_async_copy(wg_hbm.at[e], wg_st.at[slot], sem.at[slot, 0]),
                pltpu.make_async_copy(wu_hbm.at[e], wu_st.at[slot], sem.at[slot, 1]),
                pltpu.make_async_copy(wd_hbm.at[e], wd_st.at[slot], sem.at[slot, 2]))

    def start_fetch(n, slot):
        @pl.when(seq_ref[n] >= 0)
        def _():
            for cp in weight_copies(seq_ref[n], slot):
                cp.start()

    @pl.when(b < nu_ref[0])
    def _():
        n = ord_ref[b]
        slot = n % 2
        changed = jnp.logical_or(b == 0, be_ref[b] != be_ref[jnp.maximum(b - 1, 0)])

        @pl.when(b == 0)
        def _():
            start_fetch(0, 0)
            start_fetch(1, 1)

        @pl.when(changed)
        def _():
            cg, cu, cd = weight_copies(be_ref[b], slot)
            cg.wait()
            wg_bf[...] = wg_st[slot].astype(BF16)
            cu.wait()
            wu_bf[...] = wu_st[slot].astype(BF16)
            cd.wait()
            wd_bf[...] = wd_st[slot].astype(BF16)
            start_fetch(n + 2, slot)

        x = _unpack_bf16_pairs(x_ref, MOE_BLOCK)
        gate = jnp.dot(x, wg_bf[...], preferred_element_type=F32)
        up = jnp.dot(x, wu_bf[...], preferred_element_type=F32)
        hid = (jax.nn.silu(gate) * up).astype(BF16)
        y = jnp.dot(hid, wd_bf[...], preferred_element_type=F32)
        for c in range(YS_ROWS):
            y_ref[pl.ds(c, MOE_BLOCK, stride=YS_ROWS), :] = y[:, c * LANES:(c + 1) * LANES]

    @pl.when(b >= nu_ref[0])
    def _():
        y_ref[...] = jnp.zeros_like(y_ref)


def experts(block_expert, n_used, block_ord, expert_seq, xs, w_gate, w_up, w_down):
    def blk(b, be, nu, od, sq):
        return jnp.minimum(b, nu[0] - 1)

    grid_spec = pltpu.PrefetchScalarGridSpec(
        num_scalar_prefetch=4,
        grid=(MOE_BLOCKS,),
        in_specs=[pl.BlockSpec((MOE_BLOCK * PACK_ROWS, LANES), lambda b, be, nu, od, sq: (blk(b, be, nu, od, sq), 0)),
                  pl.BlockSpec(memory_space=pl.ANY),
                  pl.BlockSpec(memory_space=pl.ANY),
                  pl.BlockSpec(memory_space=pl.ANY)],
        out_specs=pl.BlockSpec((MOE_BLOCK * YS_ROWS, LANES), lambda b, be, nu, od, sq: (b, 0)),
        scratch_shapes=[pltpu.VMEM((2, D, EXPERT_FF), F32),
                        pltpu.VMEM((2, D, EXPERT_FF), F32),
                        pltpu.VMEM((2, EXPERT_FF, D), F32),
                        pltpu.VMEM((D, EXPERT_FF), BF16),
                        pltpu.VMEM((D, EXPERT_FF), BF16),
                        pltpu.VMEM((EXPERT_FF, D), BF16),
                        pltpu.SemaphoreType.DMA((2, 3))],
    )
    return pl.pallas_call(
        _expert_kernel,
        out_shape=jax.ShapeDtypeStruct((MOE_ROWS * YS_ROWS, LANES), F32),
        grid_spec=grid_spec,
        compiler_params=_cparams(("arbitrary",)),
        name="moe_experts",
    )(block_expert, n_used, block_ord, expert_seq, xs, w_gate, w_up, w_down)


COMBINE_ROWS = 256
COMBINE_SUB = 16


def _combine_kernel(dest_ref, ys_hbm, x_ref, wt_ref, g_ref, o_ref, ybuf, sem):
    s = pl.program_id(0)
    ns = pl.num_programs(0)
    slot = s % 2
    tb = COMBINE_SUB
    rows = COMBINE_ROWS

    def row_copy(step, n, k, sl):
        tok = step * rows + n
        src = ys_hbm.at[pl.ds(pl.multiple_of(dest_ref[tok * TOP_K + k] * YS_ROWS, YS_ROWS), YS_ROWS)]
        dst = ybuf.at[sl * TOP_K + k, pl.ds(pl.multiple_of(n * YS_ROWS, YS_ROWS), YS_ROWS)]
        return pltpu.make_async_copy(src, dst, sem.at[sl])

    def issue(step, sl):
        def body(n, c):
            for k in range(TOP_K):
                row_copy(step, n, k, sl).start(priority=k)
            return c
        lax.fori_loop(0, rows, body, 0, unroll=8)

    @pl.when(s == 0)
    def _():
        issue(0, 0)

    @pl.when(s + 1 < ns)
    def _():
        issue(s + 1, 1 - slot)

    def wbody(n, c):
        for k in range(TOP_K):
            row_copy(s, n, k, slot).wait()
        return c
    lax.fori_loop(0, rows, wbody, 0, unroll=8)

    g = g_ref[...]
    y0_ref = ybuf.at[slot * TOP_K]
    y1_ref = ybuf.at[slot * TOP_K + 1]
    for b in range(rows // tb):
        sub = slice(b * tb, (b + 1) * tb)
        w = wt_ref[sub, :]
        y0 = jnp.concatenate([y0_ref[pl.ds(b * tb * YS_ROWS + c, tb, stride=YS_ROWS), :] for c in range(YS_ROWS)], axis=1)
        y1 = jnp.concatenate([y1_ref[pl.ds(b * tb * YS_ROWS + c, tb, stride=YS_ROWS), :] for c in range(YS_ROWS)], axis=1)
        z = x_ref[sub, :] + (w[:, 0:1] * y0 + w[:, 1:2] * y1)
        ms = jnp.mean(z * z, axis=-1, keepdims=True)
        o_ref[sub, :] = z * lax.rsqrt(ms + NORM_EPS) * g


def combine(dest, ys, x1, wts, gain):
    rows = COMBINE_ROWS
    grid_spec = pltpu.PrefetchScalarGridSpec(
        num_scalar_prefetch=1,
        grid=(T // rows,),
        in_specs=[pl.BlockSpec(memory_space=pl.ANY),
                  pl.BlockSpec((rows, D), lambda s, dst: (s, 0)),
                  pl.BlockSpec((rows, LANES), lambda s, dst: (s, 0)),
                  pl.BlockSpec((1, D), lambda s, dst: (0, 0))],
        out_specs=pl.BlockSpec((rows, D), lambda s, dst: (s, 0)),
        scratch_shapes=[pltpu.VMEM((2 * TOP_K, rows * YS_ROWS, LANES), F32),
                        pltpu.SemaphoreType.DMA((2,))],
    )
    return pl.pallas_call(
        _combine_kernel,
        out_shape=jax.ShapeDtypeStruct((T, D), F32),
        grid_spec=grid_spec,
        compiler_params=_cparams(("arbitrary",)),
        name="moe_combine_final_norm",
    )(dest, ys, x1, wts, gain)


def dispatch_plan(ids, counts):
    experts_ = jnp.arange(N_EXPERTS, dtype=jnp.int32)
    padded = (counts + MOE_BLOCK - 1) // MOE_BLOCK * MOE_BLOCK
    pad_end = jnp.cumsum(padded)
    pad_start = pad_end - padded
    e = ids[:, :TOP_K]
    start_of = jnp.sum(jnp.where(e[:, :, None] == experts_[None, None, :], pad_start[None, None, :], 0), axis=-1)
    dest = (start_of + ids[:, TOP_K:2 * TOP_K]).reshape(N_ASSIGN).astype(jnp.int32)
    n_used = pad_end[-1] // MOE_BLOCK
    block_start = jnp.minimum(jnp.arange(MOE_BLOCKS, dtype=jnp.int32), n_used - 1) * MOE_BLOCK
    block_expert = jnp.sum((block_start[:, None] >= pad_end[None, :]).astype(jnp.int32), axis=1)
    block_expert = jnp.minimum(block_expert, N_EXPERTS - 1).astype(jnp.int32)
    tail = n_used + experts_
    zero_start = jnp.concatenate([jnp.where(counts > 0, pad_end - MOE_BLOCK, -1),
                                  jnp.where(tail < MOE_BLOCKS, tail * MOE_BLOCK, -1)]).astype(jnp.int32)
    present = counts > 0
    expert_ord = jnp.cumsum(present.astype(jnp.int32)) - 1
    slots = jnp.arange(N_EXPERTS + 2, dtype=jnp.int32)
    hit = present[None, :] & (expert_ord[None, :] == slots[:, None])
    expert_seq = jnp.where(jnp.any(hit, axis=1), jnp.sum(jnp.where(hit, experts_[None, :], 0), axis=1), -1)
    block_ord = jnp.sum(jnp.where(block_expert[:, None] == experts_[None, :], expert_ord[None, :], 0), axis=1)
    return (block_expert, n_used.astype(jnp.int32).reshape(1), dest, zero_start,
            block_ord.astype(jnp.int32), expert_seq.astype(jnp.int32))


def kernel(x, norm_mix, w_in, b_gate, ssm_a_re, ssm_a_im, ssm_log_step, ssm_b_re, ssm_b_im, ssm_c_re, ssm_c_im, ssm_d, w_glu, b_glu, w_up_attn, w_up_ssm, w_out, norm_ffn, w_router_group, b_router_group, w_router_expert, b_router_expert, w_expert_gate, w_expert_up, w_expert_down, norm_final):
    x2 = x.reshape(T, D)
    h = norm_permute(x2, norm_mix.reshape(1, D)).reshape(T, D)
    w_in_l = w_in.reshape(D, IN_COLS)
    qkv = proj(h, w_in_l, 0, 3 * QKV_COLS, tn=QKV_COLS, name="proj_qkv")
    u = proj(h, w_in_l, 3 * QKV_COLS, SSM_W, tn=512, name="proj_ssm_in")

    attn = attention(qkv.reshape(R, NI, 3 * QKV_COLS)).reshape(T, ATTN_OUT)

    G = SSM_W // SSM_CH
    wb, wc, avec, pows = ssm_params(
        ssm_a_re.reshape(G, SSM_STATE).astype(F32), ssm_a_im.reshape(G, SSM_STATE).astype(F32),
        ssm_log_step.reshape(G),
        ssm_b_re.reshape(G, SSM_STATE, SSM_CH).astype(F32), ssm_b_im.reshape(G, SSM_STATE, SSM_CH).astype(F32),
        ssm_c_re.reshape(G, SSM_CH, SSM_STATE), ssm_c_im.reshape(G, SSM_CH, SSM_STATE))
    yg = ssm_scan(u.reshape(R, NI, SSM_W), wb, wc, avec, pows, ssm_d.reshape(1, SSM_W).astype(F32))
    ssm = glu(yg.reshape(T, SSM_W), w_glu.reshape(SSM_W, SSM_W), b_glu.reshape(1, SSM_W))

    merged = merge(h, attn, ssm, w_in_l, b_gate.reshape(1, 2 * D),
                   w_up_attn.reshape(ATTN_OUT, D), w_up_ssm.reshape(SSM_W, D))
    x1 = out_proj(merged.reshape(R, NI, D), w_out.reshape(D, D), x2)

    w_r = jnp.concatenate([w_router_group.reshape(D, N_EGROUPS), w_router_expert.reshape(D, N_EXPERTS),
                           jnp.zeros((D, LANES - N_EGROUPS - N_EXPERTS), F32)], axis=1)
    b_r = jnp.concatenate([b_router_group.reshape(1, N_EGROUPS), b_router_expert.reshape(1, N_EXPERTS),
                           jnp.zeros((1, LANES - N_EGROUPS - N_EXPERTS), F32)], axis=1)
    hpk, ids, wts, counts = router(x1, norm_ffn.reshape(1, D), w_r, b_r)

    block_expert, n_used, dest, zero_start, block_ord, expert_seq = dispatch_plan(ids[:, :2 * TOP_K], counts[0, :N_EXPERTS])
    xs = dispatch(dest, zero_start, hpk)
    ys = experts(block_expert, n_used, block_ord, expert_seq, xs,
                 w_expert_gate.reshape(N_EXPERTS, D, EXPERT_FF), w_expert_up.reshape(N_EXPERTS, D, EXPERT_FF),
                 w_expert_down.reshape(N_EXPERTS, EXPERT_FF, D))
    out = combine(dest, ys, x1, wts, norm_final.reshape(1, D))
    return out.reshape(1, T, D)
```

```python
import functools
import math

import jax
import jax.numpy as jnp
from jax import lax
from jax.experimental import pallas as pl
from jax.experimental.pallas import tpu as pltpu

F32 = jnp.float32
BF16 = jnp.bfloat16

T = 8192
D = 2048
R = 16
NI = T // R
HEAD_DIM = 64
N_HEAD_SLOTS = 8
DILATIONS = (1, 4, 16)
ATTN_BLOCK = 128
QKV_COLS = 1536
ATTN_OUT = 512
SSM_W = 1024
SSM_STATE = 64
SSM_CH = 16
IN_COLS = 3 * QKV_COLS + SSM_W + 2 * D
N_EXPERTS = 32
N_EGROUPS = 4
EXPERTS_PER_GROUP = 8
TOP_K = 2
EXPERT_FF = 512
NORM_EPS = 1e-6
LANES = 128
VMEM_LIMIT = 48 * 1024 * 1024

MOE_BLOCK = 256
N_ASSIGN = T * TOP_K
MOE_BLOCKS = N_ASSIGN // MOE_BLOCK + N_EXPERTS
MOE_ROWS = MOE_BLOCKS * MOE_BLOCK


def _cparams(sem):
    return pltpu.CompilerParams(dimension_semantics=sem, vmem_limit_bytes=VMEM_LIMIT)


N_SLABS = D // LANES
NORM_TB = 32
NORM_CHUNK = 64


def _norm_permute_kernel(x_ref, g_ref, h_ref, slab):
    g = g_ref[...]

    def chunk(t, c):
        rows = pl.ds(pl.multiple_of(t * NORM_CHUNK, NORM_CHUNK), NORM_CHUNK)
        x = x_ref[rows, :]
        ms = jnp.mean(x * x, axis=-1, keepdims=True)
        hn = x * lax.rsqrt(ms + NORM_EPS) * g
        for s_ in range(N_SLABS):
            slab[s_, rows, :] = hn[:, s_ * LANES:(s_ + 1) * LANES]
        return c

    lax.fori_loop(0, NORM_TB * R // NORM_CHUNK, chunk, 0)
    for r in range(R):
        pieces = [slab[s_, pl.ds(r, NORM_TB, stride=R), :] for s_ in range(N_SLABS)]
        h_ref[r] = jnp.concatenate(pieces, axis=1).astype(BF16)


def norm_permute(x2, gain):
    return pl.pallas_call(
        _norm_permute_kernel,
        out_shape=jax.ShapeDtypeStruct((R, NI, D), BF16),
        grid=(NI // NORM_TB,),
        in_specs=[pl.BlockSpec((NORM_TB * R, D), lambda i: (i, 0)),
                  pl.BlockSpec((1, D), lambda i: (0, 0))],
        out_specs=pl.BlockSpec((R, NORM_TB, D), lambda i: (0, i, 0)),
        scratch_shapes=[pltpu.VMEM((N_SLABS, NORM_TB * R, LANES), F32)],
        compiler_params=_cparams(("arbitrary",)),
        name="norm_permute",
    )(x2, gain)


MXU_N = 256


def _cast_weight_once(w_ref, wbf_ref):
    @pl.when(pl.program_id(1) == 0)
    def _():
        wbf_ref[...] = w_ref[...].astype(BF16)


def _proj_kernel(a_ref, w_ref, o_ref, wbf_ref):
    _cast_weight_once(w_ref, wbf_ref)
    a = a_ref[...]
    for c in range(w_ref.shape[1] // MXU_N):
        cols = slice(c * MXU_N, (c + 1) * MXU_N)
        o_ref[:, cols] = jnp.dot(a, wbf_ref[:, cols], preferred_element_type=F32)


def proj(h, w_in, col_off, n_cols, tn, name="proj"):
    tm = 1024
    off = col_off // tn
    return pl.pallas_call(
        _proj_kernel,
        out_shape=jax.ShapeDtypeStruct((T, n_cols), F32),
        grid=(n_cols // tn, T // tm),
        in_specs=[pl.BlockSpec((tm, D), lambda j, i: (i, 0)),
                  pl.BlockSpec((D, tn), lambda j, i: (0, j + off), pipeline_mode=pl.Buffered(1))],
        out_specs=pl.BlockSpec((tm, tn), lambda j, i: (i, j)),
        scratch_shapes=[pltpu.VMEM((D, tn), BF16)],
        compiler_params=_cparams(("arbitrary", "arbitrary")),
        name=name,
    )(h, w_in)


def _seq_index_maps(d):
    nseg = R // d
    qlen = ATTN_BLOCK // nseg
    return nseg, qlen


def _bias_matrices(d, hp):
    nseg, qlen = _seq_index_maps(d)
    klen = 2 * qlen
    row = lax.broadcasted_iota(jnp.int32, (2 * ATTN_BLOCK, 2 * ATTN_BLOCK), 0)
    col = lax.broadcasted_iota(jnp.int32, (2 * ATTN_BLOCK, 2 * ATTN_BLOCK), 1)
    rho = row % ATTN_BLOCK
    jq = (rho % qlen) * nseg + rho // qlen
    jk = ((col % klen) - qlen) * nseg + col // klen
    steps = jq - jk
    valid = (steps >= 0) & (steps <= ATTN_BLOCK)
    head = 2 * hp + row // ATTN_BLOCK
    slope = lax.bitcast_convert_type((127 - (head + 1)) << 23, F32)
    bias = -slope * (d * steps).astype(F32)
    neg = jnp.float32(-jnp.inf)
    return jnp.where(valid, bias, neg), jnp.where(valid & (jk >= 0), bias, neg)


def _attend_pair(q, k, v, bias):
    lane = lax.broadcasted_iota(jnp.int32, (ATTN_BLOCK, LANES), 1)
    first = lane < HEAD_DIM
    zero = jnp.zeros_like(q)
    q2 = jnp.concatenate([jnp.where(first, q, zero), jnp.where(first, zero, q)], axis=0).astype(BF16)
    s = lax.dot_general(q2, k.astype(BF16), (((1,), (1,)), ((), ())), preferred_element_type=F32)
    s = s + bias
    m = jnp.max(s, axis=-1, keepdims=True)
    p = jnp.exp(s - m)
    l = jnp.sum(p, axis=-1, keepdims=True)
    o2 = jnp.dot(p.astype(BF16), v.astype(BF16), preferred_element_type=F32)
    o2 = o2 / l
    lse = m + jnp.log(l)
    o = jnp.where(first, o2[:ATTN_BLOCK], o2[ATTN_BLOCK:])
    lse_b = jnp.where(first, lse[:ATTN_BLOCK], lse[ATTN_BLOCK:])
    return o, lse_b


def _attn_kernel(q_ref, kp_ref, kc_ref, vp_ref, vc_ref, o_ref, kbuf, vbuf, obuf, lbuf, bias_ref):
    hp = pl.program_id(0)
    it = pl.program_id(1)
    g = pl.program_id(2)
    scale = HEAD_DIM ** -0.5

    kbuf[:, :ATTN_BLOCK, :] = kp_ref[...]
    kbuf[:, ATTN_BLOCK:, :] = kc_ref[...]
    vbuf[:, :ATTN_BLOCK, :] = vp_ref[...]
    vbuf[:, ATTN_BLOCK:, :] = vc_ref[...]

    for gi, d in enumerate(DILATIONS):
        nseg, qlen = _seq_index_maps(d)
        klen = 2 * qlen
        nblk = ATTN_BLOCK // qlen

        @pl.when(g == gi)
        def _(gi=gi, d=d, nseg=nseg, qlen=qlen, klen=klen, nblk=nblk):
            @pl.when(it == 0)
            def _():
                b_reg, b_first = _bias_matrices(d, hp)
                bias_ref[gi, 0] = b_reg
                bias_ref[gi, 1] = b_first

            def block(idx, carry):
                rd = idx // nblk
                bb = idx % nblk
                q0 = pl.multiple_of(bb * qlen, qlen)
                k0 = pl.multiple_of(ATTN_BLOCK + bb * qlen - qlen, qlen)
                qs, ks, vs = [], [], []
                for m_ in range(nseg):
                    rr = rd + d * m_
                    qs.append(q_ref[rr, pl.ds(q0, qlen), :])
                    ks.append(kbuf[rr, pl.ds(k0, klen), :])
                    vs.append(vbuf[rr, pl.ds(k0, klen), :])
                q = jnp.concatenate(qs, axis=0) * scale
                k = jnp.concatenate(ks, axis=0)
                v = jnp.concatenate(vs, axis=0)
                is_first = jnp.logical_and(it == 0, bb == 0)
                bias = bias_ref[gi, jnp.where(is_first, 1, 0)]
                o, lse = _attend_pair(q, k, v, bias)
                for m_ in range(nseg):
                    rr = rd + d * m_
                    obuf[gi, rr, pl.ds(q0, qlen), :] = o[m_ * qlen:(m_ + 1) * qlen]
                    lbuf[gi, rr, pl.ds(q0, qlen), :] = lse[m_ * qlen:(m_ + 1) * qlen]
                return carry

            lax.fori_loop(0, d * nblk, block, 0, unroll=8)

    @pl.when(g == len(DILATIONS) - 1)
    def _():
        for r in range(R):
            l0, l1, l2 = lbuf[0, r], lbuf[1, r], lbuf[2, r]
            mx = jnp.maximum(jnp.maximum(l0, l1), l2)
            e0, e1, e2 = jnp.exp(l0 - mx), jnp.exp(l1 - mx), jnp.exp(l2 - mx)
            den = e0 + e1 + e2
            num = e0 * obuf[0, r] + e1 * obuf[1, r] + e2 * obuf[2, r]
            o_ref[r] = (num / den).astype(BF16)


def attention(qkv3):
    n_hp = N_HEAD_SLOTS // 2
    n_it = NI // ATTN_BLOCK
    ng = len(DILATIONS)
    cb = QKV_COLS // LANES

    def cur(base):
        return pl.BlockSpec((R, ATTN_BLOCK, LANES), lambda hp, it, g: (0, it, base + g * n_hp + hp))

    def prev(base):
        return pl.BlockSpec((R, ATTN_BLOCK, LANES),
                            lambda hp, it, g: (0, jnp.maximum(it - 1, 0), base + g * n_hp + hp))

    return pl.pallas_call(
        _attn_kernel,
        out_shape=jax.ShapeDtypeStruct((R, NI, ATTN_OUT), BF16),
        grid=(n_hp, n_it, ng),
        in_specs=[cur(0), prev(cb), cur(cb), prev(2 * cb), cur(2 * cb)],
        out_specs=pl.BlockSpec((R, ATTN_BLOCK, LANES), lambda hp, it, g: (0, it, hp)),
        scratch_shapes=[pltpu.VMEM((R, 2 * ATTN_BLOCK, LANES), F32),
                        pltpu.VMEM((R, 2 * ATTN_BLOCK, LANES), F32),
                        pltpu.VMEM((ng, R, ATTN_BLOCK, LANES), F32),
                        pltpu.VMEM((ng, R, ATTN_BLOCK, LANES), F32),
                        pltpu.VMEM((ng, 2, 2 * ATTN_BLOCK, 2 * ATTN_BLOCK), F32)],
        compiler_params=_cparams(("arbitrary", "arbitrary", "arbitrary")),
        name="dilated_attention",
    )(qkv3, qkv3, qkv3, qkv3, qkv3)


SSM_SLAB = 256
SSM_SLABS = SSM_W // SSM_SLAB
SLAB_STATES = SSM_SLAB // SSM_CH * SSM_STATE
SSM_TI = 128
SSM_MM_CHUNK = 4


def _ssm_kernel(u_ref, wb_ref, wc_ref, a_ref, pw_ref, dsk_ref, o_ref, s_ref, zs_ref, zc_ref):
    ic = pl.program_id(1)
    ns = SLAB_STATES

    @pl.when(ic == 0)
    def _():
        zc_ref[...] = jnp.zeros_like(zc_ref)

    n_chunks = R // SSM_MM_CHUNK
    n_tiles = SSM_TI // 8
    wb = wb_ref[0]
    wc = wc_ref[0]
    dsk = dsk_ref[...]
    arb = jnp.broadcast_to(a_ref[0, :, :ns], (8, ns))
    aib = jnp.broadcast_to(a_ref[0, :, ns:], (8, ns))


    def bu_chunk(c):
        lo = c * SSM_MM_CHUNK
        uc = u_ref[lo:lo + SSM_MM_CHUNK].reshape(SSM_MM_CHUNK * SSM_TI, SSM_SLAB)
        bu = jnp.dot(uc.astype(BF16), wb, preferred_element_type=F32)
        s_ref[lo:lo + SSM_MM_CHUNK] = bu.reshape(SSM_MM_CHUNK, SSM_TI, 2 * ns)

    def local_chunk(c):
        lo = c * SSM_MM_CHUNK
        first = max(lo, 1)
        for t in range(n_tiles):
            rows = slice(t * 8, (t + 1) * 8)
            pr = s_ref[first - 1, rows, :ns]
            pi = s_ref[first - 1, rows, ns:]
            for r in range(first, lo + SSM_MM_CHUNK):
                nr = s_ref[r, rows, :ns] + (arb * pr - aib * pi)
                ni = s_ref[r, rows, ns:] + (arb * pi + aib * pr)
                s_ref[r, rows, :ns] = nr
                s_ref[r, rows, ns:] = ni
                pr, pi = nr, ni

    bu_chunk(0)
    for c in range(n_chunks):
        if c + 1 < n_chunks:
            bu_chunk(c + 1)
        local_chunk(c)

    a16r = pw_ref[0, R - 1:R, :ns]
    a16i = pw_ref[0, R - 1:R, ns:]

    def zstep(i, z):
        zs_ref[pl.ds(i, 1), :] = z
        e = s_ref[R - 1, pl.ds(i, 1), :]
        zr, zi = z[:, :ns], z[:, ns:]
        nz = jnp.concatenate([a16r * zr - a16i * zi, a16r * zi + a16i * zr], axis=-1)
        return nz + e

    zc_ref[...] = lax.fori_loop(0, SSM_TI, zstep, zc_ref[...])

    def fix_chunk(c):
        lo = c * SSM_MM_CHUNK
        for r in range(lo, lo + SSM_MM_CHUNK):
            prb = jnp.broadcast_to(pw_ref[0, r:r + 1, :ns], (8, ns))
            pib = jnp.broadcast_to(pw_ref[0, r:r + 1, ns:], (8, ns))
            for t in range(n_tiles):
                rows = slice(t * 8, (t + 1) * 8)
                zr = zs_ref[rows, :ns]
                zi = zs_ref[rows, ns:]
                s_ref[r, rows, :ns] = s_ref[r, rows, :ns] + (prb * zr - pib * zi)
                s_ref[r, rows, ns:] = s_ref[r, rows, ns:] + (prb * zi + pib * zr)

    def out_chunk(c):
        lo = c * SSM_MM_CHUNK
        xs = s_ref[lo:lo + SSM_MM_CHUNK].reshape(SSM_MM_CHUNK * SSM_TI, 2 * ns)
        y = jnp.dot(xs.astype(BF16), wc, preferred_element_type=F32)
        y = y.reshape(SSM_MM_CHUNK, SSM_TI, SSM_SLAB) + dsk * u_ref[lo:lo + SSM_MM_CHUNK]
        o_ref[lo:lo + SSM_MM_CHUNK] = jax.nn.gelu(y)

    fix_chunk(0)
    for c in range(n_chunks):
        if c + 1 < n_chunks:
            fix_chunk(c + 1)
        out_chunk(c)


def ssm_scan(u3, wb, wc, avec, pows, dskip):
    ns2 = 2 * SLAB_STATES
    return pl.pallas_call(
        _ssm_kernel,
        out_shape=jax.ShapeDtypeStruct((R, NI, SSM_W), F32),
        grid=(SSM_SLABS, NI // SSM_TI),
        in_specs=[pl.BlockSpec((R, SSM_TI, SSM_SLAB), lambda kb, ic: (0, ic, kb)),
                  pl.BlockSpec((1, SSM_SLAB, ns2), lambda kb, ic: (kb, 0, 0)),
                  pl.BlockSpec((1, ns2, SSM_SLAB), lambda kb, ic: (kb, 0, 0)),
                  pl.BlockSpec((1, 1, ns2), lambda kb, ic: (kb, 0, 0)),
                  pl.BlockSpec((1, R, ns2), lambda kb, ic: (kb, 0, 0)),
                  pl.BlockSpec((1, SSM_SLAB), lambda kb, ic: (0, kb))],
        out_specs=pl.BlockSpec((R, SSM_TI, SSM_SLAB), lambda kb, ic: (0, ic, kb)),
        scratch_shapes=[pltpu.VMEM((R, SSM_TI, ns2), F32),
                        pltpu.VMEM((SSM_TI, ns2), F32),
                        pltpu.VMEM((1, ns2), F32)],
        compiler_params=_cparams(("arbitrary", "arbitrary")),
        name="s5_ssm",
    )(u3, wb, wc, avec, pows, dskip)


def ssm_params(a_re, a_im, log_step, b_re, b_im, c_re, c_im):
    G, P, H = SSM_W // SSM_CH, SSM_STATE, SSM_CH
    gs = SSM_SLAB // SSM_CH
    step = jnp.exp(log_step.astype(F32))[:, None]
    mag = jnp.exp(a_re * step)
    ang = a_im * step
    abar_re, abar_im = mag * jnp.cos(ang), mag * jnp.sin(ang)
    nr, ni = abar_re - 1.0, abar_im
    den = a_re * a_re + a_im * a_im
    f_re = (nr * a_re + ni * a_im) / den
    f_im = (ni * a_re - nr * a_im) / den
    bbar_re = f_re[..., None] * b_re - f_im[..., None] * b_im
    bbar_im = f_re[..., None] * b_im + f_im[..., None] * b_re
    eye = jnp.eye(gs, dtype=F32)

    def in_mat(b):
        b4 = b.reshape(SSM_SLABS, gs, P, H)
        m = jnp.einsum('kgph,gj->kghjp', b4, eye)
        return m.reshape(SSM_SLABS, gs * H, gs * P)

    def out_mat(c):
        c4 = c.reshape(SSM_SLABS, gs, H, P)
        m = jnp.einsum('kghp,gj->kgpjh', c4, eye)
        return m.reshape(SSM_SLABS, gs * P, gs * H)

    wb = jnp.concatenate([in_mat(bbar_re), in_mat(bbar_im)], axis=-1).astype(BF16)
    wc = jnp.concatenate([out_mat(c_re.astype(F32)), -out_mat(c_im.astype(F32))], axis=1).astype(BF16)

    def slab_vec(v):
        return v.reshape(SSM_SLABS, gs * P)

    avec = jnp.concatenate([slab_vec(abar_re), slab_vec(abar_im)], axis=-1)[:, None, :]
    pr, pi = abar_re, abar_im
    prs, pis = [pr], [pi]
    for _ in range(R - 1):
        pr, pi = pr * abar_re - pi * abar_im, pr * abar_im + pi * abar_re
        prs.append(pr)
        pis.append(pi)
    pows = jnp.concatenate([jnp.stack([slab_vec(p) for p in prs], axis=1),
                            jnp.stack([slab_vec(p) for p in pis], axis=1)], axis=-1)
    return wb, wc, avec, pows


def _glu_kernel(a_ref, w_ref, b_ref, o_ref, wbf_ref):
    @pl.when(pl.program_id(0) == 0)
    def _():
        wbf_ref[...] = w_ref[...].astype(BF16)

    a = a_ref[...].astype(BF16)
    for c in range(SSM_W // MXU_N):
        cols = slice(c * MXU_N, (c + 1) * MXU_N)
        acc = jnp.dot(a, wbf_ref[:, cols], preferred_element_type=F32)
        o_ref[:, cols] = (a_ref[:, cols] * jax.nn.sigmoid(acc + b_ref[:, cols])).astype(BF16)


def glu(yg, w_glu, b_glu):
    tm = 512
    return pl.pallas_call(
        _glu_kernel,
        out_shape=jax.ShapeDtypeStruct((T, SSM_W), BF16),
        grid=(T // tm,),
        in_specs=[pl.BlockSpec((tm, SSM_W), lambda i: (i, 0)),
                  pl.BlockSpec((SSM_W, SSM_W), lambda i: (0, 0)),
                  pl.BlockSpec((1, SSM_W), lambda i: (0, 0))],
        out_specs=pl.BlockSpec((tm, SSM_W), lambda i: (i, 0)),
        scratch_shapes=[pltpu.VMEM((SSM_W, SSM_W), BF16)],
        compiler_params=_cparams(("arbitrary",)),
        name="ssm_glu",
    )(yg, w_glu, b_glu)


def _merge_kernel(h_ref, at_ref, ss_ref, wga_ref, wgs_ref, ba_ref, bs_ref, wa_ref, ws_ref, o_ref,
                  wga_bf, wgs_bf, wa_bf, ws_bf):
    _cast_weight_once(wga_ref, wga_bf)
    _cast_weight_once(wgs_ref, wgs_bf)
    _cast_weight_once(wa_ref, wa_bf)
    _cast_weight_once(ws_ref, ws_bf)
    h = h_ref[...]
    at = at_ref[...]
    ss = ss_ref[...]
    for c in range(wa_ref.shape[1] // MXU_N):
        cols = slice(c * MXU_N, (c + 1) * MXU_N)
        ga = jax.nn.sigmoid(jnp.dot(h, wga_bf[:, cols], preferred_element_type=F32) + ba_ref[:, cols])
        a = jnp.dot(at, wa_bf[:, cols], preferred_element_type=F32)
        gs = jax.nn.sigmoid(jnp.dot(h, wgs_bf[:, cols], preferred_element_type=F32) + bs_ref[:, cols])
        s = jnp.dot(ss, ws_bf[:, cols], preferred_element_type=F32)
        o_ref[:, cols] = (ga * a + gs * s).astype(BF16)


def merge(h, attn, ssm, w_in, b_gate, w_up_attn, w_up_ssm):
    tm, tn = 1024, 512
    nj = D // tn
    off_a = (3 * QKV_COLS + SSM_W) // tn
    once = pl.Buffered(1)
    return pl.pallas_call(
        _merge_kernel,
        out_shape=jax.ShapeDtypeStruct((T, D), BF16),
        grid=(nj, T // tm),
        in_specs=[pl.BlockSpec((tm, D), lambda j, i: (i, 0)),
                  pl.BlockSpec((tm, ATTN_OUT), lambda j, i: (i, 0)),
                  pl.BlockSpec((tm, SSM_W), lambda j, i: (i, 0)),
                  pl.BlockSpec((D, tn), lambda j, i: (0, j + off_a), pipeline_mode=once),
                  pl.BlockSpec((D, tn), lambda j, i: (0, j + off_a + nj), pipeline_mode=once),
                  pl.BlockSpec((1, tn), lambda j, i: (0, j)),
                  pl.BlockSpec((1, tn), lambda j, i: (0, j + nj)),
                  pl.BlockSpec((ATTN_OUT, tn), lambda j, i: (0, j), pipeline_mode=once),
                  pl.BlockSpec((SSM_W, tn), lambda j, i: (0, j), pipeline_mode=once)],
        out_specs=pl.BlockSpec((tm, tn), lambda j, i: (i, j)),
        scratch_shapes=[pltpu.VMEM((D, tn), BF16), pltpu.VMEM((D, tn), BF16),
                        pltpu.VMEM((ATTN_OUT, tn), BF16), pltpu.VMEM((SSM_W, tn), BF16)],
        compiler_params=_cparams(("arbitrary", "arbitrary")),
        name="gates_branch_merge",
    )(h, attn, ssm, w_in, w_in, b_gate, b_gate, w_up_attn, w_up_ssm)


OUTPROJ_TB = 64
OUTPROJ_PITCH = OUTPROJ_TB + 8


def _outproj_kernel(m_ref, w_ref, x_ref, o_ref, slab, wbf_ref):
    _cast_weight_once(w_ref, wbf_ref)
    tb = OUTPROJ_TB
    a = m_ref[...].reshape(R * tb, D)
    per = MXU_N // LANES
    for c in range(w_ref.shape[1] // MXU_N):
        acc = jnp.dot(a, wbf_ref[:, c * MXU_N:(c + 1) * MXU_N], preferred_element_type=F32)
        for s_ in range(per):
            lanes = slice(s_ * LANES, (s_ + 1) * LANES)
            for r in range(R):
                slab[c * per + s_, r * OUTPROJ_PITCH:r * OUTPROJ_PITCH + tb, :] = acc[r * tb:(r + 1) * tb, lanes]
        for s_ in range(per):
            lanes = slice((c * per + s_) * LANES, (c * per + s_ + 1) * LANES)
            for i in range(tb):
                rows = slice(i * R, (i + 1) * R)
                o_ref[rows, lanes] = slab[c * per + s_, pl.ds(i, R, stride=OUTPROJ_PITCH), :] + x_ref[rows, lanes]


def out_proj(merged3, w_out, x2):
    tb, tn = OUTPROJ_TB, 512
    return pl.pallas_call(
        _outproj_kernel,
        out_shape=jax.ShapeDtypeStruct((T, D), F32),
        grid=(D // tn, NI // tb),
        in_specs=[pl.BlockSpec((R, tb, D), lambda j, i: (0, i, 0)),
                  pl.BlockSpec((D, tn), lambda j, i: (0, j)),
                  pl.BlockSpec((tb * R, tn), lambda j, i: (i, j))],
        out_specs=pl.BlockSpec((tb * R, tn), lambda j, i: (i, j)),
        scratch_shapes=[pltpu.VMEM((tn // LANES, OUTPROJ_PITCH * R, LANES), F32),
                        pltpu.VMEM((D, tn), BF16)],
        compiler_params=_cparams(("arbitrary", "arbitrary")),
        name="out_proj_residual",
    )(merged3, w_out, x2)


PACK_ROWS = 8
HALF_D = D // 2


def _pack_bf16_pairs(z):
    zb = lax.bitcast_convert_type(z, jnp.uint32)
    rnd = zb + jnp.uint32(0x7FFF) + ((zb >> 16) & jnp.uint32(1))
    top = rnd & jnp.uint32(0xFFFF0000)
    return top[:, HALF_D:] | (top[:, :HALF_D] >> 16)


def _unpack_bf16_pairs(x_ref, first, n):
    lo, hi = [], []
    for c in range(PACK_ROWS):
        w = x_ref[pl.ds(first * PACK_ROWS + c, n, stride=PACK_ROWS), :]
        lo.append(lax.bitcast_convert_type(w << 16, F32))
        hi.append(lax.bitcast_convert_type(w & jnp.uint32(0xFFFF0000), F32))
    return jnp.concatenate(lo + hi, axis=1)


def _router_kernel(x_ref, g_ref, w_ref, b_ref, h_ref, id_ref, wt_ref, cnt_ref, carry_ref):
    step = pl.program_id(0)

    @pl.when(step == 0)
    def _():
        carry_ref[...] = jnp.zeros_like(carry_ref)

    x = x_ref[...]
    ms = jnp.mean(x * x, axis=-1, keepdims=True)
    z = x * lax.rsqrt(ms + NORM_EPS) * g_ref[...]
    word = _pack_bf16_pairs(z)
    for c in range(PACK_ROWS):
        h_ref[pl.ds(c, x.shape[0], stride=PACK_ROWS), :] = word[:, c * LANES:(c + 1) * LANES]
    zh = z.astype(BF16)
    zl = (z - zh.astype(F32)).astype(BF16)
    w = w_ref[...]
    wh = w.astype(BF16)
    wl = (w - wh.astype(F32)).astype(BF16)
    logits = (jnp.dot(zh, wh, preferred_element_type=F32) + jnp.dot(zl, wh, preferred_element_type=F32)
              + jnp.dot(zh, wl, preferred_element_type=F32)) + b_ref[...]
    lane = lax.broadcasted_iota(jnp.int32, logits.shape, 1)
    lanef = lane.astype(F32)
    neg = jnp.float32(-jnp.inf)
    big = jnp.float32(1e9)
    gl = jnp.where(lane < N_EGROUPS, logits, neg)
    gmax = jnp.max(gl, axis=-1, keepdims=True)
    gidx = jnp.min(jnp.where(gl == gmax, lanef, big), axis=-1, keepdims=True)
    pg = 1.0 / jnp.sum(jnp.exp(gl - gmax), axis=-1, keepdims=True)
    lo = N_EGROUPS + EXPERTS_PER_GROUP * gidx
    el = jnp.where((lanef >= lo) & (lanef < lo + EXPERTS_PER_GROUP), logits, neg)
    t1 = jnp.max(el, axis=-1, keepdims=True)
    j1 = jnp.min(jnp.where(el == t1, lanef, big), axis=-1, keepdims=True)
    el2 = jnp.where(lanef == j1, neg, el)
    t2 = jnp.max(el2, axis=-1, keepdims=True)
    j2 = jnp.min(jnp.where(el2 == t2, lanef, big), axis=-1, keepdims=True)
    e21 = jnp.exp(t2 - t1)
    w1 = pg / (1.0 + e21)
    w2 = pg * e21 / (1.0 + e21)
    e1f = j1 - N_EGROUPS
    e2f = j2 - N_EGROUPS

    tm = x.shape[0]
    oh1 = (lanef == e1f).astype(F32)
    oh2 = (lanef == e2f).astype(F32)
    ri = lax.broadcasted_iota(jnp.int32, (tm, tm), 0)
    ci = lax.broadcasted_iota(jnp.int32, (tm, tm), 1)
    before = (ci < ri).astype(BF16)
    p1 = jnp.dot(before, oh1.astype(BF16), preferred_element_type=F32)
    p2 = jnp.dot(before, oh2.astype(BF16), preferred_element_type=F32)
    carry = carry_ref[...]
    c1 = jnp.sum(oh1, axis=0, keepdims=True)
    c2 = jnp.sum(oh2, axis=0, keepdims=True)
    rank1 = jnp.sum(oh1 * (carry + p1), axis=-1, keepdims=True)
    rank2 = jnp.sum(oh2 * (carry + c1 + p2), axis=-1, keepdims=True)
    carry = carry + c1 + c2
    carry_ref[...] = carry
    cnt_ref[...] = jnp.broadcast_to(carry, cnt_ref.shape).astype(jnp.int32)

    ids = jnp.where(lane == 0, e1f, jnp.where(lane == 1, e2f, jnp.where(lane == 2, rank1, jnp.where(lane == 3, rank2, 0.0))))
    id_ref[...] = ids.astype(jnp.int32)
    wt_ref[...] = jnp.where(lane == 0, w1, jnp.where(lane == 1, w2, 0.0))


def router(x1, gain, w_r, b_r):
    tm = 256
    return pl.pallas_call(
        _router_kernel,
        out_shape=(jax.ShapeDtypeStruct((T * PACK_ROWS, LANES), jnp.uint32),
                   jax.ShapeDtypeStruct((T, LANES), jnp.int32),
                   jax.ShapeDtypeStruct((T, LANES), F32),
                   jax.ShapeDtypeStruct((8, LANES), jnp.int32)),
        grid=(T // tm,),
        in_specs=[pl.BlockSpec((tm, D), lambda i: (i, 0)),
                  pl.BlockSpec((1, D), lambda i: (0, 0)),
                  pl.BlockSpec((D, LANES), lambda i: (0, 0)),
                  pl.BlockSpec((1, LANES), lambda i: (0, 0))],
        out_specs=(pl.BlockSpec((tm * PACK_ROWS, LANES), lambda i: (i, 0)),
                   pl.BlockSpec((tm, LANES), lambda i: (i, 0)),
                   pl.BlockSpec((tm, LANES), lambda i: (i, 0)),
                   pl.BlockSpec((8, LANES), lambda i: (0, 0))),
        scratch_shapes=[pltpu.VMEM((1, LANES), F32)],
        compiler_params=_cparams(("arbitrary",)),
        name="ffn_norm_router",
    )(x1, gain, w_r, b_r)


DISPATCH_TB = 256
N_ZERO_FILLS = 2 * N_EXPERTS


def _dispatch_kernel(dest_ref, zs_ref, h_ref, xs_hbm, zbuf, zsem, sem):
    step = pl.program_id(0)

    @pl.when(step == 0)
    def _():
        zbuf[...] = jnp.zeros_like(zbuf)

        def zero_copy(e):
            start = pl.multiple_of(jnp.maximum(zs_ref[e], 0) * PACK_ROWS, PACK_ROWS)
            return pltpu.make_async_copy(zbuf, xs_hbm.at[pl.ds(start, MOE_BLOCK * PACK_ROWS)], zsem.at[0])

        def zstart(e, c):
            @pl.when(zs_ref[e] >= 0)
            def _():
                zero_copy(e).start()
            return c

        def zwait(e, c):
            @pl.when(zs_ref[e] >= 0)
            def _():
                zero_copy(e).wait()
            return c

        lax.fori_loop(0, N_ZERO_FILLS, zstart, 0)
        lax.fori_loop(0, N_ZERO_FILLS, zwait, 0)

    def row_copy(n, k):
        a = (step * DISPATCH_TB + n) * TOP_K + k
        src = h_ref.at[pl.ds(pl.multiple_of(n * PACK_ROWS, PACK_ROWS), PACK_ROWS)]
        dst = xs_hbm.at[pl.ds(pl.multiple_of(dest_ref[a] * PACK_ROWS, PACK_ROWS), PACK_ROWS)]
        return pltpu.make_async_copy(src, dst, sem.at[0])

    def issue(n, c):
        for k in range(TOP_K):
            row_copy(n, k).start(priority=k)
        return c

    def drain(n, c):
        for k in range(TOP_K):
            row_copy(n, k).wait()
        return c

    lax.fori_loop(0, DISPATCH_TB, issue, 0, unroll=8)
    lax.fori_loop(0, DISPATCH_TB, drain, 0, unroll=8)


def dispatch(dest, zero_start, hpk):
    grid_spec = pltpu.PrefetchScalarGridSpec(
        num_scalar_prefetch=2,
        grid=(T // DISPATCH_TB,),
        in_specs=[pl.BlockSpec((DISPATCH_TB * PACK_ROWS, LANES), lambda i, dst, zs: (i, 0))],
        out_specs=pl.BlockSpec(memory_space=pl.ANY),
        scratch_shapes=[pltpu.VMEM((MOE_BLOCK * PACK_ROWS, LANES), jnp.uint32),
                        pltpu.SemaphoreType.DMA((1,)),
                        pltpu.SemaphoreType.DMA((1,))],
    )
    return pl.pallas_call(
        _dispatch_kernel,
        out_shape=jax.ShapeDtypeStruct((MOE_ROWS * PACK_ROWS, LANES), jnp.uint32),
        grid_spec=grid_spec,
        compiler_params=_cparams(("arbitrary",)),
        name="moe_dispatch",
    )(dest, zero_start, hpk)


def _expert_kernel(be_ref, nu_ref, ord_ref, seq_ref, x_ref, wg_hbm, wu_hbm, wd_hbm, y_ref,
                   wg_st, wu_st, wd_st, wg_bf, wu_bf, wd_bf, sem):
    b = pl.program_id(0)

    def weight_copies(e, slot):
        return (pltpu.make_async_copy(wg_hbm.at[e], wg_st.at[slot], sem.at[slot, 0]),
                pltpu.make_async_copy(wu_hbm.at[e], wu_st.at[slot], sem.at[slot, 1]),
                pltpu.make_async_copy(wd_hbm.at[e], wd_st.at[slot], sem.at[slot, 2]))

    def start_fetch(n, slot):
        @pl.when(seq_ref[n] >= 0)
        def _():
            for cp in weight_copies(seq_ref[n], slot):
                cp.start()

    @pl.when(b < nu_ref[0])
    def _():
        n = ord_ref[b]
        slot = n % 2
        changed = jnp.logical_or(b == 0, be_ref[b] != be_ref[jnp.maximum(b - 1, 0)])

        @pl.when(b == 0)
        def _():
            start_fetch(0, 0)
            start_fetch(1, 1)

        @pl.when(changed)
        def _():
            cg, cu, cd = weight_copies(be_ref[b], slot)
            cg.wait()
            wg_bf[...] = wg_st[slot].astype(BF16)
            cu.wait()
            wu_bf[...] = wu_st[slot].astype(BF16)
            cd.wait()
            wd_bf[...] = wd_st[slot].astype(BF16)
            start_fetch(n + 2, slot)

        x = _unpack_bf16_pairs(x_ref, 0, MOE_BLOCK).astype(BF16)
        gate = jnp.dot(x, wg_bf[...], preferred_element_type=F32)
        up = jnp.dot(x, wu_bf[...], preferred_element_type=F32)
        hid = (jax.nn.silu(gate) * up).astype(BF16)
        y = jnp.dot(hid, wd_bf[...], preferred_element_type=F32)
        word = _pack_bf16_pairs(y)
        for c in range(PACK_ROWS):
            y_ref[pl.ds(c, MOE_BLOCK, stride=PACK_ROWS), :] = word[:, c * LANES:(c + 1) * LANES]

    @pl.when(b >= nu_ref[0])
    def _():
        y_ref[...] = jnp.zeros_like(y_ref)


def experts(block_expert, n_used, block_ord, expert_seq, xs, w_gate, w_up, w_down):
    def blk(b, be, nu, od, sq):
        return jnp.minimum(b, nu[0] - 1)

    grid_spec = pltpu.PrefetchScalarGridSpec(
        num_scalar_prefetch=4,
        grid=(MOE_BLOCKS,),
        in_specs=[pl.BlockSpec((MOE_BLOCK * PACK_ROWS, LANES), lambda b, be, nu, od, sq: (blk(b, be, nu, od, sq), 0)),
                  pl.BlockSpec(memory_space=pl.ANY),
                  pl.BlockSpec(memory_space=pl.ANY),
                  pl.BlockSpec(memory_space=pl.ANY)],
        out_specs=pl.BlockSpec((MOE_BLOCK * PACK_ROWS, LANES), lambda b, be, nu, od, sq: (b, 0)),
        scratch_shapes=[pltpu.VMEM((2, D, EXPERT_FF), F32),
                        pltpu.VMEM((2, D, EXPERT_FF), F32),
                        pltpu.VMEM((2, EXPERT_FF, D), F32),
                        pltpu.VMEM((D, EXPERT_FF), BF16),
                        pltpu.VMEM((D, EXPERT_FF), BF16),
                        pltpu.VMEM((EXPERT_FF, D), BF16),
                        pltpu.SemaphoreType.DMA((2, 3))],
    )
    return pl.pallas_call(
        _expert_kernel,
        out_shape=jax.ShapeDtypeStruct((MOE_ROWS * PACK_ROWS, LANES), jnp.uint32),
        grid_spec=grid_spec,
        compiler_params=_cparams(("arbitrary",)),
        name="moe_experts",
    )(block_expert, n_used, block_ord, expert_seq, xs, w_gate, w_up, w_down)


COMBINE_ROWS = 256
COMBINE_SUB = 16


def _combine_kernel(dest_ref, ys_hbm, x_ref, wt_ref, g_ref, o_ref, ybuf, sem):
    s = pl.program_id(0)
    ns = pl.num_programs(0)
    slot = s % 2
    tb = COMBINE_SUB
    rows = COMBINE_ROWS
    nxt = jnp.minimum(s + 1, ns - 1)

    def row_copy(step, n, k, sl):
        tok = step * rows + n
        src = ys_hbm.at[pl.ds(pl.multiple_of(dest_ref[tok * TOP_K + k] * PACK_ROWS, PACK_ROWS), PACK_ROWS)]
        dst = ybuf.at[sl * TOP_K + k, pl.ds(pl.multiple_of(n * PACK_ROWS, PACK_ROWS), PACK_ROWS)]
        return pltpu.make_async_copy(src, dst, sem.at[sl])

    def wait_all(step, sl):
        def body(n, c):
            for k in range(TOP_K):
                row_copy(step, n, k, sl).wait()
            return c
        lax.fori_loop(0, rows, body, 0, unroll=8)

    @pl.when(s == 0)
    def _():
        def body(n, c):
            for k in range(TOP_K):
                row_copy(0, n, k, 0).start(priority=k)
            return c
        lax.fori_loop(0, rows, body, 0, unroll=8)

    wait_all(s, slot)

    g = g_ref[...]
    y0_ref = ybuf.at[slot * TOP_K]
    y1_ref = ybuf.at[slot * TOP_K + 1]
    for b in range(rows // tb):
        for n in range(b * tb, (b + 1) * tb):
            for k in range(TOP_K):
                row_copy(nxt, n, k, 1 - slot).start(priority=k)
        sub = slice(b * tb, (b + 1) * tb)
        w = wt_ref[sub, :]
        y0 = _unpack_bf16_pairs(y0_ref, b * tb, tb)
        y1 = _unpack_bf16_pairs(y1_ref, b * tb, tb)
        z = x_ref[sub, :] + (w[:, 0:1] * y0 + w[:, 1:2] * y1)
        ms = jnp.mean(z * z, axis=-1, keepdims=True)
        o_ref[sub, :] = z * lax.rsqrt(ms + NORM_EPS) * g

    @pl.when(s == ns - 1)
    def _():
        wait_all(nxt, 1 - slot)


def combine(dest, ys, x1, wts, gain):
    rows = COMBINE_ROWS
    grid_spec = pltpu.PrefetchScalarGridSpec(
        num_scalar_prefetch=1,
        grid=(T // rows,),
        in_specs=[pl.BlockSpec(memory_space=pl.ANY),
                  pl.BlockSpec((rows, D), lambda s, dst: (s, 0)),
                  pl.BlockSpec((rows, LANES), lambda s, dst: (s, 0)),
                  pl.BlockSpec((1, D), lambda s, dst: (0, 0))],
        out_specs=pl.BlockSpec((rows, D), lambda s, dst: (s, 0)),
        scratch_shapes=[pltpu.VMEM((2 * TOP_K, rows * PACK_ROWS, LANES), jnp.uint32),
                        pltpu.SemaphoreType.DMA((2,))],
    )
    return pl.pallas_call(
        _combine_kernel,
        out_shape=jax.ShapeDtypeStruct((T, D), F32),
        grid_spec=grid_spec,
        compiler_params=_cparams(("arbitrary",)),
        name="moe_combine_final_norm",
    )(dest, ys, x1, wts, gain)


def dispatch_plan(ids, counts):
    experts_ = jnp.arange(N_EXPERTS, dtype=jnp.int32)
    padded = (counts + MOE_BLOCK - 1) // MOE_BLOCK * MOE_BLOCK
    pad_end = jnp.cumsum(padded)
    pad_start = pad_end - padded
    e = ids[:, :TOP_K]
    start_of = jnp.sum(jnp.where(e[:, :, None] == experts_[None, None, :], pad_start[None, None, :], 0), axis=-1)
    dest = (start_of + ids[:, TOP_K:2 * TOP_K]).reshape(N_ASSIGN).astype(jnp.int32)
    n_used = pad_end[-1] // MOE_BLOCK
    block_start = jnp.minimum(jnp.arange(MOE_BLOCKS, dtype=jnp.int32), n_used - 1) * MOE_BLOCK
    block_expert = jnp.sum((block_start[:, None] >= pad_end[None, :]).astype(jnp.int32), axis=1)
    block_expert = jnp.minimum(block_expert, N_EXPERTS - 1).astype(jnp.int32)
    tail = n_used + experts_
    zero_start = jnp.concatenate([jnp.where(counts > 0, pad_end - MOE_BLOCK, -1),
                                  jnp.where(tail < MOE_BLOCKS, tail * MOE_BLOCK, -1)]).astype(jnp.int32)
    present = counts > 0
    expert_ord = jnp.cumsum(present.astype(jnp.int32)) - 1
    slots = jnp.arange(N_EXPERTS + 2, dtype=jnp.int32)
    hit = present[None, :] & (expert_ord[None, :] == slots[:, None])
    expert_seq = jnp.where(jnp.any(hit, axis=1), jnp.sum(jnp.where(hit, experts_[None, :], 0), axis=1), -1)
    block_ord = jnp.sum(jnp.where(block_expert[:, None] == experts_[None, :], expert_ord[None, :], 0), axis=1)
    return (block_expert, n_used.astype(jnp.int32).reshape(1), dest, zero_start,
            block_ord.astype(jnp.int32), expert_seq.astype(jnp.int32))


def kernel(x, norm_mix, w_in, b_gate, ssm_a_re, ssm_a_im, ssm_log_step, ssm_b_re, ssm_b_im, ssm_c_re, ssm_c_im, ssm_d, w_glu, b_glu, w_up_attn, w_up_ssm, w_out, norm_ffn, w_router_group, b_router_group, w_router_expert, b_router_expert, w_expert_gate, w_expert_up, w_expert_down, norm_final):
    x2 = x.reshape(T, D)
    h = norm_permute(x2, norm_mix.reshape(1, D)).reshape(T, D)
    w_in_l = w_in.reshape(D, IN_COLS)
    qkv = proj(h, w_in_l, 0, 3 * QKV_COLS, tn=QKV_COLS, name="proj_qkv")
    u = proj(h, w_in_l, 3 * QKV_COLS, SSM_W, tn=512, name="proj_ssm_in")

    attn = attention(qkv.reshape(R, NI, 3 * QKV_COLS)).reshape(T, ATTN_OUT)

    G = SSM_W // SSM_CH
    wb, wc, avec, pows = ssm_params(
        ssm_a_re.reshape(G, SSM_STATE).astype(F32), ssm_a_im.reshape(G, SSM_STATE).astype(F32),
        ssm_log_step.reshape(G),
        ssm_b_re.reshape(G, SSM_STATE, SSM_CH).astype(F32), ssm_b_im.reshape(G, SSM_STATE, SSM_CH).astype(F32),
        ssm_c_re.reshape(G, SSM_CH, SSM_STATE), ssm_c_im.reshape(G, SSM_CH, SSM_STATE))
    yg = ssm_scan(u.reshape(R, NI, SSM_W), wb, wc, avec, pows, ssm_d.reshape(1, SSM_W).astype(F32))
    ssm = glu(yg.reshape(T, SSM_W), w_glu.reshape(SSM_W, SSM_W), b_glu.reshape(1, SSM_W))

    merged = merge(h, attn, ssm, w_in_l, b_gate.reshape(1, 2 * D),
                   w_up_attn.reshape(ATTN_OUT, D), w_up_ssm.reshape(SSM_W, D))
    x1 = out_proj(merged.reshape(R, NI, D), w_out.reshape(D, D), x2)

    w_r = jnp.concatenate([w_router_group.reshape(D, N_EGROUPS), w_router_expert.reshape(D, N_EXPERTS),
                           jnp.zeros((D, LANES - N_EGROUPS - N_EXPERTS), F32)], axis=1)
    b_r = jnp.concatenate([b_router_group.reshape(1, N_EGROUPS), b_router_expert.reshape(1, N_EXPERTS),
                           jnp.zeros((1, LANES - N_EGROUPS - N_EXPERTS), F32)], axis=1)
    hpk, ids, wts, counts = router(x1, norm_ffn.reshape(1, D), w_r, b_r)

    block_expert, n_used, dest, zero_start, block_ord, expert_seq = dispatch_plan(ids[:, :2 * TOP_K], counts[0, :N_EXPERTS])
    xs = dispatch(dest, zero_start, hpk)
    ys = experts(block_expert, n_used, block_ord, expert_seq, xs,
                 w_expert_gate.reshape(N_EXPERTS, D, EXPERT_FF), w_expert_up.reshape(N_EXPERTS, D, EXPERT_FF),
                 w_expert_down.reshape(N_EXPERTS, EXPERT_FF, D))
    out = combine(dest, ys, x1, wts, norm_final.reshape(1, D))
    return out.reshape(1, T, D)
```

```python
import functools
import math

import jax
import jax.numpy as jnp
from jax import lax
from jax.experimental import pallas as pl
from jax.experimental.pallas import tpu as pltpu

F32 = jnp.float32
BF16 = jnp.bfloat16

T = 8192
D = 2048
R = 16
NI = T // R
HEAD_DIM = 64
N_HEAD_SLOTS = 8
DILATIONS = (1, 4, 16)
ATTN_BLOCK = 128
QKV_COLS = 1536
ATTN_OUT = 512
SSM_W = 1024
SSM_STATE = 64
SSM_CH = 16
IN_COLS = 3 * QKV_COLS + SSM_W + 2 * D
N_EXPERTS = 32
N_EGROUPS = 4
EXPERTS_PER_GROUP = 8
TOP_K = 2
EXPERT_FF = 512
NORM_EPS = 1e-6
LANES = 128
VMEM_LIMIT = 48 * 1024 * 1024

MOE_BLOCK = 256
N_ASSIGN = T * TOP_K
MOE_BLOCKS = N_ASSIGN // MOE_BLOCK + N_EXPERTS
MOE_ROWS = MOE_BLOCKS * MOE_BLOCK


def _cparams(sem):
    return pltpu.CompilerParams(dimension_semantics=sem, vmem_limit_bytes=VMEM_LIMIT)


N_SLABS = D // LANES
NORM_TB = 32
NORM_CHUNK = 64
NORM_PITCH = R + 8


def _norm_permute_kernel(x_ref, g_ref, h_ref, slab):
    g = g_ref[...]

    def chunk(t, c):
        rows = pl.ds(pl.multiple_of(t * NORM_CHUNK, NORM_CHUNK), NORM_CHUNK)
        x = x_ref[rows, :]
        ms = jnp.mean(x * x, axis=-1, keepdims=True)
        hn = x * lax.rsqrt(ms + NORM_EPS) * g
        for k in range(NORM_CHUNK // R):
            dst = pl.ds(pl.multiple_of((t * (NORM_CHUNK // R) + k) * NORM_PITCH, 8), R)
            for s_ in range(N_SLABS):
                slab[s_, dst, :] = hn[k * R:(k + 1) * R, s_ * LANES:(s_ + 1) * LANES]
        return c

    lax.fori_loop(0, NORM_TB * R // NORM_CHUNK, chunk, 0)
    for r in range(R):
        pieces = [slab[s_, pl.ds(r, NORM_TB, stride=NORM_PITCH), :] for s_ in range(N_SLABS)]
        h_ref[r] = jnp.concatenate(pieces, axis=1).astype(BF16)


def norm_permute(x2, gain):
    return pl.pallas_call(
        _norm_permute_kernel,
        out_shape=jax.ShapeDtypeStruct((R, NI, D), BF16),
        grid=(NI // NORM_TB,),
        in_specs=[pl.BlockSpec((NORM_TB * R, D), lambda i: (i, 0)),
                  pl.BlockSpec((1, D), lambda i: (0, 0))],
        out_specs=pl.BlockSpec((R, NORM_TB, D), lambda i: (0, i, 0)),
        scratch_shapes=[pltpu.VMEM((N_SLABS, NORM_TB * NORM_PITCH, LANES), F32)],
        compiler_params=_cparams(("arbitrary",)),
        name="norm_permute",
    )(x2, gain)


MXU_N = 256


def _cast_weight_once(w_ref, wbf_ref):
    @pl.when(pl.program_id(1) == 0)
    def _():
        wbf_ref[...] = w_ref[...].astype(BF16)


def _proj_kernel(a_ref, w_ref, o_ref, wbf_ref):
    _cast_weight_once(w_ref, wbf_ref)
    a = a_ref[...]
    for c in range(w_ref.shape[1] // MXU_N):
        cols = slice(c * MXU_N, (c + 1) * MXU_N)
        o_ref[:, cols] = jnp.dot(a, wbf_ref[:, cols], preferred_element_type=F32)


def proj(h, w_in, col_off, n_cols, tn, name="proj"):
    tm = 1024
    off = col_off // tn
    return pl.pallas_call(
        _proj_kernel,
        out_shape=jax.ShapeDtypeStruct((T, n_cols), F32),
        grid=(n_cols // tn, T // tm),
        in_specs=[pl.BlockSpec((tm, D), lambda j, i: (i, 0)),
                  pl.BlockSpec((D, tn), lambda j, i: (0, j + off), pipeline_mode=pl.Buffered(1))],
        out_specs=pl.BlockSpec((tm, tn), lambda j, i: (i, j)),
        scratch_shapes=[pltpu.VMEM((D, tn), BF16)],
        compiler_params=_cparams(("arbitrary", "arbitrary")),
        name=name,
    )(h, w_in)


def _seq_index_maps(d):
    nseg = R // d
    qlen = ATTN_BLOCK // nseg
    return nseg, qlen


def _bias_matrices(d, hp):
    nseg, qlen = _seq_index_maps(d)
    klen = 2 * qlen
    row = lax.broadcasted_iota(jnp.int32, (2 * ATTN_BLOCK, 2 * ATTN_BLOCK), 0)
    col = lax.broadcasted_iota(jnp.int32, (2 * ATTN_BLOCK, 2 * ATTN_BLOCK), 1)
    rho = row % ATTN_BLOCK
    jq = (rho % qlen) * nseg + rho // qlen
    jk = ((col % klen) - qlen) * nseg + col // klen
    steps = jq - jk
    valid = (steps >= 0) & (steps <= ATTN_BLOCK)
    head = 2 * hp + row // ATTN_BLOCK
    slope = lax.bitcast_convert_type((127 - (head + 1)) << 23, F32)
    bias = -slope * (d * steps).astype(F32)
    neg = jnp.float32(-jnp.inf)
    return jnp.where(valid, bias, neg), jnp.where(valid & (jk >= 0), bias, neg)


def _attend_pair(q, k, v, bias):
    lane = lax.broadcasted_iota(jnp.int32, (ATTN_BLOCK, LANES), 1)
    first = lane < HEAD_DIM
    zero = jnp.zeros_like(q)
    q2 = jnp.concatenate([jnp.where(first, q, zero), jnp.where(first, zero, q)], axis=0).astype(BF16)
    s = lax.dot_general(q2, k.astype(BF16), (((1,), (1,)), ((), ())), preferred_element_type=F32)
    s = s + bias
    m = jnp.max(s, axis=-1, keepdims=True)
    p = jnp.exp(s - m)
    l = jnp.sum(p, axis=-1, keepdims=True)
    o2 = jnp.dot(p.astype(BF16), v.astype(BF16), preferred_element_type=F32)
    o = jnp.where(first, o2[:ATTN_BLOCK], o2[ATTN_BLOCK:])
    m_b = jnp.where(first, m[:ATTN_BLOCK], m[ATTN_BLOCK:])
    l_b = jnp.where(first, l[:ATTN_BLOCK], l[ATTN_BLOCK:])
    return o, m_b, l_b


def _attn_kernel(q_ref, kp_ref, kc_ref, vp_ref, vc_ref, o_ref, kbuf, vbuf, obuf, mbuf, lbuf, bias_ref):
    hp = pl.program_id(0)
    it = pl.program_id(1)
    g = pl.program_id(2)
    scale = HEAD_DIM ** -0.5

    kbuf[:, :ATTN_BLOCK, :] = kp_ref[...]
    kbuf[:, ATTN_BLOCK:, :] = kc_ref[...]
    vbuf[:, :ATTN_BLOCK, :] = vp_ref[...]
    vbuf[:, ATTN_BLOCK:, :] = vc_ref[...]

    for gi, d in enumerate(DILATIONS):
        nseg, qlen = _seq_index_maps(d)
        klen = 2 * qlen
        nblk = ATTN_BLOCK // qlen

        @pl.when(g == gi)
        def _(gi=gi, d=d, nseg=nseg, qlen=qlen, klen=klen, nblk=nblk):
            @pl.when(it == 0)
            def _():
                b_reg, b_first = _bias_matrices(d, hp)
                bias_ref[gi, 0] = b_reg
                bias_ref[gi, 1] = b_first

            def block(idx, carry):
                rd = idx // nblk
                bb = idx % nblk
                q0 = pl.multiple_of(bb * qlen, qlen)
                k0 = pl.multiple_of(ATTN_BLOCK + bb * qlen - qlen, qlen)
                qs, ks, vs = [], [], []
                for m_ in range(nseg):
                    rr = rd + d * m_
                    qs.append(q_ref[rr, pl.ds(q0, qlen), :])
                    ks.append(kbuf[rr, pl.ds(k0, klen), :])
                    vs.append(vbuf[rr, pl.ds(k0, klen), :])
                q = jnp.concatenate(qs, axis=0) * scale
                k = jnp.concatenate(ks, axis=0)
                v = jnp.concatenate(vs, axis=0)
                is_first = jnp.logical_and(it == 0, bb == 0)
                bias = bias_ref[gi, jnp.where(is_first, 1, 0)]
                o, mx, den = _attend_pair(q, k, v, bias)
                for m_ in range(nseg):
                    rr = rd + d * m_
                    seg = slice(m_ * qlen, (m_ + 1) * qlen)
                    obuf[gi, rr, pl.ds(q0, qlen), :] = o[seg]
                    mbuf[gi, rr, pl.ds(q0, qlen), :] = mx[seg]
                    lbuf[gi, rr, pl.ds(q0, qlen), :] = den[seg]
                return carry

            lax.fori_loop(0, d * nblk, block, 0, unroll=8)

    @pl.when(g == len(DILATIONS) - 1)
    def _():
        for r in range(R):
            m0, m1, m2 = mbuf[0, r], mbuf[1, r], mbuf[2, r]
            mx = jnp.maximum(jnp.maximum(m0, m1), m2)
            e0, e1, e2 = jnp.exp(m0 - mx), jnp.exp(m1 - mx), jnp.exp(m2 - mx)
            den = e0 * lbuf[0, r] + e1 * lbuf[1, r] + e2 * lbuf[2, r]
            num = e0 * obuf[0, r] + e1 * obuf[1, r] + e2 * obuf[2, r]
            o_ref[r] = (num / den).astype(BF16)


def attention(qkv3):
    n_hp = N_HEAD_SLOTS // 2
    n_it = NI // ATTN_BLOCK
    ng = len(DILATIONS)
    cb = QKV_COLS // LANES

    def cur(base):
        return pl.BlockSpec((R, ATTN_BLOCK, LANES), lambda hp, it, g: (0, it, base + g * n_hp + hp))

    def prev(base):
        return pl.BlockSpec((R, ATTN_BLOCK, LANES),
                            lambda hp, it, g: (0, jnp.maximum(it - 1, 0), base + g * n_hp + hp))

    return pl.pallas_call(
        _attn_kernel,
        out_shape=jax.ShapeDtypeStruct((R, NI, ATTN_OUT), BF16),
        grid=(n_hp, n_it, ng),
        in_specs=[cur(0), prev(cb), cur(cb), prev(2 * cb), cur(2 * cb)],
        out_specs=pl.BlockSpec((R, ATTN_BLOCK, LANES), lambda hp, it, g: (0, it, hp)),
        scratch_shapes=[pltpu.VMEM((R, 2 * ATTN_BLOCK, LANES), F32),
                        pltpu.VMEM((R, 2 * ATTN_BLOCK, LANES), F32),
                        pltpu.VMEM((ng, R, ATTN_BLOCK, LANES), F32),
                        pltpu.VMEM((ng, R, ATTN_BLOCK, LANES), F32),
                        pltpu.VMEM((ng, R, ATTN_BLOCK, LANES), F32),
                        pltpu.VMEM((ng, 2, 2 * ATTN_BLOCK, 2 * ATTN_BLOCK), F32)],
        compiler_params=_cparams(("arbitrary", "arbitrary", "arbitrary")),
        name="dilated_attention",
    )(qkv3, qkv3, qkv3, qkv3, qkv3)


SSM_SLAB = 256
SSM_SLABS = SSM_W // SSM_SLAB
SLAB_STATES = SSM_SLAB // SSM_CH * SSM_STATE
SSM_TI = 128
SSM_MM_CHUNK = 4


def _ssm_kernel(u_ref, wb_ref, wc_ref, pw_ref, dsk_ref, o_ref, s_ref, zs_ref, zc_ref):
    ic = pl.program_id(1)
    ns = SLAB_STATES

    @pl.when(ic == 0)
    def _():
        zc_ref[...] = jnp.zeros_like(zc_ref)

    n_chunks = R // SSM_MM_CHUNK
    n_tiles = SSM_TI // 8
    wb = wb_ref[0]
    wc = wc_ref[0]
    dsk = dsk_ref[...]
    arb = jnp.broadcast_to(pw_ref[0, 0:1, :ns], (8, ns))
    aib = jnp.broadcast_to(pw_ref[0, 0:1, ns:], (8, ns))


    def bu_chunk(c):
        lo = c * SSM_MM_CHUNK
        uc = u_ref[lo:lo + SSM_MM_CHUNK].reshape(SSM_MM_CHUNK * SSM_TI, SSM_SLAB)
        bu = jnp.dot(uc.astype(BF16), wb, preferred_element_type=F32)
        s_ref[lo:lo + SSM_MM_CHUNK] = bu.reshape(SSM_MM_CHUNK, SSM_TI, 2 * ns)

    def local_chunk(c):
        lo = c * SSM_MM_CHUNK
        first = max(lo, 1)
        for t in range(n_tiles):
            rows = slice(t * 8, (t + 1) * 8)
            pr = s_ref[first - 1, rows, :ns]
            pi = s_ref[first - 1, rows, ns:]
            for r in range(first, lo + SSM_MM_CHUNK):
                nr = s_ref[r, rows, :ns] + (arb * pr - aib * pi)
                ni = s_ref[r, rows, ns:] + (arb * pi + aib * pr)
                s_ref[r, rows, :ns] = nr
                s_ref[r, rows, ns:] = ni
                pr, pi = nr, ni

    bu_chunk(0)
    for c in range(n_chunks):
        if c + 1 < n_chunks:
            bu_chunk(c + 1)
        local_chunk(c)

    a16r = pw_ref[0, R - 1:R, :ns]
    a16i = pw_ref[0, R - 1:R, ns:]

    def zstep(i, z):
        zs_ref[pl.ds(i, 1), :] = z
        e = s_ref[R - 1, pl.ds(i, 1), :]
        zr, zi = z[:, :ns], z[:, ns:]
        nz = jnp.concatenate([a16r * zr - a16i * zi, a16r * zi + a16i * zr], axis=-1)
        return nz + e

    zc_ref[...] = lax.fori_loop(0, SSM_TI, zstep, zc_ref[...])

    def fix_chunk(c):
        lo = c * SSM_MM_CHUNK
        for r in range(lo, lo + SSM_MM_CHUNK):
            prb = jnp.broadcast_to(pw_ref[0, r:r + 1, :ns], (8, ns))
            pib = jnp.broadcast_to(pw_ref[0, r:r + 1, ns:], (8, ns))
            for t in range(n_tiles):
                rows = slice(t * 8, (t + 1) * 8)
                zr = zs_ref[rows, :ns]
                zi = zs_ref[rows, ns:]
                s_ref[r, rows, :ns] = s_ref[r, rows, :ns] + (prb * zr - pib * zi)
                s_ref[r, rows, ns:] = s_ref[r, rows, ns:] + (prb * zi + pib * zr)

    def out_chunk(c):
        lo = c * SSM_MM_CHUNK
        xs = s_ref[lo:lo + SSM_MM_CHUNK].reshape(SSM_MM_CHUNK * SSM_TI, 2 * ns)
        y = jnp.dot(xs.astype(BF16), wc, preferred_element_type=F32)
        y = y.reshape(SSM_MM_CHUNK, SSM_TI, SSM_SLAB) + dsk * u_ref[lo:lo + SSM_MM_CHUNK]
        o_ref[lo:lo + SSM_MM_CHUNK] = jax.nn.gelu(y)

    fix_chunk(0)
    for c in range(n_chunks):
        if c + 1 < n_chunks:
            fix_chunk(c + 1)
        out_chunk(c)


def ssm_scan(u3, wb, wc, pows, dskip):
    ns2 = 2 * SLAB_STATES
    return pl.pallas_call(
        _ssm_kernel,
        out_shape=jax.ShapeDtypeStruct((R, NI, SSM_W), F32),
        grid=(SSM_SLABS, NI // SSM_TI),
        in_specs=[pl.BlockSpec((R, SSM_TI, SSM_SLAB), lambda kb, ic: (0, ic, kb)),
                  pl.BlockSpec((1, SSM_SLAB, ns2), lambda kb, ic: (kb, 0, 0)),
                  pl.BlockSpec((1, ns2, SSM_SLAB), lambda kb, ic: (kb, 0, 0)),
                  pl.BlockSpec((1, R, ns2), lambda kb, ic: (kb, 0, 0)),
                  pl.BlockSpec((1, SSM_SLAB), lambda kb, ic: (0, kb))],
        out_specs=pl.BlockSpec((R, SSM_TI, SSM_SLAB), lambda kb, ic: (0, ic, kb)),
        scratch_shapes=[pltpu.VMEM((R, SSM_TI, ns2), F32),
                        pltpu.VMEM((SSM_TI, ns2), F32),
                        pltpu.VMEM((1, ns2), F32)],
        compiler_params=_cparams(("arbitrary", "arbitrary")),
        name="s5_ssm",
    )(u3, wb, wc, pows, dskip)


def ssm_params(a_re, a_im, log_step, b_re, b_im, c_re, c_im):
    G, P, H = SSM_W // SSM_CH, SSM_STATE, SSM_CH
    gs = SSM_SLAB // SSM_CH
    step = jnp.exp(log_step.astype(F32))[:, None]
    kk = jnp.arange(1, R + 1, dtype=F32)[:, None, None]
    mag = jnp.exp(kk * (a_re * step))
    ang = kk * (a_im * step)
    pw_re, pw_im = mag * jnp.cos(ang), mag * jnp.sin(ang)
    abar_re, abar_im = pw_re[0], pw_im[0]
    nr, ni = abar_re - 1.0, abar_im
    den = a_re * a_re + a_im * a_im
    f_re = (nr * a_re + ni * a_im) / den
    f_im = (ni * a_re - nr * a_im) / den
    bbar_re = f_re[..., None] * b_re - f_im[..., None] * b_im
    bbar_im = f_re[..., None] * b_im + f_im[..., None] * b_re
    eye = jnp.eye(gs, dtype=F32)
    bb = jnp.stack([bbar_re, bbar_im]).reshape(2, SSM_SLABS, gs, P, H)
    wb = jnp.einsum('rkgph,gj->kghrjp', bb, eye).reshape(SSM_SLABS, gs * H, 2 * gs * P).astype(BF16)
    cc = jnp.stack([c_re.astype(F32), -c_im.astype(F32)]).reshape(2, SSM_SLABS, gs, H, P)
    wc = jnp.einsum('rkghp,gj->krgpjh', cc, eye).reshape(SSM_SLABS, 2 * gs * P, gs * H).astype(BF16)
    pows = jnp.concatenate([pw_re.reshape(R, SSM_SLABS, gs * P), pw_im.reshape(R, SSM_SLABS, gs * P)], axis=-1)
    return wb, wc, jnp.transpose(pows, (1, 0, 2))


def _glu_kernel(a_ref, w_ref, b_ref, o_ref, wbf_ref):
    @pl.when(pl.program_id(0) == 0)
    def _():
        wbf_ref[...] = w_ref[...].astype(BF16)

    a = a_ref[...].astype(BF16)
    for c in range(SSM_W // MXU_N):
        cols = slice(c * MXU_N, (c + 1) * MXU_N)
        acc = jnp.dot(a, wbf_ref[:, cols], preferred_element_type=F32)
        o_ref[:, cols] = (a_ref[:, cols] * jax.nn.sigmoid(acc + b_ref[:, cols])).astype(BF16)


def glu(yg, w_glu, b_glu):
    tm = 512
    return pl.pallas_call(
        _glu_kernel,
        out_shape=jax.ShapeDtypeStruct((T, SSM_W), BF16),
        grid=(T // tm,),
        in_specs=[pl.BlockSpec((tm, SSM_W), lambda i: (i, 0)),
                  pl.BlockSpec((SSM_W, SSM_W), lambda i: (0, 0)),
                  pl.BlockSpec((1, SSM_W), lambda i: (0, 0))],
        out_specs=pl.BlockSpec((tm, SSM_W), lambda i: (i, 0)),
        scratch_shapes=[pltpu.VMEM((SSM_W, SSM_W), BF16)],
        compiler_params=_cparams(("arbitrary",)),
        name="ssm_glu",
    )(yg, w_glu, b_glu)


def _merge_kernel(h_ref, at_ref, ss_ref, wga_ref, wgs_ref, ba_ref, bs_ref, wa_ref, ws_ref, o_ref,
                  wga_bf, wgs_bf, wa_bf, ws_bf):
    _cast_weight_once(wga_ref, wga_bf)
    _cast_weight_once(wgs_ref, wgs_bf)
    _cast_weight_once(wa_ref, wa_bf)
    _cast_weight_once(ws_ref, ws_bf)
    h = h_ref[...]
    at = at_ref[...]
    ss = ss_ref[...]
    for c in range(wa_ref.shape[1] // MXU_N):
        cols = slice(c * MXU_N, (c + 1) * MXU_N)
        ga = jax.nn.sigmoid(jnp.dot(h, wga_bf[:, cols], preferred_element_type=F32) + ba_ref[:, cols])
        a = jnp.dot(at, wa_bf[:, cols], preferred_element_type=F32)
        gs = jax.nn.sigmoid(jnp.dot(h, wgs_bf[:, cols], preferred_element_type=F32) + bs_ref[:, cols])
        s = jnp.dot(ss, ws_bf[:, cols], preferred_element_type=F32)
        o_ref[:, cols] = (ga * a + gs * s).astype(BF16)


def merge(h, attn, ssm, w_in, b_gate, w_up_attn, w_up_ssm):
    tm, tn = 1024, 512
    nj = D // tn
    off_a = (3 * QKV_COLS + SSM_W) // tn
    once = pl.Buffered(1)
    return pl.pallas_call(
        _merge_kernel,
        out_shape=jax.ShapeDtypeStruct((T, D), BF16),
        grid=(nj, T // tm),
        in_specs=[pl.BlockSpec((tm, D), lambda j, i: (i, 0)),
                  pl.BlockSpec((tm, ATTN_OUT), lambda j, i: (i, 0)),
                  pl.BlockSpec((tm, SSM_W), lambda j, i: (i, 0)),
                  pl.BlockSpec((D, tn), lambda j, i: (0, j + off_a), pipeline_mode=once),
                  pl.BlockSpec((D, tn), lambda j, i: (0, j + off_a + nj), pipeline_mode=once),
                  pl.BlockSpec((1, tn), lambda j, i: (0, j)),
                  pl.BlockSpec((1, tn), lambda j, i: (0, j + nj)),
                  pl.BlockSpec((ATTN_OUT, tn), lambda j, i: (0, j), pipeline_mode=once),
                  pl.BlockSpec((SSM_W, tn), lambda j, i: (0, j), pipeline_mode=once)],
        out_specs=pl.BlockSpec((tm, tn), lambda j, i: (i, j)),
        scratch_shapes=[pltpu.VMEM((D, tn), BF16), pltpu.VMEM((D, tn), BF16),
                        pltpu.VMEM((ATTN_OUT, tn), BF16), pltpu.VMEM((SSM_W, tn), BF16)],
        compiler_params=_cparams(("arbitrary", "arbitrary")),
        name="gates_branch_merge",
    )(h, attn, ssm, w_in, w_in, b_gate, b_gate, w_up_attn, w_up_ssm)


OUTPROJ_TB = 64
OUTPROJ_PITCH = OUTPROJ_TB + 8


def _outproj_kernel(m_ref, w_ref, x_ref, o_ref, slab, wbf_ref):
    _cast_weight_once(w_ref, wbf_ref)
    tb = OUTPROJ_TB
    a = m_ref[...].reshape(R * tb, D)
    per = MXU_N // LANES
    for c in range(w_ref.shape[1] // MXU_N):
        acc = jnp.dot(a, wbf_ref[:, c * MXU_N:(c + 1) * MXU_N], preferred_element_type=F32)
        for s_ in range(per):
            lanes = slice(s_ * LANES, (s_ + 1) * LANES)
            for r in range(R):
                slab[c * per + s_, r * OUTPROJ_PITCH:r * OUTPROJ_PITCH + tb, :] = acc[r * tb:(r + 1) * tb, lanes]
        for s_ in range(per):
            lanes = slice((c * per + s_) * LANES, (c * per + s_ + 1) * LANES)
            for i in range(tb):
                rows = slice(i * R, (i + 1) * R)
                o_ref[rows, lanes] = slab[c * per + s_, pl.ds(i, R, stride=OUTPROJ_PITCH), :] + x_ref[rows, lanes]


def out_proj(merged3, w_out, x2):
    tb, tn = OUTPROJ_TB, 512
    return pl.pallas_call(
        _outproj_kernel,
        out_shape=jax.ShapeDtypeStruct((T, D), F32),
        grid=(D // tn, NI // tb),
        in_specs=[pl.BlockSpec((R, tb, D), lambda j, i: (0, i, 0)),
                  pl.BlockSpec((D, tn), lambda j, i: (0, j)),
                  pl.BlockSpec((tb * R, tn), lambda j, i: (i, j))],
        out_specs=pl.BlockSpec((tb * R, tn), lambda j, i: (i, j)),
        scratch_shapes=[pltpu.VMEM((tn // LANES, OUTPROJ_PITCH * R, LANES), F32),
                        pltpu.VMEM((D, tn), BF16)],
        compiler_params=_cparams(("arbitrary", "arbitrary")),
        name="out_proj_residual",
    )(merged3, w_out, x2)


PACK_ROWS = 8
HALF_D = D // 2


def _pack_bf16_pairs(zf):
    top = lax.bitcast_convert_type(zf, jnp.uint32)
    return top[:, HALF_D:] | (top[:, :HALF_D] >> 16)


def _unpack_bf16_pairs(x_ref, first, n):
    lo, hi = [], []
    for c in range(PACK_ROWS):
        w = x_ref[pl.ds(first * PACK_ROWS + c, n, stride=PACK_ROWS), :]
        lo.append(lax.bitcast_convert_type(w << 16, F32))
        hi.append(lax.bitcast_convert_type(w & jnp.uint32(0xFFFF0000), F32))
    return jnp.concatenate(lo + hi, axis=1)


def _router_kernel(x_ref, g_ref, w_ref, b_ref, h_ref, id_ref, wt_ref, cnt_ref, carry_ref):
    step = pl.program_id(0)

    @pl.when(step == 0)
    def _():
        carry_ref[...] = jnp.zeros_like(carry_ref)

    x = x_ref[...]
    ms = jnp.mean(x * x, axis=-1, keepdims=True)
    z = x * lax.rsqrt(ms + NORM_EPS) * g_ref[...]
    zh = z.astype(BF16)
    zf = zh.astype(F32)
    word = _pack_bf16_pairs(zf)
    for c in range(PACK_ROWS):
        h_ref[pl.ds(c, x.shape[0], stride=PACK_ROWS), :] = word[:, c * LANES:(c + 1) * LANES]
    zl = (z - zf).astype(BF16)
    w = w_ref[...]
    wh = w.astype(BF16)
    wl = (w - wh.astype(F32)).astype(BF16)
    logits = (jnp.dot(zh, wh, preferred_element_type=F32) + jnp.dot(zl, wh, preferred_element_type=F32)
              + jnp.dot(zh, wl, preferred_element_type=F32)) + b_ref[...]
    lane = lax.broadcasted_iota(jnp.int32, logits.shape, 1)
    lanef = lane.astype(F32)
    neg = jnp.float32(-jnp.inf)
    big = jnp.float32(1e9)
    gl = jnp.where(lane < N_EGROUPS, logits, neg)
    gmax = jnp.max(gl, axis=-1, keepdims=True)
    gidx = jnp.min(jnp.where(gl == gmax, lanef, big), axis=-1, keepdims=True)
    pg = 1.0 / jnp.sum(jnp.exp(gl - gmax), axis=-1, keepdims=True)
    lo = N_EGROUPS + EXPERTS_PER_GROUP * gidx
    el = jnp.where((lanef >= lo) & (lanef < lo + EXPERTS_PER_GROUP), logits, neg)
    t1 = jnp.max(el, axis=-1, keepdims=True)
    j1 = jnp.min(jnp.where(el == t1, lanef, big), axis=-1, keepdims=True)
    el2 = jnp.where(lanef == j1, neg, el)
    t2 = jnp.max(el2, axis=-1, keepdims=True)
    j2 = jnp.min(jnp.where(el2 == t2, lanef, big), axis=-1, keepdims=True)
    e21 = jnp.exp(t2 - t1)
    w1 = pg / (1.0 + e21)
    w2 = pg * e21 / (1.0 + e21)
    e1f = j1 - N_EGROUPS
    e2f = j2 - N_EGROUPS

    tm = x.shape[0]
    oh1 = (lanef == e1f).astype(F32)
    oh2 = (lanef == e2f).astype(F32)
    ri = lax.broadcasted_iota(jnp.int32, (tm, tm), 0)
    ci = lax.broadcasted_iota(jnp.int32, (tm, tm), 1)
    before = (ci < ri).astype(BF16)
    p1 = jnp.dot(before, oh1.astype(BF16), preferred_element_type=F32)
    p2 = jnp.dot(before, oh2.astype(BF16), preferred_element_type=F32)
    carry = carry_ref[...]
    c1 = jnp.sum(oh1, axis=0, keepdims=True)
    c2 = jnp.sum(oh2, axis=0, keepdims=True)
    rank1 = jnp.sum(oh1 * (carry + p1), axis=-1, keepdims=True)
    rank2 = jnp.sum(oh2 * (carry + c1 + p2), axis=-1, keepdims=True)
    carry = carry + c1 + c2
    carry_ref[...] = carry
    cnt_ref[...] = jnp.broadcast_to(carry, cnt_ref.shape).astype(jnp.int32)

    ids = jnp.where(lane == 0, e1f, jnp.where(lane == 1, e2f, jnp.where(lane == 2, rank1, jnp.where(lane == 3, rank2, 0.0))))
    id_ref[...] = ids.astype(jnp.int32)
    wt_ref[...] = jnp.where(lane == 0, w1, jnp.where(lane == 1, w2, 0.0))


def router(x1, gain, w_r, b_r):
    tm = 256
    return pl.pallas_call(
        _router_kernel,
        out_shape=(jax.ShapeDtypeStruct((T * PACK_ROWS, LANES), jnp.uint32),
                   jax.ShapeDtypeStruct((T, LANES), jnp.int32),
                   jax.ShapeDtypeStruct((T, LANES), F32),
                   jax.ShapeDtypeStruct((8, LANES), jnp.int32)),
        grid=(T // tm,),
        in_specs=[pl.BlockSpec((tm, D), lambda i: (i, 0)),
                  pl.BlockSpec((1, D), lambda i: (0, 0)),
                  pl.BlockSpec((D, LANES), lambda i: (0, 0)),
                  pl.BlockSpec((1, LANES), lambda i: (0, 0))],
        out_specs=(pl.BlockSpec((tm * PACK_ROWS, LANES), lambda i: (i, 0)),
                   pl.BlockSpec((tm, LANES), lambda i: (i, 0)),
                   pl.BlockSpec((tm, LANES), lambda i: (i, 0)),
                   pl.BlockSpec((8, LANES), lambda i: (0, 0))),
        scratch_shapes=[pltpu.VMEM((1, LANES), F32)],
        compiler_params=_cparams(("arbitrary",)),
        name="ffn_norm_router",
    )(x1, gain, w_r, b_r)


DISPATCH_TB = 256
N_ZERO_FILLS = 2 * N_EXPERTS


def _dispatch_kernel(dest_ref, zs_ref, h_ref, xs_hbm, zbuf, zsem, sem):
    step = pl.program_id(0)

    @pl.when(step == 0)
    def _():
        zbuf[...] = jnp.zeros_like(zbuf)

        def zero_copy(e):
            start = pl.multiple_of(jnp.maximum(zs_ref[e], 0) * PACK_ROWS, PACK_ROWS)
            return pltpu.make_async_copy(zbuf, xs_hbm.at[pl.ds(start, MOE_BLOCK * PACK_ROWS)], zsem.at[0])

        def zstart(e, c):
            @pl.when(zs_ref[e] >= 0)
            def _():
                zero_copy(e).start()
            return c

        def zwait(e, c):
            @pl.when(zs_ref[e] >= 0)
            def _():
                zero_copy(e).wait()
            return c

        lax.fori_loop(0, N_ZERO_FILLS, zstart, 0)
        lax.fori_loop(0, N_ZERO_FILLS, zwait, 0)

    def row_copy(n, k):
        a = (step * DISPATCH_TB + n) * TOP_K + k
        src = h_ref.at[pl.ds(pl.multiple_of(n * PACK_ROWS, PACK_ROWS), PACK_ROWS)]
        dst = xs_hbm.at[pl.ds(pl.multiple_of(dest_ref[a] * PACK_ROWS, PACK_ROWS), PACK_ROWS)]
        return pltpu.make_async_copy(src, dst, sem.at[0])

    def issue(n, c):
        for k in range(TOP_K):
            row_copy(n, k).start(priority=k)
        return c

    def drain(n, c):
        for k in range(TOP_K):
            row_copy(n, k).wait()
        return c

    lax.fori_loop(0, DISPATCH_TB, issue, 0, unroll=8)
    lax.fori_loop(0, DISPATCH_TB, drain, 0, unroll=8)


def dispatch(dest, zero_start, hpk):
    grid_spec = pltpu.PrefetchScalarGridSpec(
        num_scalar_prefetch=2,
        grid=(T // DISPATCH_TB,),
        in_specs=[pl.BlockSpec((DISPATCH_TB * PACK_ROWS, LANES), lambda i, dst, zs: (i, 0))],
        out_specs=pl.BlockSpec(memory_space=pl.ANY),
        scratch_shapes=[pltpu.VMEM((MOE_BLOCK * PACK_ROWS, LANES), jnp.uint32),
                        pltpu.SemaphoreType.DMA((1,)),
                        pltpu.SemaphoreType.DMA((1,))],
    )
    return pl.pallas_call(
        _dispatch_kernel,
        out_shape=jax.ShapeDtypeStruct((MOE_ROWS * PACK_ROWS, LANES), jnp.uint32),
        grid_spec=grid_spec,
        compiler_params=_cparams(("arbitrary",)),
        name="moe_dispatch",
    )(dest, zero_start, hpk)


def _expert_kernel(be_ref, nu_ref, ord_ref, seq_ref, x_ref, wg_hbm, wu_hbm, wd_hbm, y_ref,
                   wg_st, wu_st, wd_st, wg_bf, wu_bf, wd_bf, sem):
    b = pl.program_id(0)

    def weight_copies(e, slot):
        return (pltpu.make_async_copy(wg_hbm.at[e], wg_st.at[slot], sem.at[slot, 0]),
                pltpu.make_async_copy(wu_hbm.at[e], wu_st.at[slot], sem.at[slot, 1]),
                pltpu.make_async_copy(wd_hbm.at[e], wd_st.at[slot], sem.at[slot, 2]))

    def start_fetch(n, slot):
        @pl.when(seq_ref[n] >= 0)
        def _():
            for cp in weight_copies(seq_ref[n], slot):
                cp.start()

    @pl.when(b < nu_ref[0])
    def _():
        n = ord_ref[b]
        slot = n % 2
        changed = jnp.logical_or(b == 0, be_ref[b] != be_ref[jnp.maximum(b - 1, 0)])

        @pl.when(b == 0)
        def _():
            start_fetch(0, 0)
            start_fetch(1, 1)

        @pl.when(changed)
        def _():
            cg, cu, cd = weight_copies(be_ref[b], slot)
            cg.wait()
            wg_bf[...] = wg_st[slot].astype(BF16)
            cu.wait()
            wu_bf[...] = wu_st[slot].astype(BF16)
            cd.wait()
            wd_bf[...] = wd_st[slot].astype(BF16)
            start_fetch(n + 2, slot)

        x = _unpack_bf16_pairs(x_ref, 0, MOE_BLOCK).astype(BF16)
        gate = jnp.dot(x, wg_bf[...], preferred_element_type=F32)
        up = jnp.dot(x, wu_bf[...], preferred_element_type=F32)
        hid = (jax.nn.silu(gate) * up).astype(BF16)
        y = jnp.dot(hid, wd_bf[...], preferred_element_type=F32)
        word = _pack_bf16_pairs(y.astype(BF16).astype(F32))
        for c in range(PACK_ROWS):
            y_ref[pl.ds(c, MOE_BLOCK, stride=PACK_ROWS), :] = word[:, c * LANES:(c + 1) * LANES]

    @pl.when(b >= nu_ref[0])
    def _():
        y_ref[...] = jnp.zeros_like(y_ref)


def experts(block_expert, n_used, block_ord, expert_seq, xs, w_gate, w_up, w_down):
    def blk(b, be, nu, od, sq):
        return jnp.minimum(b, nu[0] - 1)

    grid_spec = pltpu.PrefetchScalarGridSpec(
        num_scalar_prefetch=4,
        grid=(MOE_BLOCKS,),
        in_specs=[pl.BlockSpec((MOE_BLOCK * PACK_ROWS, LANES), lambda b, be, nu, od, sq: (blk(b, be, nu, od, sq), 0)),
                  pl.BlockSpec(memory_space=pl.ANY),
                  pl.BlockSpec(memory_space=pl.ANY),
                  pl.BlockSpec(memory_space=pl.ANY)],
        out_specs=pl.BlockSpec((MOE_BLOCK * PACK_ROWS, LANES), lambda b, be, nu, od, sq: (b, 0)),
        scratch_shapes=[pltpu.VMEM((2, D, EXPERT_FF), F32),
                        pltpu.VMEM((2, D, EXPERT_FF), F32),
                        pltpu.VMEM((2, EXPERT_FF, D), F32),
                        pltpu.VMEM((D, EXPERT_FF), BF16),
                        pltpu.VMEM((D, EXPERT_FF), BF16),
                        pltpu.VMEM((EXPERT_FF, D), BF16),
                        pltpu.SemaphoreType.DMA((2, 3))],
    )
    return pl.pallas_call(
        _expert_kernel,
        out_shape=jax.ShapeDtypeStruct((MOE_ROWS * PACK_ROWS, LANES), jnp.uint32),
        grid_spec=grid_spec,
        compiler_params=_cparams(("arbitrary",)),
        name="moe_experts",
    )(block_expert, n_used, block_ord, expert_seq, xs, w_gate, w_up, w_down)


COMBINE_ROWS = 256
COMBINE_SUB = 16


def _combine_kernel(dest_ref, ys_hbm, x_ref, wt_ref, g_ref, o_ref, ybuf, sem):
    s = pl.program_id(0)
    ns = pl.num_programs(0)
    slot = s % 2
    tb = COMBINE_SUB
    rows = COMBINE_ROWS
    nxt = jnp.minimum(s + 1, ns - 1)

    def row_copy(step, n, k, sl):
        tok = step * rows + n
        src = ys_hbm.at[pl.ds(pl.multiple_of(dest_ref[tok * TOP_K + k] * PACK_ROWS, PACK_ROWS), PACK_ROWS)]
        dst = ybuf.at[sl * TOP_K + k, pl.ds(pl.multiple_of(n * PACK_ROWS, PACK_ROWS), PACK_ROWS)]
        return pltpu.make_async_copy(src, dst, sem.at[sl])

    def wait_all(step, sl):
        def body(n, c):
            for k in range(TOP_K):
                row_copy(step, n, k, sl).wait()
            return c
        lax.fori_loop(0, rows, body, 0, unroll=8)

    @pl.when(s == 0)
    def _():
        def body(n, c):
            for k in range(TOP_K):
                row_copy(0, n, k, 0).start(priority=k)
            return c
        lax.fori_loop(0, rows, body, 0, unroll=8)

    wait_all(s, slot)

    g = g_ref[...]
    y0_ref = ybuf.at[slot * TOP_K]
    y1_ref = ybuf.at[slot * TOP_K + 1]
    for b in range(rows // tb):
        for n in range(b * tb, (b + 1) * tb):
            for k in range(TOP_K):
                row_copy(nxt, n, k, 1 - slot).start(priority=k)
        sub = slice(b * tb, (b + 1) * tb)
        w = wt_ref[sub, :]
        y0 = _unpack_bf16_pairs(y0_ref, b * tb, tb)
        y1 = _unpack_bf16_pairs(y1_ref, b * tb, tb)
        z = x_ref[sub, :] + (w[:, 0:1] * y0 + w[:, 1:2] * y1)
        ms = jnp.mean(z * z, axis=-1, keepdims=True)
        o_ref[sub, :] = z * lax.rsqrt(ms + NORM_EPS) * g

    @pl.when(s == ns - 1)
    def _():
        wait_all(nxt, 1 - slot)


def combine(dest, ys, x1, wts, gain):
    rows = COMBINE_ROWS
    grid_spec = pltpu.PrefetchScalarGridSpec(
        num_scalar_prefetch=1,
        grid=(T // rows,),
        in_specs=[pl.BlockSpec(memory_space=pl.ANY),
                  pl.BlockSpec((rows, D), lambda s, dst: (s, 0)),
                  pl.BlockSpec((rows, LANES), lambda s, dst: (s, 0)),
                  pl.BlockSpec((1, D), lambda s, dst: (0, 0))],
        out_specs=pl.BlockSpec((rows, D), lambda s, dst: (s, 0)),
        scratch_shapes=[pltpu.VMEM((2 * TOP_K, rows * PACK_ROWS, LANES), jnp.uint32),
                        pltpu.SemaphoreType.DMA((2,))],
    )
    return pl.pallas_call(
        _combine_kernel,
        out_shape=jax.ShapeDtypeStruct((T, D), F32),
        grid_spec=grid_spec,
        compiler_params=_cparams(("arbitrary",)),
        name="moe_combine_final_norm",
    )(dest, ys, x1, wts, gain)


def dispatch_plan(ids, counts):
    experts_ = jnp.arange(N_EXPERTS, dtype=jnp.int32)
    padded = (counts + MOE_BLOCK - 1) // MOE_BLOCK * MOE_BLOCK
    pad_end = jnp.cumsum(padded)
    pad_start = pad_end - padded
    e = ids[:, :TOP_K]
    start_of = jnp.sum(jnp.where(e[:, :, None] == experts_[None, None, :], pad_start[None, None, :], 0), axis=-1)
    dest = (start_of + ids[:, TOP_K:2 * TOP_K]).reshape(N_ASSIGN).astype(jnp.int32)
    n_used = pad_end[-1] // MOE_BLOCK
    block_start = jnp.minimum(jnp.arange(MOE_BLOCKS, dtype=jnp.int32), n_used - 1) * MOE_BLOCK
    block_expert = jnp.sum((block_start[:, None] >= pad_end[None, :]).astype(jnp.int32), axis=1)
    block_expert = jnp.minimum(block_expert, N_EXPERTS - 1).astype(jnp.int32)
    tail = n_used + experts_
    zero_start = jnp.concatenate([jnp.where(counts > 0, pad_end - MOE_BLOCK, -1),
                                  jnp.where(tail < MOE_BLOCKS, tail * MOE_BLOCK, -1)]).astype(jnp.int32)
    present = counts > 0
    expert_ord = jnp.cumsum(present.astype(jnp.int32)) - 1
    slots = jnp.arange(N_EXPERTS + 2, dtype=jnp.int32)
    hit = present[None, :] & (expert_ord[None, :] == slots[:, None])
    expert_seq = jnp.where(jnp.any(hit, axis=1), jnp.sum(jnp.where(hit, experts_[None, :], 0), axis=1), -1)
    block_ord = jnp.sum(jnp.where(block_expert[:, None] == experts_[None, :], expert_ord[None, :], 0), axis=1)
    return (block_expert, n_used.astype(jnp.int32).reshape(1), dest, zero_start,
            block_ord.astype(jnp.int32), expert_seq.astype(jnp.int32))


def kernel(x, norm_mix, w_in, b_gate, ssm_a_re, ssm_a_im, ssm_log_step, ssm_b_re, ssm_b_im, ssm_c_re, ssm_c_im, ssm_d, w_glu, b_glu, w_up_attn, w_up_ssm, w_out, norm_ffn, w_router_group, b_router_group, w_router_expert, b_router_expert, w_expert_gate, w_expert_up, w_expert_down, norm_final):
    x2 = x.reshape(T, D)
    h = norm_permute(x2, norm_mix.reshape(1, D)).reshape(T, D)
    w_in_l = w_in.reshape(D, IN_COLS)
    qkv = proj(h, w_in_l, 0, 3 * QKV_COLS, tn=QKV_COLS, name="proj_qkv")
    u = proj(h, w_in_l, 3 * QKV_COLS, SSM_W, tn=512, name="proj_ssm_in")

    attn = attention(qkv.reshape(R, NI, 3 * QKV_COLS)).reshape(T, ATTN_OUT)

    G = SSM_W // SSM_CH
    wb, wc, pows = ssm_params(
        ssm_a_re.reshape(G, SSM_STATE).astype(F32), ssm_a_im.reshape(G, SSM_STATE).astype(F32),
        ssm_log_step.reshape(G),
        ssm_b_re.reshape(G, SSM_STATE, SSM_CH).astype(F32), ssm_b_im.reshape(G, SSM_STATE, SSM_CH).astype(F32),
        ssm_c_re.reshape(G, SSM_CH, SSM_STATE), ssm_c_im.reshape(G, SSM_CH, SSM_STATE))
    yg = ssm_scan(u.reshape(R, NI, SSM_W), wb, wc, pows, ssm_d.reshape(1, SSM_W).astype(F32))
    ssm = glu(yg.reshape(T, SSM_W), w_glu.reshape(SSM_W, SSM_W), b_glu.reshape(1, SSM_W))

    merged = merge(h, attn, ssm, w_in_l, b_gate.reshape(1, 2 * D),
                   w_up_attn.reshape(ATTN_OUT, D), w_up_ssm.reshape(SSM_W, D))
    x1 = out_proj(merged.reshape(R, NI, D), w_out.reshape(D, D), x2)

    w_r = jnp.concatenate([w_router_group.reshape(D, N_EGROUPS), w_router_expert.reshape(D, N_EXPERTS),
                           jnp.zeros((D, LANES - N_EGROUPS - N_EXPERTS), F32)], axis=1)
    b_r = jnp.concatenate([b_router_group.reshape(1, N_EGROUPS), b_router_expert.reshape(1, N_EXPERTS),
                           jnp.zeros((1, LANES - N_EGROUPS - N_EXPERTS), F32)], axis=1)
    hpk, ids, wts, counts = router(x1, norm_ffn.reshape(1, D), w_r, b_r)

    block_expert, n_used, dest, zero_start, block_ord, expert_seq = dispatch_plan(ids[:, :2 * TOP_K], counts[0, :N_EXPERTS])
    xs = dispatch(dest, zero_start, hpk)
    ys = experts(block_expert, n_used, block_ord, expert_seq, xs,
                 w_expert_gate.reshape(N_EXPERTS, D, EXPERT_FF), w_expert_up.reshape(N_EXPERTS, D, EXPERT_FF),
                 w_expert_down.reshape(N_EXPERTS, EXPERT_FF, D))
    out = combine(dest, ys, x1, wts, norm_final.reshape(1, D))
    return out.reshape(1, T, D)
```

```python
import functools
import math

import jax
import jax.numpy as jnp
from jax import lax
from jax.experimental import pallas as pl
from jax.experimental.pallas import tpu as pltpu

F32 = jnp.float32
BF16 = jnp.bfloat16

T = 8192
D = 2048
R = 16
NI = T // R
HEAD_DIM = 64
N_HEAD_SLOTS = 8
DILATIONS = (1, 4, 16)
ATTN_BLOCK = 128
QKV_COLS = 1536
ATTN_OUT = 512
SSM_W = 1024
SSM_STATE = 64
SSM_CH = 16
IN_COLS = 3 * QKV_COLS + SSM_W + 2 * D
N_EXPERTS = 32
N_EGROUPS = 4
EXPERTS_PER_GROUP = 8
TOP_K = 2
EXPERT_FF = 512
NORM_EPS = 1e-6
LANES = 128
VMEM_LIMIT = 48 * 1024 * 1024

MOE_BLOCK = 256
N_ASSIGN = T * TOP_K
MOE_BLOCKS = N_ASSIGN // MOE_BLOCK + N_EXPERTS
MOE_ROWS = MOE_BLOCKS * MOE_BLOCK


def _cparams(sem):
    return pltpu.CompilerParams(dimension_semantics=sem, vmem_limit_bytes=VMEM_LIMIT)


N_SLABS = D // LANES
NORM_TB = 32
NORM_CHUNK = 64
NORM_PITCH = R + 8


def _norm_permute_kernel(x_ref, g_ref, h_ref, slab):
    g = g_ref[...]

    def chunk(t, c):
        rows = pl.ds(pl.multiple_of(t * NORM_CHUNK, NORM_CHUNK), NORM_CHUNK)
        x = x_ref[rows, :]
        ms = jnp.mean(x * x, axis=-1, keepdims=True)
        hn = x * lax.rsqrt(ms + NORM_EPS) * g
        for k in range(NORM_CHUNK // R):
            dst = pl.ds(pl.multiple_of((t * (NORM_CHUNK // R) + k) * NORM_PITCH, 8), R)
            for s_ in range(N_SLABS):
                slab[s_, dst, :] = hn[k * R:(k + 1) * R, s_ * LANES:(s_ + 1) * LANES]
        return c

    lax.fori_loop(0, NORM_TB * R // NORM_CHUNK, chunk, 0)
    for r in range(R):
        pieces = [slab[s_, pl.ds(r, NORM_TB, stride=NORM_PITCH), :] for s_ in range(N_SLABS)]
        h_ref[r] = jnp.concatenate(pieces, axis=1).astype(BF16)


def norm_permute(x2, gain):
    return pl.pallas_call(
        _norm_permute_kernel,
        out_shape=jax.ShapeDtypeStruct((R, NI, D), BF16),
        grid=(NI // NORM_TB,),
        in_specs=[pl.BlockSpec((NORM_TB * R, D), lambda i: (i, 0)),
                  pl.BlockSpec((1, D), lambda i: (0, 0))],
        out_specs=pl.BlockSpec((R, NORM_TB, D), lambda i: (0, i, 0)),
        scratch_shapes=[pltpu.VMEM((N_SLABS, NORM_TB * NORM_PITCH, LANES), F32)],
        compiler_params=_cparams(("arbitrary",)),
        name="norm_permute",
    )(x2, gain)


MXU_N = 256


def _cast_weight_once(w_ref, wbf_ref):
    @pl.when(pl.program_id(1) == 0)
    def _():
        wbf_ref[...] = w_ref[...].astype(BF16)


def _proj_kernel(a_ref, w_ref, o_ref, wbf_ref):
    _cast_weight_once(w_ref, wbf_ref)
    a = a_ref[...]
    for c in range(w_ref.shape[1] // MXU_N):
        cols = slice(c * MXU_N, (c + 1) * MXU_N)
        o_ref[:, cols] = jnp.dot(a, wbf_ref[:, cols], preferred_element_type=F32)


def proj(h, w_in, col_off, n_cols, tn, name="proj"):
    tm = 1024
    off = col_off // tn
    return pl.pallas_call(
        _proj_kernel,
        out_shape=jax.ShapeDtypeStruct((T, n_cols), F32),
        grid=(n_cols // tn, T // tm),
        in_specs=[pl.BlockSpec((tm, D), lambda j, i: (i, 0)),
                  pl.BlockSpec((D, tn), lambda j, i: (0, j + off), pipeline_mode=pl.Buffered(1))],
        out_specs=pl.BlockSpec((tm, tn), lambda j, i: (i, j)),
        scratch_shapes=[pltpu.VMEM((D, tn), BF16)],
        compiler_params=_cparams(("arbitrary", "arbitrary")),
        name=name,
    )(h, w_in)


def _seq_index_maps(d):
    nseg = R // d
    qlen = ATTN_BLOCK // nseg
    return nseg, qlen


def _bias_matrices(d, hp):
    nseg, qlen = _seq_index_maps(d)
    klen = 2 * qlen
    row = lax.broadcasted_iota(jnp.int32, (2 * ATTN_BLOCK, 2 * ATTN_BLOCK), 0)
    col = lax.broadcasted_iota(jnp.int32, (2 * ATTN_BLOCK, 2 * ATTN_BLOCK), 1)
    rho = row % ATTN_BLOCK
    jq = (rho % qlen) * nseg + rho // qlen
    jk = ((col % klen) - qlen) * nseg + col // klen
    steps = jq - jk
    valid = (steps >= 0) & (steps <= ATTN_BLOCK)
    head = 2 * hp + row // ATTN_BLOCK
    slope = lax.bitcast_convert_type((127 - (head + 1)) << 23, F32)
    bias = -slope * (d * steps).astype(F32)
    neg = jnp.float32(-jnp.inf)
    return jnp.where(valid, bias, neg), jnp.where(valid & (jk >= 0), bias, neg)


def _attend_pair(q, k, v, bias):
    lane = lax.broadcasted_iota(jnp.int32, (ATTN_BLOCK, LANES), 1)
    first = lane < HEAD_DIM
    zero = jnp.zeros_like(q)
    q2 = jnp.concatenate([jnp.where(first, q, zero), jnp.where(first, zero, q)], axis=0).astype(BF16)
    s = lax.dot_general(q2, k.astype(BF16), (((1,), (1,)), ((), ())), preferred_element_type=F32)
    s = s + bias
    m = jnp.max(s, axis=-1, keepdims=True)
    p = jnp.exp(s - m)
    l = jnp.sum(p, axis=-1, keepdims=True)
    o2 = jnp.dot(p.astype(BF16), v.astype(BF16), preferred_element_type=F32)
    o = jnp.where(first, o2[:ATTN_BLOCK], o2[ATTN_BLOCK:])
    m_b = jnp.where(first, m[:ATTN_BLOCK], m[ATTN_BLOCK:])
    l_b = jnp.where(first, l[:ATTN_BLOCK], l[ATTN_BLOCK:])
    return o, m_b, l_b


def _attn_kernel(q_ref, kp_ref, kc_ref, vp_ref, vc_ref, o_ref, kbuf, vbuf, obuf, mbuf, lbuf, bias_ref):
    hp = pl.program_id(0)
    it = pl.program_id(1)
    g = pl.program_id(2)
    scale = HEAD_DIM ** -0.5

    kbuf[:, :ATTN_BLOCK, :] = kp_ref[...]
    kbuf[:, ATTN_BLOCK:, :] = kc_ref[...]
    vbuf[:, :ATTN_BLOCK, :] = vp_ref[...]
    vbuf[:, ATTN_BLOCK:, :] = vc_ref[...]

    for gi, d in enumerate(DILATIONS):
        nseg, qlen = _seq_index_maps(d)
        klen = 2 * qlen
        nblk = ATTN_BLOCK // qlen

        @pl.when(g == gi)
        def _(gi=gi, d=d, nseg=nseg, qlen=qlen, klen=klen, nblk=nblk):
            @pl.when(it == 0)
            def _():
                b_reg, b_first = _bias_matrices(d, hp)
                bias_ref[gi, 0] = b_reg
                bias_ref[gi, 1] = b_first

            def block(idx, carry):
                rd = idx // nblk
                bb = idx % nblk
                q0 = pl.multiple_of(bb * qlen, qlen)
                k0 = pl.multiple_of(ATTN_BLOCK + bb * qlen - qlen, qlen)
                qs, ks, vs = [], [], []
                for m_ in range(nseg):
                    rr = rd + d * m_
                    qs.append(q_ref[rr, pl.ds(q0, qlen), :])
                    ks.append(kbuf[rr, pl.ds(k0, klen), :])
                    vs.append(vbuf[rr, pl.ds(k0, klen), :])
                q = jnp.concatenate(qs, axis=0) * scale
                k = jnp.concatenate(ks, axis=0)
                v = jnp.concatenate(vs, axis=0)
                is_first = jnp.logical_and(it == 0, bb == 0)
                bias = bias_ref[gi, jnp.where(is_first, 1, 0)]
                o, mx, den = _attend_pair(q, k, v, bias)
                for m_ in range(nseg):
                    rr = rd + d * m_
                    seg = slice(m_ * qlen, (m_ + 1) * qlen)
                    obuf[gi, rr, pl.ds(q0, qlen), :] = o[seg]
                    mbuf[gi, rr, pl.ds(q0, qlen), :] = mx[seg]
                    lbuf[gi, rr, pl.ds(q0, qlen), :] = den[seg]
                return carry

            lax.fori_loop(0, d * nblk, block, 0, unroll=True)

    @pl.when(g == len(DILATIONS) - 1)
    def _():
        for r in range(R):
            m0, m1, m2 = mbuf[0, r], mbuf[1, r], mbuf[2, r]
            mx = jnp.maximum(jnp.maximum(m0, m1), m2)
            e0, e1, e2 = jnp.exp(m0 - mx), jnp.exp(m1 - mx), jnp.exp(m2 - mx)
            den = e0 * lbuf[0, r] + e1 * lbuf[1, r] + e2 * lbuf[2, r]
            num = e0 * obuf[0, r] + e1 * obuf[1, r] + e2 * obuf[2, r]
            o_ref[r] = (num / den).astype(BF16)


def attention(qkv3):
    n_hp = N_HEAD_SLOTS // 2
    n_it = NI // ATTN_BLOCK
    ng = len(DILATIONS)
    cb = QKV_COLS // LANES

    def cur(base):
        return pl.BlockSpec((R, ATTN_BLOCK, LANES), lambda hp, it, g: (0, it, base + g * n_hp + hp))

    def prev(base):
        return pl.BlockSpec((R, ATTN_BLOCK, LANES),
                            lambda hp, it, g: (0, jnp.maximum(it - 1, 0), base + g * n_hp + hp))

    return pl.pallas_call(
        _attn_kernel,
        out_shape=jax.ShapeDtypeStruct((R, NI, ATTN_OUT), BF16),
        grid=(n_hp, n_it, ng),
        in_specs=[cur(0), prev(cb), cur(cb), prev(2 * cb), cur(2 * cb)],
        out_specs=pl.BlockSpec((R, ATTN_BLOCK, LANES), lambda hp, it, g: (0, it, hp)),
        scratch_shapes=[pltpu.VMEM((R, 2 * ATTN_BLOCK, LANES), F32),
                        pltpu.VMEM((R, 2 * ATTN_BLOCK, LANES), F32),
                        pltpu.VMEM((ng, R, ATTN_BLOCK, LANES), F32),
                        pltpu.VMEM((ng, R, ATTN_BLOCK, LANES), F32),
                        pltpu.VMEM((ng, R, ATTN_BLOCK, LANES), F32),
                        pltpu.VMEM((ng, 2, 2 * ATTN_BLOCK, 2 * ATTN_BLOCK), F32)],
        compiler_params=_cparams(("arbitrary", "arbitrary", "arbitrary")),
        name="dilated_attention",
    )(qkv3, qkv3, qkv3, qkv3, qkv3)


SSM_SLAB = 256
SSM_SLABS = SSM_W // SSM_SLAB
SLAB_STATES = SSM_SLAB // SSM_CH * SSM_STATE
SSM_TI = 128
SSM_MM_CHUNK = 4


def _ssm_kernel(u_ref, wb_ref, wc_ref, pw_ref, dsk_ref, o_ref, s_ref, zs_ref, zc_ref):
    ic = pl.program_id(1)
    ns = SLAB_STATES

    @pl.when(ic == 0)
    def _():
        zc_ref[...] = jnp.zeros_like(zc_ref)

    n_chunks = R // SSM_MM_CHUNK
    n_tiles = SSM_TI // 8
    wb = wb_ref[0]
    wc = wc_ref[0]
    dsk = dsk_ref[...]
    arb = jnp.broadcast_to(pw_ref[0, 0:1, :ns], (8, ns))
    aib = jnp.broadcast_to(pw_ref[0, 0:1, ns:], (8, ns))


    def bu_chunk(c):
        lo = c * SSM_MM_CHUNK
        uc = u_ref[lo:lo + SSM_MM_CHUNK].reshape(SSM_MM_CHUNK * SSM_TI, SSM_SLAB)
        bu = jnp.dot(uc.astype(BF16), wb, preferred_element_type=F32)
        s_ref[lo:lo + SSM_MM_CHUNK] = bu.reshape(SSM_MM_CHUNK, SSM_TI, 2 * ns)

    def local_chunk(c):
        lo = c * SSM_MM_CHUNK
        first = max(lo, 1)
        for t in range(n_tiles):
            rows = slice(t * 8, (t + 1) * 8)
            pr = s_ref[first - 1, rows, :ns]
            pi = s_ref[first - 1, rows, ns:]
            for r in range(first, lo + SSM_MM_CHUNK):
                nr = s_ref[r, rows, :ns] + (arb * pr - aib * pi)
                ni = s_ref[r, rows, ns:] + (arb * pi + aib * pr)
                s_ref[r, rows, :ns] = nr
                s_ref[r, rows, ns:] = ni
                pr, pi = nr, ni

    bu_chunk(0)
    for c in range(n_chunks):
        if c + 1 < n_chunks:
            bu_chunk(c + 1)
        local_chunk(c)

    a16r = pw_ref[0, R - 1:R, :ns]
    a16i = pw_ref[0, R - 1:R, ns:]

    def zstep(i, z):
        zs_ref[pl.ds(i, 1), :] = z
        e = s_ref[R - 1, pl.ds(i, 1), :]
        zr, zi = z[:, :ns], z[:, ns:]
        nz = jnp.concatenate([a16r * zr - a16i * zi, a16r * zi + a16i * zr], axis=-1)
        return nz + e

    zc_ref[...] = lax.fori_loop(0, SSM_TI, zstep, zc_ref[...])

    def fix_chunk(c):
        lo = c * SSM_MM_CHUNK
        for r in range(lo, lo + SSM_MM_CHUNK):
            prb = jnp.broadcast_to(pw_ref[0, r:r + 1, :ns], (8, ns))
            pib = jnp.broadcast_to(pw_ref[0, r:r + 1, ns:], (8, ns))
            for t in range(n_tiles):
                rows = slice(t * 8, (t + 1) * 8)
                zr = zs_ref[rows, :ns]
                zi = zs_ref[rows, ns:]
                s_ref[r, rows, :ns] = s_ref[r, rows, :ns] + (prb * zr - pib * zi)
                s_ref[r, rows, ns:] = s_ref[r, rows, ns:] + (prb * zi + pib * zr)

    def out_chunk(c):
        lo = c * SSM_MM_CHUNK
        xs = s_ref[lo:lo + SSM_MM_CHUNK].reshape(SSM_MM_CHUNK * SSM_TI, 2 * ns)
        y = jnp.dot(xs.astype(BF16), wc, preferred_element_type=F32)
        y = y.reshape(SSM_MM_CHUNK, SSM_TI, SSM_SLAB) + dsk * u_ref[lo:lo + SSM_MM_CHUNK]
        o_ref[lo:lo + SSM_MM_CHUNK] = jax.nn.gelu(y)

    fix_chunk(0)
    for c in range(n_chunks):
        if c + 1 < n_chunks:
            fix_chunk(c + 1)
        out_chunk(c)


def ssm_scan(u3, wb, wc, pows, dskip):
    ns2 = 2 * SLAB_STATES
    return pl.pallas_call(
        _ssm_kernel,
        out_shape=jax.ShapeDtypeStruct((R, NI, SSM_W), F32),
        grid=(SSM_SLABS, NI // SSM_TI),
        in_specs=[pl.BlockSpec((R, SSM_TI, SSM_SLAB), lambda kb, ic: (0, ic, kb)),
                  pl.BlockSpec((1, SSM_SLAB, ns2), lambda kb, ic: (kb, 0, 0)),
                  pl.BlockSpec((1, ns2, SSM_SLAB), lambda kb, ic: (kb, 0, 0)),
                  pl.BlockSpec((1, R, ns2), lambda kb, ic: (kb, 0, 0)),
                  pl.BlockSpec((1, SSM_SLAB), lambda kb, ic: (0, kb))],
        out_specs=pl.BlockSpec((R, SSM_TI, SSM_SLAB), lambda kb, ic: (0, ic, kb)),
        scratch_shapes=[pltpu.VMEM((R, SSM_TI, ns2), F32),
                        pltpu.VMEM((SSM_TI, ns2), F32),
                        pltpu.VMEM((1, ns2), F32)],
        compiler_params=_cparams(("arbitrary", "arbitrary")),
        name="s5_ssm",
    )(u3, wb, wc, pows, dskip)


def ssm_params(a_re, a_im, log_step, b_re, b_im, c_re, c_im):
    G, P, H = SSM_W // SSM_CH, SSM_STATE, SSM_CH
    gs = SSM_SLAB // SSM_CH
    step = jnp.exp(log_step.astype(F32))[:, None]
    kk = jnp.arange(1, R + 1, dtype=F32)[:, None, None]
    mag = jnp.exp(kk * (a_re * step))
    ang = kk * (a_im * step)
    pw_re, pw_im = mag * jnp.cos(ang), mag * jnp.sin(ang)
    abar_re, abar_im = pw_re[0], pw_im[0]
    nr, ni = abar_re - 1.0, abar_im
    den = a_re * a_re + a_im * a_im
    f_re = (nr * a_re + ni * a_im) / den
    f_im = (ni * a_re - nr * a_im) / den
    bbar_re = f_re[..., None] * b_re - f_im[..., None] * b_im
    bbar_im = f_re[..., None] * b_im + f_im[..., None] * b_re
    rows_g = jnp.arange(gs * H, dtype=jnp.int32) // H
    cols_g = jnp.arange(gs * P, dtype=jnp.int32) // P
    b_t = jnp.concatenate([jnp.swapaxes(bbar_re, 1, 2).reshape(SSM_SLABS, gs * H, P),
                           jnp.swapaxes(bbar_im, 1, 2).reshape(SSM_SLABS, gs * H, P)], axis=-1)
    tile_p = (jnp.arange(2 * P, dtype=jnp.int32)[:, None] ==
              (jnp.arange(2 * gs * P, dtype=jnp.int32) // (gs * P) * P + jnp.arange(2 * gs * P, dtype=jnp.int32) % P)[None, :])
    mask_b = rows_g[:, None] == jnp.tile(cols_g, 2)[None, :]
    wb = jnp.where(mask_b[None], jnp.einsum('kap,pc->kac', b_t.astype(BF16), tile_p.astype(BF16),
                                            preferred_element_type=F32), 0.0).astype(BF16)
    c_t = jnp.concatenate([jnp.swapaxes(c_re.astype(F32), 1, 2).reshape(SSM_SLABS, gs * P, H),
                           -jnp.swapaxes(c_im.astype(F32), 1, 2).reshape(SSM_SLABS, gs * P, H)], axis=1)
    tile_h = jnp.arange(H, dtype=jnp.int32)[:, None] == (jnp.arange(gs * H, dtype=jnp.int32) % H)[None, :]
    mask_c = jnp.tile(cols_g, 2)[:, None] == rows_g[None, :]
    wc = jnp.where(mask_c[None], jnp.einsum('kah,hc->kac', c_t.astype(BF16), tile_h.astype(BF16),
                                            preferred_element_type=F32), 0.0).astype(BF16)
    pows = jnp.concatenate([pw_re.reshape(R, SSM_SLABS, gs * P), pw_im.reshape(R, SSM_SLABS, gs * P)], axis=-1)
    return wb, wc, jnp.transpose(pows, (1, 0, 2))


def _glu_kernel(a_ref, w_ref, b_ref, o_ref, wbf_ref):
    @pl.when(pl.program_id(0) == 0)
    def _():
        wbf_ref[...] = w_ref[...].astype(BF16)

    a = a_ref[...].astype(BF16)
    for c in range(SSM_W // MXU_N):
        cols = slice(c * MXU_N, (c + 1) * MXU_N)
        acc = jnp.dot(a, wbf_ref[:, cols], preferred_element_type=F32)
        o_ref[:, cols] = (a_ref[:, cols] * jax.nn.sigmoid(acc + b_ref[:, cols])).astype(BF16)


def glu(yg, w_glu, b_glu):
    tm = 512
    return pl.pallas_call(
        _glu_kernel,
        out_shape=jax.ShapeDtypeStruct((T, SSM_W), BF16),
        grid=(T // tm,),
        in_specs=[pl.BlockSpec((tm, SSM_W), lambda i: (i, 0)),
                  pl.BlockSpec((SSM_W, SSM_W), lambda i: (0, 0)),
                  pl.BlockSpec((1, SSM_W), lambda i: (0, 0))],
        out_specs=pl.BlockSpec((tm, SSM_W), lambda i: (i, 0)),
        scratch_shapes=[pltpu.VMEM((SSM_W, SSM_W), BF16)],
        compiler_params=_cparams(("arbitrary",)),
        name="ssm_glu",
    )(yg, w_glu, b_glu)


def _merge_kernel(h_ref, at_ref, ss_ref, wga_ref, wgs_ref, ba_ref, bs_ref, wa_ref, ws_ref, o_ref,
                  wga_bf, wgs_bf, wa_bf, ws_bf):
    _cast_weight_once(wga_ref, wga_bf)
    _cast_weight_once(wgs_ref, wgs_bf)
    _cast_weight_once(wa_ref, wa_bf)
    _cast_weight_once(ws_ref, ws_bf)
    h = h_ref[...]
    at = at_ref[...]
    ss = ss_ref[...]
    for c in range(wa_ref.shape[1] // MXU_N):
        cols = slice(c * MXU_N, (c + 1) * MXU_N)
        ga = jax.nn.sigmoid(jnp.dot(h, wga_bf[:, cols], preferred_element_type=F32) + ba_ref[:, cols])
        a = jnp.dot(at, wa_bf[:, cols], preferred_element_type=F32)
        gs = jax.nn.sigmoid(jnp.dot(h, wgs_bf[:, cols], preferred_element_type=F32) + bs_ref[:, cols])
        s = jnp.dot(ss, ws_bf[:, cols], preferred_element_type=F32)
        o_ref[:, cols] = (ga * a + gs * s).astype(BF16)


def merge(h, attn, ssm, w_in, b_gate, w_up_attn, w_up_ssm):
    tm, tn = 1024, 512
    nj = D // tn
    off_a = (3 * QKV_COLS + SSM_W) // tn
    once = pl.Buffered(1)
    return pl.pallas_call(
        _merge_kernel,
        out_shape=jax.ShapeDtypeStruct((T, D), BF16),
        grid=(nj, T // tm),
        in_specs=[pl.BlockSpec((tm, D), lambda j, i: (i, 0)),
                  pl.BlockSpec((tm, ATTN_OUT), lambda j, i: (i, 0)),
                  pl.BlockSpec((tm, SSM_W), lambda j, i: (i, 0)),
                  pl.BlockSpec((D, tn), lambda j, i: (0, j + off_a), pipeline_mode=once),
                  pl.BlockSpec((D, tn), lambda j, i: (0, j + off_a + nj), pipeline_mode=once),
                  pl.BlockSpec((1, tn), lambda j, i: (0, j)),
                  pl.BlockSpec((1, tn), lambda j, i: (0, j + nj)),
                  pl.BlockSpec((ATTN_OUT, tn), lambda j, i: (0, j), pipeline_mode=once),
                  pl.BlockSpec((SSM_W, tn), lambda j, i: (0, j), pipeline_mode=once)],
        out_specs=pl.BlockSpec((tm, tn), lambda j, i: (i, j)),
        scratch_shapes=[pltpu.VMEM((D, tn), BF16), pltpu.VMEM((D, tn), BF16),
                        pltpu.VMEM((ATTN_OUT, tn), BF16), pltpu.VMEM((SSM_W, tn), BF16)],
        compiler_params=_cparams(("arbitrary", "arbitrary")),
        name="gates_branch_merge",
    )(h, attn, ssm, w_in, w_in, b_gate, b_gate, w_up_attn, w_up_ssm)


OUTPROJ_TB = 64
OUTPROJ_PITCH = OUTPROJ_TB + 8


def _outproj_kernel(m_ref, w_ref, x_ref, o_ref, slab, wbf_ref):
    _cast_weight_once(w_ref, wbf_ref)
    tb = OUTPROJ_TB
    a = m_ref[...].reshape(R * tb, D)
    per = MXU_N // LANES
    for c in range(w_ref.shape[1] // MXU_N):
        acc = jnp.dot(a, wbf_ref[:, c * MXU_N:(c + 1) * MXU_N], preferred_element_type=F32)
        for s_ in range(per):
            lanes = slice(s_ * LANES, (s_ + 1) * LANES)
            for r in range(R):
                slab[c * per + s_, r * OUTPROJ_PITCH:r * OUTPROJ_PITCH + tb, :] = acc[r * tb:(r + 1) * tb, lanes]
        for s_ in range(per):
            lanes = slice((c * per + s_) * LANES, (c * per + s_ + 1) * LANES)
            for i in range(tb):
                rows = slice(i * R, (i + 1) * R)
                o_ref[rows, lanes] = slab[c * per + s_, pl.ds(i, R, stride=OUTPROJ_PITCH), :] + x_ref[rows, lanes]


def out_proj(merged3, w_out, x2):
    tb, tn = OUTPROJ_TB, 1024
    return pl.pallas_call(
        _outproj_kernel,
        out_shape=jax.ShapeDtypeStruct((T, D), F32),
        grid=(D // tn, NI // tb),
        in_specs=[pl.BlockSpec((R, tb, D), lambda j, i: (0, i, 0)),
                  pl.BlockSpec((D, tn), lambda j, i: (0, j), pipeline_mode=pl.Buffered(1)),
                  pl.BlockSpec((tb * R, tn), lambda j, i: (i, j))],
        out_specs=pl.BlockSpec((tb * R, tn), lambda j, i: (i, j)),
        scratch_shapes=[pltpu.VMEM((tn // LANES, OUTPROJ_PITCH * R, LANES), F32),
                        pltpu.VMEM((D, tn), BF16)],
        compiler_params=_cparams(("arbitrary", "arbitrary")),
        name="out_proj_residual",
    )(merged3, w_out, x2)


PACK_ROWS = 8
HALF_D = D // 2


def _pack_bf16_pairs(zf):
    top = lax.bitcast_convert_type(zf, jnp.uint32)
    return top[:, HALF_D:] | (top[:, :HALF_D] >> 16)


def _unpack_bf16_pairs(x_ref, first, n):
    lo, hi = [], []
    for c in range(PACK_ROWS):
        w = x_ref[pl.ds(first * PACK_ROWS + c, n, stride=PACK_ROWS), :]
        lo.append(lax.bitcast_convert_type(w << 16, F32))
        hi.append(lax.bitcast_convert_type(w & jnp.uint32(0xFFFF0000), F32))
    return jnp.concatenate(lo + hi, axis=1)


def _router_kernel(x_ref, g_ref, w_ref, b_ref, h_ref, id_ref, wt_ref, cnt_ref, carry_ref):
    step = pl.program_id(0)

    @pl.when(step == 0)
    def _():
        carry_ref[...] = jnp.zeros_like(carry_ref)

    x = x_ref[...]
    ms = jnp.mean(x * x, axis=-1, keepdims=True)
    z = x * lax.rsqrt(ms + NORM_EPS) * g_ref[...]
    zh = z.astype(BF16)
    zf = zh.astype(F32)
    word = _pack_bf16_pairs(zf)
    for c in range(PACK_ROWS):
        h_ref[pl.ds(c, x.shape[0], stride=PACK_ROWS), :] = word[:, c * LANES:(c + 1) * LANES]
    zl = (z - zf).astype(BF16)
    w = w_ref[...]
    wh = w.astype(BF16)
    wl = (w - wh.astype(F32)).astype(BF16)
    logits = (jnp.dot(zh, wh, preferred_element_type=F32) + jnp.dot(zl, wh, preferred_element_type=F32)
              + jnp.dot(zh, wl, preferred_element_type=F32)) + b_ref[...]
    lane = lax.broadcasted_iota(jnp.int32, logits.shape, 1)
    lanef = lane.astype(F32)
    neg = jnp.float32(-jnp.inf)
    big = jnp.float32(1e9)
    gl = jnp.where(lane < N_EGROUPS, logits, neg)
    gmax = jnp.max(gl, axis=-1, keepdims=True)
    gidx = jnp.min(jnp.where(gl == gmax, lanef, big), axis=-1, keepdims=True)
    pg = 1.0 / jnp.sum(jnp.exp(gl - gmax), axis=-1, keepdims=True)
    lo = N_EGROUPS + EXPERTS_PER_GROUP * gidx
    el = jnp.where((lanef >= lo) & (lanef < lo + EXPERTS_PER_GROUP), logits, neg)
    t1 = jnp.max(el, axis=-1, keepdims=True)
    j1 = jnp.min(jnp.where(el == t1, lanef, big), axis=-1, keepdims=True)
    el2 = jnp.where(lanef == j1, neg, el)
    t2 = jnp.max(el2, axis=-1, keepdims=True)
    j2 = jnp.min(jnp.where(el2 == t2, lanef, big), axis=-1, keepdims=True)
    e21 = jnp.exp(t2 - t1)
    w1 = pg / (1.0 + e21)
    w2 = pg * e21 / (1.0 + e21)
    e1f = j1 - N_EGROUPS
    e2f = j2 - N_EGROUPS

    tm = x.shape[0]
    oh1 = (lanef == e1f).astype(F32)
    oh2 = (lanef == e2f).astype(F32)
    ri = lax.broadcasted_iota(jnp.int32, (tm, tm), 0)
    ci = lax.broadcasted_iota(jnp.int32, (tm, tm), 1)
    before = (ci < ri).astype(BF16)
    p1 = jnp.dot(before, oh1.astype(BF16), preferred_element_type=F32)
    p2 = jnp.dot(before, oh2.astype(BF16), preferred_element_type=F32)
    carry = carry_ref[...]
    c1 = jnp.sum(oh1, axis=0, keepdims=True)
    c2 = jnp.sum(oh2, axis=0, keepdims=True)
    rank1 = jnp.sum(oh1 * (carry + p1), axis=-1, keepdims=True)
    rank2 = jnp.sum(oh2 * (carry + c1 + p2), axis=-1, keepdims=True)
    carry = carry + c1 + c2
    carry_ref[...] = carry
    cnt_ref[...] = jnp.broadcast_to(carry, cnt_ref.shape).astype(jnp.int32)

    ids = jnp.where(lane == 0, e1f, jnp.where(lane == 1, e2f, jnp.where(lane == 2, rank1, jnp.where(lane == 3, rank2, 0.0))))
    id_ref[...] = ids.astype(jnp.int32)
    wt_ref[...] = jnp.where(lane == 0, w1, jnp.where(lane == 1, w2, 0.0))


def router(x1, gain, w_r, b_r):
    tm = 256
    return pl.pallas_call(
        _router_kernel,
        out_shape=(jax.ShapeDtypeStruct((T * PACK_ROWS, LANES), jnp.uint32),
                   jax.ShapeDtypeStruct((T, LANES), jnp.int32),
                   jax.ShapeDtypeStruct((T, LANES), F32),
                   jax.ShapeDtypeStruct((8, LANES), jnp.int32)),
        grid=(T // tm,),
        in_specs=[pl.BlockSpec((tm, D), lambda i: (i, 0)),
                  pl.BlockSpec((1, D), lambda i: (0, 0)),
                  pl.BlockSpec((D, LANES), lambda i: (0, 0)),
                  pl.BlockSpec((1, LANES), lambda i: (0, 0))],
        out_specs=(pl.BlockSpec((tm * PACK_ROWS, LANES), lambda i: (i, 0)),
                   pl.BlockSpec((tm, LANES), lambda i: (i, 0)),
                   pl.BlockSpec((tm, LANES), lambda i: (i, 0)),
                   pl.BlockSpec((8, LANES), lambda i: (0, 0))),
        scratch_shapes=[pltpu.VMEM((1, LANES), F32)],
        compiler_params=_cparams(("arbitrary",)),
        name="ffn_norm_router",
    )(x1, gain, w_r, b_r)


DISPATCH_TB = 256
N_ZERO_FILLS = 2 * N_EXPERTS


def _dispatch_kernel(dest_ref, zs_ref, h_ref, xs_hbm, zbuf, zsem, sem):
    step = pl.program_id(0)

    @pl.when(step == 0)
    def _():
        zbuf[...] = jnp.zeros_like(zbuf)

        def zero_copy(e):
            start = pl.multiple_of(jnp.maximum(zs_ref[e], 0) * PACK_ROWS, PACK_ROWS)
            return pltpu.make_async_copy(zbuf, xs_hbm.at[pl.ds(start, MOE_BLOCK * PACK_ROWS)], zsem.at[0])

        def zstart(e, c):
            @pl.when(zs_ref[e] >= 0)
            def _():
                zero_copy(e).start()
            return c

        def zwait(e, c):
            @pl.when(zs_ref[e] >= 0)
            def _():
                zero_copy(e).wait()
            return c

        lax.fori_loop(0, N_ZERO_FILLS, zstart, 0)
        lax.fori_loop(0, N_ZERO_FILLS, zwait, 0)

    def row_copy(n, k):
        a = (step * DISPATCH_TB + n) * TOP_K + k
        src = h_ref.at[pl.ds(pl.multiple_of(n * PACK_ROWS, PACK_ROWS), PACK_ROWS)]
        dst = xs_hbm.at[pl.ds(pl.multiple_of(dest_ref[a] * PACK_ROWS, PACK_ROWS), PACK_ROWS)]
        return pltpu.make_async_copy(src, dst, sem.at[0])

    def issue(n, c):
        for k in range(TOP_K):
            row_copy(n, k).start(priority=k)
        return c

    def drain(n, c):
        for k in range(TOP_K):
            row_copy(n, k).wait()
        return c

    lax.fori_loop(0, DISPATCH_TB, issue, 0, unroll=8)
    lax.fori_loop(0, DISPATCH_TB, drain, 0, unroll=8)


def dispatch(dest, zero_start, hpk):
    grid_spec = pltpu.PrefetchScalarGridSpec(
        num_scalar_prefetch=2,
        grid=(T // DISPATCH_TB,),
        in_specs=[pl.BlockSpec((DISPATCH_TB * PACK_ROWS, LANES), lambda i, dst, zs: (i, 0))],
        out_specs=pl.BlockSpec(memory_space=pl.ANY),
        scratch_shapes=[pltpu.VMEM((MOE_BLOCK * PACK_ROWS, LANES), jnp.uint32),
                        pltpu.SemaphoreType.DMA((1,)),
                        pltpu.SemaphoreType.DMA((1,))],
    )
    return pl.pallas_call(
        _dispatch_kernel,
        out_shape=jax.ShapeDtypeStruct((MOE_ROWS * PACK_ROWS, LANES), jnp.uint32),
        grid_spec=grid_spec,
        compiler_params=_cparams(("arbitrary",)),
        name="moe_dispatch",
    )(dest, zero_start, hpk)


def _expert_kernel(be_ref, nu_ref, ord_ref, seq_ref, x_ref, wg_hbm, wu_hbm, wd_hbm, y_ref,
                   wg_st, wu_st, wd_st, wg_bf, wu_bf, wd_bf, sem):
    b = pl.program_id(0)

    def weight_copies(e, slot):
        return (pltpu.make_async_copy(wg_hbm.at[e], wg_st.at[slot], sem.at[slot, 0]),
                pltpu.make_async_copy(wu_hbm.at[e], wu_st.at[slot], sem.at[slot, 1]),
                pltpu.make_async_copy(wd_hbm.at[e], wd_st.at[slot], sem.at[slot, 2]))

    def start_fetch(n, slot):
        @pl.when(seq_ref[n] >= 0)
        def _():
            for cp in weight_copies(seq_ref[n], slot):
                cp.start()

    @pl.when(b < nu_ref[0])
    def _():
        n = ord_ref[b]
        slot = n % 2
        changed = jnp.logical_or(b == 0, be_ref[b] != be_ref[jnp.maximum(b - 1, 0)])

        @pl.when(b == 0)
        def _():
            start_fetch(0, 0)
            start_fetch(1, 1)

        @pl.when(changed)
        def _():
            cg, cu, cd = weight_copies(be_ref[b], slot)
            cg.wait()
            wg_bf[...] = wg_st[slot].astype(BF16)
            cu.wait()
            wu_bf[...] = wu_st[slot].astype(BF16)
            cd.wait()
            wd_bf[...] = wd_st[slot].astype(BF16)
            start_fetch(n + 2, slot)

        x = _unpack_bf16_pairs(x_ref, 0, MOE_BLOCK).astype(BF16)
        gate = jnp.dot(x, wg_bf[...], preferred_element_type=F32)
        up = jnp.dot(x, wu_bf[...], preferred_element_type=F32)
        hid = (jax.nn.silu(gate) * up).astype(BF16)
        y = jnp.dot(hid, wd_bf[...], preferred_element_type=F32)
        word = _pack_bf16_pairs(y.astype(BF16).astype(F32))
        for c in range(PACK_ROWS):
            y_ref[pl.ds(c, MOE_BLOCK, stride=PACK_ROWS), :] = word[:, c * LANES:(c + 1) * LANES]

    @pl.when(b >= nu_ref[0])
    def _():
        y_ref[...] = jnp.zeros_like(y_ref)


def experts(block_expert, n_used, block_ord, expert_seq, xs, w_gate, w_up, w_down):
    def blk(b, be, nu, od, sq):
        return jnp.minimum(b, nu[0] - 1)

    grid_spec = pltpu.PrefetchScalarGridSpec(
        num_scalar_prefetch=4,
        grid=(MOE_BLOCKS,),
        in_specs=[pl.BlockSpec((MOE_BLOCK * PACK_ROWS, LANES), lambda b, be, nu, od, sq: (blk(b, be, nu, od, sq), 0)),
                  pl.BlockSpec(memory_space=pl.ANY),
                  pl.BlockSpec(memory_space=pl.ANY),
                  pl.BlockSpec(memory_space=pl.ANY)],
        out_specs=pl.BlockSpec((MOE_BLOCK * PACK_ROWS, LANES), lambda b, be, nu, od, sq: (b, 0)),
        scratch_shapes=[pltpu.VMEM((2, D, EXPERT_FF), F32),
                        pltpu.VMEM((2, D, EXPERT_FF), F32),
                        pltpu.VMEM((2, EXPERT_FF, D), F32),
                        pltpu.VMEM((D, EXPERT_FF), BF16),
                        pltpu.VMEM((D, EXPERT_FF), BF16),
                        pltpu.VMEM((EXPERT_FF, D), BF16),
                        pltpu.SemaphoreType.DMA((2, 3))],
    )
    return pl.pallas_call(
        _expert_kernel,
        out_shape=jax.ShapeDtypeStruct((MOE_ROWS * PACK_ROWS, LANES), jnp.uint32),
        grid_spec=grid_spec,
        compiler_params=_cparams(("arbitrary",)),
        name="moe_experts",
    )(block_expert, n_used, block_ord, expert_seq, xs, w_gate, w_up, w_down)


COMBINE_ROWS = 256
COMBINE_SUB = 16


def _combine_kernel(dest_ref, ys_hbm, x_ref, wt_ref, g_ref, o_ref, ybuf, sem):
    s = pl.program_id(0)
    ns = pl.num_programs(0)
    slot = s % 2
    tb = COMBINE_SUB
    rows = COMBINE_ROWS
    nxt = jnp.minimum(s + 1, ns - 1)

    def row_copy(step, n, k, sl):
        tok = step * rows + n
        src = ys_hbm.at[pl.ds(pl.multiple_of(dest_ref[tok * TOP_K + k] * PACK_ROWS, PACK_ROWS), PACK_ROWS)]
        dst = ybuf.at[sl * TOP_K + k, pl.ds(pl.multiple_of(n * PACK_ROWS, PACK_ROWS), PACK_ROWS)]
        return pltpu.make_async_copy(src, dst, sem.at[sl])

    def wait_all(step, sl):
        def body(n, c):
            for k in range(TOP_K):
                row_copy(step, n, k, sl).wait()
            return c
        lax.fori_loop(0, rows, body, 0, unroll=8)

    @pl.when(s == 0)
    def _():
        def body(n, c):
            for k in range(TOP_K):
                row_copy(0, n, k, 0).start(priority=k)
            return c
        lax.fori_loop(0, rows, body, 0, unroll=8)

    wait_all(s, slot)

    g = g_ref[...]
    y0_ref = ybuf.at[slot * TOP_K]
    y1_ref = ybuf.at[slot * TOP_K + 1]
    for b in range(rows // tb):
        for n in range(b * tb, (b + 1) * tb):
            for k in range(TOP_K):
                row_copy(nxt, n, k, 1 - slot).start(priority=k)
        sub = slice(b * tb, (b + 1) * tb)
        w = wt_ref[sub, :]
        y0 = _unpack_bf16_pairs(y0_ref, b * tb, tb)
        y1 = _unpack_bf16_pairs(y1_ref, b * tb, tb)
        z = x_ref[sub, :] + (w[:, 0:1] * y0 + w[:, 1:2] * y1)
        ms = jnp.mean(z * z, axis=-1, keepdims=True)
        o_ref[sub, :] = z * lax.rsqrt(ms + NORM_EPS) * g

    @pl.when(s == ns - 1)
    def _():
        wait_all(nxt, 1 - slot)


def combine(dest, ys, x1, wts, gain):
    rows = COMBINE_ROWS
    grid_spec = pltpu.PrefetchScalarGridSpec(
        num_scalar_prefetch=1,
        grid=(T // rows,),
        in_specs=[pl.BlockSpec(memory_space=pl.ANY),
                  pl.BlockSpec((rows, D), lambda s, dst: (s, 0)),
                  pl.BlockSpec((rows, LANES), lambda s, dst: (s, 0)),
                  pl.BlockSpec((1, D), lambda s, dst: (0, 0))],
        out_specs=pl.BlockSpec((rows, D), lambda s, dst: (s, 0)),
        scratch_shapes=[pltpu.VMEM((2 * TOP_K, rows * PACK_ROWS, LANES), jnp.uint32),
                        pltpu.SemaphoreType.DMA((2,))],
    )
    return pl.pallas_call(
        _combine_kernel,
        out_shape=jax.ShapeDtypeStruct((T, D), F32),
        grid_spec=grid_spec,
        compiler_params=_cparams(("arbitrary",)),
        name="moe_combine_final_norm",
    )(dest, ys, x1, wts, gain)


def dispatch_plan(ids, counts):
    experts_ = jnp.arange(N_EXPERTS, dtype=jnp.int32)
    padded = (counts + MOE_BLOCK - 1) // MOE_BLOCK * MOE_BLOCK
    pad_end = jnp.cumsum(padded)
    pad_start = pad_end - padded
    e = ids[:, :TOP_K]
    start_of = jnp.sum(jnp.where(e[:, :, None] == experts_[None, None, :], pad_start[None, None, :], 0), axis=-1)
    dest = (start_of + ids[:, TOP_K:2 * TOP_K]).reshape(N_ASSIGN).astype(jnp.int32)
    n_used = pad_end[-1] // MOE_BLOCK
    block_start = jnp.minimum(jnp.arange(MOE_BLOCKS, dtype=jnp.int32), n_used - 1) * MOE_BLOCK
    block_expert = jnp.sum((block_start[:, None] >= pad_end[None, :]).astype(jnp.int32), axis=1)
    block_expert = jnp.minimum(block_expert, N_EXPERTS - 1).astype(jnp.int32)
    tail = n_used + experts_
    zero_start = jnp.concatenate([jnp.where(counts > 0, pad_end - MOE_BLOCK, -1),
                                  jnp.where(tail < MOE_BLOCKS, tail * MOE_BLOCK, -1)]).astype(jnp.int32)
    present = counts > 0
    expert_ord = jnp.cumsum(present.astype(jnp.int32)) - 1
    slots = jnp.arange(N_EXPERTS + 2, dtype=jnp.int32)
    hit = present[None, :] & (expert_ord[None, :] == slots[:, None])
    expert_seq = jnp.where(jnp.any(hit, axis=1), jnp.sum(jnp.where(hit, experts_[None, :], 0), axis=1), -1)
    block_ord = jnp.sum(jnp.where(block_expert[:, None] == experts_[None, :], expert_ord[None, :], 0), axis=1)
    return (block_expert, n_used.astype(jnp.int32).reshape(1), dest, zero_start,
            block_ord.astype(jnp.int32), expert_seq.astype(jnp.int32))


def kernel(x, norm_mix, w_in, b_gate, ssm_a_re, ssm_a_im, ssm_log_step, ssm_b_re, ssm_b_im, ssm_c_re, ssm_c_im, ssm_d, w_glu, b_glu, w_up_attn, w_up_ssm, w_out, norm_ffn, w_router_group, b_router_group, w_router_expert, b_router_expert, w_expert_gate, w_expert_up, w_expert_down, norm_final):
    x2 = x.reshape(T, D)
    h = norm_permute(x2, norm_mix.reshape(1, D)).reshape(T, D)
    w_in_l = w_in.reshape(D, IN_COLS)
    qkv = proj(h, w_in_l, 0, 3 * QKV_COLS, tn=QKV_COLS, name="proj_qkv")
    u = proj(h, w_in_l, 3 * QKV_COLS, SSM_W, tn=512, name="proj_ssm_in")

    attn = attention(qkv.reshape(R, NI, 3 * QKV_COLS)).reshape(T, ATTN_OUT)

    G = SSM_W // SSM_CH
    wb, wc, pows = ssm_params(
        ssm_a_re.reshape(G, SSM_STATE).astype(F32), ssm_a_im.reshape(G, SSM_STATE).astype(F32),
        ssm_log_step.reshape(G),
        ssm_b_re.reshape(G, SSM_STATE, SSM_CH).astype(F32), ssm_b_im.reshape(G, SSM_STATE, SSM_CH).astype(F32),
        ssm_c_re.reshape(G, SSM_CH, SSM_STATE), ssm_c_im.reshape(G, SSM_CH, SSM_STATE))
    yg = ssm_scan(u.reshape(R, NI, SSM_W), wb, wc, pows, ssm_d.reshape(1, SSM_W).astype(F32))
    ssm = glu(yg.reshape(T, SSM_W), w_glu.reshape(SSM_W, SSM_W), b_glu.reshape(1, SSM_W))

    merged = merge(h, attn, ssm, w_in_l, b_gate.reshape(1, 2 * D),
                   w_up_attn.reshape(ATTN_OUT, D), w_up_ssm.reshape(SSM_W, D))
    x1 = out_proj(merged.reshape(R, NI, D), w_out.reshape(D, D), x2)

    w_r = jnp.concatenate([w_router_group.reshape(D, N_EGROUPS), w_router_expert.reshape(D, N_EXPERTS),
                           jnp.zeros((D, LANES - N_EGROUPS - N_EXPERTS), F32)], axis=1)
    b_r = jnp.concatenate([b_router_group.reshape(1, N_EGROUPS), b_router_expert.reshape(1, N_EXPERTS),
                           jnp.zeros((1, LANES - N_EGROUPS - N_EXPERTS), F32)], axis=1)
    hpk, ids, wts, counts = router(x1, norm_ffn.reshape(1, D), w_r, b_r)

    block_expert, n_used, dest, zero_start, block_ord, expert_seq = dispatch_plan(ids[:, :2 * TOP_K], counts[0, :N_EXPERTS])
    xs = dispatch(dest, zero_start, hpk)
    ys = experts(block_expert, n_used, block_ord, expert_seq, xs,
                 w_expert_gate.reshape(N_EXPERTS, D, EXPERT_FF), w_expert_up.reshape(N_EXPERTS, D, EXPERT_FF),
                 w_expert_down.reshape(N_EXPERTS, EXPERT_FF, D))
    out = combine(dest, ys, x1, wts, norm_final.reshape(1, D))
    return out.reshape(1, T, D)
```

```python
import functools
import math

import jax
import jax.numpy as jnp
from jax import lax
from jax.experimental import pallas as pl
from jax.experimental.pallas import tpu as pltpu

F32 = jnp.float32
BF16 = jnp.bfloat16

T = 8192
D = 2048
R = 16
NI = T // R
HEAD_DIM = 64
N_HEAD_SLOTS = 8
DILATIONS = (1, 4, 16)
ATTN_BLOCK = 128
QKV_COLS = 1536
ATTN_OUT = 512
SSM_W = 1024
SSM_STATE = 64
SSM_CH = 16
IN_COLS = 3 * QKV_COLS + SSM_W + 2 * D
N_EXPERTS = 32
N_EGROUPS = 4
EXPERTS_PER_GROUP = 8
TOP_K = 2
EXPERT_FF = 512
NORM_EPS = 1e-6
LANES = 128
VMEM_LIMIT = 48 * 1024 * 1024

MOE_BLOCK = 256
N_ASSIGN = T * TOP_K
MOE_BLOCKS = N_ASSIGN // MOE_BLOCK + N_EXPERTS
MOE_ROWS = MOE_BLOCKS * MOE_BLOCK


def _cparams(sem):
    return pltpu.CompilerParams(dimension_semantics=sem, vmem_limit_bytes=VMEM_LIMIT)


N_SLABS = D // LANES
NORM_TB = 32
NORM_CHUNK = 64
NORM_PITCH = R + 8


def _norm_permute_kernel(x_ref, g_ref, h_ref, slab):
    g = g_ref[...]

    def chunk(t, c):
        rows = pl.ds(pl.multiple_of(t * NORM_CHUNK, NORM_CHUNK), NORM_CHUNK)
        x = x_ref[rows, :]
        ms = jnp.mean(x * x, axis=-1, keepdims=True)
        hn = x * lax.rsqrt(ms + NORM_EPS) * g
        for k in range(NORM_CHUNK // R):
            dst = pl.ds(pl.multiple_of((t * (NORM_CHUNK // R) + k) * NORM_PITCH, 8), R)
            for s_ in range(N_SLABS):
                slab[s_, dst, :] = hn[k * R:(k + 1) * R, s_ * LANES:(s_ + 1) * LANES]
        return c

    lax.fori_loop(0, NORM_TB * R // NORM_CHUNK, chunk, 0)
    for r in range(R):
        pieces = [slab[s_, pl.ds(r, NORM_TB, stride=NORM_PITCH), :] for s_ in range(N_SLABS)]
        h_ref[r] = jnp.concatenate(pieces, axis=1).astype(BF16)


def norm_permute(x2, gain):
    return pl.pallas_call(
        _norm_permute_kernel,
        out_shape=jax.ShapeDtypeStruct((R, NI, D), BF16),
        grid=(NI // NORM_TB,),
        in_specs=[pl.BlockSpec((NORM_TB * R, D), lambda i: (i, 0)),
                  pl.BlockSpec((1, D), lambda i: (0, 0))],
        out_specs=pl.BlockSpec((R, NORM_TB, D), lambda i: (0, i, 0)),
        scratch_shapes=[pltpu.VMEM((N_SLABS, NORM_TB * NORM_PITCH, LANES), F32)],
        compiler_params=_cparams(("arbitrary",)),
        name="norm_permute",
    )(x2, gain)


MXU_N = 256


def _cast_weight_once(w_ref, wbf_ref):
    @pl.when(pl.program_id(1) == 0)
    def _():
        wbf_ref[...] = w_ref[...].astype(BF16)


def _proj_kernel(a_ref, w_ref, o_ref, wbf_ref):
    _cast_weight_once(w_ref, wbf_ref)
    a = a_ref[...]
    for c in range(w_ref.shape[1] // MXU_N):
        cols = slice(c * MXU_N, (c + 1) * MXU_N)
        o_ref[:, cols] = jnp.dot(a, wbf_ref[:, cols], preferred_element_type=F32)


def proj(h, w_in, col_off, n_cols, tn, name="proj"):
    tm = 1024
    off = col_off // tn
    return pl.pallas_call(
        _proj_kernel,
        out_shape=jax.ShapeDtypeStruct((T, n_cols), F32),
        grid=(n_cols // tn, T // tm),
        in_specs=[pl.BlockSpec((tm, D), lambda j, i: (i, 0)),
                  pl.BlockSpec((D, tn), lambda j, i: (0, j + off), pipeline_mode=pl.Buffered(1))],
        out_specs=pl.BlockSpec((tm, tn), lambda j, i: (i, j)),
        scratch_shapes=[pltpu.VMEM((D, tn), BF16)],
        compiler_params=_cparams(("arbitrary", "arbitrary")),
        name=name,
    )(h, w_in)


def _seq_index_maps(d):
    nseg = R // d
    qlen = ATTN_BLOCK // nseg
    return nseg, qlen


def _bias_matrices(d, hp):
    nseg, qlen = _seq_index_maps(d)
    klen = 2 * qlen
    row = lax.broadcasted_iota(jnp.int32, (2 * ATTN_BLOCK, 2 * ATTN_BLOCK), 0)
    col = lax.broadcasted_iota(jnp.int32, (2 * ATTN_BLOCK, 2 * ATTN_BLOCK), 1)
    rho = row % ATTN_BLOCK
    jq = (rho % qlen) * nseg + rho // qlen
    jk = ((col % klen) - qlen) * nseg + col // klen
    steps = jq - jk
    valid = (steps >= 0) & (steps <= ATTN_BLOCK)
    head = 2 * hp + row // ATTN_BLOCK
    slope = lax.bitcast_convert_type((127 - (head + 1)) << 23, F32)
    bias = -slope * (d * steps).astype(F32)
    neg = jnp.float32(-jnp.inf)
    return jnp.where(valid, bias, neg), jnp.where(valid & (jk >= 0), bias, neg)


def _attend_pair(q, k, v, bias):
    lane = lax.broadcasted_iota(jnp.int32, (ATTN_BLOCK, LANES), 1)
    first = lane < HEAD_DIM
    zero = jnp.zeros_like(q)
    q2 = jnp.concatenate([jnp.where(first, q, zero), jnp.where(first, zero, q)], axis=0).astype(BF16)
    s = lax.dot_general(q2, k.astype(BF16), (((1,), (1,)), ((), ())), preferred_element_type=F32)
    s = s + bias
    m = jnp.max(s, axis=-1, keepdims=True)
    p = jnp.exp(s - m)
    l = jnp.sum(p, axis=-1, keepdims=True)
    o2 = jnp.dot(p.astype(BF16), v.astype(BF16), preferred_element_type=F32)
    o = jnp.where(first, o2[:ATTN_BLOCK], o2[ATTN_BLOCK:])
    m_b = jnp.where(first, m[:ATTN_BLOCK], m[ATTN_BLOCK:])
    l_b = jnp.where(first, l[:ATTN_BLOCK], l[ATTN_BLOCK:])
    return o, m_b, l_b


def _attn_kernel(q_ref, kp_ref, kc_ref, vp_ref, vc_ref, o_ref, kbuf, vbuf, obuf, mbuf, lbuf, bias_ref):
    hp = pl.program_id(0)
    it = pl.program_id(1)
    g = pl.program_id(2)
    scale = HEAD_DIM ** -0.5

    kbuf[:, :ATTN_BLOCK, :] = kp_ref[...]
    kbuf[:, ATTN_BLOCK:, :] = kc_ref[...]
    vbuf[:, :ATTN_BLOCK, :] = vp_ref[...]
    vbuf[:, ATTN_BLOCK:, :] = vc_ref[...]

    for gi, d in enumerate(DILATIONS):
        nseg, qlen = _seq_index_maps(d)
        klen = 2 * qlen
        nblk = ATTN_BLOCK // qlen

        @pl.when(g == gi)
        def _(gi=gi, d=d, nseg=nseg, qlen=qlen, klen=klen, nblk=nblk):
            @pl.when(it == 0)
            def _():
                b_reg, b_first = _bias_matrices(d, hp)
                bias_ref[gi, 0] = b_reg
                bias_ref[gi, 1] = b_first

            def block(idx, carry):
                rd = idx // nblk
                bb = idx % nblk
                q0 = pl.multiple_of(bb * qlen, qlen)
                k0 = pl.multiple_of(ATTN_BLOCK + bb * qlen - qlen, qlen)
                qs, ks, vs = [], [], []
                for m_ in range(nseg):
                    rr = rd + d * m_
                    qs.append(q_ref[rr, pl.ds(q0, qlen), :])
                    ks.append(kbuf[rr, pl.ds(k0, klen), :])
                    vs.append(vbuf[rr, pl.ds(k0, klen), :])
                q = jnp.concatenate(qs, axis=0) * scale
                k = jnp.concatenate(ks, axis=0)
                v = jnp.concatenate(vs, axis=0)
                is_first = jnp.logical_and(it == 0, bb == 0)
                bias = bias_ref[gi, jnp.where(is_first, 1, 0)]
                o, mx, den = _attend_pair(q, k, v, bias)
                for m_ in range(nseg):
                    rr = rd + d * m_
                    seg = slice(m_ * qlen, (m_ + 1) * qlen)
                    obuf[gi, rr, pl.ds(q0, qlen), :] = o[seg]
                    mbuf[gi, rr, pl.ds(q0, qlen), :] = mx[seg]
                    lbuf[gi, rr, pl.ds(q0, qlen), :] = den[seg]
                return carry

            lax.fori_loop(0, d * nblk, block, 0, unroll=True)

    @pl.when(g == len(DILATIONS) - 1)
    def _():
        for r in range(R):
            m0, m1, m2 = mbuf[0, r], mbuf[1, r], mbuf[2, r]
            mx = jnp.maximum(jnp.maximum(m0, m1), m2)
            e0, e1, e2 = jnp.exp(m0 - mx), jnp.exp(m1 - mx), jnp.exp(m2 - mx)
            den = e0 * lbuf[0, r] + e1 * lbuf[1, r] + e2 * lbuf[2, r]
            num = e0 * obuf[0, r] + e1 * obuf[1, r] + e2 * obuf[2, r]
            o_ref[r] = (num / den).astype(BF16)


def attention(qkv3):
    n_hp = N_HEAD_SLOTS // 2
    n_it = NI // ATTN_BLOCK
    ng = len(DILATIONS)
    cb = QKV_COLS // LANES

    def cur(base):
        return pl.BlockSpec((R, ATTN_BLOCK, LANES), lambda hp, it, g: (0, it, base + g * n_hp + hp))

    def prev(base):
        return pl.BlockSpec((R, ATTN_BLOCK, LANES),
                            lambda hp, it, g: (0, jnp.maximum(it - 1, 0), base + g * n_hp + hp))

    return pl.pallas_call(
        _attn_kernel,
        out_shape=jax.ShapeDtypeStruct((R, NI, ATTN_OUT), BF16),
        grid=(n_hp, n_it, ng),
        in_specs=[cur(0), prev(cb), cur(cb), prev(2 * cb), cur(2 * cb)],
        out_specs=pl.BlockSpec((R, ATTN_BLOCK, LANES), lambda hp, it, g: (0, it, hp)),
        scratch_shapes=[pltpu.VMEM((R, 2 * ATTN_BLOCK, LANES), F32),
                        pltpu.VMEM((R, 2 * ATTN_BLOCK, LANES), F32),
                        pltpu.VMEM((ng, R, ATTN_BLOCK, LANES), F32),
                        pltpu.VMEM((ng, R, ATTN_BLOCK, LANES), F32),
                        pltpu.VMEM((ng, R, ATTN_BLOCK, LANES), F32),
                        pltpu.VMEM((ng, 2, 2 * ATTN_BLOCK, 2 * ATTN_BLOCK), F32)],
        compiler_params=_cparams(("arbitrary", "arbitrary", "arbitrary")),
        name="dilated_attention",
    )(qkv3, qkv3, qkv3, qkv3, qkv3)


SSM_SLAB = 256
SSM_SLABS = SSM_W // SSM_SLAB
SLAB_STATES = SSM_SLAB // SSM_CH * SSM_STATE
SSM_TI = 128
SSM_MM_CHUNK = 4


def _ssm_kernel(u_ref, wb_ref, wc_ref, pw_ref, dsk_ref, o_ref, s_ref, zs_ref, zc_ref):
    ic = pl.program_id(1)
    ns = SLAB_STATES

    @pl.when(ic == 0)
    def _():
        zc_ref[...] = jnp.zeros_like(zc_ref)

    n_chunks = R // SSM_MM_CHUNK
    n_tiles = SSM_TI // 8
    wb = wb_ref[0]
    wc = wc_ref[0]
    dsk = dsk_ref[...]
    arb = jnp.broadcast_to(pw_ref[0, 0:1, :ns], (8, ns))
    aib = jnp.broadcast_to(pw_ref[0, 0:1, ns:], (8, ns))


    def bu_chunk(c):
        lo = c * SSM_MM_CHUNK
        uc = u_ref[lo:lo + SSM_MM_CHUNK].reshape(SSM_MM_CHUNK * SSM_TI, SSM_SLAB)
        bu = jnp.dot(uc.astype(BF16), wb, preferred_element_type=F32)
        s_ref[lo:lo + SSM_MM_CHUNK] = bu.reshape(SSM_MM_CHUNK, SSM_TI, 2 * ns)

    def local_chunk(c):
        lo = c * SSM_MM_CHUNK
        first = max(lo, 1)
        for t in range(n_tiles):
            rows = slice(t * 8, (t + 1) * 8)
            pr = s_ref[first - 1, rows, :ns]
            pi = s_ref[first - 1, rows, ns:]
            for r in range(first, lo + SSM_MM_CHUNK):
                nr = s_ref[r, rows, :ns] + (arb * pr - aib * pi)
                ni = s_ref[r, rows, ns:] + (arb * pi + aib * pr)
                s_ref[r, rows, :ns] = nr
                s_ref[r, rows, ns:] = ni
                pr, pi = nr, ni

    bu_chunk(0)
    for c in range(n_chunks):
        if c + 1 < n_chunks:
            bu_chunk(c + 1)
        local_chunk(c)

    a16r = pw_ref[0, R - 1:R, :ns]
    a16i = pw_ref[0, R - 1:R, ns:]

    def zstep(i, z):
        zs_ref[pl.ds(i, 1), :] = z
        e = s_ref[R - 1, pl.ds(i, 1), :]
        zr, zi = z[:, :ns], z[:, ns:]
        nz = jnp.concatenate([a16r * zr - a16i * zi, a16r * zi + a16i * zr], axis=-1)
        return nz + e

    zc_ref[...] = lax.fori_loop(0, SSM_TI, zstep, zc_ref[...])

    def fix_chunk(c):
        lo = c * SSM_MM_CHUNK
        for r in range(lo, lo + SSM_MM_CHUNK):
            prb = jnp.broadcast_to(pw_ref[0, r:r + 1, :ns], (8, ns))
            pib = jnp.broadcast_to(pw_ref[0, r:r + 1, ns:], (8, ns))
            for t in range(n_tiles):
                rows = slice(t * 8, (t + 1) * 8)
                zr = zs_ref[rows, :ns]
                zi = zs_ref[rows, ns:]
                s_ref[r, rows, :ns] = s_ref[r, rows, :ns] + (prb * zr - pib * zi)
                s_ref[r, rows, ns:] = s_ref[r, rows, ns:] + (prb * zi + pib * zr)

    def out_chunk(c):
        lo = c * SSM_MM_CHUNK
        xs = s_ref[lo:lo + SSM_MM_CHUNK].reshape(SSM_MM_CHUNK * SSM_TI, 2 * ns)
        y = jnp.dot(xs.astype(BF16), wc, preferred_element_type=F32)
        y = y.reshape(SSM_MM_CHUNK, SSM_TI, SSM_SLAB) + dsk * u_ref[lo:lo + SSM_MM_CHUNK]
        o_ref[lo:lo + SSM_MM_CHUNK] = jax.nn.gelu(y)

    fix_chunk(0)
    for c in range(n_chunks):
        if c + 1 < n_chunks:
            fix_chunk(c + 1)
        out_chunk(c)


def ssm_scan(u3, wb, wc, pows, dskip):
    ns2 = 2 * SLAB_STATES
    return pl.pallas_call(
        _ssm_kernel,
        out_shape=jax.ShapeDtypeStruct((R, NI, SSM_W), F32),
        grid=(SSM_SLABS, NI // SSM_TI),
        in_specs=[pl.BlockSpec((R, SSM_TI, SSM_SLAB), lambda kb, ic: (0, ic, kb)),
                  pl.BlockSpec((1, SSM_SLAB, ns2), lambda kb, ic: (kb, 0, 0)),
                  pl.BlockSpec((1, ns2, SSM_SLAB), lambda kb, ic: (kb, 0, 0)),
                  pl.BlockSpec((1, R, ns2), lambda kb, ic: (kb, 0, 0)),
                  pl.BlockSpec((1, SSM_SLAB), lambda kb, ic: (0, kb))],
        out_specs=pl.BlockSpec((R, SSM_TI, SSM_SLAB), lambda kb, ic: (0, ic, kb)),
        scratch_shapes=[pltpu.VMEM((R, SSM_TI, ns2), F32),
                        pltpu.VMEM((SSM_TI, ns2), F32),
                        pltpu.VMEM((1, ns2), F32)],
        compiler_params=_cparams(("arbitrary", "arbitrary")),
        name="s5_ssm",
    )(u3, wb, wc, pows, dskip)


def ssm_params(a_re, a_im, log_step, b_re, b_im, c_re, c_im):
    G, P, H = SSM_W // SSM_CH, SSM_STATE, SSM_CH
    gs = SSM_SLAB // SSM_CH
    step = jnp.exp(log_step.astype(F32))[:, None]
    kk = jnp.arange(1, R + 1, dtype=F32)[:, None, None]
    mag = jnp.exp(kk * (a_re * step))
    ang = kk * (a_im * step)
    pw_re, pw_im = mag * jnp.cos(ang), mag * jnp.sin(ang)
    abar_re, abar_im = pw_re[0], pw_im[0]
    nr, ni = abar_re - 1.0, abar_im
    den = a_re * a_re + a_im * a_im
    f_re = (nr * a_re + ni * a_im) / den
    f_im = (ni * a_re - nr * a_im) / den
    bbar_re = f_re[..., None] * b_re - f_im[..., None] * b_im
    bbar_im = f_re[..., None] * b_im + f_im[..., None] * b_re
    rows_g = jnp.arange(gs * H, dtype=jnp.int32) // H
    cols_g = jnp.arange(gs * P, dtype=jnp.int32) // P
    b_t = jnp.concatenate([jnp.swapaxes(bbar_re, 1, 2).reshape(SSM_SLABS, gs * H, P),
                           jnp.swapaxes(bbar_im, 1, 2).reshape(SSM_SLABS, gs * H, P)], axis=-1)
    tile_p = (jnp.arange(2 * P, dtype=jnp.int32)[:, None] ==
              (jnp.arange(2 * gs * P, dtype=jnp.int32) // (gs * P) * P + jnp.arange(2 * gs * P, dtype=jnp.int32) % P)[None, :])
    mask_b = rows_g[:, None] == jnp.tile(cols_g, 2)[None, :]
    wb = jnp.where(mask_b[None], jnp.einsum('kap,pc->kac', b_t.astype(BF16), tile_p.astype(BF16),
                                            preferred_element_type=F32), 0.0).astype(BF16)
    c_t = jnp.concatenate([jnp.swapaxes(c_re.astype(F32), 1, 2).reshape(SSM_SLABS, gs * P, H),
                           -jnp.swapaxes(c_im.astype(F32), 1, 2).reshape(SSM_SLABS, gs * P, H)], axis=1)
    tile_h = jnp.arange(H, dtype=jnp.int32)[:, None] == (jnp.arange(gs * H, dtype=jnp.int32) % H)[None, :]
    mask_c = jnp.tile(cols_g, 2)[:, None] == rows_g[None, :]
    wc = jnp.where(mask_c[None], jnp.einsum('kah,hc->kac', c_t.astype(BF16), tile_h.astype(BF16),
                                            preferred_element_type=F32), 0.0).astype(BF16)
    pows = jnp.concatenate([pw_re.reshape(R, SSM_SLABS, gs * P), pw_im.reshape(R, SSM_SLABS, gs * P)], axis=-1)
    return wb, wc, jnp.transpose(pows, (1, 0, 2))


def _glu_kernel(a_ref, w_ref, b_ref, o_ref, wbf_ref):
    @pl.when(pl.program_id(0) == 0)
    def _():
        wbf_ref[...] = w_ref[...].astype(BF16)

    a = a_ref[...].astype(BF16)
    for c in range(SSM_W // MXU_N):
        cols = slice(c * MXU_N, (c + 1) * MXU_N)
        acc = jnp.dot(a, wbf_ref[:, cols], preferred_element_type=F32)
        o_ref[:, cols] = (a_ref[:, cols] * jax.nn.sigmoid(acc + b_ref[:, cols])).astype(BF16)


def glu(yg, w_glu, b_glu):
    tm = 512
    return pl.pallas_call(
        _glu_kernel,
        out_shape=jax.ShapeDtypeStruct((T, SSM_W), BF16),
        grid=(T // tm,),
        in_specs=[pl.BlockSpec((tm, SSM_W), lambda i: (i, 0)),
                  pl.BlockSpec((SSM_W, SSM_W), lambda i: (0, 0)),
                  pl.BlockSpec((1, SSM_W), lambda i: (0, 0))],
        out_specs=pl.BlockSpec((tm, SSM_W), lambda i: (i, 0)),
        scratch_shapes=[pltpu.VMEM((SSM_W, SSM_W), BF16)],
        compiler_params=_cparams(("arbitrary",)),
        name="ssm_glu",
    )(yg, w_glu, b_glu)


def _merge_kernel(h_ref, at_ref, ss_ref, wga_ref, wgs_ref, ba_ref, bs_ref, wa_ref, ws_ref, o_ref,
                  wga_bf, wgs_bf, wa_bf, ws_bf):
    _cast_weight_once(wga_ref, wga_bf)
    _cast_weight_once(wgs_ref, wgs_bf)
    _cast_weight_once(wa_ref, wa_bf)
    _cast_weight_once(ws_ref, ws_bf)
    h = h_ref[...]
    at = at_ref[...]
    ss = ss_ref[...]
    for c in range(wa_ref.shape[1] // MXU_N):
        cols = slice(c * MXU_N, (c + 1) * MXU_N)
        ga = jax.nn.sigmoid(jnp.dot(h, wga_bf[:, cols], preferred_element_type=F32) + ba_ref[:, cols])
        a = jnp.dot(at, wa_bf[:, cols], preferred_element_type=F32)
        gs = jax.nn.sigmoid(jnp.dot(h, wgs_bf[:, cols], preferred_element_type=F32) + bs_ref[:, cols])
        s = jnp.dot(ss, ws_bf[:, cols], preferred_element_type=F32)
        o_ref[:, cols] = (ga * a + gs * s).astype(BF16)


def merge(h, attn, ssm, w_in, b_gate, w_up_attn, w_up_ssm):
    tm, tn = 1024, 512
    nj = D // tn
    off_a = (3 * QKV_COLS + SSM_W) // tn
    once = pl.Buffered(1)
    return pl.pallas_call(
        _merge_kernel,
        out_shape=jax.ShapeDtypeStruct((T, D), BF16),
        grid=(nj, T // tm),
        in_specs=[pl.BlockSpec((tm, D), lambda j, i: (i, 0)),
                  pl.BlockSpec((tm, ATTN_OUT), lambda j, i: (i, 0)),
                  pl.BlockSpec((tm, SSM_W), lambda j, i: (i, 0)),
                  pl.BlockSpec((D, tn), lambda j, i: (0, j + off_a), pipeline_mode=once),
                  pl.BlockSpec((D, tn), lambda j, i: (0, j + off_a + nj), pipeline_mode=once),
                  pl.BlockSpec((1, tn), lambda j, i: (0, j)),
                  pl.BlockSpec((1, tn), lambda j, i: (0, j + nj)),
                  pl.BlockSpec((ATTN_OUT, tn), lambda j, i: (0, j), pipeline_mode=once),
                  pl.BlockSpec((SSM_W, tn), lambda j, i: (0, j), pipeline_mode=once)],
        out_specs=pl.BlockSpec((tm, tn), lambda j, i: (i, j)),
        scratch_shapes=[pltpu.VMEM((D, tn), BF16), pltpu.VMEM((D, tn), BF16),
                        pltpu.VMEM((ATTN_OUT, tn), BF16), pltpu.VMEM((SSM_W, tn), BF16)],
        compiler_params=_cparams(("arbitrary", "arbitrary")),
        name="gates_branch_merge",
    )(h, attn, ssm, w_in, w_in, b_gate, b_gate, w_up_attn, w_up_ssm)


OUTPROJ_TB = 64
OUTPROJ_PITCH = OUTPROJ_TB + 8


def _outproj_kernel(m_ref, w_ref, x_ref, o_ref, slab, wbf_ref):
    _cast_weight_once(w_ref, wbf_ref)
    tb = OUTPROJ_TB
    a = m_ref[...].reshape(R * tb, D)
    per = MXU_N // LANES
    for c in range(w_ref.shape[1] // MXU_N):
        acc = jnp.dot(a, wbf_ref[:, c * MXU_N:(c + 1) * MXU_N], preferred_element_type=F32)
        for s_ in range(per):
            lanes = slice(s_ * LANES, (s_ + 1) * LANES)
            for r in range(R):
                slab[c * per + s_, r * OUTPROJ_PITCH:r * OUTPROJ_PITCH + tb, :] = acc[r * tb:(r + 1) * tb, lanes]
        for s_ in range(per):
            lanes = slice((c * per + s_) * LANES, (c * per + s_ + 1) * LANES)
            for i in range(tb):
                rows = slice(i * R, (i + 1) * R)
                o_ref[rows, lanes] = slab[c * per + s_, pl.ds(i, R, stride=OUTPROJ_PITCH), :] + x_ref[rows, lanes]


def out_proj(merged3, w_out, x2):
    tb, tn = OUTPROJ_TB, 1024
    return pl.pallas_call(
        _outproj_kernel,
        out_shape=jax.ShapeDtypeStruct((T, D), F32),
        grid=(D // tn, NI // tb),
        in_specs=[pl.BlockSpec((R, tb, D), lambda j, i: (0, i, 0)),
                  pl.BlockSpec((D, tn), lambda j, i: (0, j), pipeline_mode=pl.Buffered(1)),
                  pl.BlockSpec((tb * R, tn), lambda j, i: (i, j))],
        out_specs=pl.BlockSpec((tb * R, tn), lambda j, i: (i, j)),
        scratch_shapes=[pltpu.VMEM((tn // LANES, OUTPROJ_PITCH * R, LANES), F32),
                        pltpu.VMEM((D, tn), BF16)],
        compiler_params=_cparams(("arbitrary", "arbitrary")),
        name="out_proj_residual",
    )(merged3, w_out, x2)


PACK_ROWS = 8
HALF_D = D // 2


def _pack_bf16_pairs(zf):
    top = lax.bitcast_convert_type(zf, jnp.uint32)
    return top[:, HALF_D:] | (top[:, :HALF_D] >> 16)


def _unpack_bf16_pairs(x_ref, first, n):
    lo, hi = [], []
    for c in range(PACK_ROWS):
        w = x_ref[pl.ds(first * PACK_ROWS + c, n, stride=PACK_ROWS), :]
        lo.append(lax.bitcast_convert_type(w << 16, F32))
        hi.append(lax.bitcast_convert_type(w & jnp.uint32(0xFFFF0000), F32))
    return jnp.concatenate(lo + hi, axis=1)


def _router_kernel(x_ref, g_ref, w_ref, b_ref, h_ref, id_ref, wt_ref, cnt_ref, carry_ref):
    step = pl.program_id(0)

    @pl.when(step == 0)
    def _():
        carry_ref[...] = jnp.zeros_like(carry_ref)

    x = x_ref[...]
    ms = jnp.mean(x * x, axis=-1, keepdims=True)
    z = x * lax.rsqrt(ms + NORM_EPS) * g_ref[...]
    zh = z.astype(BF16)
    zf = zh.astype(F32)
    word = _pack_bf16_pairs(zf)
    for c in range(PACK_ROWS):
        h_ref[pl.ds(c, x.shape[0], stride=PACK_ROWS), :] = word[:, c * LANES:(c + 1) * LANES]
    zl = (z - zf).astype(BF16)
    w = w_ref[...]
    wh = w.astype(BF16)
    wl = (w - wh.astype(F32)).astype(BF16)
    logits = (jnp.dot(zh, wh, preferred_element_type=F32) + jnp.dot(zl, wh, preferred_element_type=F32)
              + jnp.dot(zh, wl, preferred_element_type=F32)) + b_ref[...]
    lane = lax.broadcasted_iota(jnp.int32, logits.shape, 1)
    lanef = lane.astype(F32)
    neg = jnp.float32(-jnp.inf)
    big = jnp.float32(1e9)
    gl = jnp.where(lane < N_EGROUPS, logits, neg)
    gmax = jnp.max(gl, axis=-1, keepdims=True)
    gidx = jnp.min(jnp.where(gl == gmax, lanef, big), axis=-1, keepdims=True)
    pg = 1.0 / jnp.sum(jnp.exp(gl - gmax), axis=-1, keepdims=True)
    lo = N_EGROUPS + EXPERTS_PER_GROUP * gidx
    el = jnp.where((lanef >= lo) & (lanef < lo + EXPERTS_PER_GROUP), logits, neg)
    t1 = jnp.max(el, axis=-1, keepdims=True)
    j1 = jnp.min(jnp.where(el == t1, lanef, big), axis=-1, keepdims=True)
    el2 = jnp.where(lanef == j1, neg, el)
    t2 = jnp.max(el2, axis=-1, keepdims=True)
    j2 = jnp.min(jnp.where(el2 == t2, lanef, big), axis=-1, keepdims=True)
    e21 = jnp.exp(t2 - t1)
    w1 = pg / (1.0 + e21)
    w2 = pg * e21 / (1.0 + e21)
    e1f = j1 - N_EGROUPS
    e2f = j2 - N_EGROUPS

    tm = x.shape[0]
    oh1 = (lanef == e1f).astype(F32)
    oh2 = (lanef == e2f).astype(F32)
    ri = lax.broadcasted_iota(jnp.int32, (tm, tm), 0)
    ci = lax.broadcasted_iota(jnp.int32, (tm, tm), 1)
    before = (ci < ri).astype(BF16)
    p1 = jnp.dot(before, oh1.astype(BF16), preferred_element_type=F32)
    p2 = jnp.dot(before, oh2.astype(BF16), preferred_element_type=F32)
    carry = carry_ref[...]
    c1 = jnp.sum(oh1, axis=0, keepdims=True)
    c2 = jnp.sum(oh2, axis=0, keepdims=True)
    rank1 = jnp.sum(oh1 * (carry + p1), axis=-1, keepdims=True)
    rank2 = jnp.sum(oh2 * (carry + c1 + p2), axis=-1, keepdims=True)
    carry = carry + c1 + c2
    carry_ref[...] = carry
    cnt_ref[...] = jnp.broadcast_to(carry, cnt_ref.shape).astype(jnp.int32)

    ids = jnp.where(lane == 0, e1f, jnp.where(lane == 1, e2f, jnp.where(lane == 2, rank1, jnp.where(lane == 3, rank2, 0.0))))
    id_ref[...] = ids.astype(jnp.int32)
    wt_ref[...] = jnp.where(lane == 0, w1, jnp.where(lane == 1, w2, 0.0))


def router(x1, gain, w_r, b_r):
    tm = 256
    return pl.pallas_call(
        _router_kernel,
        out_shape=(jax.ShapeDtypeStruct((T * PACK_ROWS, LANES), jnp.uint32),
                   jax.ShapeDtypeStruct((T, LANES), jnp.int32),
                   jax.ShapeDtypeStruct((T, LANES), F32),
                   jax.ShapeDtypeStruct((8, LANES), jnp.int32)),
        grid=(T // tm,),
        in_specs=[pl.BlockSpec((tm, D), lambda i: (i, 0)),
                  pl.BlockSpec((1, D), lambda i: (0, 0)),
                  pl.BlockSpec((D, LANES), lambda i: (0, 0)),
                  pl.BlockSpec((1, LANES), lambda i: (0, 0))],
        out_specs=(pl.BlockSpec((tm * PACK_ROWS, LANES), lambda i: (i, 0)),
                   pl.BlockSpec((tm, LANES), lambda i: (i, 0)),
                   pl.BlockSpec((tm, LANES), lambda i: (i, 0)),
                   pl.BlockSpec((8, LANES), lambda i: (0, 0))),
        scratch_shapes=[pltpu.VMEM((1, LANES), F32)],
        compiler_params=_cparams(("arbitrary",)),
        name="ffn_norm_router",
    )(x1, gain, w_r, b_r)


GATHER_PIECES = 2 * (EXPERT_FF // MXU_N) + D // MXU_N
ROWS_PER_PIECE = -(-MOE_BLOCK // GATHER_PIECES)


def _expert_kernel(be_ref, nu_ref, ord_ref, seq_ref, dest_ref, h_hbm, wg_hbm, wu_hbm, wd_hbm, y_ref,
                   rt_ref, xbuf, gsem, wg_st, wu_st, wd_st, wg_bf, wu_bf, wd_bf, sem):
    b = pl.program_id(0)
    n_used = nu_ref[0]

    def weight_copies(e, slot):
        return (pltpu.make_async_copy(wg_hbm.at[e], wg_st.at[slot], sem.at[slot, 0]),
                pltpu.make_async_copy(wu_hbm.at[e], wu_st.at[slot], sem.at[slot, 1]),
                pltpu.make_async_copy(wd_hbm.at[e], wd_st.at[slot], sem.at[slot, 2]))

    def start_fetch(n, slot):
        @pl.when(seq_ref[n] >= 0)
        def _():
            for cp in weight_copies(seq_ref[n], slot):
                cp.start(priority=1)

    def row_copy(blk, j, slot):
        tok = rt_ref[blk * MOE_BLOCK + j]
        src = h_hbm.at[pl.ds(pl.multiple_of(tok * PACK_ROWS, PACK_ROWS), PACK_ROWS)]
        dst = xbuf.at[slot, pl.ds(pl.multiple_of(j * PACK_ROWS, PACK_ROWS), PACK_ROWS)]
        return pltpu.make_async_copy(src, dst, gsem.at[slot])

    def wait_rows(blk, slot):
        def body(j, c):
            row_copy(blk, j, slot).wait()
            return c
        lax.fori_loop(0, MOE_BLOCK, body, 0, unroll=8)

    @pl.when(b == 0)
    def _():
        start_fetch(0, 0)
        start_fetch(1, 1)

        def clear(p, c):
            rt_ref[p] = 0
            return c
        lax.fori_loop(0, MOE_ROWS, clear, 0, unroll=8)

        def invert(a, c):
            rt_ref[dest_ref[a]] = a // TOP_K
            return c
        lax.fori_loop(0, N_ASSIGN, invert, 0, unroll=8)

        def first(j, c):
            row_copy(0, j, 0).start()
            return c
        lax.fori_loop(0, MOE_BLOCK, first, 0, unroll=8)

    @pl.when(b < n_used)
    def _():
        n = ord_ref[b]
        wslot = n % 2
        slot = b % 2
        nxt = jnp.minimum(b + 1, n_used - 1)
        changed = jnp.logical_or(b == 0, be_ref[b] != be_ref[jnp.maximum(b - 1, 0)])

        @pl.when(changed)
        def _():
            cg, cu, cd = weight_copies(be_ref[b], wslot)
            cg.wait()
            wg_bf[...] = wg_st[wslot].astype(BF16)
            cu.wait()
            wu_bf[...] = wu_st[wslot].astype(BF16)
            cd.wait()
            wd_bf[...] = wd_st[wslot].astype(BF16)
            start_fetch(n + 2, wslot)

        wait_rows(b, slot)
        x = _unpack_bf16_pairs(xbuf.at[slot], 0, MOE_BLOCK).astype(BF16)

        piece = [0]

        def issue_piece():
            lo = piece[0] * ROWS_PER_PIECE
            for j in range(lo, min(lo + ROWS_PER_PIECE, MOE_BLOCK)):
                row_copy(nxt, j, 1 - slot).start()
            piece[0] += 1

        gate, up = [], []
        for c in range(EXPERT_FF // MXU_N):
            cols = slice(c * MXU_N, (c + 1) * MXU_N)
            issue_piece()
            gate.append(jnp.dot(x, wg_bf[:, cols], preferred_element_type=F32))
            issue_piece()
            up.append(jnp.dot(x, wu_bf[:, cols], preferred_element_type=F32))
        hid = (jax.nn.silu(jnp.concatenate(gate, axis=1)) * jnp.concatenate(up, axis=1)).astype(BF16)
        ys = []
        for c in range(D // MXU_N):
            issue_piece()
            ys.append(jnp.dot(hid, wd_bf[:, c * MXU_N:(c + 1) * MXU_N], preferred_element_type=F32))
        y = jnp.concatenate(ys, axis=1)
        word = _pack_bf16_pairs(y.astype(BF16).astype(F32))
        for c in range(PACK_ROWS):
            y_ref[pl.ds(c, MOE_BLOCK, stride=PACK_ROWS), :] = word[:, c * LANES:(c + 1) * LANES]

        @pl.when(b == n_used - 1)
        def _():
            wait_rows(nxt, 1 - slot)

    @pl.when(b >= n_used)
    def _():
        y_ref[...] = jnp.zeros_like(y_ref)


def experts(block_expert, n_used, block_ord, expert_seq, dest, hpk, w_gate, w_up, w_down):
    grid_spec = pltpu.PrefetchScalarGridSpec(
        num_scalar_prefetch=5,
        grid=(MOE_BLOCKS,),
        in_specs=[pl.BlockSpec(memory_space=pl.ANY),
                  pl.BlockSpec(memory_space=pl.ANY),
                  pl.BlockSpec(memory_space=pl.ANY),
                  pl.BlockSpec(memory_space=pl.ANY)],
        out_specs=pl.BlockSpec((MOE_BLOCK * PACK_ROWS, LANES), lambda b, be, nu, od, sq, ds: (b, 0)),
        scratch_shapes=[pltpu.SMEM((MOE_ROWS,), jnp.int32),
                        pltpu.VMEM((2, MOE_BLOCK * PACK_ROWS, LANES), jnp.uint32),
                        pltpu.SemaphoreType.DMA((2,)),
                        pltpu.VMEM((2, D, EXPERT_FF), F32),
                        pltpu.VMEM((2, D, EXPERT_FF), F32),
                        pltpu.VMEM((2, EXPERT_FF, D), F32),
                        pltpu.VMEM((D, EXPERT_FF), BF16),
                        pltpu.VMEM((D, EXPERT_FF), BF16),
                        pltpu.VMEM((EXPERT_FF, D), BF16),
                        pltpu.SemaphoreType.DMA((2, 3))],
    )
    return pl.pallas_call(
        _expert_kernel,
        out_shape=jax.ShapeDtypeStruct((MOE_ROWS * PACK_ROWS, LANES), jnp.uint32),
        grid_spec=grid_spec,
        compiler_params=_cparams(("arbitrary",)),
        name="moe_experts",
    )(block_expert, n_used, block_ord, expert_seq, dest, hpk, w_gate, w_up, w_down)


COMBINE_ROWS = 256
COMBINE_SUB = 16


def _combine_kernel(dest_ref, ys_hbm, x_ref, wt_ref, g_ref, o_ref, ybuf, sem):
    s = pl.program_id(0)
    ns = pl.num_programs(0)
    slot = s % 2
    tb = COMBINE_SUB
    rows = COMBINE_ROWS
    nxt = jnp.minimum(s + 1, ns - 1)

    def row_copy(step, n, k, sl):
        tok = step * rows + n
        src = ys_hbm.at[pl.ds(pl.multiple_of(dest_ref[tok * TOP_K + k] * PACK_ROWS, PACK_ROWS), PACK_ROWS)]
        dst = ybuf.at[sl * TOP_K + k, pl.ds(pl.multiple_of(n * PACK_ROWS, PACK_ROWS), PACK_ROWS)]
        return pltpu.make_async_copy(src, dst, sem.at[sl])

    def wait_all(step, sl):
        def body(n, c):
            for k in range(TOP_K):
                row_copy(step, n, k, sl).wait()
            return c
        lax.fori_loop(0, rows, body, 0, unroll=8)

    @pl.when(s == 0)
    def _():
        def body(n, c):
            for k in range(TOP_K):
                row_copy(0, n, k, 0).start(priority=k)
            return c
        lax.fori_loop(0, rows, body, 0, unroll=8)

    wait_all(s, slot)

    g = g_ref[...]
    y0_ref = ybuf.at[slot * TOP_K]
    y1_ref = ybuf.at[slot * TOP_K + 1]
    for b in range(rows // tb):
        for n in range(b * tb, (b + 1) * tb):
            for k in range(TOP_K):
                row_copy(nxt, n, k, 1 - slot).start(priority=k)
        sub = slice(b * tb, (b + 1) * tb)
        w = wt_ref[sub, :]
        y0 = _unpack_bf16_pairs(y0_ref, b * tb, tb)
        y1 = _unpack_bf16_pairs(y1_ref, b * tb, tb)
        z = x_ref[sub, :] + (w[:, 0:1] * y0 + w[:, 1:2] * y1)
        ms = jnp.mean(z * z, axis=-1, keepdims=True)
        o_ref[sub, :] = z * lax.rsqrt(ms + NORM_EPS) * g

    @pl.when(s == ns - 1)
    def _():
        wait_all(nxt, 1 - slot)


def combine(dest, ys, x1, wts, gain):
    rows = COMBINE_ROWS
    grid_spec = pltpu.PrefetchScalarGridSpec(
        num_scalar_prefetch=1,
        grid=(T // rows,),
        in_specs=[pl.BlockSpec(memory_space=pl.ANY),
                  pl.BlockSpec((rows, D), lambda s, dst: (s, 0)),
                  pl.BlockSpec((rows, LANES), lambda s, dst: (s, 0)),
                  pl.BlockSpec((1, D), lambda s, dst: (0, 0))],
        out_specs=pl.BlockSpec((rows, D), lambda s, dst: (s, 0)),
        scratch_shapes=[pltpu.VMEM((2 * TOP_K, rows * PACK_ROWS, LANES), jnp.uint32),
                        pltpu.SemaphoreType.DMA((2,))],
    )
    return pl.pallas_call(
        _combine_kernel,
        out_shape=jax.ShapeDtypeStruct((T, D), F32),
        grid_spec=grid_spec,
        compiler_params=_cparams(("arbitrary",)),
        name="moe_combine_final_norm",
    )(dest, ys, x1, wts, gain)


def dispatch_plan(ids, counts):
    experts_ = jnp.arange(N_EXPERTS, dtype=jnp.int32)
    padded = (counts + MOE_BLOCK - 1) // MOE_BLOCK * MOE_BLOCK
    pad_end = jnp.cumsum(padded)
    pad_start = pad_end - padded
    e = ids[:, :TOP_K]
    start_of = jnp.sum(jnp.where(e[:, :, None] == experts_[None, None, :], pad_start[None, None, :], 0), axis=-1)
    dest = (start_of + ids[:, TOP_K:2 * TOP_K]).reshape(N_ASSIGN).astype(jnp.int32)
    n_used = pad_end[-1] // MOE_BLOCK
    block_start = jnp.minimum(jnp.arange(MOE_BLOCKS, dtype=jnp.int32), n_used - 1) * MOE_BLOCK
    block_expert = jnp.sum((block_start[:, None] >= pad_end[None, :]).astype(jnp.int32), axis=1)
    block_expert = jnp.minimum(block_expert, N_EXPERTS - 1).astype(jnp.int32)
    present = counts > 0
    expert_ord = jnp.cumsum(present.astype(jnp.int32)) - 1
    slots = jnp.arange(N_EXPERTS + 2, dtype=jnp.int32)
    hit = present[None, :] & (expert_ord[None, :] == slots[:, None])
    expert_seq = jnp.where(jnp.any(hit, axis=1), jnp.sum(jnp.where(hit, experts_[None, :], 0), axis=1), -1)
    block_ord = jnp.sum(jnp.where(block_expert[:, None] == experts_[None, :], expert_ord[None, :], 0), axis=1)
    return (block_expert, n_used.astype(jnp.int32).reshape(1), dest,
            block_ord.astype(jnp.int32), expert_seq.astype(jnp.int32))


def kernel(x, norm_mix, w_in, b_gate, ssm_a_re, ssm_a_im, ssm_log_step, ssm_b_re, ssm_b_im, ssm_c_re, ssm_c_im, ssm_d, w_glu, b_glu, w_up_attn, w_up_ssm, w_out, norm_ffn, w_router_group, b_router_group, w_router_expert, b_router_expert, w_expert_gate, w_expert_up, w_expert_down, norm_final):
    x2 = x.reshape(T, D)
    h = norm_permute(x2, norm_mix.reshape(1, D)).reshape(T, D)
    w_in_l = w_in.reshape(D, IN_COLS)
    qkv = proj(h, w_in_l, 0, 3 * QKV_COLS, tn=QKV_COLS, name="proj_qkv")
    u = proj(h, w_in_l, 3 * QKV_COLS, SSM_W, tn=512, name="proj_ssm_in")

    attn = attention(qkv.reshape(R, NI, 3 * QKV_COLS)).reshape(T, ATTN_OUT)

    G = SSM_W // SSM_CH
    wb, wc, pows = ssm_params(
        ssm_a_re.reshape(G, SSM_STATE).astype(F32), ssm_a_im.reshape(G, SSM_STATE).astype(F32),
        ssm_log_step.reshape(G),
        ssm_b_re.reshape(G, SSM_STATE, SSM_CH).astype(F32), ssm_b_im.reshape(G, SSM_STATE, SSM_CH).astype(F32),
        ssm_c_re.reshape(G, SSM_CH, SSM_STATE), ssm_c_im.reshape(G, SSM_CH, SSM_STATE))
    yg = ssm_scan(u.reshape(R, NI, SSM_W), wb, wc, pows, ssm_d.reshape(1, SSM_W).astype(F32))
    ssm = glu(yg.reshape(T, SSM_W), w_glu.reshape(SSM_W, SSM_W), b_glu.reshape(1, SSM_W))

    merged = merge(h, attn, ssm, w_in_l, b_gate.reshape(1, 2 * D),
                   w_up_attn.reshape(ATTN_OUT, D), w_up_ssm.reshape(SSM_W, D))
    x1 = out_proj(merged.reshape(R, NI, D), w_out.reshape(D, D), x2)

    w_r = jnp.concatenate([w_router_group.reshape(D, N_EGROUPS), w_router_expert.reshape(D, N_EXPERTS),
                           jnp.zeros((D, LANES - N_EGROUPS - N_EXPERTS), F32)], axis=1)
    b_r = jnp.concatenate([b_router_group.reshape(1, N_EGROUPS), b_router_expert.reshape(1, N_EXPERTS),
                           jnp.zeros((1, LANES - N_EGROUPS - N_EXPERTS), F32)], axis=1)
    hpk, ids, wts, counts = router(x1, norm_ffn.reshape(1, D), w_r, b_r)

    block_expert, n_used, dest, block_ord, expert_seq = dispatch_plan(ids[:, :2 * TOP_K], counts[0, :N_EXPERTS])
    ys = experts(block_expert, n_used, block_ord, expert_seq, dest, hpk,
                 w_expert_gate.reshape(N_EXPERTS, D, EXPERT_FF), w_expert_up.reshape(N_EXPERTS, D, EXPERT_FF),
                 w_expert_down.reshape(N_EXPERTS, EXPERT_FF, D))
    out = combine(dest, ys, x1, wts, norm_final.reshape(1, D))
    return out.reshape(1, T, D)
```

```python
import functools
import math

import jax
import jax.numpy as jnp
from jax import lax
from jax.experimental import pallas as pl
from jax.experimental.pallas import tpu as pltpu

F32 = jnp.float32
BF16 = jnp.bfloat16

T = 8192
D = 2048
R = 16
NI = T // R
HEAD_DIM = 64
N_HEAD_SLOTS = 8
DILATIONS = (1, 4, 16)
ATTN_BLOCK = 128
QKV_COLS = 1536
ATTN_OUT = 512
SSM_W = 1024
SSM_STATE = 64
SSM_CH = 16
IN_COLS = 3 * QKV_COLS + SSM_W + 2 * D
N_EXPERTS = 32
N_EGROUPS = 4
EXPERTS_PER_GROUP = 8
TOP_K = 2
EXPERT_FF = 512
NORM_EPS = 1e-6
LANES = 128
VMEM_LIMIT = 48 * 1024 * 1024

MOE_BLOCK = 256
N_ASSIGN = T * TOP_K
MOE_BLOCKS = N_ASSIGN // MOE_BLOCK + N_EXPERTS
MOE_ROWS = MOE_BLOCKS * MOE_BLOCK


def _cparams(sem):
    return pltpu.CompilerParams(dimension_semantics=sem, vmem_limit_bytes=VMEM_LIMIT)


N_SLABS = D // LANES
NORM_TB = 32
NORM_CHUNK = 64
NORM_PITCH = R + 8


def _norm_permute_kernel(x_ref, g_ref, h_ref, slab):
    g = g_ref[...]

    def chunk(t, c):
        rows = pl.ds(pl.multiple_of(t * NORM_CHUNK, NORM_CHUNK), NORM_CHUNK)
        x = x_ref[rows, :]
        ms = jnp.mean(x * x, axis=-1, keepdims=True)
        hn = x * lax.rsqrt(ms + NORM_EPS) * g
        for k in range(NORM_CHUNK // R):
            dst = pl.ds(pl.multiple_of((t * (NORM_CHUNK // R) + k) * NORM_PITCH, 8), R)
            for s_ in range(N_SLABS):
                slab[s_, dst, :] = hn[k * R:(k + 1) * R, s_ * LANES:(s_ + 1) * LANES]
        return c

    lax.fori_loop(0, NORM_TB * R // NORM_CHUNK, chunk, 0)
    for r in range(R):
        pieces = [slab[s_, pl.ds(r, NORM_TB, stride=NORM_PITCH), :] for s_ in range(N_SLABS)]
        h_ref[r] = jnp.concatenate(pieces, axis=1).astype(BF16)


def norm_permute(x2, gain):
    return pl.pallas_call(
        _norm_permute_kernel,
        out_shape=jax.ShapeDtypeStruct((R, NI, D), BF16),
        grid=(NI // NORM_TB,),
        in_specs=[pl.BlockSpec((NORM_TB * R, D), lambda i: (i, 0)),
                  pl.BlockSpec((1, D), lambda i: (0, 0))],
        out_specs=pl.BlockSpec((R, NORM_TB, D), lambda i: (0, i, 0)),
        scratch_shapes=[pltpu.VMEM((N_SLABS, NORM_TB * NORM_PITCH, LANES), F32)],
        compiler_params=_cparams(("arbitrary",)),
        name="norm_permute",
    )(x2, gain)


MXU_N = 256


def _cast_weight_once(w_ref, wbf_ref):
    @pl.when(pl.program_id(1) == 0)
    def _():
        wbf_ref[...] = w_ref[...].astype(BF16)


def _proj_kernel(a_ref, w_ref, o_ref, wbf_ref):
    _cast_weight_once(w_ref, wbf_ref)
    a = a_ref[...]
    for c in range(w_ref.shape[1] // MXU_N):
        cols = slice(c * MXU_N, (c + 1) * MXU_N)
        o_ref[:, cols] = jnp.dot(a, wbf_ref[:, cols], preferred_element_type=F32)


def proj(h, w_in, col_off, n_cols, tn, name="proj"):
    tm = 1024
    off = col_off // tn
    return pl.pallas_call(
        _proj_kernel,
        out_shape=jax.ShapeDtypeStruct((T, n_cols), F32),
        grid=(n_cols // tn, T // tm),
        in_specs=[pl.BlockSpec((tm, D), lambda j, i: (i, 0)),
                  pl.BlockSpec((D, tn), lambda j, i: (0, j + off), pipeline_mode=pl.Buffered(1))],
        out_specs=pl.BlockSpec((tm, tn), lambda j, i: (i, j)),
        scratch_shapes=[pltpu.VMEM((D, tn), BF16)],
        compiler_params=_cparams(("arbitrary", "arbitrary")),
        name=name,
    )(h, w_in)


def _seq_index_maps(d):
    nseg = R // d
    qlen = ATTN_BLOCK // nseg
    return nseg, qlen


def _bias_matrices(d, hp):
    nseg, qlen = _seq_index_maps(d)
    klen = 2 * qlen
    row = lax.broadcasted_iota(jnp.int32, (2 * ATTN_BLOCK, 2 * ATTN_BLOCK), 0)
    col = lax.broadcasted_iota(jnp.int32, (2 * ATTN_BLOCK, 2 * ATTN_BLOCK), 1)
    rho = row % ATTN_BLOCK
    jq = (rho % qlen) * nseg + rho // qlen
    jk = ((col % klen) - qlen) * nseg + col // klen
    steps = jq - jk
    valid = (steps >= 0) & (steps <= ATTN_BLOCK)
    head = 2 * hp + row // ATTN_BLOCK
    slope = lax.bitcast_convert_type((127 - (head + 1)) << 23, F32)
    bias = -slope * (d * steps).astype(F32)
    neg = jnp.float32(-jnp.inf)
    return jnp.where(valid, bias, neg), jnp.where(valid & (jk >= 0), bias, neg)


def _attend_pair(q, k, v, bias):
    lane = lax.broadcasted_iota(jnp.int32, (ATTN_BLOCK, LANES), 1)
    first = lane < HEAD_DIM
    zero = jnp.zeros_like(q)
    q2 = jnp.concatenate([jnp.where(first, q, zero), jnp.where(first, zero, q)], axis=0).astype(BF16)
    s = lax.dot_general(q2, k.astype(BF16), (((1,), (1,)), ((), ())), preferred_element_type=F32)
    s = s + bias
    m = jnp.max(s, axis=-1, keepdims=True)
    p = jnp.exp(s - m)
    l = jnp.sum(p, axis=-1, keepdims=True)
    o2 = jnp.dot(p.astype(BF16), v.astype(BF16), preferred_element_type=F32)
    o = jnp.where(first, o2[:ATTN_BLOCK], o2[ATTN_BLOCK:])
    m_b = jnp.where(first, m[:ATTN_BLOCK], m[ATTN_BLOCK:])
    l_b = jnp.where(first, l[:ATTN_BLOCK], l[ATTN_BLOCK:])
    return o, m_b, l_b


def _attn_kernel(q_ref, kp_ref, kc_ref, vp_ref, vc_ref, o_ref, kbuf, vbuf, obuf, mbuf, lbuf, bias_ref):
    hp = pl.program_id(0)
    it = pl.program_id(1)
    g = pl.program_id(2)
    scale = HEAD_DIM ** -0.5

    kbuf[:, :ATTN_BLOCK, :] = kp_ref[...]
    kbuf[:, ATTN_BLOCK:, :] = kc_ref[...]
    vbuf[:, :ATTN_BLOCK, :] = vp_ref[...]
    vbuf[:, ATTN_BLOCK:, :] = vc_ref[...]

    for gi, d in enumerate(DILATIONS):
        nseg, qlen = _seq_index_maps(d)
        klen = 2 * qlen
        nblk = ATTN_BLOCK // qlen

        @pl.when(g == gi)
        def _(gi=gi, d=d, nseg=nseg, qlen=qlen, klen=klen, nblk=nblk):
            @pl.when(it == 0)
            def _():
                b_reg, b_first = _bias_matrices(d, hp)
                bias_ref[gi, 0] = b_reg
                bias_ref[gi, 1] = b_first

            def block(idx, carry):
                rd = idx // nblk
                bb = idx % nblk
                q0 = pl.multiple_of(bb * qlen, qlen)
                k0 = pl.multiple_of(ATTN_BLOCK + bb * qlen - qlen, qlen)
                qs, ks, vs = [], [], []
                for m_ in range(nseg):
                    rr = rd + d * m_
                    qs.append(q_ref[rr, pl.ds(q0, qlen), :])
                    ks.append(kbuf[rr, pl.ds(k0, klen), :])
                    vs.append(vbuf[rr, pl.ds(k0, klen), :])
                q = jnp.concatenate(qs, axis=0) * scale
                k = jnp.concatenate(ks, axis=0)
                v = jnp.concatenate(vs, axis=0)
                is_first = jnp.logical_and(it == 0, bb == 0)
                bias = bias_ref[gi, jnp.where(is_first, 1, 0)]
                o, mx, den = _attend_pair(q, k, v, bias)
                for m_ in range(nseg):
                    rr = rd + d * m_
                    seg = slice(m_ * qlen, (m_ + 1) * qlen)
                    obuf[gi, rr, pl.ds(q0, qlen), :] = o[seg]
                    mbuf[gi, rr, pl.ds(q0, qlen), :] = mx[seg]
                    lbuf[gi, rr, pl.ds(q0, qlen), :] = den[seg]
                return carry

            lax.fori_loop(0, d * nblk, block, 0, unroll=True)

    @pl.when(g == len(DILATIONS) - 1)
    def _():
        for r in range(R):
            m0, m1, m2 = mbuf[0, r], mbuf[1, r], mbuf[2, r]
            mx = jnp.maximum(jnp.maximum(m0, m1), m2)
            e0, e1, e2 = jnp.exp(m0 - mx), jnp.exp(m1 - mx), jnp.exp(m2 - mx)
            den = e0 * lbuf[0, r] + e1 * lbuf[1, r] + e2 * lbuf[2, r]
            num = e0 * obuf[0, r] + e1 * obuf[1, r] + e2 * obuf[2, r]
            o_ref[r] = (num / den).astype(BF16)


def attention(qkv3):
    n_hp = N_HEAD_SLOTS // 2
    n_it = NI // ATTN_BLOCK
    ng = len(DILATIONS)
    cb = QKV_COLS // LANES

    def cur(base):
        return pl.BlockSpec((R, ATTN_BLOCK, LANES), lambda hp, it, g: (0, it, base + g * n_hp + hp))

    def prev(base):
        return pl.BlockSpec((R, ATTN_BLOCK, LANES),
                            lambda hp, it, g: (0, jnp.maximum(it - 1, 0), base + g * n_hp + hp))

    return pl.pallas_call(
        _attn_kernel,
        out_shape=jax.ShapeDtypeStruct((R, NI, ATTN_OUT), BF16),
        grid=(n_hp, n_it, ng),
        in_specs=[cur(0), prev(cb), cur(cb), prev(2 * cb), cur(2 * cb)],
        out_specs=pl.BlockSpec((R, ATTN_BLOCK, LANES), lambda hp, it, g: (0, it, hp)),
        scratch_shapes=[pltpu.VMEM((R, 2 * ATTN_BLOCK, LANES), F32),
                        pltpu.VMEM((R, 2 * ATTN_BLOCK, LANES), F32),
                        pltpu.VMEM((ng, R, ATTN_BLOCK, LANES), F32),
                        pltpu.VMEM((ng, R, ATTN_BLOCK, LANES), F32),
                        pltpu.VMEM((ng, R, ATTN_BLOCK, LANES), F32),
                        pltpu.VMEM((ng, 2, 2 * ATTN_BLOCK, 2 * ATTN_BLOCK), F32)],
        compiler_params=_cparams(("arbitrary", "arbitrary", "arbitrary")),
        name="dilated_attention",
    )(qkv3, qkv3, qkv3, qkv3, qkv3)


SSM_SLAB = 256
SSM_SLABS = SSM_W // SSM_SLAB
SLAB_STATES = SSM_SLAB // SSM_CH * SSM_STATE
SSM_TI = 128
SSM_MM_CHUNK = 4


def _ssm_kernel(u_ref, wb_ref, wc_ref, pw_ref, dsk_ref, o_ref, s_ref, zs_ref, zc_ref, tr_ref, zt_ref):
    ic = pl.program_id(1)
    ns = SLAB_STATES

    @pl.when(ic == 0)
    def _():
        zc_ref[...] = jnp.zeros_like(zc_ref)

    n_chunks = R // SSM_MM_CHUNK
    n_tiles = SSM_TI // 8
    wb = wb_ref[0]
    wc = wc_ref[0]
    dsk = dsk_ref[...]
    arb = jnp.broadcast_to(pw_ref[0, 0:1, :ns], (8, ns))
    aib = jnp.broadcast_to(pw_ref[0, 0:1, ns:], (8, ns))


    def bu_chunk(c):
        lo = c * SSM_MM_CHUNK
        uc = u_ref[lo:lo + SSM_MM_CHUNK].reshape(SSM_MM_CHUNK * SSM_TI, SSM_SLAB)
        bu = jnp.dot(uc.astype(BF16), wb, preferred_element_type=F32)
        s_ref[lo:lo + SSM_MM_CHUNK] = bu.reshape(SSM_MM_CHUNK, SSM_TI, 2 * ns)

    def local_chunk(c):
        lo = c * SSM_MM_CHUNK
        first = max(lo, 1)
        for t in range(n_tiles):
            rows = slice(t * 8, (t + 1) * 8)
            pr = s_ref[first - 1, rows, :ns]
            pi = s_ref[first - 1, rows, ns:]
            for r in range(first, lo + SSM_MM_CHUNK):
                nr = s_ref[r, rows, :ns] + (arb * pr - aib * pi)
                ni = s_ref[r, rows, ns:] + (arb * pi + aib * pr)
                s_ref[r, rows, :ns] = nr
                s_ref[r, rows, ns:] = ni
                pr, pi = nr, ni

    bu_chunk(0)
    for c in range(n_chunks):
        if c + 1 < n_chunks:
            bu_chunk(c + 1)
        local_chunk(c)

    n_ch = ns // LANES
    for c in range(n_ch):
        tr_ref[0, c * 8:c * 8 + 1, :] = pw_ref[0, R - 1:R, c * LANES:(c + 1) * LANES]
        tr_ref[1, c * 8:c * 8 + 1, :] = pw_ref[0, R - 1:R, ns + c * LANES:ns + (c + 1) * LANES]
    a16r = tr_ref[0, pl.ds(0, n_ch, stride=8), :]
    a16i = tr_ref[1, pl.ds(0, n_ch, stride=8), :]

    def ztile(t, carry):
        zr, zi = carry
        rows = pl.ds(pl.multiple_of(t * 8, 8), 8)
        for c in range(n_ch):
            tr_ref[0, c * 8:(c + 1) * 8, :] = s_ref[R - 1, rows, c * LANES:(c + 1) * LANES]
            tr_ref[1, c * 8:(c + 1) * 8, :] = s_ref[R - 1, rows, ns + c * LANES:ns + (c + 1) * LANES]
        for i in range(8):
            zt_ref[0, pl.ds(i, n_ch, stride=8), :] = zr
            zt_ref[1, pl.ds(i, n_ch, stride=8), :] = zi
            er = tr_ref[0, pl.ds(i, n_ch, stride=8), :]
            ei = tr_ref[1, pl.ds(i, n_ch, stride=8), :]
            zr, zi = a16r * zr - a16i * zi + er, a16r * zi + a16i * zr + ei
        for c in range(n_ch):
            zs_ref[rows, c * LANES:(c + 1) * LANES] = zt_ref[0, c * 8:(c + 1) * 8, :]
            zs_ref[rows, ns + c * LANES:ns + (c + 1) * LANES] = zt_ref[1, c * 8:(c + 1) * 8, :]
        return zr, zi

    zr_end, zi_end = lax.fori_loop(0, SSM_TI // 8, ztile, (zc_ref[0], zc_ref[1]))
    zc_ref[0] = zr_end
    zc_ref[1] = zi_end

    def fix_chunk(c):
        lo = c * SSM_MM_CHUNK
        for r in range(lo, lo + SSM_MM_CHUNK):
            prb = jnp.broadcast_to(pw_ref[0, r:r + 1, :ns], (8, ns))
            pib = jnp.broadcast_to(pw_ref[0, r:r + 1, ns:], (8, ns))
            for t in range(n_tiles):
                rows = slice(t * 8, (t + 1) * 8)
                zr = zs_ref[rows, :ns]
                zi = zs_ref[rows, ns:]
                s_ref[r, rows, :ns] = s_ref[r, rows, :ns] + (prb * zr - pib * zi)
                s_ref[r, rows, ns:] = s_ref[r, rows, ns:] + (prb * zi + pib * zr)

    def out_chunk(c):
        lo = c * SSM_MM_CHUNK
        xs = s_ref[lo:lo + SSM_MM_CHUNK].reshape(SSM_MM_CHUNK * SSM_TI, 2 * ns)
        y = jnp.dot(xs.astype(BF16), wc, preferred_element_type=F32)
        y = y.reshape(SSM_MM_CHUNK, SSM_TI, SSM_SLAB) + dsk * u_ref[lo:lo + SSM_MM_CHUNK]
        o_ref[lo:lo + SSM_MM_CHUNK] = jax.nn.gelu(y)

    fix_chunk(0)
    for c in range(n_chunks):
        if c + 1 < n_chunks:
            fix_chunk(c + 1)
        out_chunk(c)


def ssm_scan(u3, wb, wc, pows, dskip):
    ns2 = 2 * SLAB_STATES
    return pl.pallas_call(
        _ssm_kernel,
        out_shape=jax.ShapeDtypeStruct((R, NI, SSM_W), F32),
        grid=(SSM_SLABS, NI // SSM_TI),
        in_specs=[pl.BlockSpec((R, SSM_TI, SSM_SLAB), lambda kb, ic: (0, ic, kb)),
                  pl.BlockSpec((1, SSM_SLAB, ns2), lambda kb, ic: (kb, 0, 0)),
                  pl.BlockSpec((1, ns2, SSM_SLAB), lambda kb, ic: (kb, 0, 0)),
                  pl.BlockSpec((1, R, ns2), lambda kb, ic: (kb, 0, 0)),
                  pl.BlockSpec((1, SSM_SLAB), lambda kb, ic: (0, kb))],
        out_specs=pl.BlockSpec((R, SSM_TI, SSM_SLAB), lambda kb, ic: (0, ic, kb)),
        scratch_shapes=[pltpu.VMEM((R, SSM_TI, ns2), F32),
                        pltpu.VMEM((SSM_TI, ns2), F32),
                        pltpu.VMEM((2, SLAB_STATES // LANES, LANES), F32),
                        pltpu.VMEM((2, 8 * SLAB_STATES // LANES, LANES), F32),
                        pltpu.VMEM((2, 8 * SLAB_STATES // LANES, LANES), F32)],
        compiler_params=_cparams(("arbitrary", "arbitrary")),
        name="s5_ssm",
    )(u3, wb, wc, pows, dskip)


def ssm_params(a_re, a_im, log_step, b_re, b_im, c_re, c_im):
    G, P, H = SSM_W // SSM_CH, SSM_STATE, SSM_CH
    gs = SSM_SLAB // SSM_CH
    step = jnp.exp(log_step.astype(F32))[:, None]
    kk = jnp.arange(1, R + 1, dtype=F32)[:, None, None]
    mag = jnp.exp(kk * (a_re * step))
    ang = kk * (a_im * step)
    pw_re, pw_im = mag * jnp.cos(ang), mag * jnp.sin(ang)
    abar_re, abar_im = pw_re[0], pw_im[0]
    nr, ni = abar_re - 1.0, abar_im
    den = a_re * a_re + a_im * a_im
    f_re = (nr * a_re + ni * a_im) / den
    f_im = (ni * a_re - nr * a_im) / den
    bbar_re = f_re[..., None] * b_re - f_im[..., None] * b_im
    bbar_im = f_re[..., None] * b_im + f_im[..., None] * b_re
    rows_g = jnp.arange(gs * H, dtype=jnp.int32) // H
    cols_g = jnp.arange(gs * P, dtype=jnp.int32) // P
    b_t = jnp.concatenate([jnp.swapaxes(bbar_re, 1, 2).reshape(SSM_SLABS, gs * H, P),
                           jnp.swapaxes(bbar_im, 1, 2).reshape(SSM_SLABS, gs * H, P)], axis=-1)
    tile_p = (jnp.arange(2 * P, dtype=jnp.int32)[:, None] ==
              (jnp.arange(2 * gs * P, dtype=jnp.int32) // (gs * P) * P + jnp.arange(2 * gs * P, dtype=jnp.int32) % P)[None, :])
    mask_b = rows_g[:, None] == jnp.tile(cols_g, 2)[None, :]
    wb = jnp.where(mask_b[None], jnp.einsum('kap,pc->kac', b_t.astype(BF16), tile_p.astype(BF16),
                                            preferred_element_type=F32), 0.0).astype(BF16)
    c_t = jnp.concatenate([jnp.swapaxes(c_re.astype(F32), 1, 2).reshape(SSM_SLABS, gs * P, H),
                           -jnp.swapaxes(c_im.astype(F32), 1, 2).reshape(SSM_SLABS, gs * P, H)], axis=1)
    tile_h = jnp.arange(H, dtype=jnp.int32)[:, None] == (jnp.arange(gs * H, dtype=jnp.int32) % H)[None, :]
    mask_c = jnp.tile(cols_g, 2)[:, None] == rows_g[None, :]
    wc = jnp.where(mask_c[None], jnp.einsum('kah,hc->kac', c_t.astype(BF16), tile_h.astype(BF16),
                                            preferred_element_type=F32), 0.0).astype(BF16)
    pows = jnp.concatenate([pw_re.reshape(R, SSM_SLABS, gs * P), pw_im.reshape(R, SSM_SLABS, gs * P)], axis=-1)
    return wb, wc, jnp.transpose(pows, (1, 0, 2))


def _glu_kernel(a_ref, w_ref, b_ref, o_ref, wbf_ref):
    @pl.when(pl.program_id(0) == 0)
    def _():
        wbf_ref[...] = w_ref[...].astype(BF16)

    a = a_ref[...].astype(BF16)
    for c in range(SSM_W // MXU_N):
        cols = slice(c * MXU_N, (c + 1) * MXU_N)
        acc = jnp.dot(a, wbf_ref[:, cols], preferred_element_type=F32)
        o_ref[:, cols] = (a_ref[:, cols] * jax.nn.sigmoid(acc + b_ref[:, cols])).astype(BF16)


def glu(yg, w_glu, b_glu):
    tm = 512
    return pl.pallas_call(
        _glu_kernel,
        out_shape=jax.ShapeDtypeStruct((T, SSM_W), BF16),
        grid=(T // tm,),
        in_specs=[pl.BlockSpec((tm, SSM_W), lambda i: (i, 0)),
                  pl.BlockSpec((SSM_W, SSM_W), lambda i: (0, 0)),
                  pl.BlockSpec((1, SSM_W), lambda i: (0, 0))],
        out_specs=pl.BlockSpec((tm, SSM_W), lambda i: (i, 0)),
        scratch_shapes=[pltpu.VMEM((SSM_W, SSM_W), BF16)],
        compiler_params=_cparams(("arbitrary",)),
        name="ssm_glu",
    )(yg, w_glu, b_glu)


def _merge_kernel(h_ref, at_ref, ss_ref, wga_ref, wgs_ref, ba_ref, bs_ref, wa_ref, ws_ref, o_ref,
                  wga_bf, wgs_bf, wa_bf, ws_bf):
    _cast_weight_once(wga_ref, wga_bf)
    _cast_weight_once(wgs_ref, wgs_bf)
    _cast_weight_once(wa_ref, wa_bf)
    _cast_weight_once(ws_ref, ws_bf)
    h = h_ref[...]
    at = at_ref[...]
    ss = ss_ref[...]
    for c in range(wa_ref.shape[1] // MXU_N):
        cols = slice(c * MXU_N, (c + 1) * MXU_N)
        ga = jax.nn.sigmoid(jnp.dot(h, wga_bf[:, cols], preferred_element_type=F32) + ba_ref[:, cols])
        a = jnp.dot(at, wa_bf[:, cols], preferred_element_type=F32)
        gs = jax.nn.sigmoid(jnp.dot(h, wgs_bf[:, cols], preferred_element_type=F32) + bs_ref[:, cols])
        s = jnp.dot(ss, ws_bf[:, cols], preferred_element_type=F32)
        o_ref[:, cols] = (ga * a + gs * s).astype(BF16)


def merge(h, attn, ssm, w_in, b_gate, w_up_attn, w_up_ssm):
    tm, tn = 1024, 512
    nj = D // tn
    off_a = (3 * QKV_COLS + SSM_W) // tn
    once = pl.Buffered(1)
    return pl.pallas_call(
        _merge_kernel,
        out_shape=jax.ShapeDtypeStruct((T, D), BF16),
        grid=(nj, T // tm),
        in_specs=[pl.BlockSpec((tm, D), lambda j, i: (i, 0)),
                  pl.BlockSpec((tm, ATTN_OUT), lambda j, i: (i, 0)),
                  pl.BlockSpec((tm, SSM_W), lambda j, i: (i, 0)),
                  pl.BlockSpec((D, tn), lambda j, i: (0, j + off_a), pipeline_mode=once),
                  pl.BlockSpec((D, tn), lambda j, i: (0, j + off_a + nj), pipeline_mode=once),
                  pl.BlockSpec((1, tn), lambda j, i: (0, j)),
                  pl.BlockSpec((1, tn), lambda j, i: (0, j + nj)),
                  pl.BlockSpec((ATTN_OUT, tn), lambda j, i: (0, j), pipeline_mode=once),
                  pl.BlockSpec((SSM_W, tn), lambda j, i: (0, j), pipeline_mode=once)],
        out_specs=pl.BlockSpec((tm, tn), lambda j, i: (i, j)),
        scratch_shapes=[pltpu.VMEM((D, tn), BF16), pltpu.VMEM((D, tn), BF16),
                        pltpu.VMEM((ATTN_OUT, tn), BF16), pltpu.VMEM((SSM_W, tn), BF16)],
        compiler_params=_cparams(("arbitrary", "arbitrary")),
        name="gates_branch_merge",
    )(h, attn, ssm, w_in, w_in, b_gate, b_gate, w_up_attn, w_up_ssm)


OUTPROJ_TB = 64
OUTPROJ_PITCH = OUTPROJ_TB + 8


def _outproj_kernel(m_ref, w_ref, x_ref, o_ref, slab, wbf_ref):
    _cast_weight_once(w_ref, wbf_ref)
    tb = OUTPROJ_TB
    a = m_ref[...].reshape(R * tb, D)
    per = MXU_N // LANES
    for c in range(w_ref.shape[1] // MXU_N):
        acc = jnp.dot(a, wbf_ref[:, c * MXU_N:(c + 1) * MXU_N], preferred_element_type=F32)
        for s_ in range(per):
            lanes = slice(s_ * LANES, (s_ + 1) * LANES)
            for r in range(R):
                slab[c * per + s_, r * OUTPROJ_PITCH:r * OUTPROJ_PITCH + tb, :] = acc[r * tb:(r + 1) * tb, lanes]
        for s_ in range(per):
            lanes = slice((c * per + s_) * LANES, (c * per + s_ + 1) * LANES)
            for i in range(tb):
                rows = slice(i * R, (i + 1) * R)
                o_ref[rows, lanes] = slab[c * per + s_, pl.ds(i, R, stride=OUTPROJ_PITCH), :] + x_ref[rows, lanes]


def out_proj(merged3, w_out, x2):
    tb, tn = OUTPROJ_TB, 1024
    return pl.pallas_call(
        _outproj_kernel,
        out_shape=jax.ShapeDtypeStruct((T, D), F32),
        grid=(D // tn, NI // tb),
        in_specs=[pl.BlockSpec((R, tb, D), lambda j, i: (0, i, 0)),
                  pl.BlockSpec((D, tn), lambda j, i: (0, j), pipeline_mode=pl.Buffered(1)),
                  pl.BlockSpec((tb * R, tn), lambda j, i: (i, j))],
        out_specs=pl.BlockSpec((tb * R, tn), lambda j, i: (i, j)),
        scratch_shapes=[pltpu.VMEM((tn // LANES, OUTPROJ_PITCH * R, LANES), F32),
                        pltpu.VMEM((D, tn), BF16)],
        compiler_params=_cparams(("arbitrary", "arbitrary")),
        name="out_proj_residual",
    )(merged3, w_out, x2)


PACK_ROWS = 8
HALF_D = D // 2


def _pack_bf16_pairs(zf):
    top = lax.bitcast_convert_type(zf, jnp.uint32)
    return top[:, HALF_D:] | (top[:, :HALF_D] >> 16)


def _unpack_bf16_pairs(x_ref, first, n):
    lo, hi = [], []
    for c in range(PACK_ROWS):
        w = x_ref[pl.ds(first * PACK_ROWS + c, n, stride=PACK_ROWS), :]
        lo.append(lax.bitcast_convert_type(w << 16, F32))
        hi.append(lax.bitcast_convert_type(w & jnp.uint32(0xFFFF0000), F32))
    return jnp.concatenate(lo + hi, axis=1)


def _router_kernel(x_ref, g_ref, w_ref, b_ref, h_ref, id_ref, wt_ref, cnt_ref, carry_ref):
    step = pl.program_id(0)

    @pl.when(step == 0)
    def _():
        carry_ref[...] = jnp.zeros_like(carry_ref)

    x = x_ref[...]
    ms = jnp.mean(x * x, axis=-1, keepdims=True)
    z = x * lax.rsqrt(ms + NORM_EPS) * g_ref[...]
    zh = z.astype(BF16)
    zf = zh.astype(F32)
    word = _pack_bf16_pairs(zf)
    for c in range(PACK_ROWS):
        h_ref[pl.ds(c, x.shape[0], stride=PACK_ROWS), :] = word[:, c * LANES:(c + 1) * LANES]
    zl = (z - zf).astype(BF16)
    w = w_ref[...]
    wh = w.astype(BF16)
    wl = (w - wh.astype(F32)).astype(BF16)
    logits = (jnp.dot(zh, wh, preferred_element_type=F32) + jnp.dot(zl, wh, preferred_element_type=F32)
              + jnp.dot(zh, wl, preferred_element_type=F32)) + b_ref[...]
    lane = lax.broadcasted_iota(jnp.int32, logits.shape, 1)
    lanef = lane.astype(F32)
    neg = jnp.float32(-jnp.inf)
    big = jnp.float32(1e9)
    gl = jnp.where(lane < N_EGROUPS, logits, neg)
    gmax = jnp.max(gl, axis=-1, keepdims=True)
    gidx = jnp.min(jnp.where(gl == gmax, lanef, big), axis=-1, keepdims=True)
    pg = 1.0 / jnp.sum(jnp.exp(gl - gmax), axis=-1, keepdims=True)
    lo = N_EGROUPS + EXPERTS_PER_GROUP * gidx
    el = jnp.where((lanef >= lo) & (lanef < lo + EXPERTS_PER_GROUP), logits, neg)
    t1 = jnp.max(el, axis=-1, keepdims=True)
    j1 = jnp.min(jnp.where(el == t1, lanef, big), axis=-1, keepdims=True)
    el2 = jnp.where(lanef == j1, neg, el)
    t2 = jnp.max(el2, axis=-1, keepdims=True)
    j2 = jnp.min(jnp.where(el2 == t2, lanef, big), axis=-1, keepdims=True)
    e21 = jnp.exp(t2 - t1)
    w1 = pg / (1.0 + e21)
    w2 = pg * e21 / (1.0 + e21)
    e1f = j1 - N_EGROUPS
    e2f = j2 - N_EGROUPS

    tm = x.shape[0]
    oh1 = (lanef == e1f).astype(F32)
    oh2 = (lanef == e2f).astype(F32)
    ri = lax.broadcasted_iota(jnp.int32, (tm, tm), 0)
    ci = lax.broadcasted_iota(jnp.int32, (tm, tm), 1)
    before = (ci < ri).astype(BF16)
    p1 = jnp.dot(before, oh1.astype(BF16), preferred_element_type=F32)
    p2 = jnp.dot(before, oh2.astype(BF16), preferred_element_type=F32)
    carry = carry_ref[...]
    c1 = jnp.sum(oh1, axis=0, keepdims=True)
    c2 = jnp.sum(oh2, axis=0, keepdims=True)
    rank1 = jnp.sum(oh1 * (carry + p1), axis=-1, keepdims=True)
    rank2 = jnp.sum(oh2 * (carry + c1 + p2), axis=-1, keepdims=True)
    carry = carry + c1 + c2
    carry_ref[...] = carry
    cnt_ref[...] = jnp.broadcast_to(carry, cnt_ref.shape).astype(jnp.int32)

    ids = jnp.where(lane == 0, e1f, jnp.where(lane == 1, e2f, jnp.where(lane == 2, rank1, jnp.where(lane == 3, rank2, 0.0))))
    id_ref[...] = ids.astype(jnp.int32)
    wt_ref[...] = jnp.where(lane == 0, w1, jnp.where(lane == 1, w2, 0.0))


def router(x1, gain, w_r, b_r):
    tm = 256
    return pl.pallas_call(
        _router_kernel,
        out_shape=(jax.ShapeDtypeStruct((T * PACK_ROWS, LANES), jnp.uint32),
                   jax.ShapeDtypeStruct((T, LANES), jnp.int32),
                   jax.ShapeDtypeStruct((T, LANES), F32),
                   jax.ShapeDtypeStruct((8, LANES), jnp.int32)),
        grid=(T // tm,),
        in_specs=[pl.BlockSpec((tm, D), lambda i: (i, 0)),
                  pl.BlockSpec((1, D), lambda i: (0, 0)),
                  pl.BlockSpec((D, LANES), lambda i: (0, 0)),
                  pl.BlockSpec((1, LANES), lambda i: (0, 0))],
        out_specs=(pl.BlockSpec((tm * PACK_ROWS, LANES), lambda i: (i, 0)),
                   pl.BlockSpec((tm, LANES), lambda i: (i, 0)),
                   pl.BlockSpec((tm, LANES), lambda i: (i, 0)),
                   pl.BlockSpec((8, LANES), lambda i: (0, 0))),
        scratch_shapes=[pltpu.VMEM((1, LANES), F32)],
        compiler_params=_cparams(("arbitrary",)),
        name="ffn_norm_router",
    )(x1, gain, w_r, b_r)


DISPATCH_TB = 256
N_ZERO_FILLS = 2 * N_EXPERTS


def _dispatch_kernel(dest_ref, zs_ref, h_ref, xs_hbm, zbuf, zsem, sem):
    step = pl.program_id(0)

    @pl.when(step == 0)
    def _():
        zbuf[...] = jnp.zeros_like(zbuf)

        def zero_copy(e):
            start = pl.multiple_of(jnp.maximum(zs_ref[e], 0) * PACK_ROWS, PACK_ROWS)
            return pltpu.make_async_copy(zbuf, xs_hbm.at[pl.ds(start, MOE_BLOCK * PACK_ROWS)], zsem.at[0])

        def zstart(e, c):
            @pl.when(zs_ref[e] >= 0)
            def _():
                zero_copy(e).start()
            return c

        def zwait(e, c):
            @pl.when(zs_ref[e] >= 0)
            def _():
                zero_copy(e).wait()
            return c

        lax.fori_loop(0, N_ZERO_FILLS, zstart, 0)
        lax.fori_loop(0, N_ZERO_FILLS, zwait, 0)

    def row_copy(n, k):
        a = (step * DISPATCH_TB + n) * TOP_K + k
        src = h_ref.at[pl.ds(pl.multiple_of(n * PACK_ROWS, PACK_ROWS), PACK_ROWS)]
        dst = xs_hbm.at[pl.ds(pl.multiple_of(dest_ref[a] * PACK_ROWS, PACK_ROWS), PACK_ROWS)]
        return pltpu.make_async_copy(src, dst, sem.at[0])

    def issue(n, c):
        for k in range(TOP_K):
            row_copy(n, k).start(priority=k)
        return c

    def drain(n, c):
        for k in range(TOP_K):
            row_copy(n, k).wait()
        return c

    lax.fori_loop(0, DISPATCH_TB, issue, 0, unroll=8)
    lax.fori_loop(0, DISPATCH_TB, drain, 0, unroll=8)


def dispatch(dest, zero_start, hpk):
    grid_spec = pltpu.PrefetchScalarGridSpec(
        num_scalar_prefetch=2,
        grid=(T // DISPATCH_TB,),
        in_specs=[pl.BlockSpec((DISPATCH_TB * PACK_ROWS, LANES), lambda i, dst, zs: (i, 0))],
        out_specs=pl.BlockSpec(memory_space=pl.ANY),
        scratch_shapes=[pltpu.VMEM((MOE_BLOCK * PACK_ROWS, LANES), jnp.uint32),
                        pltpu.SemaphoreType.DMA((1,)),
                        pltpu.SemaphoreType.DMA((1,))],
    )
    return pl.pallas_call(
        _dispatch_kernel,
        out_shape=jax.ShapeDtypeStruct((MOE_ROWS * PACK_ROWS, LANES), jnp.uint32),
        grid_spec=grid_spec,
        compiler_params=_cparams(("arbitrary",)),
        name="moe_dispatch",
    )(dest, zero_start, hpk)


def _expert_kernel(be_ref, nu_ref, ord_ref, seq_ref, x_ref, wg_hbm, wu_hbm, wd_hbm, y_ref,
                   wg_st, wu_st, wd_st, wg_bf, wu_bf, wd_bf, sem):
    b = pl.program_id(0)

    def weight_copies(e, slot):
        return (pltpu.make_async_copy(wg_hbm.at[e], wg_st.at[slot], sem.at[slot, 0]),
                pltpu.make_async_copy(wu_hbm.at[e], wu_st.at[slot], sem.at[slot, 1]),
                pltpu.make_async_copy(wd_hbm.at[e], wd_st.at[slot], sem.at[slot, 2]))

    def start_fetch(n, slot):
        @pl.when(seq_ref[n] >= 0)
        def _():
            for cp in weight_copies(seq_ref[n], slot):
                cp.start()

    @pl.when(b < nu_ref[0])
    def _():
        n = ord_ref[b]
        slot = n % 2
        changed = jnp.logical_or(b == 0, be_ref[b] != be_ref[jnp.maximum(b - 1, 0)])

        @pl.when(b == 0)
        def _():
            start_fetch(0, 0)
            start_fetch(1, 1)

        @pl.when(changed)
        def _():
            cg, cu, cd = weight_copies(be_ref[b], slot)
            cg.wait()
            wg_bf[...] = wg_st[slot].astype(BF16)
            cu.wait()
            wu_bf[...] = wu_st[slot].astype(BF16)
            cd.wait()
            wd_bf[...] = wd_st[slot].astype(BF16)
            start_fetch(n + 2, slot)

        x = _unpack_bf16_pairs(x_ref, 0, MOE_BLOCK).astype(BF16)
        gate = jnp.dot(x, wg_bf[...], preferred_element_type=F32)
        up = jnp.dot(x, wu_bf[...], preferred_element_type=F32)
        hid = (jax.nn.silu(gate) * up).astype(BF16)
        y = jnp.dot(hid, wd_bf[...], preferred_element_type=F32)
        word = _pack_bf16_pairs(y.astype(BF16).astype(F32))
        for c in range(PACK_ROWS):
            y_ref[pl.ds(c, MOE_BLOCK, stride=PACK_ROWS), :] = word[:, c * LANES:(c + 1) * LANES]

    @pl.when(b >= nu_ref[0])
    def _():
        y_ref[...] = jnp.zeros_like(y_ref)


def experts(block_expert, n_used, block_ord, expert_seq, xs, w_gate, w_up, w_down):
    def blk(b, be, nu, od, sq):
        return jnp.minimum(b, nu[0] - 1)

    grid_spec = pltpu.PrefetchScalarGridSpec(
        num_scalar_prefetch=4,
        grid=(MOE_BLOCKS,),
        in_specs=[pl.BlockSpec((MOE_BLOCK * PACK_ROWS, LANES), lambda b, be, nu, od, sq: (blk(b, be, nu, od, sq), 0)),
                  pl.BlockSpec(memory_space=pl.ANY),
                  pl.BlockSpec(memory_space=pl.ANY),
                  pl.BlockSpec(memory_space=pl.ANY)],
        out_specs=pl.BlockSpec((MOE_BLOCK * PACK_ROWS, LANES), lambda b, be, nu, od, sq: (b, 0)),
        scratch_shapes=[pltpu.VMEM((2, D, EXPERT_FF), F32),
                        pltpu.VMEM((2, D, EXPERT_FF), F32),
                        pltpu.VMEM((2, EXPERT_FF, D), F32),
                        pltpu.VMEM((D, EXPERT_FF), BF16),
                        pltpu.VMEM((D, EXPERT_FF), BF16),
                        pltpu.VMEM((EXPERT_FF, D), BF16),
                        pltpu.SemaphoreType.DMA((2, 3))],
    )
    return pl.pallas_call(
        _expert_kernel,
        out_shape=jax.ShapeDtypeStruct((MOE_ROWS * PACK_ROWS, LANES), jnp.uint32),
        grid_spec=grid_spec,
        compiler_params=_cparams(("arbitrary",)),
        name="moe_experts",
    )(block_expert, n_used, block_ord, expert_seq, xs, w_gate, w_up, w_down)


COMBINE_ROWS = 256
COMBINE_SUB = 16


def _combine_kernel(dest_ref, ys_hbm, x_ref, wt_ref, g_ref, o_ref, ybuf, sem):
    s = pl.program_id(0)
    ns = pl.num_programs(0)
    slot = s % 2
    tb = COMBINE_SUB
    rows = COMBINE_ROWS
    nxt = jnp.minimum(s + 1, ns - 1)

    def row_copy(step, n, k, sl):
        tok = step * rows + n
        src = ys_hbm.at[pl.ds(pl.multiple_of(dest_ref[tok * TOP_K + k] * PACK_ROWS, PACK_ROWS), PACK_ROWS)]
        dst = ybuf.at[sl * TOP_K + k, pl.ds(pl.multiple_of(n * PACK_ROWS, PACK_ROWS), PACK_ROWS)]
        return pltpu.make_async_copy(src, dst, sem.at[sl])

    def wait_all(step, sl):
        def body(n, c):
            for k in range(TOP_K):
                row_copy(step, n, k, sl).wait()
            return c
        lax.fori_loop(0, rows, body, 0, unroll=8)

    @pl.when(s == 0)
    def _():
        def body(n, c):
            for k in range(TOP_K):
                row_copy(0, n, k, 0).start(priority=k)
            return c
        lax.fori_loop(0, rows, body, 0, unroll=8)

    wait_all(s, slot)

    g = g_ref[...]
    y0_ref = ybuf.at[slot * TOP_K]
    y1_ref = ybuf.at[slot * TOP_K + 1]
    for b in range(rows // tb):
        for n in range(b * tb, (b + 1) * tb):
            for k in range(TOP_K):
                row_copy(nxt, n, k, 1 - slot).start(priority=k)
        sub = slice(b * tb, (b + 1) * tb)
        w = wt_ref[sub, :]
        y0 = _unpack_bf16_pairs(y0_ref, b * tb, tb)
        y1 = _unpack_bf16_pairs(y1_ref, b * tb, tb)
        z = x_ref[sub, :] + (w[:, 0:1] * y0 + w[:, 1:2] * y1)
        ms = jnp.mean(z * z, axis=-1, keepdims=True)
        o_ref[sub, :] = z * lax.rsqrt(ms + NORM_EPS) * g

    @pl.when(s == ns - 1)
    def _():
        wait_all(nxt, 1 - slot)


def combine(dest, ys, x1, wts, gain):
    rows = COMBINE_ROWS
    grid_spec = pltpu.PrefetchScalarGridSpec(
        num_scalar_prefetch=1,
        grid=(T // rows,),
        in_specs=[pl.BlockSpec(memory_space=pl.ANY),
                  pl.BlockSpec((rows, D), lambda s, dst: (s, 0)),
                  pl.BlockSpec((rows, LANES), lambda s, dst: (s, 0)),
                  pl.BlockSpec((1, D), lambda s, dst: (0, 0))],
        out_specs=pl.BlockSpec((rows, D), lambda s, dst: (s, 0)),
        scratch_shapes=[pltpu.VMEM((2 * TOP_K, rows * PACK_ROWS, LANES), jnp.uint32),
                        pltpu.SemaphoreType.DMA((2,))],
    )
    return pl.pallas_call(
        _combine_kernel,
        out_shape=jax.ShapeDtypeStruct((T, D), F32),
        grid_spec=grid_spec,
        compiler_params=_cparams(("arbitrary",)),
        name="moe_combine_final_norm",
    )(dest, ys, x1, wts, gain)


def dispatch_plan(ids, counts):
    experts_ = jnp.arange(N_EXPERTS, dtype=jnp.int32)
    padded = (counts + MOE_BLOCK - 1) // MOE_BLOCK * MOE_BLOCK
    pad_end = jnp.cumsum(padded)
    pad_start = pad_end - padded
    e = ids[:, :TOP_K]
    start_of = jnp.sum(jnp.where(e[:, :, None] == experts_[None, None, :], pad_start[None, None, :], 0), axis=-1)
    dest = (start_of + ids[:, TOP_K:2 * TOP_K]).reshape(N_ASSIGN).astype(jnp.int32)
    n_used = pad_end[-1] // MOE_BLOCK
    block_start = jnp.minimum(jnp.arange(MOE_BLOCKS, dtype=jnp.int32), n_used - 1) * MOE_BLOCK
    block_expert = jnp.sum((block_start[:, None] >= pad_end[None, :]).astype(jnp.int32), axis=1)
    block_expert = jnp.minimum(block_expert, N_EXPERTS - 1).astype(jnp.int32)
    tail = n_used + experts_
    zero_start = jnp.concatenate([jnp.where(counts > 0, pad_end - MOE_BLOCK, -1),
                                  jnp.where(tail < MOE_BLOCKS, tail * MOE_BLOCK, -1)]).astype(jnp.int32)
    present = counts > 0
    expert_ord = jnp.cumsum(present.astype(jnp.int32)) - 1
    slots = jnp.arange(N_EXPERTS + 2, dtype=jnp.int32)
    hit = present[None, :] & (expert_ord[None, :] == slots[:, None])
    expert_seq = jnp.where(jnp.any(hit, axis=1), jnp.sum(jnp.where(hit, experts_[None, :], 0), axis=1), -1)
    block_ord = jnp.sum(jnp.where(block_expert[:, None] == experts_[None, :], expert_ord[None, :], 0), axis=1)
    return (block_expert, n_used.astype(jnp.int32).reshape(1), dest, zero_start,
            block_ord.astype(jnp.int32), expert_seq.astype(jnp.int32))


def kernel(x, norm_mix, w_in, b_gate, ssm_a_re, ssm_a_im, ssm_log_step, ssm_b_re, ssm_b_im, ssm_c_re, ssm_c_im, ssm_d, w_glu, b_glu, w_up_attn, w_up_ssm, w_out, norm_ffn, w_router_group, b_router_group, w_router_expert, b_router_expert, w_expert_gate, w_expert_up, w_expert_down, norm_final):
    x2 = x.reshape(T, D)
    h = norm_permute(x2, norm_mix.reshape(1, D)).reshape(T, D)
    w_in_l = w_in.reshape(D, IN_COLS)
    qkv = proj(h, w_in_l, 0, 3 * QKV_COLS, tn=QKV_COLS, name="proj_qkv")
    u = proj(h, w_in_l, 3 * QKV_COLS, SSM_W, tn=512, name="proj_ssm_in")

    attn = attention(qkv.reshape(R, NI, 3 * QKV_COLS)).reshape(T, ATTN_OUT)

    G = SSM_W // SSM_CH
    wb, wc, pows = ssm_params(
        ssm_a_re.reshape(G, SSM_STATE).astype(F32), ssm_a_im.reshape(G, SSM_STATE).astype(F32),
        ssm_log_step.reshape(G),
        ssm_b_re.reshape(G, SSM_STATE, SSM_CH).astype(F32), ssm_b_im.reshape(G, SSM_STATE, SSM_CH).astype(F32),
        ssm_c_re.reshape(G, SSM_CH, SSM_STATE), ssm_c_im.reshape(G, SSM_CH, SSM_STATE))
    yg = ssm_scan(u.reshape(R, NI, SSM_W), wb, wc, pows, ssm_d.reshape(1, SSM_W).astype(F32))
    ssm = glu(yg.reshape(T, SSM_W), w_glu.reshape(SSM_W, SSM_W), b_glu.reshape(1, SSM_W))

    merged = merge(h, attn, ssm, w_in_l, b_gate.reshape(1, 2 * D),
                   w_up_attn.reshape(ATTN_OUT, D), w_up_ssm.reshape(SSM_W, D))
    x1 = out_proj(merged.reshape(R, NI, D), w_out.reshape(D, D), x2)

    w_r = jnp.concatenate([w_router_group.reshape(D, N_EGROUPS), w_router_expert.reshape(D, N_EXPERTS),
                           jnp.zeros((D, LANES - N_EGROUPS - N_EXPERTS), F32)], axis=1)
    b_r = jnp.concatenate([b_router_group.reshape(1, N_EGROUPS), b_router_expert.reshape(1, N_EXPERTS),
                           jnp.zeros((1, LANES - N_EGROUPS - N_EXPERTS), F32)], axis=1)
    hpk, ids, wts, counts = router(x1, norm_ffn.reshape(1, D), w_r, b_r)

    block_expert, n_used, dest, zero_start, block_ord, expert_seq = dispatch_plan(ids[:, :2 * TOP_K], counts[0, :N_EXPERTS])
    xs = dispatch(dest, zero_start, hpk)
    ys = experts(block_expert, n_used, block_ord, expert_seq, xs,
                 w_expert_gate.reshape(N_EXPERTS, D, EXPERT_FF), w_expert_up.reshape(N_EXPERTS, D, EXPERT_FF),
                 w_expert_down.reshape(N_EXPERTS, EXPERT_FF, D))
    out = combine(dest, ys, x1, wts, norm_final.reshape(1, D))
    return out.reshape(1, T, D)
```

```python
import functools
import math

import jax
import jax.numpy as jnp
from jax import lax
from jax.experimental import pallas as pl
from jax.experimental.pallas import tpu as pltpu

F32 = jnp.float32
BF16 = jnp.bfloat16

T = 8192
D = 2048
R = 16
NI = T // R
HEAD_DIM = 64
N_HEAD_SLOTS = 8
DILATIONS = (1, 4, 16)
ATTN_BLOCK = 128
QKV_COLS = 1536
ATTN_OUT = 512
SSM_W = 1024
SSM_STATE = 64
SSM_CH = 16
IN_COLS = 3 * QKV_COLS + SSM_W + 2 * D
N_EXPERTS = 32
N_EGROUPS = 4
EXPERTS_PER_GROUP = 8
TOP_K = 2
EXPERT_FF = 512
NORM_EPS = 1e-6
LANES = 128
VMEM_LIMIT = 48 * 1024 * 1024

MOE_BLOCK = 256
N_ASSIGN = T * TOP_K
MOE_BLOCKS = N_ASSIGN // MOE_BLOCK + N_EXPERTS
MOE_ROWS = MOE_BLOCKS * MOE_BLOCK


def _cparams(sem):
    return pltpu.CompilerParams(dimension_semantics=sem, vmem_limit_bytes=VMEM_LIMIT)


N_SLABS = D // LANES
NORM_TB = 32
NORM_CHUNK = 64
NORM_PITCH = R + 8


def _norm_permute_kernel(x_ref, g_ref, h_ref, slab):
    g = g_ref[...]

    def chunk(t, c):
        rows = pl.ds(pl.multiple_of(t * NORM_CHUNK, NORM_CHUNK), NORM_CHUNK)
        x = x_ref[rows, :]
        ms = jnp.mean(x * x, axis=-1, keepdims=True)
        hn = x * lax.rsqrt(ms + NORM_EPS) * g
        for k in range(NORM_CHUNK // R):
            dst = pl.ds(pl.multiple_of((t * (NORM_CHUNK // R) + k) * NORM_PITCH, 8), R)
            for s_ in range(N_SLABS):
                slab[s_, dst, :] = hn[k * R:(k + 1) * R, s_ * LANES:(s_ + 1) * LANES]
        return c

    lax.fori_loop(0, NORM_TB * R // NORM_CHUNK, chunk, 0)
    for r in range(R):
        pieces = [slab[s_, pl.ds(r, NORM_TB, stride=NORM_PITCH), :] for s_ in range(N_SLABS)]
        h_ref[r] = jnp.concatenate(pieces, axis=1).astype(BF16)


def norm_permute(x2, gain):
    return pl.pallas_call(
        _norm_permute_kernel,
        out_shape=jax.ShapeDtypeStruct((R, NI, D), BF16),
        grid=(NI // NORM_TB,),
        in_specs=[pl.BlockSpec((NORM_TB * R, D), lambda i: (i, 0)),
                  pl.BlockSpec((1, D), lambda i: (0, 0))],
        out_specs=pl.BlockSpec((R, NORM_TB, D), lambda i: (0, i, 0)),
        scratch_shapes=[pltpu.VMEM((N_SLABS, NORM_TB * NORM_PITCH, LANES), F32)],
        compiler_params=_cparams(("arbitrary",)),
        name="norm_permute",
    )(x2, gain)


MXU_N = 256


def _cast_weight_once(w_ref, wbf_ref):
    @pl.when(pl.program_id(1) == 0)
    def _():
        wbf_ref[...] = w_ref[...].astype(BF16)


def _proj_kernel(a_ref, *refs):
    w_refs, o_ref, wbf_ref = refs[:-2], refs[-2], refs[-1]
    wt = w_refs[0].shape[1]

    @pl.when(pl.program_id(1) == 0)
    def _():
        for k, w_ref in enumerate(w_refs):
            wbf_ref[:, k * wt:(k + 1) * wt] = w_ref[...].astype(BF16)

    a = a_ref[...]
    for c in range(o_ref.shape[1] // MXU_N):
        cols = slice(c * MXU_N, (c + 1) * MXU_N)
        o_ref[:, cols] = jnp.dot(a, wbf_ref[:, cols], preferred_element_type=F32)


def proj(h, w_in, col_off, n_cols, tn, wt, name="proj"):
    tm = 1024
    pieces = tn // wt
    off = col_off // wt

    def piece(k):
        return pl.BlockSpec((D, wt), lambda j, i: (0, j * pieces + off + k), pipeline_mode=pl.Buffered(1))

    return pl.pallas_call(
        _proj_kernel,
        out_shape=jax.ShapeDtypeStruct((T, n_cols), F32),
        grid=(n_cols // tn, T // tm),
        in_specs=[pl.BlockSpec((tm, D), lambda j, i: (i, 0))] + [piece(k) for k in range(pieces)],
        out_specs=pl.BlockSpec((tm, tn), lambda j, i: (i, j)),
        scratch_shapes=[pltpu.VMEM((D, tn), BF16)],
        compiler_params=_cparams(("arbitrary", "arbitrary")),
        name=name,
    )(h, *([w_in] * pieces))


def _seq_index_maps(d):
    nseg = R // d
    qlen = ATTN_BLOCK // nseg
    return nseg, qlen


def _bias_matrices(d, hp):
    nseg, qlen = _seq_index_maps(d)
    klen = 2 * qlen
    row = lax.broadcasted_iota(jnp.int32, (2 * ATTN_BLOCK, 2 * ATTN_BLOCK), 0)
    col = lax.broadcasted_iota(jnp.int32, (2 * ATTN_BLOCK, 2 * ATTN_BLOCK), 1)
    rho = row % ATTN_BLOCK
    jq = (rho % qlen) * nseg + rho // qlen
    jk = ((col % klen) - qlen) * nseg + col // klen
    steps = jq - jk
    valid = (steps >= 0) & (steps <= ATTN_BLOCK)
    head = 2 * hp + row // ATTN_BLOCK
    slope = lax.bitcast_convert_type((127 - (head + 1)) << 23, F32)
    bias = -slope * (d * steps).astype(F32)
    neg = jnp.float32(-jnp.inf)
    return jnp.where(valid, bias, neg), jnp.where(valid & (jk >= 0), bias, neg)


def _attend_pair(q, k, v, bias):
    lane = lax.broadcasted_iota(jnp.int32, (ATTN_BLOCK, LANES), 1)
    first = lane < HEAD_DIM
    zero = jnp.zeros_like(q)
    q2 = jnp.concatenate([jnp.where(first, q, zero), jnp.where(first, zero, q)], axis=0).astype(BF16)
    s = lax.dot_general(q2, k.astype(BF16), (((1,), (1,)), ((), ())), preferred_element_type=F32)
    s = s + bias
    m = jnp.max(s, axis=-1, keepdims=True)
    p = jnp.exp(s - m)
    l = jnp.sum(p, axis=-1, keepdims=True)
    o2 = jnp.dot(p.astype(BF16), v.astype(BF16), preferred_element_type=F32)
    o = jnp.where(first, o2[:ATTN_BLOCK], o2[ATTN_BLOCK:])
    m_b = jnp.where(first, m[:ATTN_BLOCK], m[ATTN_BLOCK:])
    l_b = jnp.where(first, l[:ATTN_BLOCK], l[ATTN_BLOCK:])
    return o, m_b, l_b


def _attn_kernel(q_ref, kp_ref, kc_ref, vp_ref, vc_ref, o_ref, kbuf, vbuf, obuf, mbuf, lbuf, bias_ref):
    hp = pl.program_id(0)
    it = pl.program_id(1)
    g = pl.program_id(2)
    scale = HEAD_DIM ** -0.5

    kbuf[:, :ATTN_BLOCK, :] = kp_ref[...]
    kbuf[:, ATTN_BLOCK:, :] = kc_ref[...]
    vbuf[:, :ATTN_BLOCK, :] = vp_ref[...]
    vbuf[:, ATTN_BLOCK:, :] = vc_ref[...]

    for gi, d in enumerate(DILATIONS):
        nseg, qlen = _seq_index_maps(d)
        klen = 2 * qlen
        nblk = ATTN_BLOCK // qlen

        @pl.when(g == gi)
        def _(gi=gi, d=d, nseg=nseg, qlen=qlen, klen=klen, nblk=nblk):
            @pl.when(it == 0)
            def _():
                b_reg, b_first = _bias_matrices(d, hp)
                bias_ref[gi, 0] = b_reg
                bias_ref[gi, 1] = b_first

            def block(idx, carry):
                rd = idx // nblk
                bb = idx % nblk
                q0 = pl.multiple_of(bb * qlen, qlen)
                k0 = pl.multiple_of(ATTN_BLOCK + bb * qlen - qlen, qlen)
                qs, ks, vs = [], [], []
                for m_ in range(nseg):
                    rr = rd + d * m_
                    qs.append(q_ref[rr, pl.ds(q0, qlen), :])
                    ks.append(kbuf[rr, pl.ds(k0, klen), :])
                    vs.append(vbuf[rr, pl.ds(k0, klen), :])
                q = jnp.concatenate(qs, axis=0) * scale
                k = jnp.concatenate(ks, axis=0)
                v = jnp.concatenate(vs, axis=0)
                is_first = jnp.logical_and(it == 0, bb == 0)
                bias = bias_ref[gi, jnp.where(is_first, 1, 0)]
                o, mx, den = _attend_pair(q, k, v, bias)
                for m_ in range(nseg):
                    rr = rd + d * m_
                    seg = slice(m_ * qlen, (m_ + 1) * qlen)
                    obuf[gi, rr, pl.ds(q0, qlen), :] = o[seg]
                    mbuf[gi, rr, pl.ds(q0, qlen), :] = mx[seg]
                    lbuf[gi, rr, pl.ds(q0, qlen), :] = den[seg]
                return carry

            lax.fori_loop(0, d * nblk, block, 0, unroll=True)

    @pl.when(g == len(DILATIONS) - 1)
    def _():
        for r in range(R):
            m0, m1, m2 = mbuf[0, r], mbuf[1, r], mbuf[2, r]
            mx = jnp.maximum(jnp.maximum(m0, m1), m2)
            e0, e1, e2 = jnp.exp(m0 - mx), jnp.exp(m1 - mx), jnp.exp(m2 - mx)
            den = e0 * lbuf[0, r] + e1 * lbuf[1, r] + e2 * lbuf[2, r]
            num = e0 * obuf[0, r] + e1 * obuf[1, r] + e2 * obuf[2, r]
            o_ref[r] = (num / den).astype(BF16)


def attention(qkv3):
    n_hp = N_HEAD_SLOTS // 2
    n_it = NI // ATTN_BLOCK
    ng = len(DILATIONS)
    cb = QKV_COLS // LANES

    def cur(base):
        return pl.BlockSpec((R, ATTN_BLOCK, LANES), lambda hp, it, g: (0, it, base + g * n_hp + hp))

    def prev(base):
        return pl.BlockSpec((R, ATTN_BLOCK, LANES),
                            lambda hp, it, g: (0, jnp.maximum(it - 1, 0), base + g * n_hp + hp))

    return pl.pallas_call(
        _attn_kernel,
        out_shape=jax.ShapeDtypeStruct((R, NI, ATTN_OUT), BF16),
        grid=(n_hp, n_it, ng),
        in_specs=[cur(0), prev(cb), cur(cb), prev(2 * cb), cur(2 * cb)],
        out_specs=pl.BlockSpec((R, ATTN_BLOCK, LANES), lambda hp, it, g: (0, it, hp)),
        scratch_shapes=[pltpu.VMEM((R, 2 * ATTN_BLOCK, LANES), F32),
                        pltpu.VMEM((R, 2 * ATTN_BLOCK, LANES), F32),
                        pltpu.VMEM((ng, R, ATTN_BLOCK, LANES), F32),
                        pltpu.VMEM((ng, R, ATTN_BLOCK, LANES), F32),
                        pltpu.VMEM((ng, R, ATTN_BLOCK, LANES), F32),
                        pltpu.VMEM((ng, 2, 2 * ATTN_BLOCK, 2 * ATTN_BLOCK), F32)],
        compiler_params=_cparams(("arbitrary", "arbitrary", "arbitrary")),
        name="dilated_attention",
    )(qkv3, qkv3, qkv3, qkv3, qkv3)


SSM_SLAB = 256
SSM_SLABS = SSM_W // SSM_SLAB
SLAB_STATES = SSM_SLAB // SSM_CH * SSM_STATE
SSM_TI = 128
SSM_MM_CHUNK = 4


def _ssm_kernel(u_ref, wb_ref, wc_ref, pw_ref, dsk_ref, o_ref, s_ref, zs_ref, zc_ref, tr_ref, zt_ref):
    ic = pl.program_id(1)
    ns = SLAB_STATES

    @pl.when(ic == 0)
    def _():
        zc_ref[...] = jnp.zeros_like(zc_ref)

    n_chunks = R // SSM_MM_CHUNK
    n_tiles = SSM_TI // 8
    wb = wb_ref[0]
    wc = wc_ref[0]
    dsk = dsk_ref[...]
    arb = jnp.broadcast_to(pw_ref[0, 0:1, :ns], (8, ns))
    aib = jnp.broadcast_to(pw_ref[0, 0:1, ns:], (8, ns))


    def bu_chunk(c):
        lo = c * SSM_MM_CHUNK
        uc = u_ref[lo:lo + SSM_MM_CHUNK].reshape(SSM_MM_CHUNK * SSM_TI, SSM_SLAB)
        bu = jnp.dot(uc.astype(BF16), wb, preferred_element_type=F32)
        s_ref[lo:lo + SSM_MM_CHUNK] = bu.reshape(SSM_MM_CHUNK, SSM_TI, 2 * ns)

    def local_chunk(c):
        lo = c * SSM_MM_CHUNK
        first = max(lo, 1)
        for t in range(n_tiles):
            rows = slice(t * 8, (t + 1) * 8)
            pr = s_ref[first - 1, rows, :ns]
            pi = s_ref[first - 1, rows, ns:]
            for r in range(first, lo + SSM_MM_CHUNK):
                nr = s_ref[r, rows, :ns] + (arb * pr - aib * pi)
                ni = s_ref[r, rows, ns:] + (arb * pi + aib * pr)
                s_ref[r, rows, :ns] = nr
                s_ref[r, rows, ns:] = ni
                pr, pi = nr, ni

    bu_chunk(0)
    for c in range(n_chunks):
        if c + 1 < n_chunks:
            bu_chunk(c + 1)
        local_chunk(c)

    n_ch = ns // LANES
    for c in range(n_ch):
        tr_ref[0, c * 8:c * 8 + 1, :] = pw_ref[0, R - 1:R, c * LANES:(c + 1) * LANES]
        tr_ref[1, c * 8:c * 8 + 1, :] = pw_ref[0, R - 1:R, ns + c * LANES:ns + (c + 1) * LANES]
    a16r = tr_ref[0, pl.ds(0, n_ch, stride=8), :]
    a16i = tr_ref[1, pl.ds(0, n_ch, stride=8), :]

    def ztile(t, carry):
        zr, zi = carry
        rows = pl.ds(pl.multiple_of(t * 8, 8), 8)
        for c in range(n_ch):
            tr_ref[0, c * 8:(c + 1) * 8, :] = s_ref[R - 1, rows, c * LANES:(c + 1) * LANES]
            tr_ref[1, c * 8:(c + 1) * 8, :] = s_ref[R - 1, rows, ns + c * LANES:ns + (c + 1) * LANES]
        for i in range(8):
            zt_ref[0, pl.ds(i, n_ch, stride=8), :] = zr
            zt_ref[1, pl.ds(i, n_ch, stride=8), :] = zi
            er = tr_ref[0, pl.ds(i, n_ch, stride=8), :]
            ei = tr_ref[1, pl.ds(i, n_ch, stride=8), :]
            zr, zi = a16r * zr - a16i * zi + er, a16r * zi + a16i * zr + ei
        for c in range(n_ch):
            zs_ref[rows, c * LANES:(c + 1) * LANES] = zt_ref[0, c * 8:(c + 1) * 8, :]
            zs_ref[rows, ns + c * LANES:ns + (c + 1) * LANES] = zt_ref[1, c * 8:(c + 1) * 8, :]
        return zr, zi

    zr_end, zi_end = lax.fori_loop(0, SSM_TI // 8, ztile, (zc_ref[0], zc_ref[1]))
    zc_ref[0] = zr_end
    zc_ref[1] = zi_end

    def fix_chunk(c):
        lo = c * SSM_MM_CHUNK
        for r in range(lo, lo + SSM_MM_CHUNK):
            prb = jnp.broadcast_to(pw_ref[0, r:r + 1, :ns], (8, ns))
            pib = jnp.broadcast_to(pw_ref[0, r:r + 1, ns:], (8, ns))
            for t in range(n_tiles):
                rows = slice(t * 8, (t + 1) * 8)
                zr = zs_ref[rows, :ns]
                zi = zs_ref[rows, ns:]
                s_ref[r, rows, :ns] = s_ref[r, rows, :ns] + (prb * zr - pib * zi)
                s_ref[r, rows, ns:] = s_ref[r, rows, ns:] + (prb * zi + pib * zr)

    def out_chunk(c):
        lo = c * SSM_MM_CHUNK
        xs = s_ref[lo:lo + SSM_MM_CHUNK].reshape(SSM_MM_CHUNK * SSM_TI, 2 * ns)
        y = jnp.dot(xs.astype(BF16), wc, preferred_element_type=F32)
        y = y.reshape(SSM_MM_CHUNK, SSM_TI, SSM_SLAB) + dsk * u_ref[lo:lo + SSM_MM_CHUNK]
        o_ref[lo:lo + SSM_MM_CHUNK] = jax.nn.gelu(y)

    fix_chunk(0)
    for c in range(n_chunks):
        if c + 1 < n_chunks:
            fix_chunk(c + 1)
        out_chunk(c)


def ssm_scan(u3, wb, wc, pows, dskip):
    ns2 = 2 * SLAB_STATES
    return pl.pallas_call(
        _ssm_kernel,
        out_shape=jax.ShapeDtypeStruct((R, NI, SSM_W), F32),
        grid=(SSM_SLABS, NI // SSM_TI),
        in_specs=[pl.BlockSpec((R, SSM_TI, SSM_SLAB), lambda kb, ic: (0, ic, kb)),
                  pl.BlockSpec((1, SSM_SLAB, ns2), lambda kb, ic: (kb, 0, 0)),
                  pl.BlockSpec((1, ns2, SSM_SLAB), lambda kb, ic: (kb, 0, 0)),
                  pl.BlockSpec((1, R, ns2), lambda kb, ic: (kb, 0, 0)),
                  pl.BlockSpec((1, SSM_SLAB), lambda kb, ic: (0, kb))],
        out_specs=pl.BlockSpec((R, SSM_TI, SSM_SLAB), lambda kb, ic: (0, ic, kb)),
        scratch_shapes=[pltpu.VMEM((R, SSM_TI, ns2), F32),
                        pltpu.VMEM((SSM_TI, ns2), F32),
                        pltpu.VMEM((2, SLAB_STATES // LANES, LANES), F32),
                        pltpu.VMEM((2, 8 * SLAB_STATES // LANES, LANES), F32),
                        pltpu.VMEM((2, 8 * SLAB_STATES // LANES, LANES), F32)],
        compiler_params=_cparams(("arbitrary", "arbitrary")),
        name="s5_ssm",
    )(u3, wb, wc, pows, dskip)


def ssm_params(a_re, a_im, log_step, b_re, b_im, c_re, c_im):
    G, P, H = SSM_W // SSM_CH, SSM_STATE, SSM_CH
    gs = SSM_SLAB // SSM_CH
    step = jnp.exp(log_step.astype(F32))[:, None]
    kk = jnp.arange(1, R + 1, dtype=F32)[:, None, None]
    mag = jnp.exp(kk * (a_re * step))
    ang = kk * (a_im * step)
    pw_re, pw_im = mag * jnp.cos(ang), mag * jnp.sin(ang)
    abar_re, abar_im = pw_re[0], pw_im[0]
    nr, ni = abar_re - 1.0, abar_im
    den = a_re * a_re + a_im * a_im
    f_re = (nr * a_re + ni * a_im) / den
    f_im = (ni * a_re - nr * a_im) / den
    bbar_re = f_re[..., None] * b_re - f_im[..., None] * b_im
    bbar_im = f_re[..., None] * b_im + f_im[..., None] * b_re
    rows_g = jnp.arange(gs * H, dtype=jnp.int32) // H
    cols_g = jnp.arange(gs * P, dtype=jnp.int32) // P
    b_t = jnp.concatenate([jnp.swapaxes(bbar_re, 1, 2).reshape(SSM_SLABS, gs * H, P),
                           jnp.swapaxes(bbar_im, 1, 2).reshape(SSM_SLABS, gs * H, P)], axis=-1)
    tile_p = (jnp.arange(2 * P, dtype=jnp.int32)[:, None] ==
              (jnp.arange(2 * gs * P, dtype=jnp.int32) // (gs * P) * P + jnp.arange(2 * gs * P, dtype=jnp.int32) % P)[None, :])
    mask_b = rows_g[:, None] == jnp.tile(cols_g, 2)[None, :]
    wb = jnp.where(mask_b[None], jnp.einsum('kap,pc->kac', b_t.astype(BF16), tile_p.astype(BF16),
                                            preferred_element_type=F32), 0.0).astype(BF16)
    c_t = jnp.concatenate([jnp.swapaxes(c_re.astype(F32), 1, 2).reshape(SSM_SLABS, gs * P, H),
                           -jnp.swapaxes(c_im.astype(F32), 1, 2).reshape(SSM_SLABS, gs * P, H)], axis=1)
    tile_h = jnp.arange(H, dtype=jnp.int32)[:, None] == (jnp.arange(gs * H, dtype=jnp.int32) % H)[None, :]
    mask_c = jnp.tile(cols_g, 2)[:, None] == rows_g[None, :]
    wc = jnp.where(mask_c[None], jnp.einsum('kah,hc->kac', c_t.astype(BF16), tile_h.astype(BF16),
                                            preferred_element_type=F32), 0.0).astype(BF16)
    pows = jnp.concatenate([pw_re.reshape(R, SSM_SLABS, gs * P), pw_im.reshape(R, SSM_SLABS, gs * P)], axis=-1)
    return wb, wc, jnp.transpose(pows, (1, 0, 2))


def _glu_kernel(a_ref, w_ref, b_ref, o_ref, wbf_ref):
    @pl.when(pl.program_id(0) == 0)
    def _():
        wbf_ref[...] = w_ref[...].astype(BF16)

    a = a_ref[...].astype(BF16)
    for c in range(SSM_W // MXU_N):
        cols = slice(c * MXU_N, (c + 1) * MXU_N)
        acc = jnp.dot(a, wbf_ref[:, cols], preferred_element_type=F32)
        o_ref[:, cols] = (a_ref[:, cols] * jax.nn.sigmoid(acc + b_ref[:, cols])).astype(BF16)


def glu(yg, w_glu, b_glu):
    tm = 512
    return pl.pallas_call(
        _glu_kernel,
        out_shape=jax.ShapeDtypeStruct((T, SSM_W), BF16),
        grid=(T // tm,),
        in_specs=[pl.BlockSpec((tm, SSM_W), lambda i: (i, 0)),
                  pl.BlockSpec((SSM_W, SSM_W), lambda i: (0, 0)),
                  pl.BlockSpec((1, SSM_W), lambda i: (0, 0))],
        out_specs=pl.BlockSpec((tm, SSM_W), lambda i: (i, 0)),
        scratch_shapes=[pltpu.VMEM((SSM_W, SSM_W), BF16)],
        compiler_params=_cparams(("arbitrary",)),
        name="ssm_glu",
    )(yg, w_glu, b_glu)


def _merge_kernel(h_ref, at_ref, ss_ref, wga_ref, wgs_ref, ba_ref, bs_ref, wa_ref, ws_ref, o_ref,
                  wga_bf, wgs_bf, wa_bf, ws_bf):
    _cast_weight_once(wga_ref, wga_bf)
    _cast_weight_once(wgs_ref, wgs_bf)
    _cast_weight_once(wa_ref, wa_bf)
    _cast_weight_once(ws_ref, ws_bf)
    h = h_ref[...]
    at = at_ref[...]
    ss = ss_ref[...]
    for c in range(wa_ref.shape[1] // MXU_N):
        cols = slice(c * MXU_N, (c + 1) * MXU_N)
        ga = jax.nn.sigmoid(jnp.dot(h, wga_bf[:, cols], preferred_element_type=F32) + ba_ref[:, cols])
        a = jnp.dot(at, wa_bf[:, cols], preferred_element_type=F32)
        gs = jax.nn.sigmoid(jnp.dot(h, wgs_bf[:, cols], preferred_element_type=F32) + bs_ref[:, cols])
        s = jnp.dot(ss, ws_bf[:, cols], preferred_element_type=F32)
        o_ref[:, cols] = (ga * a + gs * s).astype(BF16)


def merge(h, attn, ssm, w_in, b_gate, w_up_attn, w_up_ssm):
    tm, tn = 1024, 512
    nj = D // tn
    off_a = (3 * QKV_COLS + SSM_W) // tn
    once = pl.Buffered(1)
    return pl.pallas_call(
        _merge_kernel,
        out_shape=jax.ShapeDtypeStruct((T, D), BF16),
        grid=(nj, T // tm),
        in_specs=[pl.BlockSpec((tm, D), lambda j, i: (i, 0)),
                  pl.BlockSpec((tm, ATTN_OUT), lambda j, i: (i, 0)),
                  pl.BlockSpec((tm, SSM_W), lambda j, i: (i, 0)),
                  pl.BlockSpec((D, tn), lambda j, i: (0, j + off_a), pipeline_mode=once),
                  pl.BlockSpec((D, tn), lambda j, i: (0, j + off_a + nj), pipeline_mode=once),
                  pl.BlockSpec((1, tn), lambda j, i: (0, j)),
                  pl.BlockSpec((1, tn), lambda j, i: (0, j + nj)),
                  pl.BlockSpec((ATTN_OUT, tn), lambda j, i: (0, j), pipeline_mode=once),
                  pl.BlockSpec((SSM_W, tn), lambda j, i: (0, j), pipeline_mode=once)],
        out_specs=pl.BlockSpec((tm, tn), lambda j, i: (i, j)),
        scratch_shapes=[pltpu.VMEM((D, tn), BF16), pltpu.VMEM((D, tn), BF16),
                        pltpu.VMEM((ATTN_OUT, tn), BF16), pltpu.VMEM((SSM_W, tn), BF16)],
        compiler_params=_cparams(("arbitrary", "arbitrary")),
        name="gates_branch_merge",
    )(h, attn, ssm, w_in, w_in, b_gate, b_gate, w_up_attn, w_up_ssm)


OUTPROJ_TB = 64
OUTPROJ_PITCH = OUTPROJ_TB + 8


def _outproj_kernel(m_ref, w_ref, x_ref, o_ref, slab, wbf_ref):
    _cast_weight_once(w_ref, wbf_ref)
    tb = OUTPROJ_TB
    a = m_ref[...].reshape(R * tb, D)
    per = MXU_N // LANES
    for c in range(w_ref.shape[1] // MXU_N):
        acc = jnp.dot(a, wbf_ref[:, c * MXU_N:(c + 1) * MXU_N], preferred_element_type=F32)
        for s_ in range(per):
            lanes = slice(s_ * LANES, (s_ + 1) * LANES)
            for r in range(R):
                slab[c * per + s_, r * OUTPROJ_PITCH:r * OUTPROJ_PITCH + tb, :] = acc[r * tb:(r + 1) * tb, lanes]
        for s_ in range(per):
            lanes = slice((c * per + s_) * LANES, (c * per + s_ + 1) * LANES)
            for i in range(tb):
                rows = slice(i * R, (i + 1) * R)
                o_ref[rows, lanes] = slab[c * per + s_, pl.ds(i, R, stride=OUTPROJ_PITCH), :] + x_ref[rows, lanes]


def out_proj(merged3, w_out, x2):
    tb, tn = OUTPROJ_TB, 1024
    return pl.pallas_call(
        _outproj_kernel,
        out_shape=jax.ShapeDtypeStruct((T, D), F32),
        grid=(D // tn, NI // tb),
        in_specs=[pl.BlockSpec((R, tb, D), lambda j, i: (0, i, 0)),
                  pl.BlockSpec((D, tn), lambda j, i: (0, j), pipeline_mode=pl.Buffered(1)),
                  pl.BlockSpec((tb * R, tn), lambda j, i: (i, j))],
        out_specs=pl.BlockSpec((tb * R, tn), lambda j, i: (i, j)),
        scratch_shapes=[pltpu.VMEM((tn // LANES, OUTPROJ_PITCH * R, LANES), F32),
                        pltpu.VMEM((D, tn), BF16)],
        compiler_params=_cparams(("arbitrary", "arbitrary")),
        name="out_proj_residual",
    )(merged3, w_out, x2)


PACK_ROWS = 8
HALF_D = D // 2


def _pack_bf16_pairs(zf):
    top = lax.bitcast_convert_type(zf, jnp.uint32)
    return top[:, HALF_D:] | (top[:, :HALF_D] >> 16)


def _unpack_bf16_pairs(x_ref, first, n):
    lo, hi = [], []
    for c in range(PACK_ROWS):
        w = x_ref[pl.ds(first * PACK_ROWS + c, n, stride=PACK_ROWS), :]
        lo.append(lax.bitcast_convert_type(w << 16, F32))
        hi.append(lax.bitcast_convert_type(w & jnp.uint32(0xFFFF0000), F32))
    return jnp.concatenate(lo + hi, axis=1)


def _router_kernel(x_ref, g_ref, w_ref, b_ref, h_ref, id_ref, wt_ref, cnt_ref, carry_ref):
    step = pl.program_id(0)

    @pl.when(step == 0)
    def _():
        carry_ref[...] = jnp.zeros_like(carry_ref)

    x = x_ref[...]
    ms = jnp.mean(x * x, axis=-1, keepdims=True)
    z = x * lax.rsqrt(ms + NORM_EPS) * g_ref[...]
    zh = z.astype(BF16)
    zf = zh.astype(F32)
    word = _pack_bf16_pairs(zf)
    for c in range(PACK_ROWS):
        h_ref[pl.ds(c, x.shape[0], stride=PACK_ROWS), :] = word[:, c * LANES:(c + 1) * LANES]
    zl = (z - zf).astype(BF16)
    w = w_ref[...]
    wh = w.astype(BF16)
    wl = (w - wh.astype(F32)).astype(BF16)
    logits = (jnp.dot(zh, wh, preferred_element_type=F32) + jnp.dot(zl, wh, preferred_element_type=F32)
              + jnp.dot(zh, wl, preferred_element_type=F32)) + b_ref[...]
    lane = lax.broadcasted_iota(jnp.int32, logits.shape, 1)
    lanef = lane.astype(F32)
    neg = jnp.float32(-jnp.inf)
    big = jnp.float32(1e9)
    gl = jnp.where(lane < N_EGROUPS, logits, neg)
    gmax = jnp.max(gl, axis=-1, keepdims=True)
    gidx = jnp.min(jnp.where(gl == gmax, lanef, big), axis=-1, keepdims=True)
    pg = 1.0 / jnp.sum(jnp.exp(gl - gmax), axis=-1, keepdims=True)
    lo = N_EGROUPS + EXPERTS_PER_GROUP * gidx
    el = jnp.where((lanef >= lo) & (lanef < lo + EXPERTS_PER_GROUP), logits, neg)
    t1 = jnp.max(el, axis=-1, keepdims=True)
    j1 = jnp.min(jnp.where(el == t1, lanef, big), axis=-1, keepdims=True)
    el2 = jnp.where(lanef == j1, neg, el)
    t2 = jnp.max(el2, axis=-1, keepdims=True)
    j2 = jnp.min(jnp.where(el2 == t2, lanef, big), axis=-1, keepdims=True)
    e21 = jnp.exp(t2 - t1)
    w1 = pg / (1.0 + e21)
    w2 = pg * e21 / (1.0 + e21)
    e1f = j1 - N_EGROUPS
    e2f = j2 - N_EGROUPS

    tm = x.shape[0]
    oh1 = (lanef == e1f).astype(F32)
    oh2 = (lanef == e2f).astype(F32)
    ri = lax.broadcasted_iota(jnp.int32, (tm, tm), 0)
    ci = lax.broadcasted_iota(jnp.int32, (tm, tm), 1)
    before = (ci < ri).astype(BF16)
    p1 = jnp.dot(before, oh1.astype(BF16), preferred_element_type=F32)
    p2 = jnp.dot(before, oh2.astype(BF16), preferred_element_type=F32)
    carry = carry_ref[...]
    c1 = jnp.sum(oh1, axis=0, keepdims=True)
    c2 = jnp.sum(oh2, axis=0, keepdims=True)
    rank1 = jnp.sum(oh1 * (carry + p1), axis=-1, keepdims=True)
    rank2 = jnp.sum(oh2 * (carry + c1 + p2), axis=-1, keepdims=True)
    carry = carry + c1 + c2
    carry_ref[...] = carry
    cnt_ref[...] = jnp.broadcast_to(carry, cnt_ref.shape).astype(jnp.int32)

    ids = jnp.where(lane == 0, e1f, jnp.where(lane == 1, e2f, jnp.where(lane == 2, rank1, jnp.where(lane == 3, rank2, 0.0))))
    id_ref[...] = ids.astype(jnp.int32)
    wt_ref[...] = jnp.where(lane == 0, w1, jnp.where(lane == 1, w2, 0.0))


def router(x1, gain, w_r, b_r):
    tm = 256
    return pl.pallas_call(
        _router_kernel,
        out_shape=(jax.ShapeDtypeStruct((T * PACK_ROWS, LANES), jnp.uint32),
                   jax.ShapeDtypeStruct((T, LANES), jnp.int32),
                   jax.ShapeDtypeStruct((T, LANES), F32),
                   jax.ShapeDtypeStruct((8, LANES), jnp.int32)),
        grid=(T // tm,),
        in_specs=[pl.BlockSpec((tm, D), lambda i: (i, 0)),
                  pl.BlockSpec((1, D), lambda i: (0, 0)),
                  pl.BlockSpec((D, LANES), lambda i: (0, 0)),
                  pl.BlockSpec((1, LANES), lambda i: (0, 0))],
        out_specs=(pl.BlockSpec((tm * PACK_ROWS, LANES), lambda i: (i, 0)),
                   pl.BlockSpec((tm, LANES), lambda i: (i, 0)),
                   pl.BlockSpec((tm, LANES), lambda i: (i, 0)),
                   pl.BlockSpec((8, LANES), lambda i: (0, 0))),
        scratch_shapes=[pltpu.VMEM((1, LANES), F32)],
        compiler_params=_cparams(("arbitrary",)),
        name="ffn_norm_router",
    )(x1, gain, w_r, b_r)


DISPATCH_TB = 256
N_ZERO_FILLS = 2 * N_EXPERTS


def _dispatch_kernel(dest_ref, zs_ref, h_ref, xs_hbm, zbuf, zsem, sem):
    step = pl.program_id(0)

    @pl.when(step == 0)
    def _():
        zbuf[...] = jnp.zeros_like(zbuf)

        def zero_copy(e):
            start = pl.multiple_of(jnp.maximum(zs_ref[e], 0) * PACK_ROWS, PACK_ROWS)
            return pltpu.make_async_copy(zbuf, xs_hbm.at[pl.ds(start, MOE_BLOCK * PACK_ROWS)], zsem.at[0])

        def zstart(e, c):
            @pl.when(zs_ref[e] >= 0)
            def _():
                zero_copy(e).start()
            return c

        def zwait(e, c):
            @pl.when(zs_ref[e] >= 0)
            def _():
                zero_copy(e).wait()
            return c

        lax.fori_loop(0, N_ZERO_FILLS, zstart, 0)
        lax.fori_loop(0, N_ZERO_FILLS, zwait, 0)

    def row_copy(n, k):
        a = (step * DISPATCH_TB + n) * TOP_K + k
        src = h_ref.at[pl.ds(pl.multiple_of(n * PACK_ROWS, PACK_ROWS), PACK_ROWS)]
        dst = xs_hbm.at[pl.ds(pl.multiple_of(dest_ref[a] * PACK_ROWS, PACK_ROWS), PACK_ROWS)]
        return pltpu.make_async_copy(src, dst, sem.at[0])

    def issue(n, c):
        for k in range(TOP_K):
            row_copy(n, k).start(priority=k)
        return c

    def drain(n, c):
        for k in range(TOP_K):
            row_copy(n, k).wait()
        return c

    lax.fori_loop(0, DISPATCH_TB, issue, 0, unroll=8)
    lax.fori_loop(0, DISPATCH_TB, drain, 0, unroll=8)


def dispatch(dest, zero_start, hpk):
    grid_spec = pltpu.PrefetchScalarGridSpec(
        num_scalar_prefetch=2,
        grid=(T // DISPATCH_TB,),
        in_specs=[pl.BlockSpec((DISPATCH_TB * PACK_ROWS, LANES), lambda i, dst, zs: (i, 0))],
        out_specs=pl.BlockSpec(memory_space=pl.ANY),
        scratch_shapes=[pltpu.VMEM((MOE_BLOCK * PACK_ROWS, LANES), jnp.uint32),
                        pltpu.SemaphoreType.DMA((1,)),
                        pltpu.SemaphoreType.DMA((1,))],
    )
    return pl.pallas_call(
        _dispatch_kernel,
        out_shape=jax.ShapeDtypeStruct((MOE_ROWS * PACK_ROWS, LANES), jnp.uint32),
        grid_spec=grid_spec,
        compiler_params=_cparams(("arbitrary",)),
        name="moe_dispatch",
    )(dest, zero_start, hpk)


def _expert_kernel(be_ref, nu_ref, ord_ref, seq_ref, x_ref, wg_hbm, wu_hbm, wd_hbm, y_ref,
                   wg_st, wu_st, wd_st, wg_bf, wu_bf, wd_bf, sem):
    b = pl.program_id(0)

    def weight_copies(e, slot):
        return (pltpu.make_async_copy(wg_hbm.at[e], wg_st.at[slot], sem.at[slot, 0]),
                pltpu.make_async_copy(wu_hbm.at[e], wu_st.at[slot], sem.at[slot, 1]),
                pltpu.make_async_copy(wd_hbm.at[e], wd_st.at[slot], sem.at[slot, 2]))

    def start_fetch(n, slot):
        @pl.when(seq_ref[n] >= 0)
        def _():
            for cp in weight_copies(seq_ref[n], slot):
                cp.start()

    @pl.when(b < nu_ref[0])
    def _():
        n = ord_ref[b]
        slot = n % 2
        changed = jnp.logical_or(b == 0, be_ref[b] != be_ref[jnp.maximum(b - 1, 0)])

        @pl.when(b == 0)
        def _():
            start_fetch(0, 0)
            start_fetch(1, 1)

        @pl.when(changed)
        def _():
            cg, cu, cd = weight_copies(be_ref[b], slot)
            cg.wait()
            wg_bf[...] = wg_st[slot].astype(BF16)
            cu.wait()
            wu_bf[...] = wu_st[slot].astype(BF16)
            cd.wait()
            wd_bf[...] = wd_st[slot].astype(BF16)
            start_fetch(n + 2, slot)

        x = _unpack_bf16_pairs(x_ref, 0, MOE_BLOCK).astype(BF16)
        gate = jnp.dot(x, wg_bf[...], preferred_element_type=F32)
        up = jnp.dot(x, wu_bf[...], preferred_element_type=F32)
        hid = (jax.nn.silu(gate) * up).astype(BF16)
        y = jnp.dot(hid, wd_bf[...], preferred_element_type=F32)
        word = _pack_bf16_pairs(y.astype(BF16).astype(F32))
        for c in range(PACK_ROWS):
            y_ref[pl.ds(c, MOE_BLOCK, stride=PACK_ROWS), :] = word[:, c * LANES:(c + 1) * LANES]

    @pl.when(b >= nu_ref[0])
    def _():
        y_ref[...] = jnp.zeros_like(y_ref)


def experts(block_expert, n_used, block_ord, expert_seq, xs, w_gate, w_up, w_down):
    def blk(b, be, nu, od, sq):
        return jnp.minimum(b, nu[0] - 1)

    grid_spec = pltpu.PrefetchScalarGridSpec(
        num_scalar_prefetch=4,
        grid=(MOE_BLOCKS,),
        in_specs=[pl.BlockSpec((MOE_BLOCK * PACK_ROWS, LANES), lambda b, be, nu, od, sq: (blk(b, be, nu, od, sq), 0)),
                  pl.BlockSpec(memory_space=pl.ANY),
                  pl.BlockSpec(memory_space=pl.ANY),
                  pl.BlockSpec(memory_space=pl.ANY)],
        out_specs=pl.BlockSpec((MOE_BLOCK * PACK_ROWS, LANES), lambda b, be, nu, od, sq: (b, 0)),
        scratch_shapes=[pltpu.VMEM((2, D, EXPERT_FF), F32),
                        pltpu.VMEM((2, D, EXPERT_FF), F32),
                        pltpu.VMEM((2, EXPERT_FF, D), F32),
                        pltpu.VMEM((D, EXPERT_FF), BF16),
                        pltpu.VMEM((D, EXPERT_FF), BF16),
                        pltpu.VMEM((EXPERT_FF, D), BF16),
                        pltpu.SemaphoreType.DMA((2, 3))],
    )
    return pl.pallas_call(
        _expert_kernel,
        out_shape=jax.ShapeDtypeStruct((MOE_ROWS * PACK_ROWS, LANES), jnp.uint32),
        grid_spec=grid_spec,
        compiler_params=_cparams(("arbitrary",)),
        name="moe_experts",
    )(block_expert, n_used, block_ord, expert_seq, xs, w_gate, w_up, w_down)


COMBINE_ROWS = 256
COMBINE_SUB = 16


def _combine_kernel(dest_ref, ys_hbm, x_ref, wt_ref, g_ref, o_ref, ybuf, sem):
    s = pl.program_id(0)
    ns = pl.num_programs(0)
    slot = s % 2
    tb = COMBINE_SUB
    rows = COMBINE_ROWS
    nxt = jnp.minimum(s + 1, ns - 1)

    def row_copy(step, n, k, sl):
        tok = step * rows + n
        src = ys_hbm.at[pl.ds(pl.multiple_of(dest_ref[tok * TOP_K + k] * PACK_ROWS, PACK_ROWS), PACK_ROWS)]
        dst = ybuf.at[sl * TOP_K + k, pl.ds(pl.multiple_of(n * PACK_ROWS, PACK_ROWS), PACK_ROWS)]
        return pltpu.make_async_copy(src, dst, sem.at[sl])

    def wait_all(step, sl):
        def body(n, c):
            for k in range(TOP_K):
                row_copy(step, n, k, sl).wait()
            return c
        lax.fori_loop(0, rows, body, 0, unroll=8)

    @pl.when(s == 0)
    def _():
        def body(n, c):
            for k in range(TOP_K):
                row_copy(0, n, k, 0).start(priority=k)
            return c
        lax.fori_loop(0, rows, body, 0, unroll=8)

    wait_all(s, slot)

    g = g_ref[...]
    y0_ref = ybuf.at[slot * TOP_K]
    y1_ref = ybuf.at[slot * TOP_K + 1]
    for b in range(rows // tb):
        for n in range(b * tb, (b + 1) * tb):
            for k in range(TOP_K):
                row_copy(nxt, n, k, 1 - slot).start(priority=k)
        sub = slice(b * tb, (b + 1) * tb)
        w = wt_ref[sub, :]
        y0 = _unpack_bf16_pairs(y0_ref, b * tb, tb)
        y1 = _unpack_bf16_pairs(y1_ref, b * tb, tb)
        z = x_ref[sub, :] + (w[:, 0:1] * y0 + w[:, 1:2] * y1)
        ms = jnp.mean(z * z, axis=-1, keepdims=True)
        o_ref[sub, :] = z * lax.rsqrt(ms + NORM_EPS) * g

    @pl.when(s == ns - 1)
    def _():
        wait_all(nxt, 1 - slot)


def combine(dest, ys, x1, wts, gain):
    rows = COMBINE_ROWS
    grid_spec = pltpu.PrefetchScalarGridSpec(
        num_scalar_prefetch=1,
        grid=(T // rows,),
        in_specs=[pl.BlockSpec(memory_space=pl.ANY),
                  pl.BlockSpec((rows, D), lambda s, dst: (s, 0)),
                  pl.BlockSpec((rows, LANES), lambda s, dst: (s, 0)),
                  pl.BlockSpec((1, D), lambda s, dst: (0, 0))],
        out_specs=pl.BlockSpec((rows, D), lambda s, dst: (s, 0)),
        scratch_shapes=[pltpu.VMEM((2 * TOP_K, rows * PACK_ROWS, LANES), jnp.uint32),
                        pltpu.SemaphoreType.DMA((2,))],
    )
    return pl.pallas_call(
        _combine_kernel,
        out_shape=jax.ShapeDtypeStruct((T, D), F32),
        grid_spec=grid_spec,
        compiler_params=_cparams(("arbitrary",)),
        name="moe_combine_final_norm",
    )(dest, ys, x1, wts, gain)


def dispatch_plan(ids, counts):
    experts_ = jnp.arange(N_EXPERTS, dtype=jnp.int32)
    padded = (counts + MOE_BLOCK - 1) // MOE_BLOCK * MOE_BLOCK
    pad_end = jnp.cumsum(padded)
    pad_start = pad_end - padded
    e = ids[:, :TOP_K]
    start_of = jnp.sum(jnp.where(e[:, :, None] == experts_[None, None, :], pad_start[None, None, :], 0), axis=-1)
    dest = (start_of + ids[:, TOP_K:2 * TOP_K]).reshape(N_ASSIGN).astype(jnp.int32)
    n_used = pad_end[-1] // MOE_BLOCK
    block_start = jnp.minimum(jnp.arange(MOE_BLOCKS, dtype=jnp.int32), n_used - 1) * MOE_BLOCK
    block_expert = jnp.sum((block_start[:, None] >= pad_end[None, :]).astype(jnp.int32), axis=1)
    block_expert = jnp.minimum(block_expert, N_EXPERTS - 1).astype(jnp.int32)
    tail = n_used + experts_
    zero_start = jnp.concatenate([jnp.where(counts > 0, pad_end - MOE_BLOCK, -1),
                                  jnp.where(tail < MOE_BLOCKS, tail * MOE_BLOCK, -1)]).astype(jnp.int32)
    present = counts > 0
    expert_ord = jnp.cumsum(present.astype(jnp.int32)) - 1
    slots = jnp.arange(N_EXPERTS + 2, dtype=jnp.int32)
    hit = present[None, :] & (expert_ord[None, :] == slots[:, None])
    expert_seq = jnp.where(jnp.any(hit, axis=1), jnp.sum(jnp.where(hit, experts_[None, :], 0), axis=1), -1)
    block_ord = jnp.sum(jnp.where(block_expert[:, None] == experts_[None, :], expert_ord[None, :], 0), axis=1)
    return (block_expert, n_used.astype(jnp.int32).reshape(1), dest, zero_start,
            block_ord.astype(jnp.int32), expert_seq.astype(jnp.int32))


def kernel(x, norm_mix, w_in, b_gate, ssm_a_re, ssm_a_im, ssm_log_step, ssm_b_re, ssm_b_im, ssm_c_re, ssm_c_im, ssm_d, w_glu, b_glu, w_up_attn, w_up_ssm, w_out, norm_ffn, w_router_group, b_router_group, w_router_expert, b_router_expert, w_expert_gate, w_expert_up, w_expert_down, norm_final):
    x2 = x.reshape(T, D)
    h = norm_permute(x2, norm_mix.reshape(1, D)).reshape(T, D)
    w_in_l = w_in.reshape(D, IN_COLS)
    qkv = proj(h, w_in_l, 0, 3 * QKV_COLS, tn=QKV_COLS, wt=QKV_COLS, name="proj_qkv")
    u = proj(h, w_in_l, 3 * QKV_COLS, SSM_W, tn=SSM_W, wt=512, name="proj_ssm_in")

    attn = attention(qkv.reshape(R, NI, 3 * QKV_COLS)).reshape(T, ATTN_OUT)

    G = SSM_W // SSM_CH
    wb, wc, pows = ssm_params(
        ssm_a_re.reshape(G, SSM_STATE).astype(F32), ssm_a_im.reshape(G, SSM_STATE).astype(F32),
        ssm_log_step.reshape(G),
        ssm_b_re.reshape(G, SSM_STATE, SSM_CH).astype(F32), ssm_b_im.reshape(G, SSM_STATE, SSM_CH).astype(F32),
        ssm_c_re.reshape(G, SSM_CH, SSM_STATE), ssm_c_im.reshape(G, SSM_CH, SSM_STATE))
    yg = ssm_scan(u.reshape(R, NI, SSM_W), wb, wc, pows, ssm_d.reshape(1, SSM_W).astype(F32))
    ssm = glu(yg.reshape(T, SSM_W), w_glu.reshape(SSM_W, SSM_W), b_glu.reshape(1, SSM_W))

    merged = merge(h, attn, ssm, w_in_l, b_gate.reshape(1, 2 * D),
                   w_up_attn.reshape(ATTN_OUT, D), w_up_ssm.reshape(SSM_W, D))
    x1 = out_proj(merged.reshape(R, NI, D), w_out.reshape(D, D), x2)

    w_r = jnp.concatenate([w_router_group.reshape(D, N_EGROUPS), w_router_expert.reshape(D, N_EXPERTS),
                           jnp.zeros((D, LANES - N_EGROUPS - N_EXPERTS), F32)], axis=1)
    b_r = jnp.concatenate([b_router_group.reshape(1, N_EGROUPS), b_router_expert.reshape(1, N_EXPERTS),
                           jnp.zeros((1, LANES - N_EGROUPS - N_EXPERTS), F32)], axis=1)
    hpk, ids, wts, counts = router(x1, norm_ffn.reshape(1, D), w_r, b_r)

    block_expert, n_used, dest, zero_start, block_ord, expert_seq = dispatch_plan(ids[:, :2 * TOP_K], counts[0, :N_EXPERTS])
    xs = dispatch(dest, zero_start, hpk)
    ys = experts(block_expert, n_used, block_ord, expert_seq, xs,
                 w_expert_gate.reshape(N_EXPERTS, D, EXPERT_FF), w_expert_up.reshape(N_EXPERTS, D, EXPERT_FF),
                 w_expert_down.reshape(N_EXPERTS, EXPERT_FF, D))
    out = combine(dest, ys, x1, wts, norm_final.reshape(1, D))
    return out.reshape(1, T, D)
```

```python
import functools
import math

import jax
import jax.numpy as jnp
from jax import lax
from jax.experimental import pallas as pl
from jax.experimental.pallas import tpu as pltpu

F32 = jnp.float32
BF16 = jnp.bfloat16

T = 8192
D = 2048
R = 16
NI = T // R
HEAD_DIM = 64
N_HEAD_SLOTS = 8
DILATIONS = (1, 4, 16)
ATTN_BLOCK = 128
QKV_COLS = 1536
ATTN_OUT = 512
SSM_W = 1024
SSM_STATE = 64
SSM_CH = 16
IN_COLS = 3 * QKV_COLS + SSM_W + 2 * D
N_EXPERTS = 32
N_EGROUPS = 4
EXPERTS_PER_GROUP = 8
TOP_K = 2
EXPERT_FF = 512
NORM_EPS = 1e-6
LANES = 128
VMEM_LIMIT = 48 * 1024 * 1024

MOE_BLOCK = 256
N_ASSIGN = T * TOP_K
MOE_BLOCKS = N_ASSIGN // MOE_BLOCK + N_EXPERTS
MOE_ROWS = MOE_BLOCKS * MOE_BLOCK


def _cparams(sem):
    return pltpu.CompilerParams(dimension_semantics=sem, vmem_limit_bytes=VMEM_LIMIT)


N_SLABS = D // LANES
NORM_TB = 32
NORM_CHUNK = 64
NORM_PITCH = R + 8


def _norm_permute_kernel(x_ref, g_ref, h_ref, slab):
    g = g_ref[...]

    def chunk(t, c):
        rows = pl.ds(pl.multiple_of(t * NORM_CHUNK, NORM_CHUNK), NORM_CHUNK)
        x = x_ref[rows, :]
        ms = jnp.mean(x * x, axis=-1, keepdims=True)
        hn = x * lax.rsqrt(ms + NORM_EPS) * g
        for k in range(NORM_CHUNK // R):
            dst = pl.ds(pl.multiple_of((t * (NORM_CHUNK // R) + k) * NORM_PITCH, 8), R)
            for s_ in range(N_SLABS):
                slab[s_, dst, :] = hn[k * R:(k + 1) * R, s_ * LANES:(s_ + 1) * LANES]
        return c

    lax.fori_loop(0, NORM_TB * R // NORM_CHUNK, chunk, 0)
    for r in range(R):
        pieces = [slab[s_, pl.ds(r, NORM_TB, stride=NORM_PITCH), :] for s_ in range(N_SLABS)]
        h_ref[r] = jnp.concatenate(pieces, axis=1).astype(BF16)


def norm_permute(x2, gain):
    return pl.pallas_call(
        _norm_permute_kernel,
        out_shape=jax.ShapeDtypeStruct((R, NI, D), BF16),
        grid=(NI // NORM_TB,),
        in_specs=[pl.BlockSpec((NORM_TB * R, D), lambda i: (i, 0)),
                  pl.BlockSpec((1, D), lambda i: (0, 0))],
        out_specs=pl.BlockSpec((R, NORM_TB, D), lambda i: (0, i, 0)),
        scratch_shapes=[pltpu.VMEM((N_SLABS, NORM_TB * NORM_PITCH, LANES), F32)],
        compiler_params=_cparams(("arbitrary",)),
        name="norm_permute",
    )(x2, gain)


MXU_N = 256


def _cast_weight_once(w_ref, wbf_ref):
    @pl.when(pl.program_id(1) == 0)
    def _():
        wbf_ref[...] = w_ref[...].astype(BF16)


def _proj_kernel(a_ref, *refs):
    w_refs, o_ref, wbf_ref = refs[:-2], refs[-2], refs[-1]
    wt = w_refs[0].shape[1]

    @pl.when(pl.program_id(1) == 0)
    def _():
        for k, w_ref in enumerate(w_refs):
            wbf_ref[:, k * wt:(k + 1) * wt] = w_ref[...].astype(BF16)

    a = a_ref[...]
    for c in range(o_ref.shape[1] // MXU_N):
        cols = slice(c * MXU_N, (c + 1) * MXU_N)
        o_ref[:, cols] = jnp.dot(a, wbf_ref[:, cols], preferred_element_type=F32)


def proj(h, w_in, col_off, n_cols, tn, wt, name="proj"):
    tm = 1024
    pieces = tn // wt
    off = col_off // wt

    def piece(k):
        return pl.BlockSpec((D, wt), lambda j, i: (0, j * pieces + off + k), pipeline_mode=pl.Buffered(1))

    return pl.pallas_call(
        _proj_kernel,
        out_shape=jax.ShapeDtypeStruct((T, n_cols), F32),
        grid=(n_cols // tn, T // tm),
        in_specs=[pl.BlockSpec((tm, D), lambda j, i: (i, 0))] + [piece(k) for k in range(pieces)],
        out_specs=pl.BlockSpec((tm, tn), lambda j, i: (i, j)),
        scratch_shapes=[pltpu.VMEM((D, tn), BF16)],
        compiler_params=_cparams(("arbitrary", "arbitrary")),
        name=name,
    )(h, *([w_in] * pieces))


def _seq_index_maps(d):
    nseg = R // d
    qlen = ATTN_BLOCK // nseg
    return nseg, qlen


def _bias_matrices(d, hp):
    nseg, qlen = _seq_index_maps(d)
    klen = 2 * qlen
    row = lax.broadcasted_iota(jnp.int32, (2 * ATTN_BLOCK, 2 * ATTN_BLOCK), 0)
    col = lax.broadcasted_iota(jnp.int32, (2 * ATTN_BLOCK, 2 * ATTN_BLOCK), 1)
    rho = row % ATTN_BLOCK
    jq = (rho % qlen) * nseg + rho // qlen
    jk = ((col % klen) - qlen) * nseg + col // klen
    steps = jq - jk
    valid = (steps >= 0) & (steps <= ATTN_BLOCK)
    head = 2 * hp + row // ATTN_BLOCK
    slope = lax.bitcast_convert_type((127 - (head + 1)) << 23, F32)
    bias = -slope * (d * steps).astype(F32)
    neg = jnp.float32(-jnp.inf)
    return jnp.where(valid, bias, neg), jnp.where(valid & (jk >= 0), bias, neg)


def _attend_pair(q, k, v, bias):
    lane = lax.broadcasted_iota(jnp.int32, (ATTN_BLOCK, LANES), 1)
    first = lane < HEAD_DIM
    zero = jnp.zeros_like(q)
    q2 = jnp.concatenate([jnp.where(first, q, zero), jnp.where(first, zero, q)], axis=0).astype(BF16)
    s = lax.dot_general(q2, k.astype(BF16), (((1,), (1,)), ((), ())), preferred_element_type=F32)
    s = s + bias
    m = jnp.max(s, axis=-1, keepdims=True)
    p = jnp.exp(s - m)
    l = jnp.sum(p, axis=-1, keepdims=True)
    o2 = jnp.dot(p.astype(BF16), v.astype(BF16), preferred_element_type=F32)
    o = jnp.where(first, o2[:ATTN_BLOCK], o2[ATTN_BLOCK:])
    m_b = jnp.where(first, m[:ATTN_BLOCK], m[ATTN_BLOCK:])
    l_b = jnp.where(first, l[:ATTN_BLOCK], l[ATTN_BLOCK:])
    return o, m_b, l_b


def _attn_kernel(q_ref, kp_ref, kc_ref, vp_ref, vc_ref, o_ref, kbuf, vbuf, obuf, mbuf, lbuf, bias_ref):
    hp = pl.program_id(0)
    it = pl.program_id(1)
    g = pl.program_id(2)
    scale = HEAD_DIM ** -0.5

    for gi, d in enumerate(DILATIONS):
        nseg, qlen = _seq_index_maps(d)
        klen = 2 * qlen
        nblk = ATTN_BLOCK // qlen

        @pl.when(g == gi)
        def _(gi=gi, d=d, nseg=nseg, qlen=qlen, klen=klen, nblk=nblk):
            if nseg > 1:
                kbuf[:, :ATTN_BLOCK, :] = kp_ref[...]
                kbuf[:, ATTN_BLOCK:, :] = kc_ref[...]
                vbuf[:, :ATTN_BLOCK, :] = vp_ref[...]
                vbuf[:, ATTN_BLOCK:, :] = vc_ref[...]

            @pl.when(it == 0)
            def _():
                b_reg, b_first = _bias_matrices(d, hp)
                bias_ref[gi, 0] = b_reg
                bias_ref[gi, 1] = b_first

            def block(idx, carry):
                rd = idx // nblk
                bb = idx % nblk
                q0 = pl.multiple_of(bb * qlen, qlen)
                k0 = pl.multiple_of(ATTN_BLOCK + bb * qlen - qlen, qlen)
                qs, ks, vs = [], [], []
                for m_ in range(nseg):
                    rr = rd + d * m_
                    qs.append(q_ref[rr, pl.ds(q0, qlen), :])
                    if nseg > 1:
                        ks.append(kbuf[rr, pl.ds(k0, klen), :])
                        vs.append(vbuf[rr, pl.ds(k0, klen), :])
                    else:
                        ks += [kp_ref[rr], kc_ref[rr]]
                        vs += [vp_ref[rr], vc_ref[rr]]
                q = jnp.concatenate(qs, axis=0) * scale
                k = jnp.concatenate(ks, axis=0)
                v = jnp.concatenate(vs, axis=0)
                is_first = jnp.logical_and(it == 0, bb == 0)
                bias = bias_ref[gi, jnp.where(is_first, 1, 0)]
                o, mx, den = _attend_pair(q, k, v, bias)
                for m_ in range(nseg):
                    rr = rd + d * m_
                    seg = slice(m_ * qlen, (m_ + 1) * qlen)
                    obuf[gi, rr, pl.ds(q0, qlen), :] = o[seg]
                    mbuf[gi, rr, pl.ds(q0, qlen), :] = mx[seg]
                    lbuf[gi, rr, pl.ds(q0, qlen), :] = den[seg]
                return carry

            lax.fori_loop(0, d * nblk, block, 0, unroll=True)

    @pl.when(g == len(DILATIONS) - 1)
    def _():
        for r in range(R):
            m0, m1, m2 = mbuf[0, r], mbuf[1, r], mbuf[2, r]
            mx = jnp.maximum(jnp.maximum(m0, m1), m2)
            e0, e1, e2 = jnp.exp(m0 - mx), jnp.exp(m1 - mx), jnp.exp(m2 - mx)
            den = e0 * lbuf[0, r] + e1 * lbuf[1, r] + e2 * lbuf[2, r]
            num = e0 * obuf[0, r] + e1 * obuf[1, r] + e2 * obuf[2, r]
            o_ref[r] = (num / den).astype(BF16)


def attention(qkv3):
    n_hp = N_HEAD_SLOTS // 2
    n_it = NI // ATTN_BLOCK
    ng = len(DILATIONS)
    cb = QKV_COLS // LANES

    def cur(base):
        return pl.BlockSpec((R, ATTN_BLOCK, LANES), lambda hp, it, g: (0, it, base + g * n_hp + hp))

    def prev(base):
        return pl.BlockSpec((R, ATTN_BLOCK, LANES),
                            lambda hp, it, g: (0, jnp.maximum(it - 1, 0), base + g * n_hp + hp))

    return pl.pallas_call(
        _attn_kernel,
        out_shape=jax.ShapeDtypeStruct((R, NI, ATTN_OUT), BF16),
        grid=(n_hp, n_it, ng),
        in_specs=[cur(0), prev(cb), cur(cb), prev(2 * cb), cur(2 * cb)],
        out_specs=pl.BlockSpec((R, ATTN_BLOCK, LANES), lambda hp, it, g: (0, it, hp)),
        scratch_shapes=[pltpu.VMEM((R, 2 * ATTN_BLOCK, LANES), F32),
                        pltpu.VMEM((R, 2 * ATTN_BLOCK, LANES), F32),
                        pltpu.VMEM((ng, R, ATTN_BLOCK, LANES), F32),
                        pltpu.VMEM((ng, R, ATTN_BLOCK, LANES), F32),
                        pltpu.VMEM((ng, R, ATTN_BLOCK, LANES), F32),
                        pltpu.VMEM((ng, 2, 2 * ATTN_BLOCK, 2 * ATTN_BLOCK), F32)],
        compiler_params=_cparams(("arbitrary", "arbitrary", "arbitrary")),
        name="dilated_attention",
    )(qkv3, qkv3, qkv3, qkv3, qkv3)


SSM_SLAB = 256
SSM_SLABS = SSM_W // SSM_SLAB
SLAB_STATES = SSM_SLAB // SSM_CH * SSM_STATE
SSM_TI = 128
SSM_MM_CHUNK = 4


def _ssm_kernel(u_ref, wb_ref, wc_ref, pw_ref, dsk_ref, o_ref, s_ref, zs_ref, zc_ref, tr_ref, zt_ref):
    ic = pl.program_id(1)
    ns = SLAB_STATES

    @pl.when(ic == 0)
    def _():
        zc_ref[...] = jnp.zeros_like(zc_ref)

    n_chunks = R // SSM_MM_CHUNK
    n_tiles = SSM_TI // 8
    wb = wb_ref[0]
    wc = wc_ref[0]
    dsk = dsk_ref[...]
    arb = jnp.broadcast_to(pw_ref[0, 0:1, :ns], (8, ns))
    aib = jnp.broadcast_to(pw_ref[0, 0:1, ns:], (8, ns))


    def bu_chunk(c):
        lo = c * SSM_MM_CHUNK
        uc = u_ref[lo:lo + SSM_MM_CHUNK].reshape(SSM_MM_CHUNK * SSM_TI, SSM_SLAB)
        bu = jnp.dot(uc.astype(BF16), wb, preferred_element_type=F32)
        s_ref[lo:lo + SSM_MM_CHUNK] = bu.reshape(SSM_MM_CHUNK, SSM_TI, 2 * ns)

    def local_chunk(c):
        lo = c * SSM_MM_CHUNK
        first = max(lo, 1)
        for t in range(n_tiles):
            rows = slice(t * 8, (t + 1) * 8)
            pr = s_ref[first - 1, rows, :ns]
            pi = s_ref[first - 1, rows, ns:]
            for r in range(first, lo + SSM_MM_CHUNK):
                nr = s_ref[r, rows, :ns] + (arb * pr - aib * pi)
                ni = s_ref[r, rows, ns:] + (arb * pi + aib * pr)
                s_ref[r, rows, :ns] = nr
                s_ref[r, rows, ns:] = ni
                pr, pi = nr, ni

    bu_chunk(0)
    for c in range(n_chunks):
        if c + 1 < n_chunks:
            bu_chunk(c + 1)
        local_chunk(c)

    n_ch = ns // LANES
    for c in range(n_ch):
        tr_ref[0, c * 8:c * 8 + 1, :] = pw_ref[0, R - 1:R, c * LANES:(c + 1) * LANES]
        tr_ref[1, c * 8:c * 8 + 1, :] = pw_ref[0, R - 1:R, ns + c * LANES:ns + (c + 1) * LANES]
    a16r = tr_ref[0, pl.ds(0, n_ch, stride=8), :]
    a16i = tr_ref[1, pl.ds(0, n_ch, stride=8), :]

    def ztile(t, carry):
        zr, zi = carry
        rows = pl.ds(pl.multiple_of(t * 8, 8), 8)
        for c in range(n_ch):
            tr_ref[0, c * 8:(c + 1) * 8, :] = s_ref[R - 1, rows, c * LANES:(c + 1) * LANES]
            tr_ref[1, c * 8:(c + 1) * 8, :] = s_ref[R - 1, rows, ns + c * LANES:ns + (c + 1) * LANES]
        for i in range(8):
            zt_ref[0, pl.ds(i, n_ch, stride=8), :] = zr
            zt_ref[1, pl.ds(i, n_ch, stride=8), :] = zi
            er = tr_ref[0, pl.ds(i, n_ch, stride=8), :]
            ei = tr_ref[1, pl.ds(i, n_ch, stride=8), :]
            zr, zi = a16r * zr - a16i * zi + er, a16r * zi + a16i * zr + ei
        for c in range(n_ch):
            zs_ref[rows, c * LANES:(c + 1) * LANES] = zt_ref[0, c * 8:(c + 1) * 8, :]
            zs_ref[rows, ns + c * LANES:ns + (c + 1) * LANES] = zt_ref[1, c * 8:(c + 1) * 8, :]
        return zr, zi

    zr_end, zi_end = lax.fori_loop(0, SSM_TI // 8, ztile, (zc_ref[0], zc_ref[1]))
    zc_ref[0] = zr_end
    zc_ref[1] = zi_end

    def fix_chunk(c):
        lo = c * SSM_MM_CHUNK
        for r in range(lo, lo + SSM_MM_CHUNK):
            prb = jnp.broadcast_to(pw_ref[0, r:r + 1, :ns], (8, ns))
            pib = jnp.broadcast_to(pw_ref[0, r:r + 1, ns:], (8, ns))
            for t in range(n_tiles):
                rows = slice(t * 8, (t + 1) * 8)
                zr = zs_ref[rows, :ns]
                zi = zs_ref[rows, ns:]
                s_ref[r, rows, :ns] = s_ref[r, rows, :ns] + (prb * zr - pib * zi)
                s_ref[r, rows, ns:] = s_ref[r, rows, ns:] + (prb * zi + pib * zr)

    def out_chunk(c):
        lo = c * SSM_MM_CHUNK
        xs = s_ref[lo:lo + SSM_MM_CHUNK].reshape(SSM_MM_CHUNK * SSM_TI, 2 * ns)
        y = jnp.dot(xs.astype(BF16), wc, preferred_element_type=F32)
        y = y.reshape(SSM_MM_CHUNK, SSM_TI, SSM_SLAB) + dsk * u_ref[lo:lo + SSM_MM_CHUNK]
        o_ref[lo:lo + SSM_MM_CHUNK] = jax.nn.gelu(y)

    fix_chunk(0)
    for c in range(n_chunks):
        if c + 1 < n_chunks:
            fix_chunk(c + 1)
        out_chunk(c)


def ssm_scan(u3, wb, wc, pows, dskip):
    ns2 = 2 * SLAB_STATES
    return pl.pallas_call(
        _ssm_kernel,
        out_shape=jax.ShapeDtypeStruct((R, NI, SSM_W), F32),
        grid=(SSM_SLABS, NI // SSM_TI),
        in_specs=[pl.BlockSpec((R, SSM_TI, SSM_SLAB), lambda kb, ic: (0, ic, kb)),
                  pl.BlockSpec((1, SSM_SLAB, ns2), lambda kb, ic: (kb, 0, 0)),
                  pl.BlockSpec((1, ns2, SSM_SLAB), lambda kb, ic: (kb, 0, 0)),
                  pl.BlockSpec((1, R, ns2), lambda kb, ic: (kb, 0, 0)),
                  pl.BlockSpec((1, SSM_SLAB), lambda kb, ic: (0, kb))],
        out_specs=pl.BlockSpec((R, SSM_TI, SSM_SLAB), lambda kb, ic: (0, ic, kb)),
        scratch_shapes=[pltpu.VMEM((R, SSM_TI, ns2), F32),
                        pltpu.VMEM((SSM_TI, ns2), F32),
                        pltpu.VMEM((2, SLAB_STATES // LANES, LANES), F32),
                        pltpu.VMEM((2, 8 * SLAB_STATES // LANES, LANES), F32),
                        pltpu.VMEM((2, 8 * SLAB_STATES // LANES, LANES), F32)],
        compiler_params=_cparams(("arbitrary", "arbitrary")),
        name="s5_ssm",
    )(u3, wb, wc, pows, dskip)


def ssm_params(a_re, a_im, log_step, b_re, b_im, c_re, c_im):
    G, P, H = SSM_W // SSM_CH, SSM_STATE, SSM_CH
    gs = SSM_SLAB // SSM_CH
    step = jnp.exp(log_step.astype(F32))[:, None]
    kk = jnp.arange(1, R + 1, dtype=F32)[:, None, None]
    mag = jnp.exp(kk * (a_re * step))
    ang = kk * (a_im * step)
    pw_re, pw_im = mag * jnp.cos(ang), mag * jnp.sin(ang)
    abar_re, abar_im = pw_re[0], pw_im[0]
    nr, ni = abar_re - 1.0, abar_im
    den = a_re * a_re + a_im * a_im
    f_re = (nr * a_re + ni * a_im) / den
    f_im = (ni * a_re - nr * a_im) / den
    bbar_re = f_re[..., None] * b_re - f_im[..., None] * b_im
    bbar_im = f_re[..., None] * b_im + f_im[..., None] * b_re
    rows_g = jnp.arange(gs * H, dtype=jnp.int32) // H
    cols_g = jnp.arange(gs * P, dtype=jnp.int32) // P
    b_t = jnp.concatenate([jnp.swapaxes(bbar_re, 1, 2).reshape(SSM_SLABS, gs * H, P),
                           jnp.swapaxes(bbar_im, 1, 2).reshape(SSM_SLABS, gs * H, P)], axis=-1)
    tile_p = (jnp.arange(2 * P, dtype=jnp.int32)[:, None] ==
              (jnp.arange(2 * gs * P, dtype=jnp.int32) // (gs * P) * P + jnp.arange(2 * gs * P, dtype=jnp.int32) % P)[None, :])
    mask_b = rows_g[:, None] == jnp.tile(cols_g, 2)[None, :]
    wb = jnp.where(mask_b[None], jnp.einsum('kap,pc->kac', b_t.astype(BF16), tile_p.astype(BF16),
                                            preferred_element_type=F32), 0.0).astype(BF16)
    c_t = jnp.concatenate([jnp.swapaxes(c_re.astype(F32), 1, 2).reshape(SSM_SLABS, gs * P, H),
                           -jnp.swapaxes(c_im.astype(F32), 1, 2).reshape(SSM_SLABS, gs * P, H)], axis=1)
    tile_h = jnp.arange(H, dtype=jnp.int32)[:, None] == (jnp.arange(gs * H, dtype=jnp.int32) % H)[None, :]
    mask_c = jnp.tile(cols_g, 2)[:, None] == rows_g[None, :]
    wc = jnp.where(mask_c[None], jnp.einsum('kah,hc->kac', c_t.astype(BF16), tile_h.astype(BF16),
                                            preferred_element_type=F32), 0.0).astype(BF16)
    pows = jnp.concatenate([pw_re.reshape(R, SSM_SLABS, gs * P), pw_im.reshape(R, SSM_SLABS, gs * P)], axis=-1)
    return wb, wc, jnp.transpose(pows, (1, 0, 2))


def _glu_kernel(a_ref, w_ref, b_ref, o_ref, wbf_ref):
    @pl.when(pl.program_id(0) == 0)
    def _():
        wbf_ref[...] = w_ref[...].astype(BF16)

    a = a_ref[...].astype(BF16)
    for c in range(SSM_W // MXU_N):
        cols = slice(c * MXU_N, (c + 1) * MXU_N)
        acc = jnp.dot(a, wbf_ref[:, cols], preferred_element_type=F32)
        o_ref[:, cols] = (a_ref[:, cols] * jax.nn.sigmoid(acc + b_ref[:, cols])).astype(BF16)


def glu(yg, w_glu, b_glu):
    tm = 512
    return pl.pallas_call(
        _glu_kernel,
        out_shape=jax.ShapeDtypeStruct((T, SSM_W), BF16),
        grid=(T // tm,),
        in_specs=[pl.BlockSpec((tm, SSM_W), lambda i: (i, 0)),
                  pl.BlockSpec((SSM_W, SSM_W), lambda i: (0, 0)),
                  pl.BlockSpec((1, SSM_W), lambda i: (0, 0))],
        out_specs=pl.BlockSpec((tm, SSM_W), lambda i: (i, 0)),
        scratch_shapes=[pltpu.VMEM((SSM_W, SSM_W), BF16)],
        compiler_params=_cparams(("arbitrary",)),
        name="ssm_glu",
    )(yg, w_glu, b_glu)


def _merge_kernel(h_ref, at_ref, ss_ref, wga_ref, wgs_ref, ba_ref, bs_ref, wa_ref, ws_ref, o_ref,
                  wga_bf, wgs_bf, wa_bf, ws_bf):
    _cast_weight_once(wga_ref, wga_bf)
    _cast_weight_once(wgs_ref, wgs_bf)
    _cast_weight_once(wa_ref, wa_bf)
    _cast_weight_once(ws_ref, ws_bf)
    h = h_ref[...]
    at = at_ref[...]
    ss = ss_ref[...]
    for c in range(wa_ref.shape[1] // MXU_N):
        cols = slice(c * MXU_N, (c + 1) * MXU_N)
        ga = jax.nn.sigmoid(jnp.dot(h, wga_bf[:, cols], preferred_element_type=F32) + ba_ref[:, cols])
        a = jnp.dot(at, wa_bf[:, cols], preferred_element_type=F32)
        gs = jax.nn.sigmoid(jnp.dot(h, wgs_bf[:, cols], preferred_element_type=F32) + bs_ref[:, cols])
        s = jnp.dot(ss, ws_bf[:, cols], preferred_element_type=F32)
        o_ref[:, cols] = (ga * a + gs * s).astype(BF16)


def merge(h, attn, ssm, w_in, b_gate, w_up_attn, w_up_ssm):
    tm, tn = 1024, 512
    nj = D // tn
    off_a = (3 * QKV_COLS + SSM_W) // tn
    once = pl.Buffered(1)
    return pl.pallas_call(
        _merge_kernel,
        out_shape=jax.ShapeDtypeStruct((T, D), BF16),
        grid=(nj, T // tm),
        in_specs=[pl.BlockSpec((tm, D), lambda j, i: (i, 0)),
                  pl.BlockSpec((tm, ATTN_OUT), lambda j, i: (i, 0)),
                  pl.BlockSpec((tm, SSM_W), lambda j, i: (i, 0)),
                  pl.BlockSpec((D, tn), lambda j, i: (0, j + off_a), pipeline_mode=once),
                  pl.BlockSpec((D, tn), lambda j, i: (0, j + off_a + nj), pipeline_mode=once),
                  pl.BlockSpec((1, tn), lambda j, i: (0, j)),
                  pl.BlockSpec((1, tn), lambda j, i: (0, j + nj)),
                  pl.BlockSpec((ATTN_OUT, tn), lambda j, i: (0, j), pipeline_mode=once),
                  pl.BlockSpec((SSM_W, tn), lambda j, i: (0, j), pipeline_mode=once)],
        out_specs=pl.BlockSpec((tm, tn), lambda j, i: (i, j)),
        scratch_shapes=[pltpu.VMEM((D, tn), BF16), pltpu.VMEM((D, tn), BF16),
                        pltpu.VMEM((ATTN_OUT, tn), BF16), pltpu.VMEM((SSM_W, tn), BF16)],
        compiler_params=_cparams(("arbitrary", "arbitrary")),
        name="gates_branch_merge",
    )(h, attn, ssm, w_in, w_in, b_gate, b_gate, w_up_attn, w_up_ssm)


OUTPROJ_TB = 64
OUTPROJ_PITCH = OUTPROJ_TB + 8


def _outproj_kernel(m_ref, w_ref, x_ref, o_ref, slab, wbf_ref):
    _cast_weight_once(w_ref, wbf_ref)
    tb = OUTPROJ_TB
    a = m_ref[...].reshape(R * tb, D)
    per = MXU_N // LANES
    for c in range(w_ref.shape[1] // MXU_N):
        acc = jnp.dot(a, wbf_ref[:, c * MXU_N:(c + 1) * MXU_N], preferred_element_type=F32)
        for s_ in range(per):
            lanes = slice(s_ * LANES, (s_ + 1) * LANES)
            for r in range(R):
                slab[c * per + s_, r * OUTPROJ_PITCH:r * OUTPROJ_PITCH + tb, :] = acc[r * tb:(r + 1) * tb, lanes]
        for s_ in range(per):
            lanes = slice((c * per + s_) * LANES, (c * per + s_ + 1) * LANES)
            for i in range(tb):
                rows = slice(i * R, (i + 1) * R)
                o_ref[rows, lanes] = slab[c * per + s_, pl.ds(i, R, stride=OUTPROJ_PITCH), :] + x_ref[rows, lanes]


def out_proj(merged3, w_out, x2):
    tb, tn = OUTPROJ_TB, 1024
    return pl.pallas_call(
        _outproj_kernel,
        out_shape=jax.ShapeDtypeStruct((T, D), F32),
        grid=(D // tn, NI // tb),
        in_specs=[pl.BlockSpec((R, tb, D), lambda j, i: (0, i, 0)),
                  pl.BlockSpec((D, tn), lambda j, i: (0, j), pipeline_mode=pl.Buffered(1)),
                  pl.BlockSpec((tb * R, tn), lambda j, i: (i, j))],
        out_specs=pl.BlockSpec((tb * R, tn), lambda j, i: (i, j)),
        scratch_shapes=[pltpu.VMEM((tn // LANES, OUTPROJ_PITCH * R, LANES), F32),
                        pltpu.VMEM((D, tn), BF16)],
        compiler_params=_cparams(("arbitrary", "arbitrary")),
        name="out_proj_residual",
    )(merged3, w_out, x2)


PACK_ROWS = 8
HALF_D = D // 2


def _pack_bf16_pairs(zf):
    top = lax.bitcast_convert_type(zf, jnp.uint32)
    return top[:, HALF_D:] | (top[:, :HALF_D] >> 16)


def _unpack_bf16_pairs(x_ref, first, n):
    lo, hi = [], []
    for c in range(PACK_ROWS):
        w = x_ref[pl.ds(first * PACK_ROWS + c, n, stride=PACK_ROWS), :]
        lo.append(lax.bitcast_convert_type(w << 16, F32))
        hi.append(lax.bitcast_convert_type(w & jnp.uint32(0xFFFF0000), F32))
    return jnp.concatenate(lo + hi, axis=1)


def _router_kernel(x_ref, g_ref, w_ref, b_ref, h_ref, id_ref, wt_ref, cnt_ref, carry_ref):
    step = pl.program_id(0)

    @pl.when(step == 0)
    def _():
        carry_ref[...] = jnp.zeros_like(carry_ref)

    x = x_ref[...]
    ms = jnp.mean(x * x, axis=-1, keepdims=True)
    z = x * lax.rsqrt(ms + NORM_EPS) * g_ref[...]
    zh = z.astype(BF16)
    zf = zh.astype(F32)
    word = _pack_bf16_pairs(zf)
    for c in range(PACK_ROWS):
        h_ref[pl.ds(c, x.shape[0], stride=PACK_ROWS), :] = word[:, c * LANES:(c + 1) * LANES]
    zl = (z - zf).astype(BF16)
    w = w_ref[...]
    wh = w.astype(BF16)
    wl = (w - wh.astype(F32)).astype(BF16)
    logits = (jnp.dot(zh, wh, preferred_element_type=F32) + jnp.dot(zl, wh, preferred_element_type=F32)
              + jnp.dot(zh, wl, preferred_element_type=F32)) + b_ref[...]
    lane = lax.broadcasted_iota(jnp.int32, logits.shape, 1)
    lanef = lane.astype(F32)
    neg = jnp.float32(-jnp.inf)
    big = jnp.float32(1e9)
    gl = jnp.where(lane < N_EGROUPS, logits, neg)
    gmax = jnp.max(gl, axis=-1, keepdims=True)
    gidx = jnp.min(jnp.where(gl == gmax, lanef, big), axis=-1, keepdims=True)
    pg = 1.0 / jnp.sum(jnp.exp(gl - gmax), axis=-1, keepdims=True)
    lo = N_EGROUPS + EXPERTS_PER_GROUP * gidx
    el = jnp.where((lanef >= lo) & (lanef < lo + EXPERTS_PER_GROUP), logits, neg)
    t1 = jnp.max(el, axis=-1, keepdims=True)
    j1 = jnp.min(jnp.where(el == t1, lanef, big), axis=-1, keepdims=True)
    el2 = jnp.where(lanef == j1, neg, el)
    t2 = jnp.max(el2, axis=-1, keepdims=True)
    j2 = jnp.min(jnp.where(el2 == t2, lanef, big), axis=-1, keepdims=True)
    e21 = jnp.exp(t2 - t1)
    w1 = pg / (1.0 + e21)
    w2 = pg * e21 / (1.0 + e21)
    e1f = j1 - N_EGROUPS
    e2f = j2 - N_EGROUPS

    tm = x.shape[0]
    oh1 = (lanef == e1f).astype(F32)
    oh2 = (lanef == e2f).astype(F32)
    ri = lax.broadcasted_iota(jnp.int32, (tm, tm), 0)
    ci = lax.broadcasted_iota(jnp.int32, (tm, tm), 1)
    before = (ci < ri).astype(BF16)
    p1 = jnp.dot(before, oh1.astype(BF16), preferred_element_type=F32)
    p2 = jnp.dot(before, oh2.astype(BF16), preferred_element_type=F32)
    carry = carry_ref[...]
    c1 = jnp.sum(oh1, axis=0, keepdims=True)
    c2 = jnp.sum(oh2, axis=0, keepdims=True)
    rank1 = jnp.sum(oh1 * (carry + p1), axis=-1, keepdims=True)
    rank2 = jnp.sum(oh2 * (carry + c1 + p2), axis=-1, keepdims=True)
    carry = carry + c1 + c2
    carry_ref[...] = carry
    cnt_ref[...] = jnp.broadcast_to(carry, cnt_ref.shape).astype(jnp.int32)

    ids = jnp.where(lane == 0, e1f, jnp.where(lane == 1, e2f, jnp.where(lane == 2, rank1, jnp.where(lane == 3, rank2, 0.0))))
    id_ref[...] = ids.astype(jnp.int32)
    wt_ref[...] = jnp.where(lane == 0, w1, jnp.where(lane == 1, w2, 0.0))


def router(x1, gain, w_r, b_r):
    tm = 256
    return pl.pallas_call(
        _router_kernel,
        out_shape=(jax.ShapeDtypeStruct((T * PACK_ROWS, LANES), jnp.uint32),
                   jax.ShapeDtypeStruct((T, LANES), jnp.int32),
                   jax.ShapeDtypeStruct((T, LANES), F32),
                   jax.ShapeDtypeStruct((8, LANES), jnp.int32)),
        grid=(T // tm,),
        in_specs=[pl.BlockSpec((tm, D), lambda i: (i, 0)),
                  pl.BlockSpec((1, D), lambda i: (0, 0)),
                  pl.BlockSpec((D, LANES), lambda i: (0, 0)),
                  pl.BlockSpec((1, LANES), lambda i: (0, 0))],
        out_specs=(pl.BlockSpec((tm * PACK_ROWS, LANES), lambda i: (i, 0)),
                   pl.BlockSpec((tm, LANES), lambda i: (i, 0)),
                   pl.BlockSpec((tm, LANES), lambda i: (i, 0)),
                   pl.BlockSpec((8, LANES), lambda i: (0, 0))),
        scratch_shapes=[pltpu.VMEM((1, LANES), F32)],
        compiler_params=_cparams(("arbitrary",)),
        name="ffn_norm_router",
    )(x1, gain, w_r, b_r)


DISPATCH_TB = 512
N_ZERO_FILLS = 2 * N_EXPERTS


def _dispatch_kernel(dest_ref, zs_ref, h_ref, xs_hbm, zbuf, zsem, sem):
    step = pl.program_id(0)

    @pl.when(step == 0)
    def _():
        zbuf[...] = jnp.zeros_like(zbuf)

        def zero_copy(e):
            start = pl.multiple_of(jnp.maximum(zs_ref[e], 0) * PACK_ROWS, PACK_ROWS)
            return pltpu.make_async_copy(zbuf, xs_hbm.at[pl.ds(start, MOE_BLOCK * PACK_ROWS)], zsem.at[0])

        def zstart(e, c):
            @pl.when(zs_ref[e] >= 0)
            def _():
                zero_copy(e).start()
            return c

        def zwait(e, c):
            @pl.when(zs_ref[e] >= 0)
            def _():
                zero_copy(e).wait()
            return c

        lax.fori_loop(0, N_ZERO_FILLS, zstart, 0)
        lax.fori_loop(0, N_ZERO_FILLS, zwait, 0)

    def row_copy(n, k):
        a = (step * DISPATCH_TB + n) * TOP_K + k
        src = h_ref.at[pl.ds(pl.multiple_of(n * PACK_ROWS, PACK_ROWS), PACK_ROWS)]
        dst = xs_hbm.at[pl.ds(pl.multiple_of(dest_ref[a] * PACK_ROWS, PACK_ROWS), PACK_ROWS)]
        return pltpu.make_async_copy(src, dst, sem.at[0])

    def issue(n, c):
        for k in range(TOP_K):
            row_copy(n, k).start(priority=k)
        return c

    def drain(n, c):
        for k in range(TOP_K):
            row_copy(n, k).wait()
        return c

    lax.fori_loop(0, DISPATCH_TB, issue, 0, unroll=8)
    lax.fori_loop(0, DISPATCH_TB, drain, 0, unroll=8)


def dispatch(dest, zero_start, hpk):
    grid_spec = pltpu.PrefetchScalarGridSpec(
        num_scalar_prefetch=2,
        grid=(T // DISPATCH_TB,),
        in_specs=[pl.BlockSpec((DISPATCH_TB * PACK_ROWS, LANES), lambda i, dst, zs: (i, 0))],
        out_specs=pl.BlockSpec(memory_space=pl.ANY),
        scratch_shapes=[pltpu.VMEM((MOE_BLOCK * PACK_ROWS, LANES), jnp.uint32),
                        pltpu.SemaphoreType.DMA((1,)),
                        pltpu.SemaphoreType.DMA((1,))],
    )
    return pl.pallas_call(
        _dispatch_kernel,
        out_shape=jax.ShapeDtypeStruct((MOE_ROWS * PACK_ROWS, LANES), jnp.uint32),
        grid_spec=grid_spec,
        compiler_params=_cparams(("arbitrary",)),
        name="moe_dispatch",
    )(dest, zero_start, hpk)


def _expert_kernel(be_ref, nu_ref, ord_ref, seq_ref, x_ref, wg_hbm, wu_hbm, wd_hbm, y_ref,
                   wg_st, wu_st, wd_st, wg_bf, wu_bf, wd_bf, sem):
    b = pl.program_id(0)

    def weight_copies(e, slot):
        return (pltpu.make_async_copy(wg_hbm.at[e], wg_st.at[slot], sem.at[slot, 0]),
                pltpu.make_async_copy(wu_hbm.at[e], wu_st.at[slot], sem.at[slot, 1]),
                pltpu.make_async_copy(wd_hbm.at[e], wd_st.at[slot], sem.at[slot, 2]))

    def start_fetch(n, slot):
        @pl.when(seq_ref[n] >= 0)
        def _():
            for cp in weight_copies(seq_ref[n], slot):
                cp.start()

    @pl.when(b < nu_ref[0])
    def _():
        n = ord_ref[b]
        slot = n % 2
        changed = jnp.logical_or(b == 0, be_ref[b] != be_ref[jnp.maximum(b - 1, 0)])

        @pl.when(b == 0)
        def _():
            start_fetch(0, 0)
            start_fetch(1, 1)

        @pl.when(changed)
        def _():
            cg, cu, cd = weight_copies(be_ref[b], slot)
            cg.wait()
            wg_bf[...] = wg_st[slot].astype(BF16)
            cu.wait()
            wu_bf[...] = wu_st[slot].astype(BF16)
            cd.wait()
            wd_bf[...] = wd_st[slot].astype(BF16)
            start_fetch(n + 2, slot)

        x = _unpack_bf16_pairs(x_ref, 0, MOE_BLOCK).astype(BF16)
        gate = jnp.dot(x, wg_bf[...], preferred_element_type=F32)
        up = jnp.dot(x, wu_bf[...], preferred_element_type=F32)
        hid = (jax.nn.silu(gate) * up).astype(BF16)
        y = jnp.dot(hid, wd_bf[...], preferred_element_type=F32)
        word = _pack_bf16_pairs(y.astype(BF16).astype(F32))
        for c in range(PACK_ROWS):
            y_ref[pl.ds(c, MOE_BLOCK, stride=PACK_ROWS), :] = word[:, c * LANES:(c + 1) * LANES]

    @pl.when(b >= nu_ref[0])
    def _():
        y_ref[...] = jnp.zeros_like(y_ref)


def experts(block_expert, n_used, block_ord, expert_seq, xs, w_gate, w_up, w_down):
    def blk(b, be, nu, od, sq):
        return jnp.minimum(b, nu[0] - 1)

    grid_spec = pltpu.PrefetchScalarGridSpec(
        num_scalar_prefetch=4,
        grid=(MOE_BLOCKS,),
        in_specs=[pl.BlockSpec((MOE_BLOCK * PACK_ROWS, LANES), lambda b, be, nu, od, sq: (blk(b, be, nu, od, sq), 0)),
                  pl.BlockSpec(memory_space=pl.ANY),
                  pl.BlockSpec(memory_space=pl.ANY),
                  pl.BlockSpec(memory_space=pl.ANY)],
        out_specs=pl.BlockSpec((MOE_BLOCK * PACK_ROWS, LANES), lambda b, be, nu, od, sq: (b, 0)),
        scratch_shapes=[pltpu.VMEM((2, D, EXPERT_FF), F32),
                        pltpu.VMEM((2, D, EXPERT_FF), F32),
                        pltpu.VMEM((2, EXPERT_FF, D), F32),
                        pltpu.VMEM((D, EXPERT_FF), BF16),
                        pltpu.VMEM((D, EXPERT_FF), BF16),
                        pltpu.VMEM((EXPERT_FF, D), BF16),
                        pltpu.SemaphoreType.DMA((2, 3))],
    )
    return pl.pallas_call(
        _expert_kernel,
        out_shape=jax.ShapeDtypeStruct((MOE_ROWS * PACK_ROWS, LANES), jnp.uint32),
        grid_spec=grid_spec,
        compiler_params=_cparams(("arbitrary",)),
        name="moe_experts",
    )(block_expert, n_used, block_ord, expert_seq, xs, w_gate, w_up, w_down)


COMBINE_ROWS = 512
COMBINE_SUB = 16


def _combine_kernel(dest_ref, ys_hbm, x_ref, wt_ref, g_ref, o_ref, ybuf, sem):
    s = pl.program_id(0)
    ns = pl.num_programs(0)
    slot = s % 2
    tb = COMBINE_SUB
    rows = COMBINE_ROWS
    nxt = jnp.minimum(s + 1, ns - 1)

    def row_copy(step, n, k, sl):
        tok = step * rows + n
        src = ys_hbm.at[pl.ds(pl.multiple_of(dest_ref[tok * TOP_K + k] * PACK_ROWS, PACK_ROWS), PACK_ROWS)]
        dst = ybuf.at[sl * TOP_K + k, pl.ds(pl.multiple_of(n * PACK_ROWS, PACK_ROWS), PACK_ROWS)]
        return pltpu.make_async_copy(src, dst, sem.at[sl])

    def wait_all(step, sl):
        def body(n, c):
            for k in range(TOP_K):
                row_copy(step, n, k, sl).wait()
            return c
        lax.fori_loop(0, rows, body, 0, unroll=8)

    @pl.when(s == 0)
    def _():
        def body(n, c):
            for k in range(TOP_K):
                row_copy(0, n, k, 0).start(priority=k)
            return c
        lax.fori_loop(0, rows, body, 0, unroll=8)

    wait_all(s, slot)

    g = g_ref[...]
    y0_ref = ybuf.at[slot * TOP_K]
    y1_ref = ybuf.at[slot * TOP_K + 1]
    for b in range(rows // tb):
        for n in range(b * tb, (b + 1) * tb):
            for k in range(TOP_K):
                row_copy(nxt, n, k, 1 - slot).start(priority=k)
        sub = slice(b * tb, (b + 1) * tb)
        w = wt_ref[sub, :]
        y0 = _unpack_bf16_pairs(y0_ref, b * tb, tb)
        y1 = _unpack_bf16_pairs(y1_ref, b * tb, tb)
        z = x_ref[sub, :] + (w[:, 0:1] * y0 + w[:, 1:2] * y1)
        ms = jnp.mean(z * z, axis=-1, keepdims=True)
        o_ref[sub, :] = z * lax.rsqrt(ms + NORM_EPS) * g

    @pl.when(s == ns - 1)
    def _():
        wait_all(nxt, 1 - slot)


def combine(dest, ys, x1, wts, gain):
    rows = COMBINE_ROWS
    grid_spec = pltpu.PrefetchScalarGridSpec(
        num_scalar_prefetch=1,
        grid=(T // rows,),
        in_specs=[pl.BlockSpec(memory_space=pl.ANY),
                  pl.BlockSpec((rows, D), lambda s, dst: (s, 0)),
                  pl.BlockSpec((rows, LANES), lambda s, dst: (s, 0)),
                  pl.BlockSpec((1, D), lambda s, dst: (0, 0))],
        out_specs=pl.BlockSpec((rows, D), lambda s, dst: (s, 0)),
        scratch_shapes=[pltpu.VMEM((2 * TOP_K, rows * PACK_ROWS, LANES), jnp.uint32),
                        pltpu.SemaphoreType.DMA((2,))],
    )
    return pl.pallas_call(
        _combine_kernel,
        out_shape=jax.ShapeDtypeStruct((T, D), F32),
        grid_spec=grid_spec,
        compiler_params=_cparams(("arbitrary",)),
        name="moe_combine_final_norm",
    )(dest, ys, x1, wts, gain)


def dispatch_plan(ids, counts):
    experts_ = jnp.arange(N_EXPERTS, dtype=jnp.int32)
    padded = (counts + MOE_BLOCK - 1) // MOE_BLOCK * MOE_BLOCK
    pad_end = jnp.cumsum(padded)
    pad_start = pad_end - padded
    e = ids[:, :TOP_K]
    start_of = jnp.sum(jnp.where(e[:, :, None] == experts_[None, None, :], pad_start[None, None, :], 0), axis=-1)
    dest = (start_of + ids[:, TOP_K:2 * TOP_K]).reshape(N_ASSIGN).astype(jnp.int32)
    n_used = pad_end[-1] // MOE_BLOCK
    block_start = jnp.minimum(jnp.arange(MOE_BLOCKS, dtype=jnp.int32), n_used - 1) * MOE_BLOCK
    block_expert = jnp.sum((block_start[:, None] >= pad_end[None, :]).astype(jnp.int32), axis=1)
    block_expert = jnp.minimum(block_expert, N_EXPERTS - 1).astype(jnp.int32)
    tail = n_used + experts_
    zero_start = jnp.concatenate([jnp.where(counts > 0, pad_end - MOE_BLOCK, -1),
                                  jnp.where(tail < MOE_BLOCKS, tail * MOE_BLOCK, -1)]).astype(jnp.int32)
    present = counts > 0
    expert_ord = jnp.cumsum(present.astype(jnp.int32)) - 1
    slots = jnp.arange(N_EXPERTS + 2, dtype=jnp.int32)
    hit = present[None, :] & (expert_ord[None, :] == slots[:, None])
    expert_seq = jnp.where(jnp.any(hit, axis=1), jnp.sum(jnp.where(hit, experts_[None, :], 0), axis=1), -1)
    block_ord = jnp.sum(jnp.where(block_expert[:, None] == experts_[None, :], expert_ord[None, :], 0), axis=1)
    return (block_expert, n_used.astype(jnp.int32).reshape(1), dest, zero_start,
            block_ord.astype(jnp.int32), expert_seq.astype(jnp.int32))


def kernel(x, norm_mix, w_in, b_gate, ssm_a_re, ssm_a_im, ssm_log_step, ssm_b_re, ssm_b_im, ssm_c_re, ssm_c_im, ssm_d, w_glu, b_glu, w_up_attn, w_up_ssm, w_out, norm_ffn, w_router_group, b_router_group, w_router_expert, b_router_expert, w_expert_gate, w_expert_up, w_expert_down, norm_final):
    x2 = x.reshape(T, D)
    h = norm_permute(x2, norm_mix.reshape(1, D)).reshape(T, D)
    w_in_l = w_in.reshape(D, IN_COLS)
    qkv = proj(h, w_in_l, 0, 3 * QKV_COLS, tn=QKV_COLS, wt=QKV_COLS, name="proj_qkv")
    u = proj(h, w_in_l, 3 * QKV_COLS, SSM_W, tn=SSM_W, wt=512, name="proj_ssm_in")

    attn = attention(qkv.reshape(R, NI, 3 * QKV_COLS)).reshape(T, ATTN_OUT)

    G = SSM_W // SSM_CH
    wb, wc, pows = ssm_params(
        ssm_a_re.reshape(G, SSM_STATE).astype(F32), ssm_a_im.reshape(G, SSM_STATE).astype(F32),
        ssm_log_step.reshape(G),
        ssm_b_re.reshape(G, SSM_STATE, SSM_CH).astype(F32), ssm_b_im.reshape(G, SSM_STATE, SSM_CH).astype(F32),
        ssm_c_re.reshape(G, SSM_CH, SSM_STATE), ssm_c_im.reshape(G, SSM_CH, SSM_STATE))
    yg = ssm_scan(u.reshape(R, NI, SSM_W), wb, wc, pows, ssm_d.reshape(1, SSM_W).astype(F32))
    ssm = glu(yg.reshape(T, SSM_W), w_glu.reshape(SSM_W, SSM_W), b_glu.reshape(1, SSM_W))

    merged = merge(h, attn, ssm, w_in_l, b_gate.reshape(1, 2 * D),
                   w_up_attn.reshape(ATTN_OUT, D), w_up_ssm.reshape(SSM_W, D))
    x1 = out_proj(merged.reshape(R, NI, D), w_out.reshape(D, D), x2)

    w_r = jnp.concatenate([w_router_group.reshape(D, N_EGROUPS), w_router_expert.reshape(D, N_EXPERTS),
                           jnp.zeros((D, LANES - N_EGROUPS - N_EXPERTS), F32)], axis=1)
    b_r = jnp.concatenate([b_router_group.reshape(1, N_EGROUPS), b_router_expert.reshape(1, N_EXPERTS),
                           jnp.zeros((1, LANES - N_EGROUPS - N_EXPERTS), F32)], axis=1)
    hpk, ids, wts, counts = router(x1, norm_ffn.reshape(1, D), w_r, b_r)

    block_expert, n_used, dest, zero_start, block_ord, expert_seq = dispatch_plan(ids[:, :2 * TOP_K], counts[0, :N_EXPERTS])
    xs = dispatch(dest, zero_start, hpk)
    ys = experts(block_expert, n_used, block_ord, expert_seq, xs,
                 w_expert_gate.reshape(N_EXPERTS, D, EXPERT_FF), w_expert_up.reshape(N_EXPERTS, D, EXPERT_FF),
                 w_expert_down.reshape(N_EXPERTS, EXPERT_FF, D))
    out = combine(dest, ys, x1, wts, norm_final.reshape(1, D))
    return out.reshape(1, T, D)
```

```python
import functools
import math

import jax
import jax.numpy as jnp
from jax import lax
from jax.experimental import pallas as pl
from jax.experimental.pallas import tpu as pltpu

F32 = jnp.float32
BF16 = jnp.bfloat16

T = 8192
D = 2048
R = 16
NI = T // R
HEAD_DIM = 64
N_HEAD_SLOTS = 8
DILATIONS = (1, 4, 16)
ATTN_BLOCK = 128
QKV_COLS = 1536
ATTN_OUT = 512
SSM_W = 1024
SSM_STATE = 64
SSM_CH = 16
IN_COLS = 3 * QKV_COLS + SSM_W + 2 * D
N_EXPERTS = 32
N_EGROUPS = 4
EXPERTS_PER_GROUP = 8
TOP_K = 2
EXPERT_FF = 512
NORM_EPS = 1e-6
LANES = 128
VMEM_LIMIT = 48 * 1024 * 1024

MOE_BLOCK = 256
N_ASSIGN = T * TOP_K
MOE_BLOCKS = N_ASSIGN // MOE_BLOCK + N_EXPERTS
MOE_ROWS = MOE_BLOCKS * MOE_BLOCK


def _cparams(sem):
    return pltpu.CompilerParams(dimension_semantics=sem, vmem_limit_bytes=VMEM_LIMIT)


N_SLABS = D // LANES
NORM_TB = 32
NORM_CHUNK = 64
NORM_PITCH = R + 8


def _norm_permute_kernel(x_ref, g_ref, h_ref, slab):
    g = g_ref[...]

    def chunk(t, c):
        rows = pl.ds(pl.multiple_of(t * NORM_CHUNK, NORM_CHUNK), NORM_CHUNK)
        x = x_ref[rows, :]
        ms = jnp.mean(x * x, axis=-1, keepdims=True)
        hn = x * lax.rsqrt(ms + NORM_EPS) * g
        for k in range(NORM_CHUNK // R):
            dst = pl.ds(pl.multiple_of((t * (NORM_CHUNK // R) + k) * NORM_PITCH, 8), R)
            for s_ in range(N_SLABS):
                slab[s_, dst, :] = hn[k * R:(k + 1) * R, s_ * LANES:(s_ + 1) * LANES]
        return c

    lax.fori_loop(0, NORM_TB * R // NORM_CHUNK, chunk, 0)
    for r in range(R):
        pieces = [slab[s_, pl.ds(r, NORM_TB, stride=NORM_PITCH), :] for s_ in range(N_SLABS)]
        h_ref[r] = jnp.concatenate(pieces, axis=1).astype(BF16)


def norm_permute(x2, gain):
    return pl.pallas_call(
        _norm_permute_kernel,
        out_shape=jax.ShapeDtypeStruct((R, NI, D), BF16),
        grid=(NI // NORM_TB,),
        in_specs=[pl.BlockSpec((NORM_TB * R, D), lambda i: (i, 0)),
                  pl.BlockSpec((1, D), lambda i: (0, 0))],
        out_specs=pl.BlockSpec((R, NORM_TB, D), lambda i: (0, i, 0)),
        scratch_shapes=[pltpu.VMEM((N_SLABS, NORM_TB * NORM_PITCH, LANES), F32)],
        compiler_params=_cparams(("arbitrary",)),
        name="norm_permute",
    )(x2, gain)


MXU_N = 256


def _stationary_weight_tile(w_hbm, col0, stage_ref, wbf_ref, sem):
    j = pl.program_id(0)
    tn = stage_ref.shape[1]

    def copy(jj):
        c0 = pl.multiple_of(col0 + jj * tn, LANES)
        return pltpu.make_async_copy(w_hbm.at[:, pl.ds(c0, tn)], stage_ref, sem)

    @pl.when(pl.program_id(1) == 0)
    def _():
        @pl.when(j == 0)
        def _():
            copy(0).start()

        copy(j).wait()
        wbf_ref[...] = stage_ref[...].astype(BF16)

        @pl.when(j + 1 < pl.num_programs(0))
        def _():
            copy(j + 1).start()


def _proj_kernel(a_ref, w_hbm, o_ref, stage_ref, wbf_ref, sem, *, col_off):
    _stationary_weight_tile(w_hbm, col_off, stage_ref, wbf_ref, sem.at[0])
    a = a_ref[...]
    for c in range(o_ref.shape[1] // MXU_N):
        cols = slice(c * MXU_N, (c + 1) * MXU_N)
        o_ref[:, cols] = jnp.dot(a, wbf_ref[:, cols], preferred_element_type=F32)


def proj(h, w_in, col_off, n_cols, tn, name="proj"):
    tm = 1024
    return pl.pallas_call(
        functools.partial(_proj_kernel, col_off=col_off),
        out_shape=jax.ShapeDtypeStruct((T, n_cols), F32),
        grid=(n_cols // tn, T // tm),
        in_specs=[pl.BlockSpec((tm, D), lambda j, i: (i, 0)),
                  pl.BlockSpec(memory_space=pl.ANY)],
        out_specs=pl.BlockSpec((tm, tn), lambda j, i: (i, j)),
        scratch_shapes=[pltpu.VMEM((D, tn), F32), pltpu.VMEM((D, tn), BF16), pltpu.SemaphoreType.DMA((1,))],
        compiler_params=_cparams(("arbitrary", "arbitrary")),
        name=name,
    )(h, w_in)


def _seq_index_maps(d):
    nseg = R // d
    qlen = ATTN_BLOCK // nseg
    return nseg, qlen


def _bias_matrices(d, hp):
    nseg, qlen = _seq_index_maps(d)
    klen = 2 * qlen
    row = lax.broadcasted_iota(jnp.int32, (2 * ATTN_BLOCK, 2 * ATTN_BLOCK), 0)
    col = lax.broadcasted_iota(jnp.int32, (2 * ATTN_BLOCK, 2 * ATTN_BLOCK), 1)
    rho = row % ATTN_BLOCK
    jq = (rho % qlen) * nseg + rho // qlen
    jk = ((col % klen) - qlen) * nseg + col // klen
    steps = jq - jk
    valid = (steps >= 0) & (steps <= ATTN_BLOCK)
    head = 2 * hp + row // ATTN_BLOCK
    slope = lax.bitcast_convert_type((127 - (head + 1)) << 23, F32)
    bias = -slope * (d * steps).astype(F32)
    neg = jnp.float32(-jnp.inf)
    return jnp.where(valid, bias, neg), jnp.where(valid & (jk >= 0), bias, neg)


def _attend_pair(q, k, v, bias):
    lane = lax.broadcasted_iota(jnp.int32, (ATTN_BLOCK, LANES), 1)
    first = lane < HEAD_DIM
    zero = jnp.zeros_like(q)
    q2 = jnp.concatenate([jnp.where(first, q, zero), jnp.where(first, zero, q)], axis=0).astype(BF16)
    s = lax.dot_general(q2, k.astype(BF16), (((1,), (1,)), ((), ())), preferred_element_type=F32)
    s = s + bias
    m = jnp.max(s, axis=-1, keepdims=True)
    p = jnp.exp(s - m)
    l = jnp.sum(p, axis=-1, keepdims=True)
    o2 = jnp.dot(p.astype(BF16), v.astype(BF16), preferred_element_type=F32)
    o = jnp.where(first, o2[:ATTN_BLOCK], o2[ATTN_BLOCK:])
    m_b = jnp.where(first, m[:ATTN_BLOCK], m[ATTN_BLOCK:])
    l_b = jnp.where(first, l[:ATTN_BLOCK], l[ATTN_BLOCK:])
    return o, m_b, l_b


def _attn_kernel(q_ref, kp_ref, kc_ref, vp_ref, vc_ref, o_ref, kbuf, vbuf, obuf, mbuf, lbuf, bias_ref):
    hp = pl.program_id(0)
    it = pl.program_id(1)
    g = pl.program_id(2)
    scale = HEAD_DIM ** -0.5

    for gi, d in enumerate(DILATIONS):
        nseg, qlen = _seq_index_maps(d)
        klen = 2 * qlen
        nblk = ATTN_BLOCK // qlen

        @pl.when(g == gi)
        def _(gi=gi, d=d, nseg=nseg, qlen=qlen, klen=klen, nblk=nblk):
            if nseg > 1:
                kbuf[:, :ATTN_BLOCK, :] = kp_ref[...]
                kbuf[:, ATTN_BLOCK:, :] = kc_ref[...]
                vbuf[:, :ATTN_BLOCK, :] = vp_ref[...]
                vbuf[:, ATTN_BLOCK:, :] = vc_ref[...]

            @pl.when(it == 0)
            def _():
                b_reg, b_first = _bias_matrices(d, hp)
                bias_ref[gi, 0] = b_reg
                bias_ref[gi, 1] = b_first

            def block(idx, carry):
                rd = idx // nblk
                bb = idx % nblk
                q0 = pl.multiple_of(bb * qlen, qlen)
                k0 = pl.multiple_of(ATTN_BLOCK + bb * qlen - qlen, qlen)
                qs, ks, vs = [], [], []
                for m_ in range(nseg):
                    rr = rd + d * m_
                    qs.append(q_ref[rr, pl.ds(q0, qlen), :])
                    if nseg > 1:
                        ks.append(kbuf[rr, pl.ds(k0, klen), :])
                        vs.append(vbuf[rr, pl.ds(k0, klen), :])
                    else:
                        ks += [kp_ref[rr], kc_ref[rr]]
                        vs += [vp_ref[rr], vc_ref[rr]]
                q = jnp.concatenate(qs, axis=0) * scale
                k = jnp.concatenate(ks, axis=0)
                v = jnp.concatenate(vs, axis=0)
                is_first = jnp.logical_and(it == 0, bb == 0)
                bias = bias_ref[gi, jnp.where(is_first, 1, 0)]
                o, mx, den = _attend_pair(q, k, v, bias)
                for m_ in range(nseg):
                    rr = rd + d * m_
                    seg = slice(m_ * qlen, (m_ + 1) * qlen)
                    obuf[gi, rr, pl.ds(q0, qlen), :] = o[seg]
                    mbuf[gi, rr, pl.ds(q0, qlen), :] = mx[seg]
                    lbuf[gi, rr, pl.ds(q0, qlen), :] = den[seg]
                return carry

            lax.fori_loop(0, d * nblk, block, 0, unroll=True)

    @pl.when(g == len(DILATIONS) - 1)
    def _():
        for r in range(R):
            m0, m1, m2 = mbuf[0, r], mbuf[1, r], mbuf[2, r]
            mx = jnp.maximum(jnp.maximum(m0, m1), m2)
            e0, e1, e2 = jnp.exp(m0 - mx), jnp.exp(m1 - mx), jnp.exp(m2 - mx)
            den = e0 * lbuf[0, r] + e1 * lbuf[1, r] + e2 * lbuf[2, r]
            num = e0 * obuf[0, r] + e1 * obuf[1, r] + e2 * obuf[2, r]
            o_ref[r] = (num / den).astype(BF16)


def attention(qkv3):
    n_hp = N_HEAD_SLOTS // 2
    n_it = NI // ATTN_BLOCK
    ng = len(DILATIONS)
    cb = QKV_COLS // LANES

    def cur(base):
        return pl.BlockSpec((R, ATTN_BLOCK, LANES), lambda hp, it, g: (0, it, base + g * n_hp + hp))

    def prev(base):
        return pl.BlockSpec((R, ATTN_BLOCK, LANES),
                            lambda hp, it, g: (0, jnp.maximum(it - 1, 0), base + g * n_hp + hp))

    return pl.pallas_call(
        _attn_kernel,
        out_shape=jax.ShapeDtypeStruct((R, NI, ATTN_OUT), BF16),
        grid=(n_hp, n_it, ng),
        in_specs=[cur(0), prev(cb), cur(cb), prev(2 * cb), cur(2 * cb)],
        out_specs=pl.BlockSpec((R, ATTN_BLOCK, LANES), lambda hp, it, g: (0, it, hp)),
        scratch_shapes=[pltpu.VMEM((R, 2 * ATTN_BLOCK, LANES), F32),
                        pltpu.VMEM((R, 2 * ATTN_BLOCK, LANES), F32),
                        pltpu.VMEM((ng, R, ATTN_BLOCK, LANES), F32),
                        pltpu.VMEM((ng, R, ATTN_BLOCK, LANES), F32),
                        pltpu.VMEM((ng, R, ATTN_BLOCK, LANES), F32),
                        pltpu.VMEM((ng, 2, 2 * ATTN_BLOCK, 2 * ATTN_BLOCK), F32)],
        compiler_params=_cparams(("arbitrary", "arbitrary", "arbitrary")),
        name="dilated_attention",
    )(qkv3, qkv3, qkv3, qkv3, qkv3)


SSM_SLAB = 256
SSM_SLABS = SSM_W // SSM_SLAB
SLAB_STATES = SSM_SLAB // SSM_CH * SSM_STATE
SSM_TI = 128
SSM_MM_CHUNK = 4


def _ssm_kernel(u_ref, wb_ref, wc_ref, pw_ref, dsk_ref, o_ref, s_ref, zs_ref, zc_ref, tr_ref, zt_ref):
    ic = pl.program_id(1)
    ns = SLAB_STATES

    @pl.when(ic == 0)
    def _():
        zc_ref[...] = jnp.zeros_like(zc_ref)

    n_chunks = R // SSM_MM_CHUNK
    n_tiles = SSM_TI // 8
    wb = wb_ref[0]
    wc = wc_ref[0]
    dsk = dsk_ref[...]
    arb = jnp.broadcast_to(pw_ref[0, 0:1, :ns], (8, ns))
    aib = jnp.broadcast_to(pw_ref[0, 0:1, ns:], (8, ns))


    def bu_chunk(c):
        lo = c * SSM_MM_CHUNK
        uc = u_ref[lo:lo + SSM_MM_CHUNK].reshape(SSM_MM_CHUNK * SSM_TI, SSM_SLAB)
        bu = jnp.dot(uc.astype(BF16), wb, preferred_element_type=F32)
        s_ref[lo:lo + SSM_MM_CHUNK] = bu.reshape(SSM_MM_CHUNK, SSM_TI, 2 * ns)

    def local_chunk(c):
        lo = c * SSM_MM_CHUNK
        first = max(lo, 1)
        for t in range(n_tiles):
            rows = slice(t * 8, (t + 1) * 8)
            pr = s_ref[first - 1, rows, :ns]
            pi = s_ref[first - 1, rows, ns:]
            for r in range(first, lo + SSM_MM_CHUNK):
                nr = s_ref[r, rows, :ns] + (arb * pr - aib * pi)
                ni = s_ref[r, rows, ns:] + (arb * pi + aib * pr)
                s_ref[r, rows, :ns] = nr
                s_ref[r, rows, ns:] = ni
                pr, pi = nr, ni

    bu_chunk(0)
    for c in range(n_chunks):
        if c + 1 < n_chunks:
            bu_chunk(c + 1)
        local_chunk(c)

    n_ch = ns // LANES
    for c in range(n_ch):
        tr_ref[0, c * 8:c * 8 + 1, :] = pw_ref[0, R - 1:R, c * LANES:(c + 1) * LANES]
        tr_ref[1, c * 8:c * 8 + 1, :] = pw_ref[0, R - 1:R, ns + c * LANES:ns + (c + 1) * LANES]
    a16r = tr_ref[0, pl.ds(0, n_ch, stride=8), :]
    a16i = tr_ref[1, pl.ds(0, n_ch, stride=8), :]

    def ztile(t, carry):
        zr, zi = carry
        rows = pl.ds(pl.multiple_of(t * 8, 8), 8)
        for c in range(n_ch):
            tr_ref[0, c * 8:(c + 1) * 8, :] = s_ref[R - 1, rows, c * LANES:(c + 1) * LANES]
            tr_ref[1, c * 8:(c + 1) * 8, :] = s_ref[R - 1, rows, ns + c * LANES:ns + (c + 1) * LANES]
        for i in range(8):
            zt_ref[0, pl.ds(i, n_ch, stride=8), :] = zr
            zt_ref[1, pl.ds(i, n_ch, stride=8), :] = zi
            er = tr_ref[0, pl.ds(i, n_ch, stride=8), :]
            ei = tr_ref[1, pl.ds(i, n_ch, stride=8), :]
            zr, zi = a16r * zr - a16i * zi + er, a16r * zi + a16i * zr + ei
        for c in range(n_ch):
            zs_ref[rows, c * LANES:(c + 1) * LANES] = zt_ref[0, c * 8:(c + 1) * 8, :]
            zs_ref[rows, ns + c * LANES:ns + (c + 1) * LANES] = zt_ref[1, c * 8:(c + 1) * 8, :]
        return zr, zi

    zr_end, zi_end = lax.fori_loop(0, SSM_TI // 8, ztile, (zc_ref[0], zc_ref[1]))
    zc_ref[0] = zr_end
    zc_ref[1] = zi_end

    def fix_chunk(c):
        lo = c * SSM_MM_CHUNK
        for r in range(lo, lo + SSM_MM_CHUNK):
            prb = jnp.broadcast_to(pw_ref[0, r:r + 1, :ns], (8, ns))
            pib = jnp.broadcast_to(pw_ref[0, r:r + 1, ns:], (8, ns))
            for t in range(n_tiles):
                rows = slice(t * 8, (t + 1) * 8)
                zr = zs_ref[rows, :ns]
                zi = zs_ref[rows, ns:]
                s_ref[r, rows, :ns] = s_ref[r, rows, :ns] + (prb * zr - pib * zi)
                s_ref[r, rows, ns:] = s_ref[r, rows, ns:] + (prb * zi + pib * zr)

    def out_chunk(c):
        lo = c * SSM_MM_CHUNK
        xs = s_ref[lo:lo + SSM_MM_CHUNK].reshape(SSM_MM_CHUNK * SSM_TI, 2 * ns)
        y = jnp.dot(xs.astype(BF16), wc, preferred_element_type=F32)
        y = y.reshape(SSM_MM_CHUNK, SSM_TI, SSM_SLAB) + dsk * u_ref[lo:lo + SSM_MM_CHUNK]
        o_ref[lo:lo + SSM_MM_CHUNK] = jax.nn.gelu(y)

    fix_chunk(0)
    for c in range(n_chunks):
        if c + 1 < n_chunks:
            fix_chunk(c + 1)
        out_chunk(c)


def ssm_scan(u3, wb, wc, pows, dskip):
    ns2 = 2 * SLAB_STATES
    return pl.pallas_call(
        _ssm_kernel,
        out_shape=jax.ShapeDtypeStruct((R, NI, SSM_W), F32),
        grid=(SSM_SLABS, NI // SSM_TI),
        in_specs=[pl.BlockSpec((R, SSM_TI, SSM_SLAB), lambda kb, ic: (0, ic, kb)),
                  pl.BlockSpec((1, SSM_SLAB, ns2), lambda kb, ic: (kb, 0, 0)),
                  pl.BlockSpec((1, ns2, SSM_SLAB), lambda kb, ic: (kb, 0, 0)),
                  pl.BlockSpec((1, R, ns2), lambda kb, ic: (kb, 0, 0)),
                  pl.BlockSpec((1, SSM_SLAB), lambda kb, ic: (0, kb))],
        out_specs=pl.BlockSpec((R, SSM_TI, SSM_SLAB), lambda kb, ic: (0, ic, kb)),
        scratch_shapes=[pltpu.VMEM((R, SSM_TI, ns2), F32),
                        pltpu.VMEM((SSM_TI, ns2), F32),
                        pltpu.VMEM((2, SLAB_STATES // LANES, LANES), F32),
                        pltpu.VMEM((2, 8 * SLAB_STATES // LANES, LANES), F32),
                        pltpu.VMEM((2, 8 * SLAB_STATES // LANES, LANES), F32)],
        compiler_params=_cparams(("arbitrary", "arbitrary")),
        name="s5_ssm",
    )(u3, wb, wc, pows, dskip)


def ssm_params(a_re, a_im, log_step, b_re, b_im, c_re, c_im):
    G, P, H = SSM_W // SSM_CH, SSM_STATE, SSM_CH
    gs = SSM_SLAB // SSM_CH
    step = jnp.exp(log_step.astype(F32))[:, None]
    kk = jnp.arange(1, R + 1, dtype=F32)[:, None, None]
    mag = jnp.exp(kk * (a_re * step))
    ang = kk * (a_im * step)
    pw_re, pw_im = mag * jnp.cos(ang), mag * jnp.sin(ang)
    abar_re, abar_im = pw_re[0], pw_im[0]
    nr, ni = abar_re - 1.0, abar_im
    den = a_re * a_re + a_im * a_im
    f_re = (nr * a_re + ni * a_im) / den
    f_im = (ni * a_re - nr * a_im) / den
    bbar_re = f_re[..., None] * b_re - f_im[..., None] * b_im
    bbar_im = f_re[..., None] * b_im + f_im[..., None] * b_re
    rows_g = jnp.arange(gs * H, dtype=jnp.int32) // H
    cols_g = jnp.arange(gs * P, dtype=jnp.int32) // P
    b_t = jnp.concatenate([jnp.swapaxes(bbar_re, 1, 2).reshape(SSM_SLABS, gs * H, P),
                           jnp.swapaxes(bbar_im, 1, 2).reshape(SSM_SLABS, gs * H, P)], axis=-1)
    tile_p = (jnp.arange(2 * P, dtype=jnp.int32)[:, None] ==
              (jnp.arange(2 * gs * P, dtype=jnp.int32) // (gs * P) * P + jnp.arange(2 * gs * P, dtype=jnp.int32) % P)[None, :])
    mask_b = rows_g[:, None] == jnp.tile(cols_g, 2)[None, :]
    wb = jnp.where(mask_b[None], jnp.einsum('kap,pc->kac', b_t.astype(BF16), tile_p.astype(BF16),
                                            preferred_element_type=F32), 0.0).astype(BF16)
    c_t = jnp.concatenate([jnp.swapaxes(c_re.astype(F32), 1, 2).reshape(SSM_SLABS, gs * P, H),
                           -jnp.swapaxes(c_im.astype(F32), 1, 2).reshape(SSM_SLABS, gs * P, H)], axis=1)
    tile_h = jnp.arange(H, dtype=jnp.int32)[:, None] == (jnp.arange(gs * H, dtype=jnp.int32) % H)[None, :]
    mask_c = jnp.tile(cols_g, 2)[:, None] == rows_g[None, :]
    wc = jnp.where(mask_c[None], jnp.einsum('kah,hc->kac', c_t.astype(BF16), tile_h.astype(BF16),
                                            preferred_element_type=F32), 0.0).astype(BF16)
    pows = jnp.concatenate([pw_re.reshape(R, SSM_SLABS, gs * P), pw_im.reshape(R, SSM_SLABS, gs * P)], axis=-1)
    return wb, wc, jnp.transpose(pows, (1, 0, 2))


def _glu_kernel(a_ref, w_ref, b_ref, o_ref, wbf_ref):
    @pl.when(pl.program_id(0) == 0)
    def _():
        wbf_ref[...] = w_ref[...].astype(BF16)

    a = a_ref[...].astype(BF16)
    for c in range(SSM_W // MXU_N):
        cols = slice(c * MXU_N, (c + 1) * MXU_N)
        acc = jnp.dot(a, wbf_ref[:, cols], preferred_element_type=F32)
        o_ref[:, cols] = (a_ref[:, cols] * jax.nn.sigmoid(acc + b_ref[:, cols])).astype(BF16)


def glu(yg, w_glu, b_glu):
    tm = 512
    return pl.pallas_call(
        _glu_kernel,
        out_shape=jax.ShapeDtypeStruct((T, SSM_W), BF16),
        grid=(T // tm,),
        in_specs=[pl.BlockSpec((tm, SSM_W), lambda i: (i, 0)),
                  pl.BlockSpec((SSM_W, SSM_W), lambda i: (0, 0)),
                  pl.BlockSpec((1, SSM_W), lambda i: (0, 0))],
        out_specs=pl.BlockSpec((tm, SSM_W), lambda i: (i, 0)),
        scratch_shapes=[pltpu.VMEM((SSM_W, SSM_W), BF16)],
        compiler_params=_cparams(("arbitrary",)),
        name="ssm_glu",
    )(yg, w_glu, b_glu)


def _merge_kernel(h_ref, at_ref, ss_ref, win_hbm, ba_ref, bs_ref, wa_hbm, ws_hbm, o_ref,
                  wga_st, wgs_st, wa_st, ws_st, wga_bf, wgs_bf, wa_bf, ws_bf, sem):
    gate_col = 3 * QKV_COLS + SSM_W
    _stationary_weight_tile(win_hbm, gate_col, wga_st, wga_bf, sem.at[0])
    _stationary_weight_tile(win_hbm, gate_col + D, wgs_st, wgs_bf, sem.at[1])
    _stationary_weight_tile(wa_hbm, 0, wa_st, wa_bf, sem.at[2])
    _stationary_weight_tile(ws_hbm, 0, ws_st, ws_bf, sem.at[3])
    h = h_ref[...]
    at = at_ref[...]
    ss = ss_ref[...]
    for c in range(o_ref.shape[1] // MXU_N):
        cols = slice(c * MXU_N, (c + 1) * MXU_N)
        ga = jax.nn.sigmoid(jnp.dot(h, wga_bf[:, cols], preferred_element_type=F32) + ba_ref[:, cols])
        a = jnp.dot(at, wa_bf[:, cols], preferred_element_type=F32)
        gs = jax.nn.sigmoid(jnp.dot(h, wgs_bf[:, cols], preferred_element_type=F32) + bs_ref[:, cols])
        s = jnp.dot(ss, ws_bf[:, cols], preferred_element_type=F32)
        o_ref[:, cols] = (ga * a + gs * s).astype(BF16)


def merge(h, attn, ssm, w_in, b_gate, w_up_attn, w_up_ssm):
    tm, tn = 1024, 512
    nj = D // tn
    any_space = pl.BlockSpec(memory_space=pl.ANY)
    return pl.pallas_call(
        _merge_kernel,
        out_shape=jax.ShapeDtypeStruct((T, D), BF16),
        grid=(nj, T // tm),
        in_specs=[pl.BlockSpec((tm, D), lambda j, i: (i, 0)),
                  pl.BlockSpec((tm, ATTN_OUT), lambda j, i: (i, 0)),
                  pl.BlockSpec((tm, SSM_W), lambda j, i: (i, 0)),
                  any_space,
                  pl.BlockSpec((1, tn), lambda j, i: (0, j)),
                  pl.BlockSpec((1, tn), lambda j, i: (0, j + nj)),
                  any_space,
                  any_space],
        out_specs=pl.BlockSpec((tm, tn), lambda j, i: (i, j)),
        scratch_shapes=[pltpu.VMEM((D, tn), F32), pltpu.VMEM((D, tn), F32),
                        pltpu.VMEM((ATTN_OUT, tn), F32), pltpu.VMEM((SSM_W, tn), F32),
                        pltpu.VMEM((D, tn), BF16), pltpu.VMEM((D, tn), BF16),
                        pltpu.VMEM((ATTN_OUT, tn), BF16), pltpu.VMEM((SSM_W, tn), BF16),
                        pltpu.SemaphoreType.DMA((4,))],
        compiler_params=_cparams(("arbitrary", "arbitrary")),
        name="gates_branch_merge",
    )(h, attn, ssm, w_in, b_gate, b_gate, w_up_attn, w_up_ssm)


OUTPROJ_TB = 64
OUTPROJ_PITCH = OUTPROJ_TB + 8


def _outproj_kernel(m_ref, w_hbm, x_ref, o_ref, slab, stage_ref, wbf_ref, sem):
    _stationary_weight_tile(w_hbm, 0, stage_ref, wbf_ref, sem.at[0])
    tb = OUTPROJ_TB
    a = m_ref[...].reshape(R * tb, D)
    per = MXU_N // LANES
    for c in range(o_ref.shape[1] // MXU_N):
        acc = jnp.dot(a, wbf_ref[:, c * MXU_N:(c + 1) * MXU_N], preferred_element_type=F32)
        for s_ in range(per):
            lanes = slice(s_ * LANES, (s_ + 1) * LANES)
            for r in range(R):
                slab[c * per + s_, r * OUTPROJ_PITCH:r * OUTPROJ_PITCH + tb, :] = acc[r * tb:(r + 1) * tb, lanes]
        for s_ in range(per):
            lanes = slice((c * per + s_) * LANES, (c * per + s_ + 1) * LANES)
            for i in range(tb):
                rows = slice(i * R, (i + 1) * R)
                o_ref[rows, lanes] = slab[c * per + s_, pl.ds(i, R, stride=OUTPROJ_PITCH), :] + x_ref[rows, lanes]


def out_proj(merged3, w_out, x2):
    tb, tn = OUTPROJ_TB, 1024
    return pl.pallas_call(
        _outproj_kernel,
        out_shape=jax.ShapeDtypeStruct((T, D), F32),
        grid=(D // tn, NI // tb),
        in_specs=[pl.BlockSpec((R, tb, D), lambda j, i: (0, i, 0)),
                  pl.BlockSpec(memory_space=pl.ANY),
                  pl.BlockSpec((tb * R, tn), lambda j, i: (i, j))],
        out_specs=pl.BlockSpec((tb * R, tn), lambda j, i: (i, j)),
        scratch_shapes=[pltpu.VMEM((tn // LANES, OUTPROJ_PITCH * R, LANES), F32),
                        pltpu.VMEM((D, tn), F32), pltpu.VMEM((D, tn), BF16),
                        pltpu.SemaphoreType.DMA((1,))],
        compiler_params=_cparams(("arbitrary", "arbitrary")),
        name="out_proj_residual",
    )(merged3, w_out, x2)


PACK_ROWS = 8
HALF_D = D // 2


def _pack_bf16_pairs(zf):
    top = lax.bitcast_convert_type(zf, jnp.uint32)
    return top[:, HALF_D:] | (top[:, :HALF_D] >> 16)


def _unpack_bf16_pairs(x_ref, first, n):
    lo, hi = [], []
    for c in range(PACK_ROWS):
        w = x_ref[pl.ds(first * PACK_ROWS + c, n, stride=PACK_ROWS), :]
        lo.append(lax.bitcast_convert_type(w << 16, F32))
        hi.append(lax.bitcast_convert_type(w & jnp.uint32(0xFFFF0000), F32))
    return jnp.concatenate(lo + hi, axis=1)


def _router_kernel(x_ref, g_ref, w_ref, b_ref, h_ref, id_ref, wt_ref, cnt_ref, carry_ref):
    step = pl.program_id(0)

    @pl.when(step == 0)
    def _():
        carry_ref[...] = jnp.zeros_like(carry_ref)

    x = x_ref[...]
    ms = jnp.mean(x * x, axis=-1, keepdims=True)
    z = x * lax.rsqrt(ms + NORM_EPS) * g_ref[...]
    zh = z.astype(BF16)
    zf = zh.astype(F32)
    word = _pack_bf16_pairs(zf)
    for c in range(PACK_ROWS):
        h_ref[pl.ds(c, x.shape[0], stride=PACK_ROWS), :] = word[:, c * LANES:(c + 1) * LANES]
    zl = (z - zf).astype(BF16)
    w = w_ref[...]
    wh = w.astype(BF16)
    wl = (w - wh.astype(F32)).astype(BF16)
    logits = (jnp.dot(zh, wh, preferred_element_type=F32) + jnp.dot(zl, wh, preferred_element_type=F32)
              + jnp.dot(zh, wl, preferred_element_type=F32)) + b_ref[...]
    lane = lax.broadcasted_iota(jnp.int32, logits.shape, 1)
    lanef = lane.astype(F32)
    neg = jnp.float32(-jnp.inf)
    big = jnp.float32(1e9)
    gl = jnp.where(lane < N_EGROUPS, logits, neg)
    gmax = jnp.max(gl, axis=-1, keepdims=True)
    gidx = jnp.min(jnp.where(gl == gmax, lanef, big), axis=-1, keepdims=True)
    pg = 1.0 / jnp.sum(jnp.exp(gl - gmax), axis=-1, keepdims=True)
    lo = N_EGROUPS + EXPERTS_PER_GROUP * gidx
    el = jnp.where((lanef >= lo) & (lanef < lo + EXPERTS_PER_GROUP), logits, neg)
    t1 = jnp.max(el, axis=-1, keepdims=True)
    j1 = jnp.min(jnp.where(el == t1, lanef, big), axis=-1, keepdims=True)
    el2 = jnp.where(lanef == j1, neg, el)
    t2 = jnp.max(el2, axis=-1, keepdims=True)
    j2 = jnp.min(jnp.where(el2 == t2, lanef, big), axis=-1, keepdims=True)
    e21 = jnp.exp(t2 - t1)
    w1 = pg / (1.0 + e21)
    w2 = pg * e21 / (1.0 + e21)
    e1f = j1 - N_EGROUPS
    e2f = j2 - N_EGROUPS

    tm = x.shape[0]
    oh1 = (lanef == e1f).astype(F32)
    oh2 = (lanef == e2f).astype(F32)
    ri = lax.broadcasted_iota(jnp.int32, (tm, tm), 0)
    ci = lax.broadcasted_iota(jnp.int32, (tm, tm), 1)
    before = (ci < ri).astype(BF16)
    p1 = jnp.dot(before, oh1.astype(BF16), preferred_element_type=F32)
    p2 = jnp.dot(before, oh2.astype(BF16), preferred_element_type=F32)
    carry = carry_ref[...]
    c1 = jnp.sum(oh1, axis=0, keepdims=True)
    c2 = jnp.sum(oh2, axis=0, keepdims=True)
    rank1 = jnp.sum(oh1 * (carry + p1), axis=-1, keepdims=True)
    rank2 = jnp.sum(oh2 * (carry + c1 + p2), axis=-1, keepdims=True)
    carry = carry + c1 + c2
    carry_ref[...] = carry
    cnt_ref[...] = jnp.broadcast_to(carry, cnt_ref.shape).astype(jnp.int32)

    ids = jnp.where(lane == 0, e1f, jnp.where(lane == 1, e2f, jnp.where(lane == 2, rank1, jnp.where(lane == 3, rank2, 0.0))))
    id_ref[...] = ids.astype(jnp.int32)
    wt_ref[...] = jnp.where(lane == 0, w1, jnp.where(lane == 1, w2, 0.0))


def router(x1, gain, w_r, b_r):
    tm = 256
    return pl.pallas_call(
        _router_kernel,
        out_shape=(jax.ShapeDtypeStruct((T * PACK_ROWS, LANES), jnp.uint32),
                   jax.ShapeDtypeStruct((T, LANES), jnp.int32),
                   jax.ShapeDtypeStruct((T, LANES), F32),
                   jax.ShapeDtypeStruct((8, LANES), jnp.int32)),
        grid=(T // tm,),
        in_specs=[pl.BlockSpec((tm, D), lambda i: (i, 0)),
                  pl.BlockSpec((1, D), lambda i: (0, 0)),
                  pl.BlockSpec((D, LANES), lambda i: (0, 0)),
                  pl.BlockSpec((1, LANES), lambda i: (0, 0))],
        out_specs=(pl.BlockSpec((tm * PACK_ROWS, LANES), lambda i: (i, 0)),
                   pl.BlockSpec((tm, LANES), lambda i: (i, 0)),
                   pl.BlockSpec((tm, LANES), lambda i: (i, 0)),
                   pl.BlockSpec((8, LANES), lambda i: (0, 0))),
        scratch_shapes=[pltpu.VMEM((1, LANES), F32)],
        compiler_params=_cparams(("arbitrary",)),
        name="ffn_norm_router",
    )(x1, gain, w_r, b_r)


DISPATCH_TB = 512
N_ZERO_FILLS = 2 * N_EXPERTS


def _dispatch_kernel(dest_ref, zs_ref, h_ref, xs_hbm, zbuf, zsem, sem):
    step = pl.program_id(0)

    @pl.when(step == 0)
    def _():
        zbuf[...] = jnp.zeros_like(zbuf)

        def zero_copy(e):
            start = pl.multiple_of(jnp.maximum(zs_ref[e], 0) * PACK_ROWS, PACK_ROWS)
            return pltpu.make_async_copy(zbuf, xs_hbm.at[pl.ds(start, MOE_BLOCK * PACK_ROWS)], zsem.at[0])

        def zstart(e, c):
            @pl.when(zs_ref[e] >= 0)
            def _():
                zero_copy(e).start()
            return c

        def zwait(e, c):
            @pl.when(zs_ref[e] >= 0)
            def _():
                zero_copy(e).wait()
            return c

        lax.fori_loop(0, N_ZERO_FILLS, zstart, 0)
        lax.fori_loop(0, N_ZERO_FILLS, zwait, 0)

    def row_copy(n, k):
        a = (step * DISPATCH_TB + n) * TOP_K + k
        src = h_ref.at[pl.ds(pl.multiple_of(n * PACK_ROWS, PACK_ROWS), PACK_ROWS)]
        dst = xs_hbm.at[pl.ds(pl.multiple_of(dest_ref[a] * PACK_ROWS, PACK_ROWS), PACK_ROWS)]
        return pltpu.make_async_copy(src, dst, sem.at[0])

    def issue(n, c):
        for k in range(TOP_K):
            row_copy(n, k).start(priority=k)
        return c

    def drain(n, c):
        for k in range(TOP_K):
            row_copy(n, k).wait()
        return c

    lax.fori_loop(0, DISPATCH_TB, issue, 0, unroll=8)
    lax.fori_loop(0, DISPATCH_TB, drain, 0, unroll=8)


def dispatch(dest, zero_start, hpk):
    grid_spec = pltpu.PrefetchScalarGridSpec(
        num_scalar_prefetch=2,
        grid=(T // DISPATCH_TB,),
        in_specs=[pl.BlockSpec((DISPATCH_TB * PACK_ROWS, LANES), lambda i, dst, zs: (i, 0))],
        out_specs=pl.BlockSpec(memory_space=pl.ANY),
        scratch_shapes=[pltpu.VMEM((MOE_BLOCK * PACK_ROWS, LANES), jnp.uint32),
                        pltpu.SemaphoreType.DMA((1,)),
                        pltpu.SemaphoreType.DMA((1,))],
    )
    return pl.pallas_call(
        _dispatch_kernel,
        out_shape=jax.ShapeDtypeStruct((MOE_ROWS * PACK_ROWS, LANES), jnp.uint32),
        grid_spec=grid_spec,
        compiler_params=_cparams(("arbitrary",)),
        name="moe_dispatch",
    )(dest, zero_start, hpk)


def _expert_kernel(be_ref, nu_ref, ord_ref, seq_ref, x_ref, wg_hbm, wu_hbm, wd_hbm, y_ref,
                   wg_st, wu_st, wd_st, wg_bf, wu_bf, wd_bf, sem):
    b = pl.program_id(0)

    def weight_copies(e, slot):
        return (pltpu.make_async_copy(wg_hbm.at[e], wg_st.at[slot], sem.at[slot, 0]),
                pltpu.make_async_copy(wu_hbm.at[e], wu_st.at[slot], sem.at[slot, 1]),
                pltpu.make_async_copy(wd_hbm.at[e], wd_st.at[slot], sem.at[slot, 2]))

    def start_fetch(n, slot):
        @pl.when(seq_ref[n] >= 0)
        def _():
            for cp in weight_copies(seq_ref[n], slot):
                cp.start()

    @pl.when(b < nu_ref[0])
    def _():
        n = ord_ref[b]
        slot = n % 2
        changed = jnp.logical_or(b == 0, be_ref[b] != be_ref[jnp.maximum(b - 1, 0)])

        @pl.when(b == 0)
        def _():
            start_fetch(0, 0)
            start_fetch(1, 1)

        @pl.when(changed)
        def _():
            cg, cu, cd = weight_copies(be_ref[b], slot)
            cg.wait()
            wg_bf[...] = wg_st[slot].astype(BF16)
            cu.wait()
            wu_bf[...] = wu_st[slot].astype(BF16)
            cd.wait()
            wd_bf[...] = wd_st[slot].astype(BF16)
            start_fetch(n + 2, slot)

        x = _unpack_bf16_pairs(x_ref, 0, MOE_BLOCK).astype(BF16)
        gate = jnp.dot(x, wg_bf[...], preferred_element_type=F32)
        up = jnp.dot(x, wu_bf[...], preferred_element_type=F32)
        hid = (jax.nn.silu(gate) * up).astype(BF16)
        y = jnp.dot(hid, wd_bf[...], preferred_element_type=F32)
        word = _pack_bf16_pairs(y.astype(BF16).astype(F32))
        for c in range(PACK_ROWS):
            y_ref[pl.ds(c, MOE_BLOCK, stride=PACK_ROWS), :] = word[:, c * LANES:(c + 1) * LANES]

    @pl.when(b >= nu_ref[0])
    def _():
        y_ref[...] = jnp.zeros_like(y_ref)


def experts(block_expert, n_used, block_ord, expert_seq, xs, w_gate, w_up, w_down):
    def blk(b, be, nu, od, sq):
        return jnp.minimum(b, nu[0] - 1)

    grid_spec = pltpu.PrefetchScalarGridSpec(
        num_scalar_prefetch=4,
        grid=(MOE_BLOCKS,),
        in_specs=[pl.BlockSpec((MOE_BLOCK * PACK_ROWS, LANES), lambda b, be, nu, od, sq: (blk(b, be, nu, od, sq), 0)),
                  pl.BlockSpec(memory_space=pl.ANY),
                  pl.BlockSpec(memory_space=pl.ANY),
                  pl.BlockSpec(memory_space=pl.ANY)],
        out_specs=pl.BlockSpec((MOE_BLOCK * PACK_ROWS, LANES), lambda b, be, nu, od, sq: (b, 0)),
        scratch_shapes=[pltpu.VMEM((2, D, EXPERT_FF), F32),
                        pltpu.VMEM((2, D, EXPERT_FF), F32),
                        pltpu.VMEM((2, EXPERT_FF, D), F32),
                        pltpu.VMEM((D, EXPERT_FF), BF16),
                        pltpu.VMEM((D, EXPERT_FF), BF16),
                        pltpu.VMEM((EXPERT_FF, D), BF16),
                        pltpu.SemaphoreType.DMA((2, 3))],
    )
    return pl.pallas_call(
        _expert_kernel,
        out_shape=jax.ShapeDtypeStruct((MOE_ROWS * PACK_ROWS, LANES), jnp.uint32),
        grid_spec=grid_spec,
        compiler_params=_cparams(("arbitrary",)),
        name="moe_experts",
    )(block_expert, n_used, block_ord, expert_seq, xs, w_gate, w_up, w_down)


COMBINE_ROWS = 512
COMBINE_SUB = 16


def _combine_kernel(dest_ref, ys_hbm, x_ref, wt_ref, g_ref, o_ref, ybuf, sem):
    s = pl.program_id(0)
    ns = pl.num_programs(0)
    slot = s % 2
    tb = COMBINE_SUB
    rows = COMBINE_ROWS
    nxt = jnp.minimum(s + 1, ns - 1)

    def row_copy(step, n, k, sl):
        tok = step * rows + n
        src = ys_hbm.at[pl.ds(pl.multiple_of(dest_ref[tok * TOP_K + k] * PACK_ROWS, PACK_ROWS), PACK_ROWS)]
        dst = ybuf.at[sl * TOP_K + k, pl.ds(pl.multiple_of(n * PACK_ROWS, PACK_ROWS), PACK_ROWS)]
        return pltpu.make_async_copy(src, dst, sem.at[sl])

    def wait_all(step, sl):
        def body(n, c):
            for k in range(TOP_K):
                row_copy(step, n, k, sl).wait()
            return c
        lax.fori_loop(0, rows, body, 0, unroll=8)

    @pl.when(s == 0)
    def _():
        def body(n, c):
            for k in range(TOP_K):
                row_copy(0, n, k, 0).start(priority=k)
            return c
        lax.fori_loop(0, rows, body, 0, unroll=8)

    wait_all(s, slot)

    g = g_ref[...]
    y0_ref = ybuf.at[slot * TOP_K]
    y1_ref = ybuf.at[slot * TOP_K + 1]
    for b in range(rows // tb):
        for n in range(b * tb, (b + 1) * tb):
            for k in range(TOP_K):
                row_copy(nxt, n, k, 1 - slot).start(priority=k)
        sub = slice(b * tb, (b + 1) * tb)
        w = wt_ref[sub, :]
        y0 = _unpack_bf16_pairs(y0_ref, b * tb, tb)
        y1 = _unpack_bf16_pairs(y1_ref, b * tb, tb)
        z = x_ref[sub, :] + (w[:, 0:1] * y0 + w[:, 1:2] * y1)
        ms = jnp.mean(z * z, axis=-1, keepdims=True)
        o_ref[sub, :] = z * lax.rsqrt(ms + NORM_EPS) * g

    @pl.when(s == ns - 1)
    def _():
        wait_all(nxt, 1 - slot)


def combine(dest, ys, x1, wts, gain):
    rows = COMBINE_ROWS
    grid_spec = pltpu.PrefetchScalarGridSpec(
        num_scalar_prefetch=1,
        grid=(T // rows,),
        in_specs=[pl.BlockSpec(memory_space=pl.ANY),
                  pl.BlockSpec((rows, D), lambda s, dst: (s, 0)),
                  pl.BlockSpec((rows, LANES), lambda s, dst: (s, 0)),
                  pl.BlockSpec((1, D), lambda s, dst: (0, 0))],
        out_specs=pl.BlockSpec((rows, D), lambda s, dst: (s, 0)),
        scratch_shapes=[pltpu.VMEM((2 * TOP_K, rows * PACK_ROWS, LANES), jnp.uint32),
                        pltpu.SemaphoreType.DMA((2,))],
    )
    return pl.pallas_call(
        _combine_kernel,
        out_shape=jax.ShapeDtypeStruct((T, D), F32),
        grid_spec=grid_spec,
        compiler_params=_cparams(("arbitrary",)),
        name="moe_combine_final_norm",
    )(dest, ys, x1, wts, gain)


def dispatch_plan(ids, counts):
    experts_ = jnp.arange(N_EXPERTS, dtype=jnp.int32)
    padded = (counts + MOE_BLOCK - 1) // MOE_BLOCK * MOE_BLOCK
    pad_end = jnp.cumsum(padded)
    pad_start = pad_end - padded
    e = ids[:, :TOP_K]
    start_of = jnp.sum(jnp.where(e[:, :, None] == experts_[None, None, :], pad_start[None, None, :], 0), axis=-1)
    dest = (start_of + ids[:, TOP_K:2 * TOP_K]).reshape(N_ASSIGN).astype(jnp.int32)
    n_used = pad_end[-1] // MOE_BLOCK
    block_start = jnp.minimum(jnp.arange(MOE_BLOCKS, dtype=jnp.int32), n_used - 1) * MOE_BLOCK
    block_expert = jnp.sum((block_start[:, None] >= pad_end[None, :]).astype(jnp.int32), axis=1)
    block_expert = jnp.minimum(block_expert, N_EXPERTS - 1).astype(jnp.int32)
    tail = n_used + experts_
    zero_start = jnp.concatenate([jnp.where(counts > 0, pad_end - MOE_BLOCK, -1),
                                  jnp.where(tail < MOE_BLOCKS, tail * MOE_BLOCK, -1)]).astype(jnp.int32)
    present = counts > 0
    expert_ord = jnp.cumsum(present.astype(jnp.int32)) - 1
    slots = jnp.arange(N_EXPERTS + 2, dtype=jnp.int32)
    hit = present[None, :] & (expert_ord[None, :] == slots[:, None])
    expert_seq = jnp.where(jnp.any(hit, axis=1), jnp.sum(jnp.where(hit, experts_[None, :], 0), axis=1), -1)
    block_ord = jnp.sum(jnp.where(block_expert[:, None] == experts_[None, :], expert_ord[None, :], 0), axis=1)
    return (block_expert, n_used.astype(jnp.int32).reshape(1), dest, zero_start,
            block_ord.astype(jnp.int32), expert_seq.astype(jnp.int32))


def kernel(x, norm_mix, w_in, b_gate, ssm_a_re, ssm_a_im, ssm_log_step, ssm_b_re, ssm_b_im, ssm_c_re, ssm_c_im, ssm_d, w_glu, b_glu, w_up_attn, w_up_ssm, w_out, norm_ffn, w_router_group, b_router_group, w_router_expert, b_router_expert, w_expert_gate, w_expert_up, w_expert_down, norm_final):
    x2 = x.reshape(T, D)
    h = norm_permute(x2, norm_mix.reshape(1, D)).reshape(T, D)
    w_in_l = w_in.reshape(D, IN_COLS)
    qkv = proj(h, w_in_l, 0, 3 * QKV_COLS, tn=QKV_COLS, name="proj_qkv")
    u = proj(h, w_in_l, 3 * QKV_COLS, SSM_W, tn=SSM_W, name="proj_ssm_in")

    attn = attention(qkv.reshape(R, NI, 3 * QKV_COLS)).reshape(T, ATTN_OUT)

    G = SSM_W // SSM_CH
    wb, wc, pows = ssm_params(
        ssm_a_re.reshape(G, SSM_STATE).astype(F32), ssm_a_im.reshape(G, SSM_STATE).astype(F32),
        ssm_log_step.reshape(G),
        ssm_b_re.reshape(G, SSM_STATE, SSM_CH).astype(F32), ssm_b_im.reshape(G, SSM_STATE, SSM_CH).astype(F32),
        ssm_c_re.reshape(G, SSM_CH, SSM_STATE), ssm_c_im.reshape(G, SSM_CH, SSM_STATE))
    yg = ssm_scan(u.reshape(R, NI, SSM_W), wb, wc, pows, ssm_d.reshape(1, SSM_W).astype(F32))
    ssm = glu(yg.reshape(T, SSM_W), w_glu.reshape(SSM_W, SSM_W), b_glu.reshape(1, SSM_W))

    merged = merge(h, attn, ssm, w_in_l, b_gate.reshape(1, 2 * D),
                   w_up_attn.reshape(ATTN_OUT, D), w_up_ssm.reshape(SSM_W, D))
    x1 = out_proj(merged.reshape(R, NI, D), w_out.reshape(D, D), x2)

    w_r = jnp.concatenate([w_router_group.reshape(D, N_EGROUPS), w_router_expert.reshape(D, N_EXPERTS),
                           jnp.zeros((D, LANES - N_EGROUPS - N_EXPERTS), F32)], axis=1)
    b_r = jnp.concatenate([b_router_group.reshape(1, N_EGROUPS), b_router_expert.reshape(1, N_EXPERTS),
                           jnp.zeros((1, LANES - N_EGROUPS - N_EXPERTS), F32)], axis=1)
    hpk, ids, wts, counts = router(x1, norm_ffn.reshape(1, D), w_r, b_r)

    block_expert, n_used, dest, zero_start, block_ord, expert_seq = dispatch_plan(ids[:, :2 * TOP_K], counts[0, :N_EXPERTS])
    xs = dispatch(dest, zero_start, hpk)
    ys = experts(block_expert, n_used, block_ord, expert_seq, xs,
                 w_expert_gate.reshape(N_EXPERTS, D, EXPERT_FF), w_expert_up.reshape(N_EXPERTS, D, EXPERT_FF),
                 w_expert_down.reshape(N_EXPERTS, EXPERT_FF, D))
    out = combine(dest, ys, x1, wts, norm_final.reshape(1, D))
    return out.reshape(1, T, D)
```

```python
import functools

import jax
import jax.numpy as jnp
from jax import lax
from jax.experimental import pallas as pl
from jax.experimental.pallas import tpu as pltpu

F32 = jnp.float32
BF16 = jnp.bfloat16

T = 8192
D = 2048
R = 16
NI = T // R
HEAD_DIM = 64
N_HEAD_SLOTS = 8
DILATIONS = (1, 4, 16)
ATTN_BLOCK = 128
QKV_COLS = 1536
ATTN_OUT = 512
SSM_W = 1024
SSM_STATE = 64
SSM_CH = 16
IN_COLS = 3 * QKV_COLS + SSM_W + 2 * D
N_EXPERTS = 32
N_EGROUPS = 4
EXPERTS_PER_GROUP = 8
TOP_K = 2
EXPERT_FF = 512
NORM_EPS = 1e-6
LANES = 128
VMEM_LIMIT = 48 * 1024 * 1024

MOE_BLOCK = 256
N_ASSIGN = T * TOP_K
MOE_BLOCKS = N_ASSIGN // MOE_BLOCK + N_EXPERTS
MOE_ROWS = MOE_BLOCKS * MOE_BLOCK


def _cparams(sem):
    return pltpu.CompilerParams(dimension_semantics=sem, vmem_limit_bytes=VMEM_LIMIT)


N_SLABS = D // LANES
NORM_TB = 32
NORM_CHUNK = 64
NORM_PITCH = R + 8


def _norm_permute_kernel(x_ref, g_ref, h_ref, slab):
    g = g_ref[...]

    def chunk(t, c):
        rows = pl.ds(pl.multiple_of(t * NORM_CHUNK, NORM_CHUNK), NORM_CHUNK)
        x = x_ref[rows, :]
        ms = jnp.mean(x * x, axis=-1, keepdims=True)
        hn = x * lax.rsqrt(ms + NORM_EPS) * g
        for k in range(NORM_CHUNK // R):
            dst = pl.ds(pl.multiple_of((t * (NORM_CHUNK // R) + k) * NORM_PITCH, 8), R)
            for s_ in range(N_SLABS):
                slab[s_, dst, :] = hn[k * R:(k + 1) * R, s_ * LANES:(s_ + 1) * LANES]
        return c

    lax.fori_loop(0, NORM_TB * R // NORM_CHUNK, chunk, 0)
    for r in range(R):
        pieces = [slab[s_, pl.ds(r, NORM_TB, stride=NORM_PITCH), :] for s_ in range(N_SLABS)]
        h_ref[r] = jnp.concatenate(pieces, axis=1).astype(BF16)


def norm_permute(x2, gain):
    return pl.pallas_call(
        _norm_permute_kernel,
        out_shape=jax.ShapeDtypeStruct((R, NI, D), BF16),
        grid=(NI // NORM_TB,),
        in_specs=[pl.BlockSpec((NORM_TB * R, D), lambda i: (i, 0)),
                  pl.BlockSpec((1, D), lambda i: (0, 0))],
        out_specs=pl.BlockSpec((R, NORM_TB, D), lambda i: (0, i, 0)),
        scratch_shapes=[pltpu.VMEM((N_SLABS, NORM_TB * NORM_PITCH, LANES), F32)],
        compiler_params=_cparams(("arbitrary",)),
        name="norm_permute",
    )(x2, gain)


MXU_N = 256


def _stationary_weight_tile(w_hbm, col0, stage_ref, wbf_ref, sem):
    j = pl.program_id(0)
    tn = stage_ref.shape[1]

    def copy(jj):
        c0 = pl.multiple_of(col0 + jj * tn, LANES)
        return pltpu.make_async_copy(w_hbm.at[:, pl.ds(c0, tn)], stage_ref, sem)

    @pl.when(pl.program_id(1) == 0)
    def _():
        @pl.when(j == 0)
        def _():
            copy(0).start()

        copy(j).wait()
        wbf_ref[...] = stage_ref[...].astype(BF16)

        @pl.when(j + 1 < pl.num_programs(0))
        def _():
            copy(j + 1).start()


def _proj_kernel(a_ref, w_hbm, o_ref, stage_ref, wbf_ref, sem, *, col_off):
    _stationary_weight_tile(w_hbm, col_off, stage_ref, wbf_ref, sem.at[0])
    a = a_ref[...]
    for c in range(o_ref.shape[1] // MXU_N):
        cols = slice(c * MXU_N, (c + 1) * MXU_N)
        o_ref[:, cols] = jnp.dot(a, wbf_ref[:, cols], preferred_element_type=F32)


def proj(h, w_in, col_off, n_cols, tn, name="proj"):
    tm = 1024
    return pl.pallas_call(
        functools.partial(_proj_kernel, col_off=col_off),
        out_shape=jax.ShapeDtypeStruct((T, n_cols), F32),
        grid=(n_cols // tn, T // tm),
        in_specs=[pl.BlockSpec((tm, D), lambda j, i: (i, 0)),
                  pl.BlockSpec(memory_space=pl.ANY)],
        out_specs=pl.BlockSpec((tm, tn), lambda j, i: (i, j)),
        scratch_shapes=[pltpu.VMEM((D, tn), F32), pltpu.VMEM((D, tn), BF16), pltpu.SemaphoreType.DMA((1,))],
        compiler_params=_cparams(("arbitrary", "arbitrary")),
        name=name,
    )(h, w_in)


def _seq_index_maps(d):
    nseg = R // d
    qlen = ATTN_BLOCK // nseg
    return nseg, qlen


def _bias_matrices(d, hp):
    nseg, qlen = _seq_index_maps(d)
    klen = 2 * qlen
    row = lax.broadcasted_iota(jnp.int32, (2 * ATTN_BLOCK, 2 * ATTN_BLOCK), 0)
    col = lax.broadcasted_iota(jnp.int32, (2 * ATTN_BLOCK, 2 * ATTN_BLOCK), 1)
    rho = row % ATTN_BLOCK
    jq = (rho % qlen) * nseg + rho // qlen
    jk = ((col % klen) - qlen) * nseg + col // klen
    steps = jq - jk
    valid = (steps >= 0) & (steps <= ATTN_BLOCK)
    head = 2 * hp + row // ATTN_BLOCK
    slope = lax.bitcast_convert_type((127 - (head + 1)) << 23, F32)
    bias = -slope * (d * steps).astype(F32)
    neg = jnp.float32(-jnp.inf)
    return jnp.where(valid, bias, neg), jnp.where(valid & (jk >= 0), bias, neg)


def _attend_pair(q, k, v, bias):
    lane = lax.broadcasted_iota(jnp.int32, (ATTN_BLOCK, LANES), 1)
    first = lane < HEAD_DIM
    zero = jnp.zeros_like(q)
    q2 = jnp.concatenate([jnp.where(first, q, zero), jnp.where(first, zero, q)], axis=0).astype(BF16)
    s = lax.dot_general(q2, k.astype(BF16), (((1,), (1,)), ((), ())), preferred_element_type=F32)
    s = s + bias
    m = jnp.max(s, axis=-1, keepdims=True)
    p = jnp.exp(s - m)
    l = jnp.sum(p, axis=-1, keepdims=True)
    o2 = jnp.dot(p.astype(BF16), v.astype(BF16), preferred_element_type=F32)
    o = jnp.where(first, o2[:ATTN_BLOCK], o2[ATTN_BLOCK:])
    m_b = jnp.where(first, m[:ATTN_BLOCK], m[ATTN_BLOCK:])
    l_b = jnp.where(first, l[:ATTN_BLOCK], l[ATTN_BLOCK:])
    return o, m_b, l_b


def _attn_kernel(q_ref, kp_ref, kc_ref, vp_ref, vc_ref, o_ref, kbuf, vbuf, obuf, mbuf, lbuf, bias_ref):
    hp = pl.program_id(0)
    it = pl.program_id(1)
    g = pl.program_id(2)
    scale = HEAD_DIM ** -0.5

    for gi, d in enumerate(DILATIONS):
        nseg, qlen = _seq_index_maps(d)
        klen = 2 * qlen
        nblk = ATTN_BLOCK // qlen

        @pl.when(g == gi)
        def _(gi=gi, d=d, nseg=nseg, qlen=qlen, klen=klen, nblk=nblk):
            if nseg > 1:
                kbuf[:, :ATTN_BLOCK, :] = kp_ref[...]
                kbuf[:, ATTN_BLOCK:, :] = kc_ref[...]
                vbuf[:, :ATTN_BLOCK, :] = vp_ref[...]
                vbuf[:, ATTN_BLOCK:, :] = vc_ref[...]

            @pl.when(it == 0)
            def _():
                b_reg, b_first = _bias_matrices(d, hp)
                bias_ref[gi, 0] = b_reg
                bias_ref[gi, 1] = b_first

            def block(idx, carry):
                rd = idx // nblk
                bb = idx % nblk
                q0 = pl.multiple_of(bb * qlen, qlen)
                k0 = pl.multiple_of(ATTN_BLOCK + bb * qlen - qlen, qlen)
                qs, ks, vs = [], [], []
                for m_ in range(nseg):
                    rr = rd + d * m_
                    qs.append(q_ref[rr, pl.ds(q0, qlen), :])
                    if nseg > 1:
                        ks.append(kbuf[rr, pl.ds(k0, klen), :])
                        vs.append(vbuf[rr, pl.ds(k0, klen), :])
                    else:
                        ks += [kp_ref[rr], kc_ref[rr]]
                        vs += [vp_ref[rr], vc_ref[rr]]
                q = jnp.concatenate(qs, axis=0) * scale
                k = jnp.concatenate(ks, axis=0)
                v = jnp.concatenate(vs, axis=0)
                is_first = jnp.logical_and(it == 0, bb == 0)
                bias = bias_ref[gi, jnp.where(is_first, 1, 0)]
                o, mx, den = _attend_pair(q, k, v, bias)
                for m_ in range(nseg):
                    rr = rd + d * m_
                    seg = slice(m_ * qlen, (m_ + 1) * qlen)
                    obuf[gi, rr, pl.ds(q0, qlen), :] = o[seg]
                    mbuf[gi, rr, pl.ds(q0, qlen), :] = mx[seg]
                    lbuf[gi, rr, pl.ds(q0, qlen), :] = den[seg]
                return carry

            lax.fori_loop(0, d * nblk, block, 0, unroll=True)

    @pl.when(g == len(DILATIONS) - 1)
    def _():
        for r in range(R):
            m0, m1, m2 = mbuf[0, r], mbuf[1, r], mbuf[2, r]
            mx = jnp.maximum(jnp.maximum(m0, m1), m2)
            e0, e1, e2 = jnp.exp(m0 - mx), jnp.exp(m1 - mx), jnp.exp(m2 - mx)
            den = e0 * lbuf[0, r] + e1 * lbuf[1, r] + e2 * lbuf[2, r]
            num = e0 * obuf[0, r] + e1 * obuf[1, r] + e2 * obuf[2, r]
            o_ref[r] = (num / den).astype(BF16)


def attention(qkv3):
    n_hp = N_HEAD_SLOTS // 2
    n_it = NI // ATTN_BLOCK
    ng = len(DILATIONS)
    cb = QKV_COLS // LANES

    def cur(base):
        return pl.BlockSpec((R, ATTN_BLOCK, LANES), lambda hp, it, g: (0, it, base + g * n_hp + hp))

    def prev(base):
        return pl.BlockSpec((R, ATTN_BLOCK, LANES),
                            lambda hp, it, g: (0, jnp.maximum(it - 1, 0), base + g * n_hp + hp))

    return pl.pallas_call(
        _attn_kernel,
        out_shape=jax.ShapeDtypeStruct((R, NI, ATTN_OUT), BF16),
        grid=(n_hp, n_it, ng),
        in_specs=[cur(0), prev(cb), cur(cb), prev(2 * cb), cur(2 * cb)],
        out_specs=pl.BlockSpec((R, ATTN_BLOCK, LANES), lambda hp, it, g: (0, it, hp)),
        scratch_shapes=[pltpu.VMEM((R, 2 * ATTN_BLOCK, LANES), F32),
                        pltpu.VMEM((R, 2 * ATTN_BLOCK, LANES), F32),
                        pltpu.VMEM((ng, R, ATTN_BLOCK, LANES), F32),
                        pltpu.VMEM((ng, R, ATTN_BLOCK, LANES), F32),
                        pltpu.VMEM((ng, R, ATTN_BLOCK, LANES), F32),
                        pltpu.VMEM((ng, 2, 2 * ATTN_BLOCK, 2 * ATTN_BLOCK), F32)],
        compiler_params=_cparams(("arbitrary", "arbitrary", "arbitrary")),
        name="dilated_attention",
    )(qkv3, qkv3, qkv3, qkv3, qkv3)


SSM_SLAB = 256
SSM_SLABS = SSM_W // SSM_SLAB
SLAB_STATES = SSM_SLAB // SSM_CH * SSM_STATE
SSM_TI = 128
SSM_MM_CHUNK = 4


def _ssm_kernel(u_ref, wb_ref, wc_ref, pw_ref, dsk_ref, o_ref, s_ref, zs_ref, zc_ref, tr_ref, zt_ref):
    ic = pl.program_id(1)
    ns = SLAB_STATES

    @pl.when(ic == 0)
    def _():
        zc_ref[...] = jnp.zeros_like(zc_ref)

    n_chunks = R // SSM_MM_CHUNK
    n_tiles = SSM_TI // 8
    wb = wb_ref[0]
    wc = wc_ref[0]
    dsk = dsk_ref[...]
    arb = jnp.broadcast_to(pw_ref[0, 0:1, :ns], (8, ns))
    aib = jnp.broadcast_to(pw_ref[0, 0:1, ns:], (8, ns))


    def bu_chunk(c):
        lo = c * SSM_MM_CHUNK
        uc = u_ref[lo:lo + SSM_MM_CHUNK].reshape(SSM_MM_CHUNK * SSM_TI, SSM_SLAB)
        bu = jnp.dot(uc.astype(BF16), wb, preferred_element_type=F32)
        s_ref[lo:lo + SSM_MM_CHUNK] = bu.reshape(SSM_MM_CHUNK, SSM_TI, 2 * ns)

    def local_chunk(c):
        lo = c * SSM_MM_CHUNK
        first = max(lo, 1)
        for t in range(n_tiles):
            rows = slice(t * 8, (t + 1) * 8)
            pr = s_ref[first - 1, rows, :ns]
            pi = s_ref[first - 1, rows, ns:]
            for r in range(first, lo + SSM_MM_CHUNK):
                nr = s_ref[r, rows, :ns] + (arb * pr - aib * pi)
                ni = s_ref[r, rows, ns:] + (arb * pi + aib * pr)
                s_ref[r, rows, :ns] = nr
                s_ref[r, rows, ns:] = ni
                pr, pi = nr, ni

    bu_chunk(0)
    for c in range(n_chunks):
        if c + 1 < n_chunks:
            bu_chunk(c + 1)
        local_chunk(c)

    n_ch = ns // LANES
    for c in range(n_ch):
        tr_ref[0, c * 8:c * 8 + 1, :] = pw_ref[0, R - 1:R, c * LANES:(c + 1) * LANES]
        tr_ref[1, c * 8:c * 8 + 1, :] = pw_ref[0, R - 1:R, ns + c * LANES:ns + (c + 1) * LANES]
    a16r = tr_ref[0, pl.ds(0, n_ch, stride=8), :]
    a16i = tr_ref[1, pl.ds(0, n_ch, stride=8), :]

    def ztile(t, carry):
        zr, zi = carry
        rows = pl.ds(pl.multiple_of(t * 8, 8), 8)
        for c in range(n_ch):
            tr_ref[0, c * 8:(c + 1) * 8, :] = s_ref[R - 1, rows, c * LANES:(c + 1) * LANES]
            tr_ref[1, c * 8:(c + 1) * 8, :] = s_ref[R - 1, rows, ns + c * LANES:ns + (c + 1) * LANES]
        for i in range(8):
            zt_ref[0, pl.ds(i, n_ch, stride=8), :] = zr
            zt_ref[1, pl.ds(i, n_ch, stride=8), :] = zi
            er = tr_ref[0, pl.ds(i, n_ch, stride=8), :]
            ei = tr_ref[1, pl.ds(i, n_ch, stride=8), :]
            zr, zi = a16r * zr - a16i * zi + er, a16r * zi + a16i * zr + ei
        for c in range(n_ch):
            zs_ref[rows, c * LANES:(c + 1) * LANES] = zt_ref[0, c * 8:(c + 1) * 8, :]
            zs_ref[rows, ns + c * LANES:ns + (c + 1) * LANES] = zt_ref[1, c * 8:(c + 1) * 8, :]
        return zr, zi

    zr_end, zi_end = lax.fori_loop(0, SSM_TI // 8, ztile, (zc_ref[0], zc_ref[1]))
    zc_ref[0] = zr_end
    zc_ref[1] = zi_end

    def fix_chunk(c):
        lo = c * SSM_MM_CHUNK
        for r in range(lo, lo + SSM_MM_CHUNK):
            prb = jnp.broadcast_to(pw_ref[0, r:r + 1, :ns], (8, ns))
            pib = jnp.broadcast_to(pw_ref[0, r:r + 1, ns:], (8, ns))
            for t in range(n_tiles):
                rows = slice(t * 8, (t + 1) * 8)
                zr = zs_ref[rows, :ns]
                zi = zs_ref[rows, ns:]
                s_ref[r, rows, :ns] = s_ref[r, rows, :ns] + (prb * zr - pib * zi)
                s_ref[r, rows, ns:] = s_ref[r, rows, ns:] + (prb * zi + pib * zr)

    def out_chunk(c):
        lo = c * SSM_MM_CHUNK
        xs = s_ref[lo:lo + SSM_MM_CHUNK].reshape(SSM_MM_CHUNK * SSM_TI, 2 * ns)
        y = jnp.dot(xs.astype(BF16), wc, preferred_element_type=F32)
        y = y.reshape(SSM_MM_CHUNK, SSM_TI, SSM_SLAB) + dsk * u_ref[lo:lo + SSM_MM_CHUNK]
        o_ref[lo:lo + SSM_MM_CHUNK] = jax.nn.gelu(y)

    fix_chunk(0)
    for c in range(n_chunks):
        if c + 1 < n_chunks:
            fix_chunk(c + 1)
        out_chunk(c)


def ssm_scan(u3, wb, wc, pows, dskip):
    ns2 = 2 * SLAB_STATES
    return pl.pallas_call(
        _ssm_kernel,
        out_shape=jax.ShapeDtypeStruct((R, NI, SSM_W), F32),
        grid=(SSM_SLABS, NI // SSM_TI),
        in_specs=[pl.BlockSpec((R, SSM_TI, SSM_SLAB), lambda kb, ic: (0, ic, kb)),
                  pl.BlockSpec((1, SSM_SLAB, ns2), lambda kb, ic: (kb, 0, 0)),
                  pl.BlockSpec((1, ns2, SSM_SLAB), lambda kb, ic: (kb, 0, 0)),
                  pl.BlockSpec((1, R, ns2), lambda kb, ic: (kb, 0, 0)),
                  pl.BlockSpec((1, SSM_SLAB), lambda kb, ic: (0, kb))],
        out_specs=pl.BlockSpec((R, SSM_TI, SSM_SLAB), lambda kb, ic: (0, ic, kb)),
        scratch_shapes=[pltpu.VMEM((R, SSM_TI, ns2), F32),
                        pltpu.VMEM((SSM_TI, ns2), F32),
                        pltpu.VMEM((2, SLAB_STATES // LANES, LANES), F32),
                        pltpu.VMEM((2, 8 * SLAB_STATES // LANES, LANES), F32),
                        pltpu.VMEM((2, 8 * SLAB_STATES // LANES, LANES), F32)],
        compiler_params=_cparams(("arbitrary", "arbitrary")),
        name="s5_ssm",
    )(u3, wb, wc, pows, dskip)


def ssm_params(a_re, a_im, log_step, b_re, b_im, c_re, c_im):
    G, P, H = SSM_W // SSM_CH, SSM_STATE, SSM_CH
    gs = SSM_SLAB // SSM_CH
    step = jnp.exp(log_step.astype(F32))[:, None]
    kk = jnp.arange(1, R + 1, dtype=F32)[:, None, None]
    mag = jnp.exp(kk * (a_re * step))
    ang = kk * (a_im * step)
    pw_re, pw_im = mag * jnp.cos(ang), mag * jnp.sin(ang)
    abar_re, abar_im = pw_re[0], pw_im[0]
    nr, ni = abar_re - 1.0, abar_im
    den = a_re * a_re + a_im * a_im
    f_re = (nr * a_re + ni * a_im) / den
    f_im = (ni * a_re - nr * a_im) / den
    bbar_re = f_re[..., None] * b_re - f_im[..., None] * b_im
    bbar_im = f_re[..., None] * b_im + f_im[..., None] * b_re
    rows_g = jnp.arange(gs * H, dtype=jnp.int32) // H
    cols_g = jnp.arange(gs * P, dtype=jnp.int32) // P
    b_t = jnp.concatenate([jnp.swapaxes(bbar_re, 1, 2).reshape(SSM_SLABS, gs * H, P),
                           jnp.swapaxes(bbar_im, 1, 2).reshape(SSM_SLABS, gs * H, P)], axis=-1)
    tile_p = (jnp.arange(2 * P, dtype=jnp.int32)[:, None] ==
              (jnp.arange(2 * gs * P, dtype=jnp.int32) // (gs * P) * P + jnp.arange(2 * gs * P, dtype=jnp.int32) % P)[None, :])
    mask_b = rows_g[:, None] == jnp.tile(cols_g, 2)[None, :]
    wb = jnp.where(mask_b[None], jnp.einsum('kap,pc->kac', b_t.astype(BF16), tile_p.astype(BF16),
                                            preferred_element_type=F32), 0.0).astype(BF16)
    c_t = jnp.concatenate([jnp.swapaxes(c_re.astype(F32), 1, 2).reshape(SSM_SLABS, gs * P, H),
                           -jnp.swapaxes(c_im.astype(F32), 1, 2).reshape(SSM_SLABS, gs * P, H)], axis=1)
    tile_h = jnp.arange(H, dtype=jnp.int32)[:, None] == (jnp.arange(gs * H, dtype=jnp.int32) % H)[None, :]
    mask_c = jnp.tile(cols_g, 2)[:, None] == rows_g[None, :]
    wc = jnp.where(mask_c[None], jnp.einsum('kah,hc->kac', c_t.astype(BF16), tile_h.astype(BF16),
                                            preferred_element_type=F32), 0.0).astype(BF16)
    pows = jnp.concatenate([pw_re.reshape(R, SSM_SLABS, gs * P), pw_im.reshape(R, SSM_SLABS, gs * P)], axis=-1)
    return wb, wc, jnp.transpose(pows, (1, 0, 2))


def _glu_kernel(a_ref, w_ref, b_ref, o_ref, wbf_ref):
    @pl.when(pl.program_id(0) == 0)
    def _():
        wbf_ref[...] = w_ref[...].astype(BF16)

    a = a_ref[...].astype(BF16)
    for c in range(SSM_W // MXU_N):
        cols = slice(c * MXU_N, (c + 1) * MXU_N)
        acc = jnp.dot(a, wbf_ref[:, cols], preferred_element_type=F32)
        o_ref[:, cols] = (a_ref[:, cols] * jax.nn.sigmoid(acc + b_ref[:, cols])).astype(BF16)


def glu(yg, w_glu, b_glu):
    tm = 512
    return pl.pallas_call(
        _glu_kernel,
        out_shape=jax.ShapeDtypeStruct((T, SSM_W), BF16),
        grid=(T // tm,),
        in_specs=[pl.BlockSpec((tm, SSM_W), lambda i: (i, 0)),
                  pl.BlockSpec((SSM_W, SSM_W), lambda i: (0, 0)),
                  pl.BlockSpec((1, SSM_W), lambda i: (0, 0))],
        out_specs=pl.BlockSpec((tm, SSM_W), lambda i: (i, 0)),
        scratch_shapes=[pltpu.VMEM((SSM_W, SSM_W), BF16)],
        compiler_params=_cparams(("arbitrary",)),
        name="ssm_glu",
    )(yg, w_glu, b_glu)


def _merge_kernel(h_ref, at_ref, ss_ref, win_hbm, ba_ref, bs_ref, wa_hbm, ws_hbm, o_ref,
                  wga_st, wgs_st, wa_st, ws_st, wga_bf, wgs_bf, wa_bf, ws_bf, sem):
    gate_col = 3 * QKV_COLS + SSM_W
    _stationary_weight_tile(win_hbm, gate_col, wga_st, wga_bf, sem.at[0])
    _stationary_weight_tile(win_hbm, gate_col + D, wgs_st, wgs_bf, sem.at[1])
    _stationary_weight_tile(wa_hbm, 0, wa_st, wa_bf, sem.at[2])
    _stationary_weight_tile(ws_hbm, 0, ws_st, ws_bf, sem.at[3])
    h = h_ref[...]
    at = at_ref[...]
    ss = ss_ref[...]
    for c in range(o_ref.shape[1] // MXU_N):
        cols = slice(c * MXU_N, (c + 1) * MXU_N)
        ga = jax.nn.sigmoid(jnp.dot(h, wga_bf[:, cols], preferred_element_type=F32) + ba_ref[:, cols])
        a = jnp.dot(at, wa_bf[:, cols], preferred_element_type=F32)
        gs = jax.nn.sigmoid(jnp.dot(h, wgs_bf[:, cols], preferred_element_type=F32) + bs_ref[:, cols])
        s = jnp.dot(ss, ws_bf[:, cols], preferred_element_type=F32)
        o_ref[:, cols] = (ga * a + gs * s).astype(BF16)


def merge(h, attn, ssm, w_in, b_gate, w_up_attn, w_up_ssm):
    tm, tn = 1024, 512
    nj = D // tn
    any_space = pl.BlockSpec(memory_space=pl.ANY)
    return pl.pallas_call(
        _merge_kernel,
        out_shape=jax.ShapeDtypeStruct((T, D), BF16),
        grid=(nj, T // tm),
        in_specs=[pl.BlockSpec((tm, D), lambda j, i: (i, 0)),
                  pl.BlockSpec((tm, ATTN_OUT), lambda j, i: (i, 0)),
                  pl.BlockSpec((tm, SSM_W), lambda j, i: (i, 0)),
                  any_space,
                  pl.BlockSpec((1, tn), lambda j, i: (0, j)),
                  pl.BlockSpec((1, tn), lambda j, i: (0, j + nj)),
                  any_space,
                  any_space],
        out_specs=pl.BlockSpec((tm, tn), lambda j, i: (i, j)),
        scratch_shapes=[pltpu.VMEM((D, tn), F32), pltpu.VMEM((D, tn), F32),
                        pltpu.VMEM((ATTN_OUT, tn), F32), pltpu.VMEM((SSM_W, tn), F32),
                        pltpu.VMEM((D, tn), BF16), pltpu.VMEM((D, tn), BF16),
                        pltpu.VMEM((ATTN_OUT, tn), BF16), pltpu.VMEM((SSM_W, tn), BF16),
                        pltpu.SemaphoreType.DMA((4,))],
        compiler_params=_cparams(("arbitrary", "arbitrary")),
        name="gates_branch_merge",
    )(h, attn, ssm, w_in, b_gate, b_gate, w_up_attn, w_up_ssm)


OUTPROJ_TB = 64
OUTPROJ_PITCH = OUTPROJ_TB + 8


def _outproj_kernel(m_ref, w_hbm, x_ref, o_ref, slab, stage_ref, wbf_ref, sem):
    _stationary_weight_tile(w_hbm, 0, stage_ref, wbf_ref, sem.at[0])
    tb = OUTPROJ_TB
    a = m_ref[...].reshape(R * tb, D)
    per = MXU_N // LANES
    for c in range(o_ref.shape[1] // MXU_N):
        acc = jnp.dot(a, wbf_ref[:, c * MXU_N:(c + 1) * MXU_N], preferred_element_type=F32)
        for s_ in range(per):
            lanes = slice(s_ * LANES, (s_ + 1) * LANES)
            for r in range(R):
                slab[c * per + s_, r * OUTPROJ_PITCH:r * OUTPROJ_PITCH + tb, :] = acc[r * tb:(r + 1) * tb, lanes]
        for s_ in range(per):
            lanes = slice((c * per + s_) * LANES, (c * per + s_ + 1) * LANES)
            for i in range(tb):
                rows = slice(i * R, (i + 1) * R)
                o_ref[rows, lanes] = slab[c * per + s_, pl.ds(i, R, stride=OUTPROJ_PITCH), :] + x_ref[rows, lanes]


def out_proj(merged3, w_out, x2):
    tb, tn = OUTPROJ_TB, 1024
    return pl.pallas_call(
        _outproj_kernel,
        out_shape=jax.ShapeDtypeStruct((T, D), F32),
        grid=(D // tn, NI // tb),
        in_specs=[pl.BlockSpec((R, tb, D), lambda j, i: (0, i, 0)),
                  pl.BlockSpec(memory_space=pl.ANY),
                  pl.BlockSpec((tb * R, tn), lambda j, i: (i, j))],
        out_specs=pl.BlockSpec((tb * R, tn), lambda j, i: (i, j)),
        scratch_shapes=[pltpu.VMEM((tn // LANES, OUTPROJ_PITCH * R, LANES), F32),
                        pltpu.VMEM((D, tn), F32), pltpu.VMEM((D, tn), BF16),
                        pltpu.SemaphoreType.DMA((1,))],
        compiler_params=_cparams(("arbitrary", "arbitrary")),
        name="out_proj_residual",
    )(merged3, w_out, x2)


PACK_ROWS = 8
HALF_D = D // 2


def _pack_bf16_pairs(zf):
    top = lax.bitcast_convert_type(zf, jnp.uint32)
    return top[:, HALF_D:] | (top[:, :HALF_D] >> 16)


def _unpack_bf16_pairs(x_ref, first, n):
    lo, hi = [], []
    for c in range(PACK_ROWS):
        w = x_ref[pl.ds(first * PACK_ROWS + c, n, stride=PACK_ROWS), :]
        lo.append(lax.bitcast_convert_type(w << 16, F32))
        hi.append(lax.bitcast_convert_type(w & jnp.uint32(0xFFFF0000), F32))
    return jnp.concatenate(lo + hi, axis=1)


def _router_kernel(x_ref, g_ref, w_ref, b_ref, h_ref, id_ref, wt_ref, cnt_ref, carry_ref):
    step = pl.program_id(0)

    @pl.when(step == 0)
    def _():
        carry_ref[...] = jnp.zeros_like(carry_ref)

    x = x_ref[...]
    ms = jnp.mean(x * x, axis=-1, keepdims=True)
    z = x * lax.rsqrt(ms + NORM_EPS) * g_ref[...]
    zh = z.astype(BF16)
    zf = zh.astype(F32)
    word = _pack_bf16_pairs(zf)
    for c in range(PACK_ROWS):
        h_ref[pl.ds(c, x.shape[0], stride=PACK_ROWS), :] = word[:, c * LANES:(c + 1) * LANES]
    zl = (z - zf).astype(BF16)
    w = w_ref[...]
    wh = w.astype(BF16)
    wl = (w - wh.astype(F32)).astype(BF16)
    logits = (jnp.dot(zh, wh, preferred_element_type=F32) + jnp.dot(zl, wh, preferred_element_type=F32)
              + jnp.dot(zh, wl, preferred_element_type=F32)) + b_ref[...]
    lane = lax.broadcasted_iota(jnp.int32, logits.shape, 1)
    lanef = lane.astype(F32)
    neg = jnp.float32(-jnp.inf)
    big = jnp.float32(1e9)
    gl = jnp.where(lane < N_EGROUPS, logits, neg)
    gmax = jnp.max(gl, axis=-1, keepdims=True)
    gidx = jnp.min(jnp.where(gl == gmax, lanef, big), axis=-1, keepdims=True)
    pg = 1.0 / jnp.sum(jnp.exp(gl - gmax), axis=-1, keepdims=True)
    lo = N_EGROUPS + EXPERTS_PER_GROUP * gidx
    el = jnp.where((lanef >= lo) & (lanef < lo + EXPERTS_PER_GROUP), logits, neg)
    t1 = jnp.max(el, axis=-1, keepdims=True)
    j1 = jnp.min(jnp.where(el == t1, lanef, big), axis=-1, keepdims=True)
    el2 = jnp.where(lanef == j1, neg, el)
    t2 = jnp.max(el2, axis=-1, keepdims=True)
    j2 = jnp.min(jnp.where(el2 == t2, lanef, big), axis=-1, keepdims=True)
    e21 = jnp.exp(t2 - t1)
    w1 = pg / (1.0 + e21)
    w2 = pg * e21 / (1.0 + e21)
    e1f = j1 - N_EGROUPS
    e2f = j2 - N_EGROUPS

    tm = x.shape[0]
    oh1 = (lanef == e1f).astype(F32)
    oh2 = (lanef == e2f).astype(F32)
    ri = lax.broadcasted_iota(jnp.int32, (tm, tm), 0)
    ci = lax.broadcasted_iota(jnp.int32, (tm, tm), 1)
    before = (ci < ri).astype(BF16)
    p1 = jnp.dot(before, oh1.astype(BF16), preferred_element_type=F32)
    p2 = jnp.dot(before, oh2.astype(BF16), preferred_element_type=F32)
    carry = carry_ref[...]
    c1 = jnp.sum(oh1, axis=0, keepdims=True)
    c2 = jnp.sum(oh2, axis=0, keepdims=True)
    rank1 = jnp.sum(oh1 * (carry + p1), axis=-1, keepdims=True)
    rank2 = jnp.sum(oh2 * (carry + c1 + p2), axis=-1, keepdims=True)
    carry = carry + c1 + c2
    carry_ref[...] = carry
    cnt_ref[...] = jnp.broadcast_to(carry, cnt_ref.shape).astype(jnp.int32)

    ids = jnp.where(lane == 0, e1f, jnp.where(lane == 1, e2f, jnp.where(lane == 2, rank1, jnp.where(lane == 3, rank2, 0.0))))
    id_ref[...] = ids.astype(jnp.int32)
    wt_ref[...] = jnp.where(lane == 0, w1, jnp.where(lane == 1, w2, 0.0))


def router(x1, gain, w_r, b_r):
    tm = 512
    return pl.pallas_call(
        _router_kernel,
        out_shape=(jax.ShapeDtypeStruct((T * PACK_ROWS, LANES), jnp.uint32),
                   jax.ShapeDtypeStruct((T, LANES), jnp.int32),
                   jax.ShapeDtypeStruct((T, LANES), F32),
                   jax.ShapeDtypeStruct((8, LANES), jnp.int32)),
        grid=(T // tm,),
        in_specs=[pl.BlockSpec((tm, D), lambda i: (i, 0)),
                  pl.BlockSpec((1, D), lambda i: (0, 0)),
                  pl.BlockSpec((D, LANES), lambda i: (0, 0)),
                  pl.BlockSpec((1, LANES), lambda i: (0, 0))],
        out_specs=(pl.BlockSpec((tm * PACK_ROWS, LANES), lambda i: (i, 0)),
                   pl.BlockSpec((tm, LANES), lambda i: (i, 0)),
                   pl.BlockSpec((tm, LANES), lambda i: (i, 0)),
                   pl.BlockSpec((8, LANES), lambda i: (0, 0))),
        scratch_shapes=[pltpu.VMEM((1, LANES), F32)],
        compiler_params=_cparams(("arbitrary",)),
        name="ffn_norm_router",
    )(x1, gain, w_r, b_r)


DISPATCH_TB = 1024
N_ZERO_FILLS = 2 * N_EXPERTS


def _dispatch_kernel(dest_ref, zs_ref, h_ref, xs_hbm, zbuf, zsem, sem):
    step = pl.program_id(0)

    @pl.when(step == 0)
    def _():
        zbuf[...] = jnp.zeros_like(zbuf)

        def zero_copy(e):
            start = pl.multiple_of(jnp.maximum(zs_ref[e], 0) * PACK_ROWS, PACK_ROWS)
            return pltpu.make_async_copy(zbuf, xs_hbm.at[pl.ds(start, MOE_BLOCK * PACK_ROWS)], zsem.at[0])

        def zstart(e, c):
            @pl.when(zs_ref[e] >= 0)
            def _():
                zero_copy(e).start()
            return c

        def zwait(e, c):
            @pl.when(zs_ref[e] >= 0)
            def _():
                zero_copy(e).wait()
            return c

        lax.fori_loop(0, N_ZERO_FILLS, zstart, 0)
        lax.fori_loop(0, N_ZERO_FILLS, zwait, 0)

    def row_copy(n, k):
        a = (step * DISPATCH_TB + n) * TOP_K + k
        src = h_ref.at[pl.ds(pl.multiple_of(n * PACK_ROWS, PACK_ROWS), PACK_ROWS)]
        dst = xs_hbm.at[pl.ds(pl.multiple_of(dest_ref[a] * PACK_ROWS, PACK_ROWS), PACK_ROWS)]
        return pltpu.make_async_copy(src, dst, sem.at[0])

    def issue(n, c):
        for k in range(TOP_K):
            row_copy(n, k).start(priority=k)
        return c

    def drain(n, c):
        for k in range(TOP_K):
            row_copy(n, k).wait()
        return c

    lax.fori_loop(0, DISPATCH_TB, issue, 0, unroll=8)
    lax.fori_loop(0, DISPATCH_TB, drain, 0, unroll=8)


def dispatch(dest, zero_start, hpk):
    grid_spec = pltpu.PrefetchScalarGridSpec(
        num_scalar_prefetch=2,
        grid=(T // DISPATCH_TB,),
        in_specs=[pl.BlockSpec((DISPATCH_TB * PACK_ROWS, LANES), lambda i, dst, zs: (i, 0))],
        out_specs=pl.BlockSpec(memory_space=pl.ANY),
        scratch_shapes=[pltpu.VMEM((MOE_BLOCK * PACK_ROWS, LANES), jnp.uint32),
                        pltpu.SemaphoreType.DMA((1,)),
                        pltpu.SemaphoreType.DMA((1,))],
    )
    return pl.pallas_call(
        _dispatch_kernel,
        out_shape=jax.ShapeDtypeStruct((MOE_ROWS * PACK_ROWS, LANES), jnp.uint32),
        grid_spec=grid_spec,
        compiler_params=_cparams(("arbitrary",)),
        name="moe_dispatch",
    )(dest, zero_start, hpk)


def _expert_kernel(be_ref, nu_ref, ord_ref, seq_ref, x_ref, wg_hbm, wu_hbm, wd_hbm, y_ref,
                   wg_st, wu_st, wd_st, wg_bf, wu_bf, wd_bf, sem):
    b = pl.program_id(0)

    def weight_copies(e, slot):
        return (pltpu.make_async_copy(wg_hbm.at[e], wg_st.at[slot], sem.at[slot, 0]),
                pltpu.make_async_copy(wu_hbm.at[e], wu_st.at[slot], sem.at[slot, 1]),
                pltpu.make_async_copy(wd_hbm.at[e], wd_st.at[slot], sem.at[slot, 2]))

    def start_fetch(n, slot):
        @pl.when(seq_ref[n] >= 0)
        def _():
            for cp in weight_copies(seq_ref[n], slot):
                cp.start()

    @pl.when(b < nu_ref[0])
    def _():
        n = ord_ref[b]
        slot = n % 2
        changed = jnp.logical_or(b == 0, be_ref[b] != be_ref[jnp.maximum(b - 1, 0)])

        @pl.when(b == 0)
        def _():
            start_fetch(0, 0)
            start_fetch(1, 1)

        @pl.when(changed)
        def _():
            cg, cu, cd = weight_copies(be_ref[b], slot)
            cg.wait()
            wg_bf[...] = wg_st[slot].astype(BF16)
            cu.wait()
            wu_bf[...] = wu_st[slot].astype(BF16)
            cd.wait()
            wd_bf[...] = wd_st[slot].astype(BF16)
            start_fetch(n + 2, slot)

        x = _unpack_bf16_pairs(x_ref, 0, MOE_BLOCK).astype(BF16)
        gate = jnp.dot(x, wg_bf[...], preferred_element_type=F32)
        up = jnp.dot(x, wu_bf[...], preferred_element_type=F32)
        hid = (jax.nn.silu(gate) * up).astype(BF16)
        y = jnp.dot(hid, wd_bf[...], preferred_element_type=F32)
        word = _pack_bf16_pairs(y.astype(BF16).astype(F32))
        for c in range(PACK_ROWS):
            y_ref[pl.ds(c, MOE_BLOCK, stride=PACK_ROWS), :] = word[:, c * LANES:(c + 1) * LANES]

    @pl.when(b >= nu_ref[0])
    def _():
        y_ref[...] = jnp.zeros_like(y_ref)


def experts(block_expert, n_used, block_ord, expert_seq, xs, w_gate, w_up, w_down):
    def blk(b, be, nu, od, sq):
        return jnp.minimum(b, nu[0] - 1)

    grid_spec = pltpu.PrefetchScalarGridSpec(
        num_scalar_prefetch=4,
        grid=(MOE_BLOCKS,),
        in_specs=[pl.BlockSpec((MOE_BLOCK * PACK_ROWS, LANES), lambda b, be, nu, od, sq: (blk(b, be, nu, od, sq), 0)),
                  pl.BlockSpec(memory_space=pl.ANY),
                  pl.BlockSpec(memory_space=pl.ANY),
                  pl.BlockSpec(memory_space=pl.ANY)],
        out_specs=pl.BlockSpec((MOE_BLOCK * PACK_ROWS, LANES), lambda b, be, nu, od, sq: (b, 0)),
        scratch_shapes=[pltpu.VMEM((2, D, EXPERT_FF), F32),
                        pltpu.VMEM((2, D, EXPERT_FF), F32),
                        pltpu.VMEM((2, EXPERT_FF, D), F32),
                        pltpu.VMEM((D, EXPERT_FF), BF16),
                        pltpu.VMEM((D, EXPERT_FF), BF16),
                        pltpu.VMEM((EXPERT_FF, D), BF16),
                        pltpu.SemaphoreType.DMA((2, 3))],
    )
    return pl.pallas_call(
        _expert_kernel,
        out_shape=jax.ShapeDtypeStruct((MOE_ROWS * PACK_ROWS, LANES), jnp.uint32),
        grid_spec=grid_spec,
        compiler_params=_cparams(("arbitrary",)),
        name="moe_experts",
    )(block_expert, n_used, block_ord, expert_seq, xs, w_gate, w_up, w_down)


COMBINE_ROWS = 512
COMBINE_SUB = 16


def _combine_kernel(dest_ref, ys_hbm, x_ref, wt_ref, g_ref, o_ref, ybuf, sem):
    s = pl.program_id(0)
    ns = pl.num_programs(0)
    slot = s % 2
    tb = COMBINE_SUB
    rows = COMBINE_ROWS
    nxt = jnp.minimum(s + 1, ns - 1)

    def row_copy(step, n, k, sl):
        tok = step * rows + n
        src = ys_hbm.at[pl.ds(pl.multiple_of(dest_ref[tok * TOP_K + k] * PACK_ROWS, PACK_ROWS), PACK_ROWS)]
        dst = ybuf.at[sl * TOP_K + k, pl.ds(pl.multiple_of(n * PACK_ROWS, PACK_ROWS), PACK_ROWS)]
        return pltpu.make_async_copy(src, dst, sem.at[sl])

    def wait_all(step, sl):
        def body(n, c):
            for k in range(TOP_K):
                row_copy(step, n, k, sl).wait()
            return c
        lax.fori_loop(0, rows, body, 0, unroll=8)

    @pl.when(s == 0)
    def _():
        def body(n, c):
            for k in range(TOP_K):
                row_copy(0, n, k, 0).start(priority=k)
            return c
        lax.fori_loop(0, rows, body, 0, unroll=8)

    wait_all(s, slot)

    g = g_ref[...]
    y0_ref = ybuf.at[slot * TOP_K]
    y1_ref = ybuf.at[slot * TOP_K + 1]
    for b in range(rows // tb):
        for n in range(b * tb, (b + 1) * tb):
            for k in range(TOP_K):
                row_copy(nxt, n, k, 1 - slot).start(priority=k)
        sub = slice(b * tb, (b + 1) * tb)
        w = wt_ref[sub, :]
        y0 = _unpack_bf16_pairs(y0_ref, b * tb, tb)
        y1 = _unpack_bf16_pairs(y1_ref, b * tb, tb)
        z = x_ref[sub, :] + (w[:, 0:1] * y0 + w[:, 1:2] * y1)
        ms = jnp.mean(z * z, axis=-1, keepdims=True)
        o_ref[sub, :] = z * lax.rsqrt(ms + NORM_EPS) * g

    @pl.when(s == ns - 1)
    def _():
        wait_all(nxt, 1 - slot)


def combine(dest, ys, x1, wts, gain):
    rows = COMBINE_ROWS
    grid_spec = pltpu.PrefetchScalarGridSpec(
        num_scalar_prefetch=1,
        grid=(T // rows,),
        in_specs=[pl.BlockSpec(memory_space=pl.ANY),
                  pl.BlockSpec((rows, D), lambda s, dst: (s, 0)),
                  pl.BlockSpec((rows, LANES), lambda s, dst: (s, 0)),
                  pl.BlockSpec((1, D), lambda s, dst: (0, 0))],
        out_specs=pl.BlockSpec((rows, D), lambda s, dst: (s, 0)),
        scratch_shapes=[pltpu.VMEM((2 * TOP_K, rows * PACK_ROWS, LANES), jnp.uint32),
                        pltpu.SemaphoreType.DMA((2,))],
    )
    return pl.pallas_call(
        _combine_kernel,
        out_shape=jax.ShapeDtypeStruct((T, D), F32),
        grid_spec=grid_spec,
        compiler_params=_cparams(("arbitrary",)),
        name="moe_combine_final_norm",
    )(dest, ys, x1, wts, gain)


def dispatch_plan(ids, counts):
    experts_ = jnp.arange(N_EXPERTS, dtype=jnp.int32)
    padded = (counts + MOE_BLOCK - 1) // MOE_BLOCK * MOE_BLOCK
    pad_end = jnp.cumsum(padded)
    pad_start = pad_end - padded
    e = ids[:, :TOP_K]
    start_of = jnp.sum(jnp.where(e[:, :, None] == experts_[None, None, :], pad_start[None, None, :], 0), axis=-1)
    dest = (start_of + ids[:, TOP_K:2 * TOP_K]).reshape(N_ASSIGN).astype(jnp.int32)
    n_used = pad_end[-1] // MOE_BLOCK
    block_start = jnp.minimum(jnp.arange(MOE_BLOCKS, dtype=jnp.int32), n_used - 1) * MOE_BLOCK
    block_expert = jnp.sum((block_start[:, None] >= pad_end[None, :]).astype(jnp.int32), axis=1)
    block_expert = jnp.minimum(block_expert, N_EXPERTS - 1).astype(jnp.int32)
    tail = n_used + experts_
    zero_start = jnp.concatenate([jnp.where(counts > 0, pad_end - MOE_BLOCK, -1),
                                  jnp.where(tail < MOE_BLOCKS, tail * MOE_BLOCK, -1)]).astype(jnp.int32)
    present = counts > 0
    expert_ord = jnp.cumsum(present.astype(jnp.int32)) - 1
    slots = jnp.arange(N_EXPERTS + 2, dtype=jnp.int32)
    hit = present[None, :] & (expert_ord[None, :] == slots[:, None])
    expert_seq = jnp.where(jnp.any(hit, axis=1), jnp.sum(jnp.where(hit, experts_[None, :], 0), axis=1), -1)
    block_ord = jnp.sum(jnp.where(block_expert[:, None] == experts_[None, :], expert_ord[None, :], 0), axis=1)
    return (block_expert, n_used.astype(jnp.int32).reshape(1), dest, zero_start,
            block_ord.astype(jnp.int32), expert_seq.astype(jnp.int32))


def kernel(x, norm_mix, w_in, b_gate, ssm_a_re, ssm_a_im, ssm_log_step, ssm_b_re, ssm_b_im, ssm_c_re, ssm_c_im, ssm_d, w_glu, b_glu, w_up_attn, w_up_ssm, w_out, norm_ffn, w_router_group, b_router_group, w_router_expert, b_router_expert, w_expert_gate, w_expert_up, w_expert_down, norm_final):
    x2 = x.reshape(T, D)
    h = norm_permute(x2, norm_mix.reshape(1, D)).reshape(T, D)
    w_in_l = w_in.reshape(D, IN_COLS)
    qkv = proj(h, w_in_l, 0, 3 * QKV_COLS, tn=QKV_COLS, name="proj_qkv")
    u = proj(h, w_in_l, 3 * QKV_COLS, SSM_W, tn=SSM_W, name="proj_ssm_in")

    attn = attention(qkv.reshape(R, NI, 3 * QKV_COLS)).reshape(T, ATTN_OUT)

    G = SSM_W // SSM_CH
    wb, wc, pows = ssm_params(
        ssm_a_re.reshape(G, SSM_STATE).astype(F32), ssm_a_im.reshape(G, SSM_STATE).astype(F32),
        ssm_log_step.reshape(G),
        ssm_b_re.reshape(G, SSM_STATE, SSM_CH).astype(F32), ssm_b_im.reshape(G, SSM_STATE, SSM_CH).astype(F32),
        ssm_c_re.reshape(G, SSM_CH, SSM_STATE), ssm_c_im.reshape(G, SSM_CH, SSM_STATE))
    yg = ssm_scan(u.reshape(R, NI, SSM_W), wb, wc, pows, ssm_d.reshape(1, SSM_W).astype(F32))
    ssm = glu(yg.reshape(T, SSM_W), w_glu.reshape(SSM_W, SSM_W), b_glu.reshape(1, SSM_W))

    merged = merge(h, attn, ssm, w_in_l, b_gate.reshape(1, 2 * D),
                   w_up_attn.reshape(ATTN_OUT, D), w_up_ssm.reshape(SSM_W, D))
    x1 = out_proj(merged.reshape(R, NI, D), w_out.reshape(D, D), x2)

    w_r = jnp.concatenate([w_router_group.reshape(D, N_EGROUPS), w_router_expert.reshape(D, N_EXPERTS),
                           jnp.zeros((D, LANES - N_EGROUPS - N_EXPERTS), F32)], axis=1)
    b_r = jnp.concatenate([b_router_group.reshape(1, N_EGROUPS), b_router_expert.reshape(1, N_EXPERTS),
                           jnp.zeros((1, LANES - N_EGROUPS - N_EXPERTS), F32)], axis=1)
    hpk, ids, wts, counts = router(x1, norm_ffn.reshape(1, D), w_r, b_r)

    block_expert, n_used, dest, zero_start, block_ord, expert_seq = dispatch_plan(ids[:, :2 * TOP_K], counts[0, :N_EXPERTS])
    xs = dispatch(dest, zero_start, hpk)
    ys = experts(block_expert, n_used, block_ord, expert_seq, xs,
                 w_expert_gate.reshape(N_EXPERTS, D, EXPERT_FF), w_expert_up.reshape(N_EXPERTS, D, EXPERT_FF),
                 w_expert_down.reshape(N_EXPERTS, EXPERT_FF, D))
    out = combine(dest, ys, x1, wts, norm_final.reshape(1, D))
    return out.reshape(1, T, D)
```

```python
import functools

import jax
import jax.numpy as jnp
from jax import lax
from jax.experimental import pallas as pl
from jax.experimental.pallas import tpu as pltpu

F32 = jnp.float32
BF16 = jnp.bfloat16

T = 8192
D = 2048
R = 16
NI = T // R
HEAD_DIM = 64
N_HEAD_SLOTS = 8
DILATIONS = (1, 4, 16)
ATTN_BLOCK = 128
QKV_COLS = 1536
ATTN_OUT = 512
SSM_W = 1024
SSM_STATE = 64
SSM_CH = 16
IN_COLS = 3 * QKV_COLS + SSM_W + 2 * D
N_EXPERTS = 32
N_EGROUPS = 4
EXPERTS_PER_GROUP = 8
TOP_K = 2
EXPERT_FF = 512
NORM_EPS = 1e-6
LANES = 128
VMEM_LIMIT = 48 * 1024 * 1024

MOE_BLOCK = 256
N_ASSIGN = T * TOP_K
MOE_BLOCKS = N_ASSIGN // MOE_BLOCK + N_EXPERTS
MOE_ROWS = MOE_BLOCKS * MOE_BLOCK


def _cparams(sem):
    return pltpu.CompilerParams(dimension_semantics=sem, vmem_limit_bytes=VMEM_LIMIT)


N_SLABS = D // LANES
NORM_TB = 64
NORM_CHUNK = 64
NORM_PITCH = R + 8


def _norm_permute_kernel(x_ref, g_ref, h_ref, slab):
    g = g_ref[...]

    def chunk(t, c):
        rows = pl.ds(pl.multiple_of(t * NORM_CHUNK, NORM_CHUNK), NORM_CHUNK)
        x = x_ref[rows, :]
        ms = jnp.mean(x * x, axis=-1, keepdims=True)
        hn = x * lax.rsqrt(ms + NORM_EPS) * g
        for k in range(NORM_CHUNK // R):
            dst = pl.ds(pl.multiple_of((t * (NORM_CHUNK // R) + k) * NORM_PITCH, 8), R)
            for s_ in range(N_SLABS):
                slab[s_, dst, :] = hn[k * R:(k + 1) * R, s_ * LANES:(s_ + 1) * LANES]
        return c

    lax.fori_loop(0, NORM_TB * R // NORM_CHUNK, chunk, 0)
    for r in range(R):
        pieces = [slab[s_, pl.ds(r, NORM_TB, stride=NORM_PITCH), :] for s_ in range(N_SLABS)]
        h_ref[r] = jnp.concatenate(pieces, axis=1).astype(BF16)


def norm_permute(x2, gain):
    return pl.pallas_call(
        _norm_permute_kernel,
        out_shape=jax.ShapeDtypeStruct((R, NI, D), BF16),
        grid=(NI // NORM_TB,),
        in_specs=[pl.BlockSpec((NORM_TB * R, D), lambda i: (i, 0)),
                  pl.BlockSpec((1, D), lambda i: (0, 0))],
        out_specs=pl.BlockSpec((R, NORM_TB, D), lambda i: (0, i, 0)),
        scratch_shapes=[pltpu.VMEM((N_SLABS, NORM_TB * NORM_PITCH, LANES), F32)],
        compiler_params=_cparams(("arbitrary",)),
        name="norm_permute",
    )(x2, gain)


MXU_N = 256


def _stationary_weight_tile(w_hbm, col0, stage_ref, wbf_ref, sem):
    j = pl.program_id(0)
    tn = stage_ref.shape[1]

    def copy(jj):
        c0 = pl.multiple_of(col0 + jj * tn, LANES)
        return pltpu.make_async_copy(w_hbm.at[:, pl.ds(c0, tn)], stage_ref, sem)

    @pl.when(pl.program_id(1) == 0)
    def _():
        @pl.when(j == 0)
        def _():
            copy(0).start()

        copy(j).wait()
        wbf_ref[...] = stage_ref[...].astype(BF16)

        @pl.when(j + 1 < pl.num_programs(0))
        def _():
            copy(j + 1).start()


def _proj_kernel(a_ref, w_hbm, o_ref, stage_ref, wbf_ref, sem, *, col_off):
    _stationary_weight_tile(w_hbm, col_off, stage_ref, wbf_ref, sem.at[0])
    a = a_ref[...]
    for c in range(o_ref.shape[1] // MXU_N):
        cols = slice(c * MXU_N, (c + 1) * MXU_N)
        o_ref[:, cols] = jnp.dot(a, wbf_ref[:, cols], preferred_element_type=F32)


def proj(h, w_in, col_off, n_cols, tn, name="proj"):
    tm = 1024
    return pl.pallas_call(
        functools.partial(_proj_kernel, col_off=col_off),
        out_shape=jax.ShapeDtypeStruct((T, n_cols), F32),
        grid=(n_cols // tn, T // tm),
        in_specs=[pl.BlockSpec((tm, D), lambda j, i: (i, 0)),
                  pl.BlockSpec(memory_space=pl.ANY)],
        out_specs=pl.BlockSpec((tm, tn), lambda j, i: (i, j)),
        scratch_shapes=[pltpu.VMEM((D, tn), F32), pltpu.VMEM((D, tn), BF16), pltpu.SemaphoreType.DMA((1,))],
        compiler_params=_cparams(("arbitrary", "arbitrary")),
        name=name,
    )(h, w_in)


def _seq_index_maps(d):
    nseg = R // d
    qlen = ATTN_BLOCK // nseg
    return nseg, qlen


def _bias_matrices(d, hp):
    nseg, qlen = _seq_index_maps(d)
    klen = 2 * qlen
    row = lax.broadcasted_iota(jnp.int32, (2 * ATTN_BLOCK, 2 * ATTN_BLOCK), 0)
    col = lax.broadcasted_iota(jnp.int32, (2 * ATTN_BLOCK, 2 * ATTN_BLOCK), 1)
    rho = row % ATTN_BLOCK
    jq = (rho % qlen) * nseg + rho // qlen
    jk = ((col % klen) - qlen) * nseg + col // klen
    steps = jq - jk
    valid = (steps >= 0) & (steps <= ATTN_BLOCK)
    head = 2 * hp + row // ATTN_BLOCK
    slope = lax.bitcast_convert_type((127 - (head + 1)) << 23, F32)
    bias = -slope * (d * steps).astype(F32)
    neg = jnp.float32(-jnp.inf)
    return jnp.where(valid, bias, neg), jnp.where(valid & (jk >= 0), bias, neg)


def _attend_pair(q, k, v, bias):
    lane = lax.broadcasted_iota(jnp.int32, (ATTN_BLOCK, LANES), 1)
    first = lane < HEAD_DIM
    zero = jnp.zeros_like(q)
    q2 = jnp.concatenate([jnp.where(first, q, zero), jnp.where(first, zero, q)], axis=0).astype(BF16)
    s = lax.dot_general(q2, k.astype(BF16), (((1,), (1,)), ((), ())), preferred_element_type=F32)
    s = s + bias
    m = jnp.max(s, axis=-1, keepdims=True)
    p = jnp.exp(s - m)
    l = jnp.sum(p, axis=-1, keepdims=True)
    o2 = jnp.dot(p.astype(BF16), v.astype(BF16), preferred_element_type=F32)
    o = jnp.where(first, o2[:ATTN_BLOCK], o2[ATTN_BLOCK:])
    m_b = jnp.where(first, m[:ATTN_BLOCK], m[ATTN_BLOCK:])
    l_b = jnp.where(first, l[:ATTN_BLOCK], l[ATTN_BLOCK:])
    return o, m_b, l_b


def _attn_kernel(q_ref, kp_ref, kc_ref, vp_ref, vc_ref, o_ref, kbuf, vbuf, obuf, mbuf, lbuf, bias_ref):
    hp = pl.program_id(0)
    it = pl.program_id(1)
    g = pl.program_id(2)
    scale = HEAD_DIM ** -0.5

    for gi, d in enumerate(DILATIONS):
        nseg, qlen = _seq_index_maps(d)
        klen = 2 * qlen
        nblk = ATTN_BLOCK // qlen

        @pl.when(g == gi)
        def _(gi=gi, d=d, nseg=nseg, qlen=qlen, klen=klen, nblk=nblk):
            if nseg > 1:
                kbuf[:, :ATTN_BLOCK, :] = kp_ref[...]
                kbuf[:, ATTN_BLOCK:, :] = kc_ref[...]
                vbuf[:, :ATTN_BLOCK, :] = vp_ref[...]
                vbuf[:, ATTN_BLOCK:, :] = vc_ref[...]

            @pl.when(it == 0)
            def _():
                b_reg, b_first = _bias_matrices(d, hp)
                bias_ref[gi, 0] = b_reg
                bias_ref[gi, 1] = b_first

            def block(idx, carry):
                rd = idx // nblk
                bb = idx % nblk
                q0 = pl.multiple_of(bb * qlen, qlen)
                k0 = pl.multiple_of(ATTN_BLOCK + bb * qlen - qlen, qlen)
                qs, ks, vs = [], [], []
                for m_ in range(nseg):
                    rr = rd + d * m_
                    qs.append(q_ref[rr, pl.ds(q0, qlen), :])
                    if nseg > 1:
                        ks.append(kbuf[rr, pl.ds(k0, klen), :])
                        vs.append(vbuf[rr, pl.ds(k0, klen), :])
                    else:
                        ks += [kp_ref[rr], kc_ref[rr]]
                        vs += [vp_ref[rr], vc_ref[rr]]
                q = jnp.concatenate(qs, axis=0) * scale
                k = jnp.concatenate(ks, axis=0)
                v = jnp.concatenate(vs, axis=0)
                is_first = jnp.logical_and(it == 0, bb == 0)
                bias = bias_ref[gi, jnp.where(is_first, 1, 0)]
                o, mx, den = _attend_pair(q, k, v, bias)
                for m_ in range(nseg):
                    rr = rd + d * m_
                    seg = slice(m_ * qlen, (m_ + 1) * qlen)
                    obuf[gi, rr, pl.ds(q0, qlen), :] = o[seg]
                    mbuf[gi, rr, pl.ds(q0, qlen), :] = mx[seg]
                    lbuf[gi, rr, pl.ds(q0, qlen), :] = den[seg]
                return carry

            lax.fori_loop(0, d * nblk, block, 0, unroll=True)

    @pl.when(g == len(DILATIONS) - 1)
    def _():
        for r in range(R):
            m0, m1, m2 = mbuf[0, r], mbuf[1, r], mbuf[2, r]
            mx = jnp.maximum(jnp.maximum(m0, m1), m2)
            e0, e1, e2 = jnp.exp(m0 - mx), jnp.exp(m1 - mx), jnp.exp(m2 - mx)
            den = e0 * lbuf[0, r] + e1 * lbuf[1, r] + e2 * lbuf[2, r]
            num = e0 * obuf[0, r] + e1 * obuf[1, r] + e2 * obuf[2, r]
            o_ref[r] = (num / den).astype(BF16)


def attention(qkv3):
    n_hp = N_HEAD_SLOTS // 2
    n_it = NI // ATTN_BLOCK
    ng = len(DILATIONS)
    cb = QKV_COLS // LANES

    def cur(base):
        return pl.BlockSpec((R, ATTN_BLOCK, LANES), lambda hp, it, g: (0, it, base + g * n_hp + hp))

    def prev(base):
        return pl.BlockSpec((R, ATTN_BLOCK, LANES),
                            lambda hp, it, g: (0, jnp.maximum(it - 1, 0), base + g * n_hp + hp))

    return pl.pallas_call(
        _attn_kernel,
        out_shape=jax.ShapeDtypeStruct((R, NI, ATTN_OUT), BF16),
        grid=(n_hp, n_it, ng),
        in_specs=[cur(0), prev(cb), cur(cb), prev(2 * cb), cur(2 * cb)],
        out_specs=pl.BlockSpec((R, ATTN_BLOCK, LANES), lambda hp, it, g: (0, it, hp)),
        scratch_shapes=[pltpu.VMEM((R, 2 * ATTN_BLOCK, LANES), F32),
                        pltpu.VMEM((R, 2 * ATTN_BLOCK, LANES), F32),
                        pltpu.VMEM((ng, R, ATTN_BLOCK, LANES), F32),
                        pltpu.VMEM((ng, R, ATTN_BLOCK, LANES), F32),
                        pltpu.VMEM((ng, R, ATTN_BLOCK, LANES), F32),
                        pltpu.VMEM((ng, 2, 2 * ATTN_BLOCK, 2 * ATTN_BLOCK), F32)],
        compiler_params=_cparams(("arbitrary", "arbitrary", "arbitrary")),
        name="dilated_attention",
    )(qkv3, qkv3, qkv3, qkv3, qkv3)


SSM_SLAB = 256
SSM_SLABS = SSM_W // SSM_SLAB
SLAB_STATES = SSM_SLAB // SSM_CH * SSM_STATE
SSM_TI = 128
SSM_MM_CHUNK = 4


def _ssm_kernel(u_ref, wb_ref, wc_ref, pw_ref, dsk_ref, o_ref, s_ref, zs_ref, zc_ref, tr_ref, zt_ref):
    ic = pl.program_id(1)
    ns = SLAB_STATES

    @pl.when(ic == 0)
    def _():
        zc_ref[...] = jnp.zeros_like(zc_ref)

    n_chunks = R // SSM_MM_CHUNK
    n_tiles = SSM_TI // 8
    wb = wb_ref[0]
    wc = wc_ref[0]
    dsk = dsk_ref[...]
    arb = jnp.broadcast_to(pw_ref[0, 0:1, :ns], (8, ns))
    aib = jnp.broadcast_to(pw_ref[0, 0:1, ns:], (8, ns))


    def bu_chunk(c):
        lo = c * SSM_MM_CHUNK
        uc = u_ref[lo:lo + SSM_MM_CHUNK].reshape(SSM_MM_CHUNK * SSM_TI, SSM_SLAB)
        bu = jnp.dot(uc.astype(BF16), wb, preferred_element_type=F32)
        s_ref[lo:lo + SSM_MM_CHUNK] = bu.reshape(SSM_MM_CHUNK, SSM_TI, 2 * ns)

    def local_chunk(c):
        lo = c * SSM_MM_CHUNK
        first = max(lo, 1)
        for t in range(n_tiles):
            rows = slice(t * 8, (t + 1) * 8)
            pr = s_ref[first - 1, rows, :ns]
            pi = s_ref[first - 1, rows, ns:]
            for r in range(first, lo + SSM_MM_CHUNK):
                nr = s_ref[r, rows, :ns] + (arb * pr - aib * pi)
                ni = s_ref[r, rows, ns:] + (arb * pi + aib * pr)
                s_ref[r, rows, :ns] = nr
                s_ref[r, rows, ns:] = ni
                pr, pi = nr, ni

    bu_chunk(0)
    for c in range(n_chunks):
        if c + 1 < n_chunks:
            bu_chunk(c + 1)
        local_chunk(c)

    n_ch = ns // LANES
    for c in range(n_ch):
        tr_ref[0, c * 8:c * 8 + 1, :] = pw_ref[0, R - 1:R, c * LANES:(c + 1) * LANES]
        tr_ref[1, c * 8:c * 8 + 1, :] = pw_ref[0, R - 1:R, ns + c * LANES:ns + (c + 1) * LANES]
    a16r = tr_ref[0, pl.ds(0, n_ch, stride=8), :]
    a16i = tr_ref[1, pl.ds(0, n_ch, stride=8), :]

    def ztile(t, carry):
        zr, zi = carry
        rows = pl.ds(pl.multiple_of(t * 8, 8), 8)
        for c in range(n_ch):
            tr_ref[0, c * 8:(c + 1) * 8, :] = s_ref[R - 1, rows, c * LANES:(c + 1) * LANES]
            tr_ref[1, c * 8:(c + 1) * 8, :] = s_ref[R - 1, rows, ns + c * LANES:ns + (c + 1) * LANES]
        for i in range(8):
            zt_ref[0, pl.ds(i, n_ch, stride=8), :] = zr
            zt_ref[1, pl.ds(i, n_ch, stride=8), :] = zi
            er = tr_ref[0, pl.ds(i, n_ch, stride=8), :]
            ei = tr_ref[1, pl.ds(i, n_ch, stride=8), :]
            zr, zi = a16r * zr - a16i * zi + er, a16r * zi + a16i * zr + ei
        for c in range(n_ch):
            zs_ref[rows, c * LANES:(c + 1) * LANES] = zt_ref[0, c * 8:(c + 1) * 8, :]
            zs_ref[rows, ns + c * LANES:ns + (c + 1) * LANES] = zt_ref[1, c * 8:(c + 1) * 8, :]
        return zr, zi

    zr_end, zi_end = lax.fori_loop(0, SSM_TI // 8, ztile, (zc_ref[0], zc_ref[1]))
    zc_ref[0] = zr_end
    zc_ref[1] = zi_end

    def fix_chunk(c):
        lo = c * SSM_MM_CHUNK
        for r in range(lo, lo + SSM_MM_CHUNK):
            prb = jnp.broadcast_to(pw_ref[0, r:r + 1, :ns], (8, ns))
            pib = jnp.broadcast_to(pw_ref[0, r:r + 1, ns:], (8, ns))
            for t in range(n_tiles):
                rows = slice(t * 8, (t + 1) * 8)
                zr = zs_ref[rows, :ns]
                zi = zs_ref[rows, ns:]
                s_ref[r, rows, :ns] = s_ref[r, rows, :ns] + (prb * zr - pib * zi)
                s_ref[r, rows, ns:] = s_ref[r, rows, ns:] + (prb * zi + pib * zr)

    def out_chunk(c):
        lo = c * SSM_MM_CHUNK
        xs = s_ref[lo:lo + SSM_MM_CHUNK].reshape(SSM_MM_CHUNK * SSM_TI, 2 * ns)
        y = jnp.dot(xs.astype(BF16), wc, preferred_element_type=F32)
        y = y.reshape(SSM_MM_CHUNK, SSM_TI, SSM_SLAB) + dsk * u_ref[lo:lo + SSM_MM_CHUNK]
        o_ref[lo:lo + SSM_MM_CHUNK] = jax.nn.gelu(y)

    fix_chunk(0)
    for c in range(n_chunks):
        if c + 1 < n_chunks:
            fix_chunk(c + 1)
        out_chunk(c)


def ssm_scan(u3, wb, wc, pows, dskip):
    ns2 = 2 * SLAB_STATES
    return pl.pallas_call(
        _ssm_kernel,
        out_shape=jax.ShapeDtypeStruct((R, NI, SSM_W), F32),
        grid=(SSM_SLABS, NI // SSM_TI),
        in_specs=[pl.BlockSpec((R, SSM_TI, SSM_SLAB), lambda kb, ic: (0, ic, kb)),
                  pl.BlockSpec((1, SSM_SLAB, ns2), lambda kb, ic: (kb, 0, 0)),
                  pl.BlockSpec((1, ns2, SSM_SLAB), lambda kb, ic: (kb, 0, 0)),
                  pl.BlockSpec((1, R, ns2), lambda kb, ic: (kb, 0, 0)),
                  pl.BlockSpec((1, SSM_SLAB), lambda kb, ic: (0, kb))],
        out_specs=pl.BlockSpec((R, SSM_TI, SSM_SLAB), lambda kb, ic: (0, ic, kb)),
        scratch_shapes=[pltpu.VMEM((R, SSM_TI, ns2), F32),
                        pltpu.VMEM((SSM_TI, ns2), F32),
                        pltpu.VMEM((2, SLAB_STATES // LANES, LANES), F32),
                        pltpu.VMEM((2, 8 * SLAB_STATES // LANES, LANES), F32),
                        pltpu.VMEM((2, 8 * SLAB_STATES // LANES, LANES), F32)],
        compiler_params=_cparams(("arbitrary", "arbitrary")),
        name="s5_ssm",
    )(u3, wb, wc, pows, dskip)


def ssm_params(a_re, a_im, log_step, b_re, b_im, c_re, c_im):
    G, P, H = SSM_W // SSM_CH, SSM_STATE, SSM_CH
    gs = SSM_SLAB // SSM_CH
    step = jnp.exp(log_step.astype(F32))[:, None]
    kk = jnp.arange(1, R + 1, dtype=F32)[:, None, None]
    mag = jnp.exp(kk * (a_re * step))
    ang = kk * (a_im * step)
    pw_re, pw_im = mag * jnp.cos(ang), mag * jnp.sin(ang)
    abar_re, abar_im = pw_re[0], pw_im[0]
    nr, ni = abar_re - 1.0, abar_im
    den = a_re * a_re + a_im * a_im
    f_re = (nr * a_re + ni * a_im) / den
    f_im = (ni * a_re - nr * a_im) / den
    bbar_re = f_re[..., None] * b_re - f_im[..., None] * b_im
    bbar_im = f_re[..., None] * b_im + f_im[..., None] * b_re
    rows_g = jnp.arange(gs * H, dtype=jnp.int32) // H
    cols_g = jnp.arange(gs * P, dtype=jnp.int32) // P
    b_t = jnp.concatenate([jnp.swapaxes(bbar_re, 1, 2).reshape(SSM_SLABS, gs * H, P),
                           jnp.swapaxes(bbar_im, 1, 2).reshape(SSM_SLABS, gs * H, P)], axis=-1)
    tile_p = (jnp.arange(2 * P, dtype=jnp.int32)[:, None] ==
              (jnp.arange(2 * gs * P, dtype=jnp.int32) // (gs * P) * P + jnp.arange(2 * gs * P, dtype=jnp.int32) % P)[None, :])
    mask_b = rows_g[:, None] == jnp.tile(cols_g, 2)[None, :]
    wb = jnp.where(mask_b[None], jnp.einsum('kap,pc->kac', b_t.astype(BF16), tile_p.astype(BF16),
                                            preferred_element_type=F32), 0.0).astype(BF16)
    c_t = jnp.concatenate([jnp.swapaxes(c_re.astype(F32), 1, 2).reshape(SSM_SLABS, gs * P, H),
                           -jnp.swapaxes(c_im.astype(F32), 1, 2).reshape(SSM_SLABS, gs * P, H)], axis=1)
    tile_h = jnp.arange(H, dtype=jnp.int32)[:, None] == (jnp.arange(gs * H, dtype=jnp.int32) % H)[None, :]
    mask_c = jnp.tile(cols_g, 2)[:, None] == rows_g[None, :]
    wc = jnp.where(mask_c[None], jnp.einsum('kah,hc->kac', c_t.astype(BF16), tile_h.astype(BF16),
                                            preferred_element_type=F32), 0.0).astype(BF16)
    pows = jnp.concatenate([pw_re.reshape(R, SSM_SLABS, gs * P), pw_im.reshape(R, SSM_SLABS, gs * P)], axis=-1)
    return wb, wc, jnp.transpose(pows, (1, 0, 2))


def _glu_kernel(a_ref, w_ref, b_ref, o_ref, wbf_ref):
    @pl.when(pl.program_id(0) == 0)
    def _():
        wbf_ref[...] = w_ref[...].astype(BF16)

    a = a_ref[...].astype(BF16)
    for c in range(SSM_W // MXU_N):
        cols = slice(c * MXU_N, (c + 1) * MXU_N)
        acc = jnp.dot(a, wbf_ref[:, cols], preferred_element_type=F32)
        o_ref[:, cols] = (a_ref[:, cols] * jax.nn.sigmoid(acc + b_ref[:, cols])).astype(BF16)


def glu(yg, w_glu, b_glu):
    tm = 1024
    return pl.pallas_call(
        _glu_kernel,
        out_shape=jax.ShapeDtypeStruct((T, SSM_W), BF16),
        grid=(T // tm,),
        in_specs=[pl.BlockSpec((tm, SSM_W), lambda i: (i, 0)),
                  pl.BlockSpec((SSM_W, SSM_W), lambda i: (0, 0)),
                  pl.BlockSpec((1, SSM_W), lambda i: (0, 0))],
        out_specs=pl.BlockSpec((tm, SSM_W), lambda i: (i, 0)),
        scratch_shapes=[pltpu.VMEM((SSM_W, SSM_W), BF16)],
        compiler_params=_cparams(("arbitrary",)),
        name="ssm_glu",
    )(yg, w_glu, b_glu)


def _merge_kernel(h_ref, at_ref, ss_ref, win_hbm, ba_ref, bs_ref, wa_hbm, ws_hbm, o_ref,
                  wga_st, wgs_st, wa_st, ws_st, wga_bf, wgs_bf, wa_bf, ws_bf, sem):
    gate_col = 3 * QKV_COLS + SSM_W
    _stationary_weight_tile(win_hbm, gate_col, wga_st, wga_bf, sem.at[0])
    _stationary_weight_tile(win_hbm, gate_col + D, wgs_st, wgs_bf, sem.at[1])
    _stationary_weight_tile(wa_hbm, 0, wa_st, wa_bf, sem.at[2])
    _stationary_weight_tile(ws_hbm, 0, ws_st, ws_bf, sem.at[3])
    h = h_ref[...]
    at = at_ref[...]
    ss = ss_ref[...]
    for c in range(o_ref.shape[1] // MXU_N):
        cols = slice(c * MXU_N, (c + 1) * MXU_N)
        ga = jax.nn.sigmoid(jnp.dot(h, wga_bf[:, cols], preferred_element_type=F32) + ba_ref[:, cols])
        a = jnp.dot(at, wa_bf[:, cols], preferred_element_type=F32)
        gs = jax.nn.sigmoid(jnp.dot(h, wgs_bf[:, cols], preferred_element_type=F32) + bs_ref[:, cols])
        s = jnp.dot(ss, ws_bf[:, cols], preferred_element_type=F32)
        o_ref[:, cols] = (ga * a + gs * s).astype(BF16)


def merge(h, attn, ssm, w_in, b_gate, w_up_attn, w_up_ssm):
    tm, tn = 1024, 512
    nj = D // tn
    any_space = pl.BlockSpec(memory_space=pl.ANY)
    return pl.pallas_call(
        _merge_kernel,
        out_shape=jax.ShapeDtypeStruct((T, D), BF16),
        grid=(nj, T // tm),
        in_specs=[pl.BlockSpec((tm, D), lambda j, i: (i, 0)),
                  pl.BlockSpec((tm, ATTN_OUT), lambda j, i: (i, 0)),
                  pl.BlockSpec((tm, SSM_W), lambda j, i: (i, 0)),
                  any_space,
                  pl.BlockSpec((1, tn), lambda j, i: (0, j)),
                  pl.BlockSpec((1, tn), lambda j, i: (0, j + nj)),
                  any_space,
                  any_space],
        out_specs=pl.BlockSpec((tm, tn), lambda j, i: (i, j)),
        scratch_shapes=[pltpu.VMEM((D, tn), F32), pltpu.VMEM((D, tn), F32),
                        pltpu.VMEM((ATTN_OUT, tn), F32), pltpu.VMEM((SSM_W, tn), F32),
                        pltpu.VMEM((D, tn), BF16), pltpu.VMEM((D, tn), BF16),
                        pltpu.VMEM((ATTN_OUT, tn), BF16), pltpu.VMEM((SSM_W, tn), BF16),
                        pltpu.SemaphoreType.DMA((4,))],
        compiler_params=_cparams(("arbitrary", "arbitrary")),
        name="gates_branch_merge",
    )(h, attn, ssm, w_in, b_gate, b_gate, w_up_attn, w_up_ssm)


OUTPROJ_TB = 64
OUTPROJ_PITCH = OUTPROJ_TB + 8


def _outproj_kernel(m_ref, w_hbm, x_ref, o_ref, slab, stage_ref, wbf_ref, sem):
    _stationary_weight_tile(w_hbm, 0, stage_ref, wbf_ref, sem.at[0])
    tb = OUTPROJ_TB
    a = m_ref[...].reshape(R * tb, D)
    per = MXU_N // LANES
    for c in range(o_ref.shape[1] // MXU_N):
        acc = jnp.dot(a, wbf_ref[:, c * MXU_N:(c + 1) * MXU_N], preferred_element_type=F32)
        for s_ in range(per):
            lanes = slice(s_ * LANES, (s_ + 1) * LANES)
            for r in range(R):
                slab[c * per + s_, r * OUTPROJ_PITCH:r * OUTPROJ_PITCH + tb, :] = acc[r * tb:(r + 1) * tb, lanes]
        for s_ in range(per):
            lanes = slice((c * per + s_) * LANES, (c * per + s_ + 1) * LANES)
            for i in range(tb):
                rows = slice(i * R, (i + 1) * R)
                o_ref[rows, lanes] = slab[c * per + s_, pl.ds(i, R, stride=OUTPROJ_PITCH), :] + x_ref[rows, lanes]


def out_proj(merged3, w_out, x2):
    tb, tn = OUTPROJ_TB, 1024
    return pl.pallas_call(
        _outproj_kernel,
        out_shape=jax.ShapeDtypeStruct((T, D), F32),
        grid=(D // tn, NI // tb),
        in_specs=[pl.BlockSpec((R, tb, D), lambda j, i: (0, i, 0)),
                  pl.BlockSpec(memory_space=pl.ANY),
                  pl.BlockSpec((tb * R, tn), lambda j, i: (i, j))],
        out_specs=pl.BlockSpec((tb * R, tn), lambda j, i: (i, j)),
        scratch_shapes=[pltpu.VMEM((tn // LANES, OUTPROJ_PITCH * R, LANES), F32),
                        pltpu.VMEM((D, tn), F32), pltpu.VMEM((D, tn), BF16),
                        pltpu.SemaphoreType.DMA((1,))],
        compiler_params=_cparams(("arbitrary", "arbitrary")),
        name="out_proj_residual",
    )(merged3, w_out, x2)


PACK_ROWS = 8
HALF_D = D // 2


def _pack_bf16_pairs(zf):
    top = lax.bitcast_convert_type(zf, jnp.uint32)
    return top[:, HALF_D:] | (top[:, :HALF_D] >> 16)


def _unpack_bf16_pairs(x_ref, first, n):
    lo, hi = [], []
    for c in range(PACK_ROWS):
        w = x_ref[pl.ds(first * PACK_ROWS + c, n, stride=PACK_ROWS), :]
        lo.append(lax.bitcast_convert_type(w << 16, F32))
        hi.append(lax.bitcast_convert_type(w & jnp.uint32(0xFFFF0000), F32))
    return jnp.concatenate(lo + hi, axis=1)


def _router_kernel(x_ref, g_ref, w_ref, b_ref, h_ref, id_ref, wt_ref, cnt_ref, carry_ref):
    step = pl.program_id(0)

    @pl.when(step == 0)
    def _():
        carry_ref[...] = jnp.zeros_like(carry_ref)

    x = x_ref[...]
    ms = jnp.mean(x * x, axis=-1, keepdims=True)
    z = x * lax.rsqrt(ms + NORM_EPS) * g_ref[...]
    zh = z.astype(BF16)
    zf = zh.astype(F32)
    word = _pack_bf16_pairs(zf)
    for c in range(PACK_ROWS):
        h_ref[pl.ds(c, x.shape[0], stride=PACK_ROWS), :] = word[:, c * LANES:(c + 1) * LANES]
    zl = (z - zf).astype(BF16)
    w = w_ref[...]
    wh = w.astype(BF16)
    wl = (w - wh.astype(F32)).astype(BF16)
    logits = (jnp.dot(zh, wh, preferred_element_type=F32) + jnp.dot(zl, wh, preferred_element_type=F32)
              + jnp.dot(zh, wl, preferred_element_type=F32)) + b_ref[...]
    lane = lax.broadcasted_iota(jnp.int32, logits.shape, 1)
    lanef = lane.astype(F32)
    neg = jnp.float32(-jnp.inf)
    big = jnp.float32(1e9)
    gl = jnp.where(lane < N_EGROUPS, logits, neg)
    gmax = jnp.max(gl, axis=-1, keepdims=True)
    gidx = jnp.min(jnp.where(gl == gmax, lanef, big), axis=-1, keepdims=True)
    pg = 1.0 / jnp.sum(jnp.exp(gl - gmax), axis=-1, keepdims=True)
    lo = N_EGROUPS + EXPERTS_PER_GROUP * gidx
    el = jnp.where((lanef >= lo) & (lanef < lo + EXPERTS_PER_GROUP), logits, neg)
    t1 = jnp.max(el, axis=-1, keepdims=True)
    j1 = jnp.min(jnp.where(el == t1, lanef, big), axis=-1, keepdims=True)
    el2 = jnp.where(lanef == j1, neg, el)
    t2 = jnp.max(el2, axis=-1, keepdims=True)
    j2 = jnp.min(jnp.where(el2 == t2, lanef, big), axis=-1, keepdims=True)
    e21 = jnp.exp(t2 - t1)
    w1 = pg / (1.0 + e21)
    w2 = pg * e21 / (1.0 + e21)
    e1f = j1 - N_EGROUPS
    e2f = j2 - N_EGROUPS

    tm = x.shape[0]
    oh1 = (lanef == e1f).astype(F32)
    oh2 = (lanef == e2f).astype(F32)
    ri = lax.broadcasted_iota(jnp.int32, (tm, tm), 0)
    ci = lax.broadcasted_iota(jnp.int32, (tm, tm), 1)
    before = (ci < ri).astype(BF16)
    p1 = jnp.dot(before, oh1.astype(BF16), preferred_element_type=F32)
    p2 = jnp.dot(before, oh2.astype(BF16), preferred_element_type=F32)
    carry = carry_ref[...]
    c1 = jnp.sum(oh1, axis=0, keepdims=True)
    c2 = jnp.sum(oh2, axis=0, keepdims=True)
    rank1 = jnp.sum(oh1 * (carry + p1), axis=-1, keepdims=True)
    rank2 = jnp.sum(oh2 * (carry + c1 + p2), axis=-1, keepdims=True)
    carry = carry + c1 + c2
    carry_ref[...] = carry
    cnt_ref[...] = jnp.broadcast_to(carry, cnt_ref.shape).astype(jnp.int32)

    ids = jnp.where(lane == 0, e1f, jnp.where(lane == 1, e2f, jnp.where(lane == 2, rank1, jnp.where(lane == 3, rank2, 0.0))))
    id_ref[...] = ids.astype(jnp.int32)
    wt_ref[...] = jnp.where(lane == 0, w1, jnp.where(lane == 1, w2, 0.0))


def router(x1, gain, w_r, b_r):
    tm = 512
    return pl.pallas_call(
        _router_kernel,
        out_shape=(jax.ShapeDtypeStruct((T * PACK_ROWS, LANES), jnp.uint32),
                   jax.ShapeDtypeStruct((T, LANES), jnp.int32),
                   jax.ShapeDtypeStruct((T, LANES), F32),
                   jax.ShapeDtypeStruct((8, LANES), jnp.int32)),
        grid=(T // tm,),
        in_specs=[pl.BlockSpec((tm, D), lambda i: (i, 0)),
                  pl.BlockSpec((1, D), lambda i: (0, 0)),
                  pl.BlockSpec((D, LANES), lambda i: (0, 0)),
                  pl.BlockSpec((1, LANES), lambda i: (0, 0))],
        out_specs=(pl.BlockSpec((tm * PACK_ROWS, LANES), lambda i: (i, 0)),
                   pl.BlockSpec((tm, LANES), lambda i: (i, 0)),
                   pl.BlockSpec((tm, LANES), lambda i: (i, 0)),
                   pl.BlockSpec((8, LANES), lambda i: (0, 0))),
        scratch_shapes=[pltpu.VMEM((1, LANES), F32)],
        compiler_params=_cparams(("arbitrary",)),
        name="ffn_norm_router",
    )(x1, gain, w_r, b_r)


DISPATCH_TB = 1024
N_ZERO_FILLS = 2 * N_EXPERTS


def _dispatch_kernel(dest_ref, zs_ref, h_ref, xs_hbm, zbuf, zsem, sem):
    step = pl.program_id(0)

    @pl.when(step == 0)
    def _():
        zbuf[...] = jnp.zeros_like(zbuf)

        def zero_copy(e):
            start = pl.multiple_of(jnp.maximum(zs_ref[e], 0) * PACK_ROWS, PACK_ROWS)
            return pltpu.make_async_copy(zbuf, xs_hbm.at[pl.ds(start, MOE_BLOCK * PACK_ROWS)], zsem.at[0])

        def zstart(e, c):
            @pl.when(zs_ref[e] >= 0)
            def _():
                zero_copy(e).start()
            return c

        def zwait(e, c):
            @pl.when(zs_ref[e] >= 0)
            def _():
                zero_copy(e).wait()
            return c

        lax.fori_loop(0, N_ZERO_FILLS, zstart, 0)
        lax.fori_loop(0, N_ZERO_FILLS, zwait, 0)

    def row_copy(n, k):
        a = (step * DISPATCH_TB + n) * TOP_K + k
        src = h_ref.at[pl.ds(pl.multiple_of(n * PACK_ROWS, PACK_ROWS), PACK_ROWS)]
        dst = xs_hbm.at[pl.ds(pl.multiple_of(dest_ref[a] * PACK_ROWS, PACK_ROWS), PACK_ROWS)]
        return pltpu.make_async_copy(src, dst, sem.at[0])

    def issue(n, c):
        for k in range(TOP_K):
            row_copy(n, k).start(priority=k)
        return c

    def drain(n, c):
        for k in range(TOP_K):
            row_copy(n, k).wait()
        return c

    lax.fori_loop(0, DISPATCH_TB, issue, 0, unroll=8)
    lax.fori_loop(0, DISPATCH_TB, drain, 0, unroll=8)


def dispatch(dest, zero_start, hpk):
    grid_spec = pltpu.PrefetchScalarGridSpec(
        num_scalar_prefetch=2,
        grid=(T // DISPATCH_TB,),
        in_specs=[pl.BlockSpec((DISPATCH_TB * PACK_ROWS, LANES), lambda i, dst, zs: (i, 0))],
        out_specs=pl.BlockSpec(memory_space=pl.ANY),
        scratch_shapes=[pltpu.VMEM((MOE_BLOCK * PACK_ROWS, LANES), jnp.uint32),
                        pltpu.SemaphoreType.DMA((1,)),
                        pltpu.SemaphoreType.DMA((1,))],
    )
    return pl.pallas_call(
        _dispatch_kernel,
        out_shape=jax.ShapeDtypeStruct((MOE_ROWS * PACK_ROWS, LANES), jnp.uint32),
        grid_spec=grid_spec,
        compiler_params=_cparams(("arbitrary",)),
        name="moe_dispatch",
    )(dest, zero_start, hpk)


def _expert_kernel(be_ref, nu_ref, ord_ref, seq_ref, x_ref, wg_hbm, wu_hbm, wd_hbm, y_ref,
                   wg_st, wu_st, wd_st, wg_bf, wu_bf, wd_bf, sem):
    b = pl.program_id(0)

    def weight_copies(e, slot):
        return (pltpu.make_async_copy(wg_hbm.at[e], wg_st.at[slot], sem.at[slot, 0]),
                pltpu.make_async_copy(wu_hbm.at[e], wu_st.at[slot], sem.at[slot, 1]),
                pltpu.make_async_copy(wd_hbm.at[e], wd_st.at[slot], sem.at[slot, 2]))

    def start_fetch(n, slot):
        @pl.when(seq_ref[n] >= 0)
        def _():
            for cp in weight_copies(seq_ref[n], slot):
                cp.start()

    @pl.when(b < nu_ref[0])
    def _():
        n = ord_ref[b]
        slot = n % 2
        changed = jnp.logical_or(b == 0, be_ref[b] != be_ref[jnp.maximum(b - 1, 0)])

        @pl.when(b == 0)
        def _():
            start_fetch(0, 0)
            start_fetch(1, 1)

        @pl.when(changed)
        def _():
            cg, cu, cd = weight_copies(be_ref[b], slot)
            cg.wait()
            wg_bf[...] = wg_st[slot].astype(BF16)
            cu.wait()
            wu_bf[...] = wu_st[slot].astype(BF16)
            cd.wait()
            wd_bf[...] = wd_st[slot].astype(BF16)
            start_fetch(n + 2, slot)

        x = _unpack_bf16_pairs(x_ref, 0, MOE_BLOCK).astype(BF16)
        gate = jnp.dot(x, wg_bf[...], preferred_element_type=F32)
        up = jnp.dot(x, wu_bf[...], preferred_element_type=F32)
        hid = (jax.nn.silu(gate) * up).astype(BF16)
        y = jnp.dot(hid, wd_bf[...], preferred_element_type=F32)
        word = _pack_bf16_pairs(y.astype(BF16).astype(F32))
        for c in range(PACK_ROWS):
            y_ref[pl.ds(c, MOE_BLOCK, stride=PACK_ROWS), :] = word[:, c * LANES:(c + 1) * LANES]

    @pl.when(b >= nu_ref[0])
    def _():
        y_ref[...] = jnp.zeros_like(y_ref)


def experts(block_expert, n_used, block_ord, expert_seq, xs, w_gate, w_up, w_down):
    def blk(b, be, nu, od, sq):
        return jnp.minimum(b, nu[0] - 1)

    grid_spec = pltpu.PrefetchScalarGridSpec(
        num_scalar_prefetch=4,
        grid=(MOE_BLOCKS,),
        in_specs=[pl.BlockSpec((MOE_BLOCK * PACK_ROWS, LANES), lambda b, be, nu, od, sq: (blk(b, be, nu, od, sq), 0)),
                  pl.BlockSpec(memory_space=pl.ANY),
                  pl.BlockSpec(memory_space=pl.ANY),
                  pl.BlockSpec(memory_space=pl.ANY)],
        out_specs=pl.BlockSpec((MOE_BLOCK * PACK_ROWS, LANES), lambda b, be, nu, od, sq: (b, 0)),
        scratch_shapes=[pltpu.VMEM((2, D, EXPERT_FF), F32),
                        pltpu.VMEM((2, D, EXPERT_FF), F32),
                        pltpu.VMEM((2, EXPERT_FF, D), F32),
                        pltpu.VMEM((D, EXPERT_FF), BF16),
                        pltpu.VMEM((D, EXPERT_FF), BF16),
                        pltpu.VMEM((EXPERT_FF, D), BF16),
                        pltpu.SemaphoreType.DMA((2, 3))],
    )
    return pl.pallas_call(
        _expert_kernel,
        out_shape=jax.ShapeDtypeStruct((MOE_ROWS * PACK_ROWS, LANES), jnp.uint32),
        grid_spec=grid_spec,
        compiler_params=_cparams(("arbitrary",)),
        name="moe_experts",
    )(block_expert, n_used, block_ord, expert_seq, xs, w_gate, w_up, w_down)


COMBINE_ROWS = 512
COMBINE_SUB = 16


def _combine_kernel(dest_ref, ys_hbm, x_ref, wt_ref, g_ref, o_ref, ybuf, sem):
    s = pl.program_id(0)
    ns = pl.num_programs(0)
    slot = s % 2
    tb = COMBINE_SUB
    rows = COMBINE_ROWS
    nxt = jnp.minimum(s + 1, ns - 1)

    def row_copy(step, n, k, sl):
        tok = step * rows + n
        src = ys_hbm.at[pl.ds(pl.multiple_of(dest_ref[tok * TOP_K + k] * PACK_ROWS, PACK_ROWS), PACK_ROWS)]
        dst = ybuf.at[sl * TOP_K + k, pl.ds(pl.multiple_of(n * PACK_ROWS, PACK_ROWS), PACK_ROWS)]
        return pltpu.make_async_copy(src, dst, sem.at[sl])

    def wait_all(step, sl):
        def body(n, c):
            for k in range(TOP_K):
                row_copy(step, n, k, sl).wait()
            return c
        lax.fori_loop(0, rows, body, 0, unroll=8)

    @pl.when(s == 0)
    def _():
        def body(n, c):
            for k in range(TOP_K):
                row_copy(0, n, k, 0).start(priority=k)
            return c
        lax.fori_loop(0, rows, body, 0, unroll=8)

    wait_all(s, slot)

    g = g_ref[...]
    y0_ref = ybuf.at[slot * TOP_K]
    y1_ref = ybuf.at[slot * TOP_K + 1]
    for b in range(rows // tb):
        for n in range(b * tb, (b + 1) * tb):
            for k in range(TOP_K):
                row_copy(nxt, n, k, 1 - slot).start(priority=k)
        sub = slice(b * tb, (b + 1) * tb)
        w = wt_ref[sub, :]
        y0 = _unpack_bf16_pairs(y0_ref, b * tb, tb)
        y1 = _unpack_bf16_pairs(y1_ref, b * tb, tb)
        z = x_ref[sub, :] + (w[:, 0:1] * y0 + w[:, 1:2] * y1)
        ms = jnp.mean(z * z, axis=-1, keepdims=True)
        o_ref[sub, :] = z * lax.rsqrt(ms + NORM_EPS) * g

    @pl.when(s == ns - 1)
    def _():
        wait_all(nxt, 1 - slot)


def combine(dest, ys, x1, wts, gain):
    rows = COMBINE_ROWS
    grid_spec = pltpu.PrefetchScalarGridSpec(
        num_scalar_prefetch=1,
        grid=(T // rows,),
        in_specs=[pl.BlockSpec(memory_space=pl.ANY),
                  pl.BlockSpec((rows, D), lambda s, dst: (s, 0)),
                  pl.BlockSpec((rows, LANES), lambda s, dst: (s, 0)),
                  pl.BlockSpec((1, D), lambda s, dst: (0, 0))],
        out_specs=pl.BlockSpec((rows, D), lambda s, dst: (s, 0)),
        scratch_shapes=[pltpu.VMEM((2 * TOP_K, rows * PACK_ROWS, LANES), jnp.uint32),
                        pltpu.SemaphoreType.DMA((2,))],
    )
    return pl.pallas_call(
        _combine_kernel,
        out_shape=jax.ShapeDtypeStruct((T, D), F32),
        grid_spec=grid_spec,
        compiler_params=_cparams(("arbitrary",)),
        name="moe_combine_final_norm",
    )(dest, ys, x1, wts, gain)


def dispatch_plan(ids, counts):
    experts_ = jnp.arange(N_EXPERTS, dtype=jnp.int32)
    padded = (counts + MOE_BLOCK - 1) // MOE_BLOCK * MOE_BLOCK
    pad_end = jnp.cumsum(padded)
    pad_start = pad_end - padded
    e = ids[:, :TOP_K]
    start_of = jnp.sum(jnp.where(e[:, :, None] == experts_[None, None, :], pad_start[None, None, :], 0), axis=-1)
    dest = (start_of + ids[:, TOP_K:2 * TOP_K]).reshape(N_ASSIGN).astype(jnp.int32)
    n_used = pad_end[-1] // MOE_BLOCK
    block_start = jnp.minimum(jnp.arange(MOE_BLOCKS, dtype=jnp.int32), n_used - 1) * MOE_BLOCK
    block_expert = jnp.sum((block_start[:, None] >= pad_end[None, :]).astype(jnp.int32), axis=1)
    block_expert = jnp.minimum(block_expert, N_EXPERTS - 1).astype(jnp.int32)
    tail = n_used + experts_
    zero_start = jnp.concatenate([jnp.where(counts > 0, pad_end - MOE_BLOCK, -1),
                                  jnp.where(tail < MOE_BLOCKS, tail * MOE_BLOCK, -1)]).astype(jnp.int32)
    present = counts > 0
    expert_ord = jnp.cumsum(present.astype(jnp.int32)) - 1
    slots = jnp.arange(N_EXPERTS + 2, dtype=jnp.int32)
    hit = present[None, :] & (expert_ord[None, :] == slots[:, None])
    expert_seq = jnp.where(jnp.any(hit, axis=1), jnp.sum(jnp.where(hit, experts_[None, :], 0), axis=1), -1)
    block_ord = jnp.sum(jnp.where(block_expert[:, None] == experts_[None, :], expert_ord[None, :], 0), axis=1)
    return (block_expert, n_used.astype(jnp.int32).reshape(1), dest, zero_start,
            block_ord.astype(jnp.int32), expert_seq.astype(jnp.int32))


def kernel(x, norm_mix, w_in, b_gate, ssm_a_re, ssm_a_im, ssm_log_step, ssm_b_re, ssm_b_im, ssm_c_re, ssm_c_im, ssm_d, w_glu, b_glu, w_up_attn, w_up_ssm, w_out, norm_ffn, w_router_group, b_router_group, w_router_expert, b_router_expert, w_expert_gate, w_expert_up, w_expert_down, norm_final):
    x2 = x.reshape(T, D)
    h = norm_permute(x2, norm_mix.reshape(1, D)).reshape(T, D)
    w_in_l = w_in.reshape(D, IN_COLS)
    qkv = proj(h, w_in_l, 0, 3 * QKV_COLS, tn=QKV_COLS, name="proj_qkv")
    u = proj(h, w_in_l, 3 * QKV_COLS, SSM_W, tn=SSM_W, name="proj_ssm_in")

    attn = attention(qkv.reshape(R, NI, 3 * QKV_COLS)).reshape(T, ATTN_OUT)

    G = SSM_W // SSM_CH
    wb, wc, pows = ssm_params(
        ssm_a_re.reshape(G, SSM_STATE).astype(F32), ssm_a_im.reshape(G, SSM_STATE).astype(F32),
        ssm_log_step.reshape(G),
        ssm_b_re.reshape(G, SSM_STATE, SSM_CH).astype(F32), ssm_b_im.reshape(G, SSM_STATE, SSM_CH).astype(F32),
        ssm_c_re.reshape(G, SSM_CH, SSM_STATE), ssm_c_im.reshape(G, SSM_CH, SSM_STATE))
    yg = ssm_scan(u.reshape(R, NI, SSM_W), wb, wc, pows, ssm_d.reshape(1, SSM_W).astype(F32))
    ssm = glu(yg.reshape(T, SSM_W), w_glu.reshape(SSM_W, SSM_W), b_glu.reshape(1, SSM_W))

    merged = merge(h, attn, ssm, w_in_l, b_gate.reshape(1, 2 * D),
                   w_up_attn.reshape(ATTN_OUT, D), w_up_ssm.reshape(SSM_W, D))
    x1 = out_proj(merged.reshape(R, NI, D), w_out.reshape(D, D), x2)

    w_r = jnp.concatenate([w_router_group.reshape(D, N_EGROUPS), w_router_expert.reshape(D, N_EXPERTS),
                           jnp.zeros((D, LANES - N_EGROUPS - N_EXPERTS), F32)], axis=1)
    b_r = jnp.concatenate([b_router_group.reshape(1, N_EGROUPS), b_router_expert.reshape(1, N_EXPERTS),
                           jnp.zeros((1, LANES - N_EGROUPS - N_EXPERTS), F32)], axis=1)
    hpk, ids, wts, counts = router(x1, norm_ffn.reshape(1, D), w_r, b_r)

    block_expert, n_used, dest, zero_start, block_ord, expert_seq = dispatch_plan(ids[:, :2 * TOP_K], counts[0, :N_EXPERTS])
    xs = dispatch(dest, zero_start, hpk)
    ys = experts(block_expert, n_used, block_ord, expert_seq, xs,
                 w_expert_gate.reshape(N_EXPERTS, D, EXPERT_FF), w_expert_up.reshape(N_EXPERTS, D, EXPERT_FF),
                 w_expert_down.reshape(N_EXPERTS, EXPERT_FF, D))
    out = combine(dest, ys, x1, wts, norm_final.reshape(1, D))
    return out.reshape(1, T, D)
```

```python
import functools

import jax
import jax.numpy as jnp
from jax import lax
from jax.experimental import pallas as pl
from jax.experimental.pallas import tpu as pltpu

F32 = jnp.float32
BF16 = jnp.bfloat16

T = 8192
D = 2048
R = 16
NI = T // R
HEAD_DIM = 64
N_HEAD_SLOTS = 8
DILATIONS = (1, 4, 16)
ATTN_BLOCK = 128
QKV_COLS = 1536
ATTN_OUT = 512
SSM_W = 1024
SSM_STATE = 64
SSM_CH = 16
IN_COLS = 3 * QKV_COLS + SSM_W + 2 * D
N_EXPERTS = 32
N_EGROUPS = 4
EXPERTS_PER_GROUP = 8
TOP_K = 2
EXPERT_FF = 512
NORM_EPS = 1e-6
LANES = 128
VMEM_LIMIT = 48 * 1024 * 1024

MOE_BLOCK = 256
N_ASSIGN = T * TOP_K
MOE_BLOCKS = N_ASSIGN // MOE_BLOCK + N_EXPERTS
MOE_ROWS = MOE_BLOCKS * MOE_BLOCK


def _cparams(sem):
    return pltpu.CompilerParams(dimension_semantics=sem, vmem_limit_bytes=VMEM_LIMIT)


N_SLABS = D // LANES
NORM_TB = 64
NORM_CHUNK = 64
NORM_PITCH = R + 8


def _norm_permute_kernel(x_ref, g_ref, h_ref, slab):
    g = g_ref[...]

    def chunk(t, c):
        rows = pl.ds(pl.multiple_of(t * NORM_CHUNK, NORM_CHUNK), NORM_CHUNK)
        x = x_ref[rows, :]
        ms = jnp.mean(x * x, axis=-1, keepdims=True)
        hn = x * lax.rsqrt(ms + NORM_EPS) * g
        for k in range(NORM_CHUNK // R):
            dst = pl.ds(pl.multiple_of((t * (NORM_CHUNK // R) + k) * NORM_PITCH, 8), R)
            for s_ in range(N_SLABS):
                slab[s_, dst, :] = hn[k * R:(k + 1) * R, s_ * LANES:(s_ + 1) * LANES]
        return c

    lax.fori_loop(0, NORM_TB * R // NORM_CHUNK, chunk, 0)
    for r in range(R):
        pieces = [slab[s_, pl.ds(r, NORM_TB, stride=NORM_PITCH), :] for s_ in range(N_SLABS)]
        h_ref[r] = jnp.concatenate(pieces, axis=1).astype(BF16)


def norm_permute(x2, gain):
    return pl.pallas_call(
        _norm_permute_kernel,
        out_shape=jax.ShapeDtypeStruct((R, NI, D), BF16),
        grid=(NI // NORM_TB,),
        in_specs=[pl.BlockSpec((NORM_TB * R, D), lambda i: (i, 0)),
                  pl.BlockSpec((1, D), lambda i: (0, 0))],
        out_specs=pl.BlockSpec((R, NORM_TB, D), lambda i: (0, i, 0)),
        scratch_shapes=[pltpu.VMEM((N_SLABS, NORM_TB * NORM_PITCH, LANES), F32)],
        compiler_params=_cparams(("arbitrary",)),
        name="norm_permute",
    )(x2, gain)


MXU_N = 256


def _stationary_weight_tile(w_hbm, col0, stage_ref, wbf_ref, sem):
    j = pl.program_id(0)
    tn = stage_ref.shape[1]

    def copy(jj):
        c0 = pl.multiple_of(col0 + jj * tn, LANES)
        return pltpu.make_async_copy(w_hbm.at[:, pl.ds(c0, tn)], stage_ref, sem)

    @pl.when(pl.program_id(1) == 0)
    def _():
        @pl.when(j == 0)
        def _():
            copy(0).start()

        copy(j).wait()
        wbf_ref[...] = stage_ref[...].astype(BF16)

        @pl.when(j + 1 < pl.num_programs(0))
        def _():
            copy(j + 1).start()


def _proj_kernel(a_ref, w_hbm, o_ref, stage_ref, wbf_ref, sem, *, col_off):
    _stationary_weight_tile(w_hbm, col_off, stage_ref, wbf_ref, sem.at[0])
    a = a_ref[...]
    for c in range(o_ref.shape[1] // MXU_N):
        cols = slice(c * MXU_N, (c + 1) * MXU_N)
        o_ref[:, cols] = jnp.dot(a, wbf_ref[:, cols], preferred_element_type=F32)


def proj(h, w_in, col_off, n_cols, tn, name="proj"):
    tm = 1024
    return pl.pallas_call(
        functools.partial(_proj_kernel, col_off=col_off),
        out_shape=jax.ShapeDtypeStruct((T, n_cols), F32),
        grid=(n_cols // tn, T // tm),
        in_specs=[pl.BlockSpec((tm, D), lambda j, i: (i, 0)),
                  pl.BlockSpec(memory_space=pl.ANY)],
        out_specs=pl.BlockSpec((tm, tn), lambda j, i: (i, j)),
        scratch_shapes=[pltpu.VMEM((D, tn), F32), pltpu.VMEM((D, tn), BF16), pltpu.SemaphoreType.DMA((1,))],
        compiler_params=_cparams(("arbitrary", "arbitrary")),
        name=name,
    )(h, w_in)


def _seq_index_maps(d):
    nseg = R // d
    qlen = ATTN_BLOCK // nseg
    return nseg, qlen


def _bias_matrices(d, hp):
    nseg, qlen = _seq_index_maps(d)
    klen = 2 * qlen
    row = lax.broadcasted_iota(jnp.int32, (2 * ATTN_BLOCK, 2 * ATTN_BLOCK), 0)
    col = lax.broadcasted_iota(jnp.int32, (2 * ATTN_BLOCK, 2 * ATTN_BLOCK), 1)
    rho = row % ATTN_BLOCK
    jq = (rho % qlen) * nseg + rho // qlen
    jk = ((col % klen) - qlen) * nseg + col // klen
    steps = jq - jk
    valid = (steps >= 0) & (steps <= ATTN_BLOCK)
    head = 2 * hp + row // ATTN_BLOCK
    slope = lax.bitcast_convert_type((127 - (head + 1)) << 23, F32)
    bias = -slope * (d * steps).astype(F32)
    neg = jnp.float32(-jnp.inf)
    return jnp.where(valid, bias, neg), jnp.where(valid & (jk >= 0), bias, neg)


def _attend_pair(q, k, v, bias):
    lane = lax.broadcasted_iota(jnp.int32, (ATTN_BLOCK, LANES), 1)
    first = lane < HEAD_DIM
    zero = jnp.zeros_like(q)
    q2 = jnp.concatenate([jnp.where(first, q, zero), jnp.where(first, zero, q)], axis=0).astype(BF16)
    s = lax.dot_general(q2, k.astype(BF16), (((1,), (1,)), ((), ())), preferred_element_type=F32)
    s = s + bias
    m = jnp.max(s, axis=-1, keepdims=True)
    p = jnp.exp(s - m)
    l = jnp.sum(p, axis=-1, keepdims=True)
    o2 = jnp.dot(p.astype(BF16), v.astype(BF16), preferred_element_type=F32)
    o = jnp.where(first, o2[:ATTN_BLOCK], o2[ATTN_BLOCK:])
    m_b = jnp.where(first, m[:ATTN_BLOCK], m[ATTN_BLOCK:])
    l_b = jnp.where(first, l[:ATTN_BLOCK], l[ATTN_BLOCK:])
    return o, m_b, l_b


def _attn_kernel(q_ref, kp_ref, kc_ref, vp_ref, vc_ref, o_ref, kbuf, vbuf, obuf, mbuf, lbuf, bias_ref):
    hp = pl.program_id(0)
    it = pl.program_id(1)
    g = pl.program_id(2)
    scale = HEAD_DIM ** -0.5

    for gi, d in enumerate(DILATIONS):
        nseg, qlen = _seq_index_maps(d)
        klen = 2 * qlen
        nblk = ATTN_BLOCK // qlen

        @pl.when(g == gi)
        def _(gi=gi, d=d, nseg=nseg, qlen=qlen, klen=klen, nblk=nblk):
            if nseg > 1:
                kbuf[:, :ATTN_BLOCK, :] = kp_ref[...]
                kbuf[:, ATTN_BLOCK:, :] = kc_ref[...]
                vbuf[:, :ATTN_BLOCK, :] = vp_ref[...]
                vbuf[:, ATTN_BLOCK:, :] = vc_ref[...]

            @pl.when(it == 0)
            def _():
                b_reg, b_first = _bias_matrices(d, hp)
                bias_ref[gi, 0] = b_reg
                bias_ref[gi, 1] = b_first

            def block(idx, carry):
                rd = idx // nblk
                bb = idx % nblk
                q0 = pl.multiple_of(bb * qlen, qlen)
                k0 = pl.multiple_of(ATTN_BLOCK + bb * qlen - qlen, qlen)
                qs, ks, vs = [], [], []
                for m_ in range(nseg):
                    rr = rd + d * m_
                    qs.append(q_ref[rr, pl.ds(q0, qlen), :])
                    if nseg > 1:
                        ks.append(kbuf[rr, pl.ds(k0, klen), :])
                        vs.append(vbuf[rr, pl.ds(k0, klen), :])
                    else:
                        ks += [kp_ref[rr], kc_ref[rr]]
                        vs += [vp_ref[rr], vc_ref[rr]]
                q = jnp.concatenate(qs, axis=0) * scale
                k = jnp.concatenate(ks, axis=0)
                v = jnp.concatenate(vs, axis=0)
                is_first = jnp.logical_and(it == 0, bb == 0)
                bias = bias_ref[gi, jnp.where(is_first, 1, 0)]
                o, mx, den = _attend_pair(q, k, v, bias)
                for m_ in range(nseg):
                    rr = rd + d * m_
                    seg = slice(m_ * qlen, (m_ + 1) * qlen)
                    obuf[gi, rr, pl.ds(q0, qlen), :] = o[seg]
                    mbuf[gi, rr, pl.ds(q0, qlen), :] = mx[seg]
                    lbuf[gi, rr, pl.ds(q0, qlen), :] = den[seg]
                return carry

            lax.fori_loop(0, d * nblk, block, 0, unroll=True)

    @pl.when(g == len(DILATIONS) - 1)
    def _():
        for r in range(R):
            m0, m1, m2 = mbuf[0, r], mbuf[1, r], mbuf[2, r]
            mx = jnp.maximum(jnp.maximum(m0, m1), m2)
            e0, e1, e2 = jnp.exp(m0 - mx), jnp.exp(m1 - mx), jnp.exp(m2 - mx)
            den = e0 * lbuf[0, r] + e1 * lbuf[1, r] + e2 * lbuf[2, r]
            num = e0 * obuf[0, r] + e1 * obuf[1, r] + e2 * obuf[2, r]
            o_ref[r] = (num / den).astype(BF16)


def attention(qkv3):
    n_hp = N_HEAD_SLOTS // 2
    n_it = NI // ATTN_BLOCK
    ng = len(DILATIONS)
    cb = QKV_COLS // LANES

    def cur(base):
        return pl.BlockSpec((R, ATTN_BLOCK, LANES), lambda hp, it, g: (0, it, base + g * n_hp + hp))

    def prev(base):
        return pl.BlockSpec((R, ATTN_BLOCK, LANES),
                            lambda hp, it, g: (0, jnp.maximum(it - 1, 0), base + g * n_hp + hp))

    return pl.pallas_call(
        _attn_kernel,
        out_shape=jax.ShapeDtypeStruct((R, NI, ATTN_OUT), BF16),
        grid=(n_hp, n_it, ng),
        in_specs=[cur(0), prev(cb), cur(cb), prev(2 * cb), cur(2 * cb)],
        out_specs=pl.BlockSpec((R, ATTN_BLOCK, LANES), lambda hp, it, g: (0, it, hp)),
        scratch_shapes=[pltpu.VMEM((R, 2 * ATTN_BLOCK, LANES), F32),
                        pltpu.VMEM((R, 2 * ATTN_BLOCK, LANES), F32),
                        pltpu.VMEM((ng, R, ATTN_BLOCK, LANES), F32),
                        pltpu.VMEM((ng, R, ATTN_BLOCK, LANES), F32),
                        pltpu.VMEM((ng, R, ATTN_BLOCK, LANES), F32),
                        pltpu.VMEM((ng, 2, 2 * ATTN_BLOCK, 2 * ATTN_BLOCK), F32)],
        compiler_params=_cparams(("arbitrary", "arbitrary", "arbitrary")),
        name="dilated_attention",
    )(qkv3, qkv3, qkv3, qkv3, qkv3)


SSM_SLAB = 256
SSM_SLABS = SSM_W // SSM_SLAB
SLAB_STATES = SSM_SLAB // SSM_CH * SSM_STATE
SSM_TI = 128
SSM_MM_CHUNK = 4


def _ssm_kernel(u_ref, wb_ref, wc_ref, pw_ref, dsk_ref, o_ref, s_ref, zs_ref, zc_ref, tr_ref, zt_ref):
    ic = pl.program_id(1)
    ns = SLAB_STATES

    @pl.when(ic == 0)
    def _():
        zc_ref[...] = jnp.zeros_like(zc_ref)

    n_chunks = R // SSM_MM_CHUNK
    n_tiles = SSM_TI // 8
    wb = wb_ref[0]
    wc = wc_ref[0]
    dsk = dsk_ref[...]
    arb = jnp.broadcast_to(pw_ref[0, 0:1, :ns], (8, ns))
    aib = jnp.broadcast_to(pw_ref[0, 0:1, ns:], (8, ns))


    def bu_chunk(c):
        lo = c * SSM_MM_CHUNK
        uc = u_ref[lo:lo + SSM_MM_CHUNK].reshape(SSM_MM_CHUNK * SSM_TI, SSM_SLAB)
        bu = jnp.dot(uc.astype(BF16), wb, preferred_element_type=F32)
        s_ref[lo:lo + SSM_MM_CHUNK] = bu.reshape(SSM_MM_CHUNK, SSM_TI, 2 * ns)

    def local_chunk(c):
        lo = c * SSM_MM_CHUNK
        first = max(lo, 1)
        for t in range(n_tiles):
            rows = slice(t * 8, (t + 1) * 8)
            pr = s_ref[first - 1, rows, :ns]
            pi = s_ref[first - 1, rows, ns:]
            for r in range(first, lo + SSM_MM_CHUNK):
                nr = s_ref[r, rows, :ns] + (arb * pr - aib * pi)
                ni = s_ref[r, rows, ns:] + (arb * pi + aib * pr)
                s_ref[r, rows, :ns] = nr
                s_ref[r, rows, ns:] = ni
                pr, pi = nr, ni

    bu_chunk(0)
    for c in range(n_chunks):
        if c + 1 < n_chunks:
            bu_chunk(c + 1)
        local_chunk(c)

    n_ch = ns // LANES
    for c in range(n_ch):
        tr_ref[0, c * 8:c * 8 + 1, :] = pw_ref[0, R - 1:R, c * LANES:(c + 1) * LANES]
        tr_ref[1, c * 8:c * 8 + 1, :] = pw_ref[0, R - 1:R, ns + c * LANES:ns + (c + 1) * LANES]
    a16r = tr_ref[0, pl.ds(0, n_ch, stride=8), :]
    a16i = tr_ref[1, pl.ds(0, n_ch, stride=8), :]

    def ztile(t, carry):
        zr, zi = carry
        rows = pl.ds(pl.multiple_of(t * 8, 8), 8)
        for c in range(n_ch):
            tr_ref[0, c * 8:(c + 1) * 8, :] = s_ref[R - 1, rows, c * LANES:(c + 1) * LANES]
            tr_ref[1, c * 8:(c + 1) * 8, :] = s_ref[R - 1, rows, ns + c * LANES:ns + (c + 1) * LANES]
        for i in range(8):
            zt_ref[0, pl.ds(i, n_ch, stride=8), :] = zr
            zt_ref[1, pl.ds(i, n_ch, stride=8), :] = zi
            er = tr_ref[0, pl.ds(i, n_ch, stride=8), :]
            ei = tr_ref[1, pl.ds(i, n_ch, stride=8), :]
            zr, zi = a16r * zr - a16i * zi + er, a16r * zi + a16i * zr + ei
        for c in range(n_ch):
            zs_ref[rows, c * LANES:(c + 1) * LANES] = zt_ref[0, c * 8:(c + 1) * 8, :]
            zs_ref[rows, ns + c * LANES:ns + (c + 1) * LANES] = zt_ref[1, c * 8:(c + 1) * 8, :]
        return zr, zi

    zr_end, zi_end = lax.fori_loop(0, SSM_TI // 8, ztile, (zc_ref[0], zc_ref[1]))
    zc_ref[0] = zr_end
    zc_ref[1] = zi_end

    def fix_chunk(c):
        lo = c * SSM_MM_CHUNK
        for r in range(lo, lo + SSM_MM_CHUNK):
            prb = jnp.broadcast_to(pw_ref[0, r:r + 1, :ns], (8, ns))
            pib = jnp.broadcast_to(pw_ref[0, r:r + 1, ns:], (8, ns))
            for t in range(n_tiles):
                rows = slice(t * 8, (t + 1) * 8)
                zr = zs_ref[rows, :ns]
                zi = zs_ref[rows, ns:]
                s_ref[r, rows, :ns] = s_ref[r, rows, :ns] + (prb * zr - pib * zi)
                s_ref[r, rows, ns:] = s_ref[r, rows, ns:] + (prb * zi + pib * zr)

    def out_chunk(c):
        lo = c * SSM_MM_CHUNK
        xs = s_ref[lo:lo + SSM_MM_CHUNK].reshape(SSM_MM_CHUNK * SSM_TI, 2 * ns)
        y = jnp.dot(xs.astype(BF16), wc, preferred_element_type=F32)
        y = y.reshape(SSM_MM_CHUNK, SSM_TI, SSM_SLAB) + dsk * u_ref[lo:lo + SSM_MM_CHUNK]
        o_ref[lo:lo + SSM_MM_CHUNK] = jax.nn.gelu(y)

    fix_chunk(0)
    for c in range(n_chunks):
        if c + 1 < n_chunks:
            fix_chunk(c + 1)
        out_chunk(c)


def ssm_scan(u3, wb, wc, pows, dskip):
    ns2 = 2 * SLAB_STATES
    return pl.pallas_call(
        _ssm_kernel,
        out_shape=jax.ShapeDtypeStruct((R, NI, SSM_W), F32),
        grid=(SSM_SLABS, NI // SSM_TI),
        in_specs=[pl.BlockSpec((R, SSM_TI, SSM_SLAB), lambda kb, ic: (0, ic, kb)),
                  pl.BlockSpec((1, SSM_SLAB, ns2), lambda kb, ic: (kb, 0, 0)),
                  pl.BlockSpec((1, ns2, SSM_SLAB), lambda kb, ic: (kb, 0, 0)),
                  pl.BlockSpec((1, R, ns2), lambda kb, ic: (kb, 0, 0)),
                  pl.BlockSpec((1, SSM_SLAB), lambda kb, ic: (0, kb))],
        out_specs=pl.BlockSpec((R, SSM_TI, SSM_SLAB), lambda kb, ic: (0, ic, kb)),
        scratch_shapes=[pltpu.VMEM((R, SSM_TI, ns2), F32),
                        pltpu.VMEM((SSM_TI, ns2), F32),
                        pltpu.VMEM((2, SLAB_STATES // LANES, LANES), F32),
                        pltpu.VMEM((2, 8 * SLAB_STATES // LANES, LANES), F32),
                        pltpu.VMEM((2, 8 * SLAB_STATES // LANES, LANES), F32)],
        compiler_params=_cparams(("arbitrary", "arbitrary")),
        name="s5_ssm",
    )(u3, wb, wc, pows, dskip)


def ssm_params(a_re, a_im, log_step, b_re, b_im, c_re, c_im):
    G, P, H = SSM_W // SSM_CH, SSM_STATE, SSM_CH
    gs = SSM_SLAB // SSM_CH
    step = jnp.exp(log_step.astype(F32))[:, None]
    kk = jnp.arange(1, R + 1, dtype=F32)[:, None, None]
    mag = jnp.exp(kk * (a_re * step))
    ang = kk * (a_im * step)
    pw_re, pw_im = mag * jnp.cos(ang), mag * jnp.sin(ang)
    abar_re, abar_im = pw_re[0], pw_im[0]
    nr, ni = abar_re - 1.0, abar_im
    den = a_re * a_re + a_im * a_im
    f_re = (nr * a_re + ni * a_im) / den
    f_im = (ni * a_re - nr * a_im) / den
    bbar_re = f_re[..., None] * b_re - f_im[..., None] * b_im
    bbar_im = f_re[..., None] * b_im + f_im[..., None] * b_re
    rows_g = jnp.arange(gs * H, dtype=jnp.int32) // H
    cols_g = jnp.arange(gs * P, dtype=jnp.int32) // P
    b_t = jnp.concatenate([jnp.swapaxes(bbar_re, 1, 2).reshape(SSM_SLABS, gs * H, P),
                           jnp.swapaxes(bbar_im, 1, 2).reshape(SSM_SLABS, gs * H, P)], axis=-1)
    tile_p = (jnp.arange(2 * P, dtype=jnp.int32)[:, None] ==
              (jnp.arange(2 * gs * P, dtype=jnp.int32) // (gs * P) * P + jnp.arange(2 * gs * P, dtype=jnp.int32) % P)[None, :])
    mask_b = rows_g[:, None] == jnp.tile(cols_g, 2)[None, :]
    wb = jnp.where(mask_b[None], jnp.einsum('kap,pc->kac', b_t.astype(BF16), tile_p.astype(BF16),
                                            preferred_element_type=F32), 0.0).astype(BF16)
    c_t = jnp.concatenate([jnp.swapaxes(c_re.astype(F32), 1, 2).reshape(SSM_SLABS, gs * P, H),
                           -jnp.swapaxes(c_im.astype(F32), 1, 2).reshape(SSM_SLABS, gs * P, H)], axis=1)
    tile_h = jnp.arange(H, dtype=jnp.int32)[:, None] == (jnp.arange(gs * H, dtype=jnp.int32) % H)[None, :]
    mask_c = jnp.tile(cols_g, 2)[:, None] == rows_g[None, :]
    wc = jnp.where(mask_c[None], jnp.einsum('kah,hc->kac', c_t.astype(BF16), tile_h.astype(BF16),
                                            preferred_element_type=F32), 0.0).astype(BF16)
    pows = jnp.concatenate([pw_re.reshape(R, SSM_SLABS, gs * P), pw_im.reshape(R, SSM_SLABS, gs * P)], axis=-1)
    return wb, wc, jnp.transpose(pows, (1, 0, 2))


def _glu_kernel(a_ref, w_ref, b_ref, o_ref, wbf_ref):
    @pl.when(pl.program_id(0) == 0)
    def _():
        wbf_ref[...] = w_ref[...].astype(BF16)

    a = a_ref[...].astype(BF16)
    for c in range(SSM_W // MXU_N):
        cols = slice(c * MXU_N, (c + 1) * MXU_N)
        acc = jnp.dot(a, wbf_ref[:, cols], preferred_element_type=F32)
        o_ref[:, cols] = (a_ref[:, cols] * jax.nn.sigmoid(acc + b_ref[:, cols])).astype(BF16)


def glu(yg, w_glu, b_glu):
    tm = 1024
    return pl.pallas_call(
        _glu_kernel,
        out_shape=jax.ShapeDtypeStruct((T, SSM_W), BF16),
        grid=(T // tm,),
        in_specs=[pl.BlockSpec((tm, SSM_W), lambda i: (i, 0)),
                  pl.BlockSpec((SSM_W, SSM_W), lambda i: (0, 0)),
                  pl.BlockSpec((1, SSM_W), lambda i: (0, 0))],
        out_specs=pl.BlockSpec((tm, SSM_W), lambda i: (i, 0)),
        scratch_shapes=[pltpu.VMEM((SSM_W, SSM_W), BF16)],
        compiler_params=_cparams(("arbitrary",)),
        name="ssm_glu",
    )(yg, w_glu, b_glu)


def _merge_kernel(h_ref, at_ref, ss_ref, win_hbm, ba_ref, bs_ref, wa_hbm, ws_hbm, o_ref,
                  wga_st, wgs_st, wa_st, ws_st, wga_bf, wgs_bf, wa_bf, ws_bf, sem):
    gate_col = 3 * QKV_COLS + SSM_W
    _stationary_weight_tile(win_hbm, gate_col, wga_st, wga_bf, sem.at[0])
    _stationary_weight_tile(win_hbm, gate_col + D, wgs_st, wgs_bf, sem.at[1])
    _stationary_weight_tile(wa_hbm, 0, wa_st, wa_bf, sem.at[2])
    _stationary_weight_tile(ws_hbm, 0, ws_st, ws_bf, sem.at[3])
    h = h_ref[...]
    at = at_ref[...]
    ss = ss_ref[...]
    for c in range(o_ref.shape[1] // MXU_N):
        cols = slice(c * MXU_N, (c + 1) * MXU_N)
        ga = jax.nn.sigmoid(jnp.dot(h, wga_bf[:, cols], preferred_element_type=F32) + ba_ref[:, cols])
        a = jnp.dot(at, wa_bf[:, cols], preferred_element_type=F32)
        gs = jax.nn.sigmoid(jnp.dot(h, wgs_bf[:, cols], preferred_element_type=F32) + bs_ref[:, cols])
        s = jnp.dot(ss, ws_bf[:, cols], preferred_element_type=F32)
        o_ref[:, cols] = (ga * a + gs * s).astype(BF16)


def merge(h, attn, ssm, w_in, b_gate, w_up_attn, w_up_ssm):
    tm, tn = 1024, 512
    nj = D // tn
    any_space = pl.BlockSpec(memory_space=pl.ANY)
    return pl.pallas_call(
        _merge_kernel,
        out_shape=jax.ShapeDtypeStruct((T, D), BF16),
        grid=(nj, T // tm),
        in_specs=[pl.BlockSpec((tm, D), lambda j, i: (i, 0)),
                  pl.BlockSpec((tm, ATTN_OUT), lambda j, i: (i, 0)),
                  pl.BlockSpec((tm, SSM_W), lambda j, i: (i, 0)),
                  any_space,
                  pl.BlockSpec((1, tn), lambda j, i: (0, j)),
                  pl.BlockSpec((1, tn), lambda j, i: (0, j + nj)),
                  any_space,
                  any_space],
        out_specs=pl.BlockSpec((tm, tn), lambda j, i: (i, j)),
        scratch_shapes=[pltpu.VMEM((D, tn), F32), pltpu.VMEM((D, tn), F32),
                        pltpu.VMEM((ATTN_OUT, tn), F32), pltpu.VMEM((SSM_W, tn), F32),
                        pltpu.VMEM((D, tn), BF16), pltpu.VMEM((D, tn), BF16),
                        pltpu.VMEM((ATTN_OUT, tn), BF16), pltpu.VMEM((SSM_W, tn), BF16),
                        pltpu.SemaphoreType.DMA((4,))],
        compiler_params=_cparams(("arbitrary", "arbitrary")),
        name="gates_branch_merge",
    )(h, attn, ssm, w_in, b_gate, b_gate, w_up_attn, w_up_ssm)


OUTPROJ_TB = 64
OUTPROJ_PITCH = OUTPROJ_TB + 8


def _outproj_kernel(m_ref, w_hbm, x_ref, o_ref, slab, stage_ref, wbf_ref, sem):
    _stationary_weight_tile(w_hbm, 0, stage_ref, wbf_ref, sem.at[0])
    tb = OUTPROJ_TB
    a = m_ref[...].reshape(R * tb, D)
    per = MXU_N // LANES
    for c in range(o_ref.shape[1] // MXU_N):
        acc = jnp.dot(a, wbf_ref[:, c * MXU_N:(c + 1) * MXU_N], preferred_element_type=F32)
        for s_ in range(per):
            lanes = slice(s_ * LANES, (s_ + 1) * LANES)
            for r in range(R):
                slab[c * per + s_, r * OUTPROJ_PITCH:r * OUTPROJ_PITCH + tb, :] = acc[r * tb:(r + 1) * tb, lanes]
        for s_ in range(per):
            lanes = slice((c * per + s_) * LANES, (c * per + s_ + 1) * LANES)
            for i in range(tb):
                rows = slice(i * R, (i + 1) * R)
                o_ref[rows, lanes] = slab[c * per + s_, pl.ds(i, R, stride=OUTPROJ_PITCH), :] + x_ref[rows, lanes]


def out_proj(merged3, w_out, x2):
    tb, tn = OUTPROJ_TB, 1024
    return pl.pallas_call(
        _outproj_kernel,
        out_shape=jax.ShapeDtypeStruct((T, D), F32),
        grid=(D // tn, NI // tb),
        in_specs=[pl.BlockSpec((R, tb, D), lambda j, i: (0, i, 0)),
                  pl.BlockSpec(memory_space=pl.ANY),
                  pl.BlockSpec((tb * R, tn), lambda j, i: (i, j))],
        out_specs=pl.BlockSpec((tb * R, tn), lambda j, i: (i, j)),
        scratch_shapes=[pltpu.VMEM((tn // LANES, OUTPROJ_PITCH * R, LANES), F32),
                        pltpu.VMEM((D, tn), F32), pltpu.VMEM((D, tn), BF16),
                        pltpu.SemaphoreType.DMA((1,))],
        compiler_params=_cparams(("arbitrary", "arbitrary")),
        name="out_proj_residual",
    )(merged3, w_out, x2)


PACK_ROWS = 8
HALF_D = D // 2


def _pack_bf16_pairs(zf):
    top = lax.bitcast_convert_type(zf, jnp.uint32)
    return top[:, HALF_D:] | (top[:, :HALF_D] >> 16)


def _unpack_bf16_pairs(x_ref, first, n):
    lo, hi = [], []
    for c in range(PACK_ROWS):
        w = x_ref[pl.ds(first * PACK_ROWS + c, n, stride=PACK_ROWS), :]
        lo.append(lax.bitcast_convert_type(w << 16, F32))
        hi.append(lax.bitcast_convert_type(w & jnp.uint32(0xFFFF0000), F32))
    return jnp.concatenate(lo + hi, axis=1)


def _router_kernel(x_ref, g_ref, w_ref, b_ref, h_ref, id_ref, wt_ref, cnt_ref, carry_ref):
    step = pl.program_id(0)

    @pl.when(step == 0)
    def _():
        carry_ref[...] = jnp.zeros_like(carry_ref)

    x = x_ref[...]
    ms = jnp.mean(x * x, axis=-1, keepdims=True)
    z = x * lax.rsqrt(ms + NORM_EPS) * g_ref[...]
    zh = z.astype(BF16)
    zf = zh.astype(F32)
    word = _pack_bf16_pairs(zf)
    for c in range(PACK_ROWS):
        h_ref[pl.ds(c, x.shape[0], stride=PACK_ROWS), :] = word[:, c * LANES:(c + 1) * LANES]
    zl = (z - zf).astype(BF16)
    w = w_ref[...]
    wh = w.astype(BF16)
    wl = (w - wh.astype(F32)).astype(BF16)
    logits = (jnp.dot(zh, wh, preferred_element_type=F32) + jnp.dot(zl, wh, preferred_element_type=F32)
              + jnp.dot(zh, wl, preferred_element_type=F32)) + b_ref[...]
    lane = lax.broadcasted_iota(jnp.int32, logits.shape, 1)
    lanef = lane.astype(F32)
    neg = jnp.float32(-jnp.inf)
    big = jnp.float32(1e9)
    gl = jnp.where(lane < N_EGROUPS, logits, neg)
    gmax = jnp.max(gl, axis=-1, keepdims=True)
    gidx = jnp.min(jnp.where(gl == gmax, lanef, big), axis=-1, keepdims=True)
    pg = 1.0 / jnp.sum(jnp.exp(gl - gmax), axis=-1, keepdims=True)
    lo = N_EGROUPS + EXPERTS_PER_GROUP * gidx
    el = jnp.where((lanef >= lo) & (lanef < lo + EXPERTS_PER_GROUP), logits, neg)
    t1 = jnp.max(el, axis=-1, keepdims=True)
    j1 = jnp.min(jnp.where(el == t1, lanef, big), axis=-1, keepdims=True)
    el2 = jnp.where(lanef == j1, neg, el)
    t2 = jnp.max(el2, axis=-1, keepdims=True)
    j2 = jnp.min(jnp.where(el2 == t2, lanef, big), axis=-1, keepdims=True)
    e21 = jnp.exp(t2 - t1)
    w1 = pg / (1.0 + e21)
    w2 = pg * e21 / (1.0 + e21)
    e1f = j1 - N_EGROUPS
    e2f = j2 - N_EGROUPS

    tm = x.shape[0]
    oh1 = (lanef == e1f).astype(F32)
    oh2 = (lanef == e2f).astype(F32)
    ri = lax.broadcasted_iota(jnp.int32, (tm, tm), 0)
    ci = lax.broadcasted_iota(jnp.int32, (tm, tm), 1)
    before = (ci < ri).astype(BF16)
    p1 = jnp.dot(before, oh1.astype(BF16), preferred_element_type=F32)
    p2 = jnp.dot(before, oh2.astype(BF16), preferred_element_type=F32)
    carry = carry_ref[...]
    c1 = jnp.sum(oh1, axis=0, keepdims=True)
    c2 = jnp.sum(oh2, axis=0, keepdims=True)
    rank1 = jnp.sum(oh1 * (carry + p1), axis=-1, keepdims=True)
    rank2 = jnp.sum(oh2 * (carry + c1 + p2), axis=-1, keepdims=True)
    carry = carry + c1 + c2
    carry_ref[...] = carry
    cnt_ref[...] = jnp.broadcast_to(carry, cnt_ref.shape).astype(jnp.int32)

    ids = jnp.where(lane == 0, e1f, jnp.where(lane == 1, e2f, jnp.where(lane == 2, rank1, jnp.where(lane == 3, rank2, 0.0))))
    id_ref[...] = ids.astype(jnp.int32)
    wt_ref[...] = jnp.where(lane == 0, w1, jnp.where(lane == 1, w2, 0.0))


def router(x1, gain, w_r, b_r):
    tm = 512
    return pl.pallas_call(
        _router_kernel,
        out_shape=(jax.ShapeDtypeStruct((T * PACK_ROWS, LANES), jnp.uint32),
                   jax.ShapeDtypeStruct((T, LANES), jnp.int32),
                   jax.ShapeDtypeStruct((T, LANES), F32),
                   jax.ShapeDtypeStruct((8, LANES), jnp.int32)),
        grid=(T // tm,),
        in_specs=[pl.BlockSpec((tm, D), lambda i: (i, 0)),
                  pl.BlockSpec((1, D), lambda i: (0, 0)),
                  pl.BlockSpec((D, LANES), lambda i: (0, 0)),
                  pl.BlockSpec((1, LANES), lambda i: (0, 0))],
        out_specs=(pl.BlockSpec((tm * PACK_ROWS, LANES), lambda i: (i, 0)),
                   pl.BlockSpec((tm, LANES), lambda i: (i, 0)),
                   pl.BlockSpec((tm, LANES), lambda i: (i, 0)),
                   pl.BlockSpec((8, LANES), lambda i: (0, 0))),
        scratch_shapes=[pltpu.VMEM((1, LANES), F32)],
        compiler_params=_cparams(("arbitrary",)),
        name="ffn_norm_router",
    )(x1, gain, w_r, b_r)


DISPATCH_TB = 1024
N_ZERO_FILLS = 2 * N_EXPERTS


def _dispatch_kernel(dest_ref, zs_ref, h_ref, xs_hbm, zbuf, zsem, sem):
    step = pl.program_id(0)

    @pl.when(step == 0)
    def _():
        zbuf[...] = jnp.zeros_like(zbuf)

        def zero_copy(e):
            start = pl.multiple_of(jnp.maximum(zs_ref[e], 0) * PACK_ROWS, PACK_ROWS)
            return pltpu.make_async_copy(zbuf, xs_hbm.at[pl.ds(start, MOE_BLOCK * PACK_ROWS)], zsem.at[0])

        def zstart(e, c):
            @pl.when(zs_ref[e] >= 0)
            def _():
                zero_copy(e).start()
            return c

        def zwait(e, c):
            @pl.when(zs_ref[e] >= 0)
            def _():
                zero_copy(e).wait()
            return c

        lax.fori_loop(0, N_ZERO_FILLS, zstart, 0)
        lax.fori_loop(0, N_ZERO_FILLS, zwait, 0)

    def row_copy(n, k):
        a = (step * DISPATCH_TB + n) * TOP_K + k
        src = h_ref.at[pl.ds(pl.multiple_of(n * PACK_ROWS, PACK_ROWS), PACK_ROWS)]
        dst = xs_hbm.at[pl.ds(pl.multiple_of(dest_ref[a] * PACK_ROWS, PACK_ROWS), PACK_ROWS)]
        return pltpu.make_async_copy(src, dst, sem.at[0])

    def issue(n, c):
        for k in range(TOP_K):
            row_copy(n, k).start(priority=k)
        return c

    def drain(n, c):
        for k in range(TOP_K):
            row_copy(n, k).wait()
        return c

    lax.fori_loop(0, DISPATCH_TB, issue, 0, unroll=8)
    lax.fori_loop(0, DISPATCH_TB, drain, 0, unroll=8)


def dispatch(dest, zero_start, hpk):
    grid_spec = pltpu.PrefetchScalarGridSpec(
        num_scalar_prefetch=2,
        grid=(T // DISPATCH_TB,),
        in_specs=[pl.BlockSpec((DISPATCH_TB * PACK_ROWS, LANES), lambda i, dst, zs: (i, 0))],
        out_specs=pl.BlockSpec(memory_space=pl.ANY),
        scratch_shapes=[pltpu.VMEM((MOE_BLOCK * PACK_ROWS, LANES), jnp.uint32),
                        pltpu.SemaphoreType.DMA((1,)),
                        pltpu.SemaphoreType.DMA((1,))],
    )
    return pl.pallas_call(
        _dispatch_kernel,
        out_shape=jax.ShapeDtypeStruct((MOE_ROWS * PACK_ROWS, LANES), jnp.uint32),
        grid_spec=grid_spec,
        compiler_params=_cparams(("arbitrary",)),
        name="moe_dispatch",
    )(dest, zero_start, hpk)


def _expert_kernel(be_ref, nu_ref, ord_ref, seq_ref, x_ref, wg_hbm, wu_hbm, wd_hbm, y_ref,
                   wg_st, wu_st, wd_st, wg_bf, wu_bf, wd_bf, sem):
    b = pl.program_id(0)

    def weight_copies(e, slot):
        return (pltpu.make_async_copy(wg_hbm.at[e], wg_st.at[slot], sem.at[slot, 0]),
                pltpu.make_async_copy(wu_hbm.at[e], wu_st.at[slot], sem.at[slot, 1]),
                pltpu.make_async_copy(wd_hbm.at[e], wd_st.at[slot], sem.at[slot, 2]))

    def start_fetch(n, slot):
        @pl.when(seq_ref[n] >= 0)
        def _():
            for cp in weight_copies(seq_ref[n], slot):
                cp.start(priority=1)

    @pl.when(b < nu_ref[0])
    def _():
        n = ord_ref[b]
        slot = n % 2
        changed = jnp.logical_or(b == 0, be_ref[b] != be_ref[jnp.maximum(b - 1, 0)])

        @pl.when(b == 0)
        def _():
            start_fetch(0, 0)
            start_fetch(1, 1)

        @pl.when(changed)
        def _():
            cg, cu, cd = weight_copies(be_ref[b], slot)
            cg.wait()
            wg_bf[...] = wg_st[slot].astype(BF16)
            cu.wait()
            wu_bf[...] = wu_st[slot].astype(BF16)
            cd.wait()
            wd_bf[...] = wd_st[slot].astype(BF16)
            start_fetch(n + 2, slot)

        x = _unpack_bf16_pairs(x_ref, 0, MOE_BLOCK).astype(BF16)
        gate = jnp.dot(x, wg_bf[...], preferred_element_type=F32)
        up = jnp.dot(x, wu_bf[...], preferred_element_type=F32)
        hid = (jax.nn.silu(gate) * up).astype(BF16)
        y = jnp.dot(hid, wd_bf[...], preferred_element_type=F32)
        word = _pack_bf16_pairs(y.astype(BF16).astype(F32))
        for c in range(PACK_ROWS):
            y_ref[pl.ds(c, MOE_BLOCK, stride=PACK_ROWS), :] = word[:, c * LANES:(c + 1) * LANES]

    @pl.when(b >= nu_ref[0])
    def _():
        y_ref[...] = jnp.zeros_like(y_ref)


def experts(block_expert, n_used, block_ord, expert_seq, xs, w_gate, w_up, w_down):
    def blk(b, be, nu, od, sq):
        return jnp.minimum(b, nu[0] - 1)

    grid_spec = pltpu.PrefetchScalarGridSpec(
        num_scalar_prefetch=4,
        grid=(MOE_BLOCKS,),
        in_specs=[pl.BlockSpec((MOE_BLOCK * PACK_ROWS, LANES), lambda b, be, nu, od, sq: (blk(b, be, nu, od, sq), 0)),
                  pl.BlockSpec(memory_space=pl.ANY),
                  pl.BlockSpec(memory_space=pl.ANY),
                  pl.BlockSpec(memory_space=pl.ANY)],
        out_specs=pl.BlockSpec((MOE_BLOCK * PACK_ROWS, LANES), lambda b, be, nu, od, sq: (b, 0)),
        scratch_shapes=[pltpu.VMEM((2, D, EXPERT_FF), F32),
                        pltpu.VMEM((2, D, EXPERT_FF), F32),
                        pltpu.VMEM((2, EXPERT_FF, D), F32),
                        pltpu.VMEM((D, EXPERT_FF), BF16),
                        pltpu.VMEM((D, EXPERT_FF), BF16),
                        pltpu.VMEM((EXPERT_FF, D), BF16),
                        pltpu.SemaphoreType.DMA((2, 3))],
    )
    return pl.pallas_call(
        _expert_kernel,
        out_shape=jax.ShapeDtypeStruct((MOE_ROWS * PACK_ROWS, LANES), jnp.uint32),
        grid_spec=grid_spec,
        compiler_params=_cparams(("arbitrary",)),
        name="moe_experts",
    )(block_expert, n_used, block_ord, expert_seq, xs, w_gate, w_up, w_down)


COMBINE_ROWS = 512
COMBINE_SUB = 16


def _combine_kernel(dest_ref, ys_hbm, x_ref, wt_ref, g_ref, o_ref, ybuf, sem):
    s = pl.program_id(0)
    ns = pl.num_programs(0)
    slot = s % 2
    tb = COMBINE_SUB
    rows = COMBINE_ROWS
    nxt = jnp.minimum(s + 1, ns - 1)

    def row_copy(step, n, k, sl):
        tok = step * rows + n
        src = ys_hbm.at[pl.ds(pl.multiple_of(dest_ref[tok * TOP_K + k] * PACK_ROWS, PACK_ROWS), PACK_ROWS)]
        dst = ybuf.at[sl * TOP_K + k, pl.ds(pl.multiple_of(n * PACK_ROWS, PACK_ROWS), PACK_ROWS)]
        return pltpu.make_async_copy(src, dst, sem.at[sl])

    def wait_all(step, sl):
        def body(n, c):
            for k in range(TOP_K):
                row_copy(step, n, k, sl).wait()
            return c
        lax.fori_loop(0, rows, body, 0, unroll=8)

    @pl.when(s == 0)
    def _():
        def body(n, c):
            for k in range(TOP_K):
                row_copy(0, n, k, 0).start(priority=k)
            return c
        lax.fori_loop(0, rows, body, 0, unroll=8)

    wait_all(s, slot)

    g = g_ref[...]
    y0_ref = ybuf.at[slot * TOP_K]
    y1_ref = ybuf.at[slot * TOP_K + 1]
    for b in range(rows // tb):
        for n in range(b * tb, (b + 1) * tb):
            for k in range(TOP_K):
                row_copy(nxt, n, k, 1 - slot).start(priority=k)
        sub = slice(b * tb, (b + 1) * tb)
        w = wt_ref[sub, :]
        y0 = _unpack_bf16_pairs(y0_ref, b * tb, tb)
        y1 = _unpack_bf16_pairs(y1_ref, b * tb, tb)
        z = x_ref[sub, :] + (w[:, 0:1] * y0 + w[:, 1:2] * y1)
        ms = jnp.mean(z * z, axis=-1, keepdims=True)
        o_ref[sub, :] = z * lax.rsqrt(ms + NORM_EPS) * g

    @pl.when(s == ns - 1)
    def _():
        wait_all(nxt, 1 - slot)


def combine(dest, ys, x1, wts, gain):
    rows = COMBINE_ROWS
    grid_spec = pltpu.PrefetchScalarGridSpec(
        num_scalar_prefetch=1,
        grid=(T // rows,),
        in_specs=[pl.BlockSpec(memory_space=pl.ANY),
                  pl.BlockSpec((rows, D), lambda s, dst: (s, 0)),
                  pl.BlockSpec((rows, LANES), lambda s, dst: (s, 0)),
                  pl.BlockSpec((1, D), lambda s, dst: (0, 0))],
        out_specs=pl.BlockSpec((rows, D), lambda s, dst: (s, 0)),
        scratch_shapes=[pltpu.VMEM((2 * TOP_K, rows * PACK_ROWS, LANES), jnp.uint32),
                        pltpu.SemaphoreType.DMA((2,))],
    )
    return pl.pallas_call(
        _combine_kernel,
        out_shape=jax.ShapeDtypeStruct((T, D), F32),
        grid_spec=grid_spec,
        compiler_params=_cparams(("arbitrary",)),
        name="moe_combine_final_norm",
    )(dest, ys, x1, wts, gain)


def dispatch_plan(ids, counts):
    experts_ = jnp.arange(N_EXPERTS, dtype=jnp.int32)
    padded = (counts + MOE_BLOCK - 1) // MOE_BLOCK * MOE_BLOCK
    pad_end = jnp.cumsum(padded)
    pad_start = pad_end - padded
    e = ids[:, :TOP_K]
    start_of = jnp.sum(jnp.where(e[:, :, None] == experts_[None, None, :], pad_start[None, None, :], 0), axis=-1)
    dest = (start_of + ids[:, TOP_K:2 * TOP_K]).reshape(N_ASSIGN).astype(jnp.int32)
    n_used = pad_end[-1] // MOE_BLOCK
    block_start = jnp.minimum(jnp.arange(MOE_BLOCKS, dtype=jnp.int32), n_used - 1) * MOE_BLOCK
    block_expert = jnp.sum((block_start[:, None] >= pad_end[None, :]).astype(jnp.int32), axis=1)
    block_expert = jnp.minimum(block_expert, N_EXPERTS - 1).astype(jnp.int32)
    tail = n_used + experts_
    zero_start = jnp.concatenate([jnp.where(counts > 0, pad_end - MOE_BLOCK, -1),
                                  jnp.where(tail < MOE_BLOCKS, tail * MOE_BLOCK, -1)]).astype(jnp.int32)
    present = counts > 0
    expert_ord = jnp.cumsum(present.astype(jnp.int32)) - 1
    slots = jnp.arange(N_EXPERTS + 2, dtype=jnp.int32)
    hit = present[None, :] & (expert_ord[None, :] == slots[:, None])
    expert_seq = jnp.where(jnp.any(hit, axis=1), jnp.sum(jnp.where(hit, experts_[None, :], 0), axis=1), -1)
    block_ord = jnp.sum(jnp.where(block_expert[:, None] == experts_[None, :], expert_ord[None, :], 0), axis=1)
    return (block_expert, n_used.astype(jnp.int32).reshape(1), dest, zero_start,
            block_ord.astype(jnp.int32), expert_seq.astype(jnp.int32))


def kernel(x, norm_mix, w_in, b_gate, ssm_a_re, ssm_a_im, ssm_log_step, ssm_b_re, ssm_b_im, ssm_c_re, ssm_c_im, ssm_d, w_glu, b_glu, w_up_attn, w_up_ssm, w_out, norm_ffn, w_router_group, b_router_group, w_router_expert, b_router_expert, w_expert_gate, w_expert_up, w_expert_down, norm_final):
    x2 = x.reshape(T, D)
    h = norm_permute(x2, norm_mix.reshape(1, D)).reshape(T, D)
    w_in_l = w_in.reshape(D, IN_COLS)
    qkv = proj(h, w_in_l, 0, 3 * QKV_COLS, tn=QKV_COLS, name="proj_qkv")
    u = proj(h, w_in_l, 3 * QKV_COLS, SSM_W, tn=SSM_W, name="proj_ssm_in")

    attn = attention(qkv.reshape(R, NI, 3 * QKV_COLS)).reshape(T, ATTN_OUT)

    G = SSM_W // SSM_CH
    wb, wc, pows = ssm_params(
        ssm_a_re.reshape(G, SSM_STATE).astype(F32), ssm_a_im.reshape(G, SSM_STATE).astype(F32),
        ssm_log_step.reshape(G),
        ssm_b_re.reshape(G, SSM_STATE, SSM_CH).astype(F32), ssm_b_im.reshape(G, SSM_STATE, SSM_CH).astype(F32),
        ssm_c_re.reshape(G, SSM_CH, SSM_STATE), ssm_c_im.reshape(G, SSM_CH, SSM_STATE))
    yg = ssm_scan(u.reshape(R, NI, SSM_W), wb, wc, pows, ssm_d.reshape(1, SSM_W).astype(F32))
    ssm = glu(yg.reshape(T, SSM_W), w_glu.reshape(SSM_W, SSM_W), b_glu.reshape(1, SSM_W))

    merged = merge(h, attn, ssm, w_in_l, b_gate.reshape(1, 2 * D),
                   w_up_attn.reshape(ATTN_OUT, D), w_up_ssm.reshape(SSM_W, D))
    x1 = out_proj(merged.reshape(R, NI, D), w_out.reshape(D, D), x2)

    w_r = jnp.concatenate([w_router_group.reshape(D, N_EGROUPS), w_router_expert.reshape(D, N_EXPERTS),
                           jnp.zeros((D, LANES - N_EGROUPS - N_EXPERTS), F32)], axis=1)
    b_r = jnp.concatenate([b_router_group.reshape(1, N_EGROUPS), b_router_expert.reshape(1, N_EXPERTS),
                           jnp.zeros((1, LANES - N_EGROUPS - N_EXPERTS), F32)], axis=1)
    hpk, ids, wts, counts = router(x1, norm_ffn.reshape(1, D), w_r, b_r)

    block_expert, n_used, dest, zero_start, block_ord, expert_seq = dispatch_plan(ids[:, :2 * TOP_K], counts[0, :N_EXPERTS])
    xs = dispatch(dest, zero_start, hpk)
    ys = experts(block_expert, n_used, block_ord, expert_seq, xs,
                 w_expert_gate.reshape(N_EXPERTS, D, EXPERT_FF), w_expert_up.reshape(N_EXPERTS, D, EXPERT_FF),
                 w_expert_down.reshape(N_EXPERTS, EXPERT_FF, D))
    out = combine(dest, ys, x1, wts, norm_final.reshape(1, D))
    return out.reshape(1, T, D)
```

```python
import functools

import jax
import jax.numpy as jnp
from jax import lax
from jax.experimental import pallas as pl
from jax.experimental.pallas import tpu as pltpu

F32 = jnp.float32
BF16 = jnp.bfloat16

T = 8192
D = 2048
R = 16
NI = T // R
HEAD_DIM = 64
N_HEAD_SLOTS = 8
DILATIONS = (1, 4, 16)
ATTN_BLOCK = 128
QKV_COLS = 1536
ATTN_OUT = 512
SSM_W = 1024
SSM_STATE = 64
SSM_CH = 16
IN_COLS = 3 * QKV_COLS + SSM_W + 2 * D
N_EXPERTS = 32
N_EGROUPS = 4
EXPERTS_PER_GROUP = 8
TOP_K = 2
EXPERT_FF = 512
NORM_EPS = 1e-6
LANES = 128
VMEM_LIMIT = 48 * 1024 * 1024

MOE_BLOCK = 256
N_ASSIGN = T * TOP_K
MOE_BLOCKS = N_ASSIGN // MOE_BLOCK + N_EXPERTS
MOE_ROWS = MOE_BLOCKS * MOE_BLOCK


def _cparams(sem):
    return pltpu.CompilerParams(dimension_semantics=sem, vmem_limit_bytes=VMEM_LIMIT)


N_SLABS = D // LANES
NORM_TB = 64
NORM_CHUNK = 64
NORM_PITCH = R + 8


def _norm_permute_kernel(x_ref, g_ref, h_ref, slab):
    g = g_ref[...]

    def chunk(t, c):
        rows = pl.ds(pl.multiple_of(t * NORM_CHUNK, NORM_CHUNK), NORM_CHUNK)
        x = x_ref[rows, :]
        ms = jnp.mean(x * x, axis=-1, keepdims=True)
        hn = x * lax.rsqrt(ms + NORM_EPS) * g
        for k in range(NORM_CHUNK // R):
            dst = pl.ds(pl.multiple_of((t * (NORM_CHUNK // R) + k) * NORM_PITCH, 8), R)
            for s_ in range(N_SLABS):
                slab[s_, dst, :] = hn[k * R:(k + 1) * R, s_ * LANES:(s_ + 1) * LANES]
        return c

    lax.fori_loop(0, NORM_TB * R // NORM_CHUNK, chunk, 0)
    for r in range(R):
        pieces = [slab[s_, pl.ds(r, NORM_TB, stride=NORM_PITCH), :] for s_ in range(N_SLABS)]
        h_ref[r] = jnp.concatenate(pieces, axis=1).astype(BF16)


def norm_permute(x2, gain):
    return pl.pallas_call(
        _norm_permute_kernel,
        out_shape=jax.ShapeDtypeStruct((R, NI, D), BF16),
        grid=(NI // NORM_TB,),
        in_specs=[pl.BlockSpec((NORM_TB * R, D), lambda i: (i, 0)),
                  pl.BlockSpec((1, D), lambda i: (0, 0))],
        out_specs=pl.BlockSpec((R, NORM_TB, D), lambda i: (0, i, 0)),
        scratch_shapes=[pltpu.VMEM((N_SLABS, NORM_TB * NORM_PITCH, LANES), F32)],
        compiler_params=_cparams(("arbitrary",)),
        name="norm_permute",
    )(x2, gain)


MXU_N = 256


def _stationary_weight_tile(w_hbm, col0, stage_ref, wbf_ref, sem):
    j = pl.program_id(0)
    tn = stage_ref.shape[1]

    def copy(jj):
        c0 = pl.multiple_of(col0 + jj * tn, LANES)
        return pltpu.make_async_copy(w_hbm.at[:, pl.ds(c0, tn)], stage_ref, sem)

    @pl.when(pl.program_id(1) == 0)
    def _():
        @pl.when(j == 0)
        def _():
            copy(0).start(priority=1)

        copy(j).wait()
        wbf_ref[...] = stage_ref[...].astype(BF16)

        @pl.when(j + 1 < pl.num_programs(0))
        def _():
            copy(j + 1).start(priority=1)


def _proj_kernel(a_ref, w_hbm, o_ref, stage_ref, wbf_ref, sem, *, col_off):
    _stationary_weight_tile(w_hbm, col_off, stage_ref, wbf_ref, sem.at[0])
    a = a_ref[...]
    for c in range(o_ref.shape[1] // MXU_N):
        cols = slice(c * MXU_N, (c + 1) * MXU_N)
        o_ref[:, cols] = jnp.dot(a, wbf_ref[:, cols], preferred_element_type=F32)


def proj(h, w_in, col_off, n_cols, tn, name="proj"):
    tm = 1024
    return pl.pallas_call(
        functools.partial(_proj_kernel, col_off=col_off),
        out_shape=jax.ShapeDtypeStruct((T, n_cols), F32),
        grid=(n_cols // tn, T // tm),
        in_specs=[pl.BlockSpec((tm, D), lambda j, i: (i, 0)),
                  pl.BlockSpec(memory_space=pl.ANY)],
        out_specs=pl.BlockSpec((tm, tn), lambda j, i: (i, j)),
        scratch_shapes=[pltpu.VMEM((D, tn), F32), pltpu.VMEM((D, tn), BF16), pltpu.SemaphoreType.DMA((1,))],
        compiler_params=_cparams(("arbitrary", "arbitrary")),
        name=name,
    )(h, w_in)


def _seq_index_maps(d):
    nseg = R // d
    qlen = ATTN_BLOCK // nseg
    return nseg, qlen


def _bias_matrices(d, hp):
    nseg, qlen = _seq_index_maps(d)
    klen = 2 * qlen
    row = lax.broadcasted_iota(jnp.int32, (2 * ATTN_BLOCK, 2 * ATTN_BLOCK), 0)
    col = lax.broadcasted_iota(jnp.int32, (2 * ATTN_BLOCK, 2 * ATTN_BLOCK), 1)
    rho = row % ATTN_BLOCK
    jq = (rho % qlen) * nseg + rho // qlen
    jk = ((col % klen) - qlen) * nseg + col // klen
    steps = jq - jk
    valid = (steps >= 0) & (steps <= ATTN_BLOCK)
    head = 2 * hp + row // ATTN_BLOCK
    slope = lax.bitcast_convert_type((127 - (head + 1)) << 23, F32)
    bias = -slope * (d * steps).astype(F32)
    neg = jnp.float32(-jnp.inf)
    return jnp.where(valid, bias, neg), jnp.where(valid & (jk >= 0), bias, neg)


def _attend_pair(q, k, v, bias):
    lane = lax.broadcasted_iota(jnp.int32, (ATTN_BLOCK, LANES), 1)
    first = lane < HEAD_DIM
    zero = jnp.zeros_like(q)
    q2 = jnp.concatenate([jnp.where(first, q, zero), jnp.where(first, zero, q)], axis=0).astype(BF16)
    s = lax.dot_general(q2, k.astype(BF16), (((1,), (1,)), ((), ())), preferred_element_type=F32)
    s = s + bias
    m = jnp.max(s, axis=-1, keepdims=True)
    p = jnp.exp(s - m)
    l = jnp.sum(p, axis=-1, keepdims=True)
    o2 = jnp.dot(p.astype(BF16), v.astype(BF16), preferred_element_type=F32)
    o = jnp.where(first, o2[:ATTN_BLOCK], o2[ATTN_BLOCK:])
    m_b = jnp.where(first, m[:ATTN_BLOCK], m[ATTN_BLOCK:])
    l_b = jnp.where(first, l[:ATTN_BLOCK], l[ATTN_BLOCK:])
    return o, m_b, l_b


def _attn_kernel(q_ref, kp_ref, kc_ref, vp_ref, vc_ref, o_ref, kbuf, vbuf, obuf, mbuf, lbuf, bias_ref):
    hp = pl.program_id(0)
    it = pl.program_id(1)
    g = pl.program_id(2)
    scale = HEAD_DIM ** -0.5

    for gi, d in enumerate(DILATIONS):
        nseg, qlen = _seq_index_maps(d)
        klen = 2 * qlen
        nblk = ATTN_BLOCK // qlen

        @pl.when(g == gi)
        def _(gi=gi, d=d, nseg=nseg, qlen=qlen, klen=klen, nblk=nblk):
            if nseg > 1:
                kbuf[:, :ATTN_BLOCK, :] = kp_ref[...]
                kbuf[:, ATTN_BLOCK:, :] = kc_ref[...]
                vbuf[:, :ATTN_BLOCK, :] = vp_ref[...]
                vbuf[:, ATTN_BLOCK:, :] = vc_ref[...]

            @pl.when(it == 0)
            def _():
                b_reg, b_first = _bias_matrices(d, hp)
                bias_ref[gi, 0] = b_reg
                bias_ref[gi, 1] = b_first

            def block(idx, carry):
                rd = idx // nblk
                bb = idx % nblk
                q0 = pl.multiple_of(bb * qlen, qlen)
                k0 = pl.multiple_of(ATTN_BLOCK + bb * qlen - qlen, qlen)
                qs, ks, vs = [], [], []
                for m_ in range(nseg):
                    rr = rd + d * m_
                    qs.append(q_ref[rr, pl.ds(q0, qlen), :])
                    if nseg > 1:
                        ks.append(kbuf[rr, pl.ds(k0, klen), :])
                        vs.append(vbuf[rr, pl.ds(k0, klen), :])
                    else:
                        ks += [kp_ref[rr], kc_ref[rr]]
                        vs += [vp_ref[rr], vc_ref[rr]]
                q = jnp.concatenate(qs, axis=0) * scale
                k = jnp.concatenate(ks, axis=0)
                v = jnp.concatenate(vs, axis=0)
                is_first = jnp.logical_and(it == 0, bb == 0)
                bias = bias_ref[gi, jnp.where(is_first, 1, 0)]
                o, mx, den = _attend_pair(q, k, v, bias)
                for m_ in range(nseg):
                    rr = rd + d * m_
                    seg = slice(m_ * qlen, (m_ + 1) * qlen)
                    obuf[gi, rr, pl.ds(q0, qlen), :] = o[seg]
                    mbuf[gi, rr, pl.ds(q0, qlen), :] = mx[seg]
                    lbuf[gi, rr, pl.ds(q0, qlen), :] = den[seg]
                return carry

            lax.fori_loop(0, d * nblk, block, 0, unroll=True)

    @pl.when(g == len(DILATIONS) - 1)
    def _():
        for r in range(R):
            m0, m1, m2 = mbuf[0, r], mbuf[1, r], mbuf[2, r]
            mx = jnp.maximum(jnp.maximum(m0, m1), m2)
            e0, e1, e2 = jnp.exp(m0 - mx), jnp.exp(m1 - mx), jnp.exp(m2 - mx)
            den = e0 * lbuf[0, r] + e1 * lbuf[1, r] + e2 * lbuf[2, r]
            num = e0 * obuf[0, r] + e1 * obuf[1, r] + e2 * obuf[2, r]
            o_ref[r] = (num / den).astype(BF16)


def attention(qkv3):
    n_hp = N_HEAD_SLOTS // 2
    n_it = NI // ATTN_BLOCK
    ng = len(DILATIONS)
    cb = QKV_COLS // LANES

    def cur(base):
        return pl.BlockSpec((R, ATTN_BLOCK, LANES), lambda hp, it, g: (0, it, base + g * n_hp + hp))

    def prev(base):
        return pl.BlockSpec((R, ATTN_BLOCK, LANES),
                            lambda hp, it, g: (0, jnp.maximum(it - 1, 0), base + g * n_hp + hp))

    return pl.pallas_call(
        _attn_kernel,
        out_shape=jax.ShapeDtypeStruct((R, NI, ATTN_OUT), BF16),
        grid=(n_hp, n_it, ng),
        in_specs=[cur(0), prev(cb), cur(cb), prev(2 * cb), cur(2 * cb)],
        out_specs=pl.BlockSpec((R, ATTN_BLOCK, LANES), lambda hp, it, g: (0, it, hp)),
        scratch_shapes=[pltpu.VMEM((R, 2 * ATTN_BLOCK, LANES), F32),
                        pltpu.VMEM((R, 2 * ATTN_BLOCK, LANES), F32),
                        pltpu.VMEM((ng, R, ATTN_BLOCK, LANES), F32),
                        pltpu.VMEM((ng, R, ATTN_BLOCK, LANES), F32),
                        pltpu.VMEM((ng, R, ATTN_BLOCK, LANES), F32),
                        pltpu.VMEM((ng, 2, 2 * ATTN_BLOCK, 2 * ATTN_BLOCK), F32)],
        compiler_params=_cparams(("arbitrary", "arbitrary", "arbitrary")),
        name="dilated_attention",
    )(qkv3, qkv3, qkv3, qkv3, qkv3)


SSM_SLAB = 256
SSM_SLABS = SSM_W // SSM_SLAB
SLAB_STATES = SSM_SLAB // SSM_CH * SSM_STATE
SSM_TI = 128
SSM_MM_CHUNK = 4


def _ssm_kernel(u_ref, wb_ref, wc_ref, pw_ref, dsk_ref, o_ref, s_ref, zs_ref, zc_ref, tr_ref, zt_ref):
    ic = pl.program_id(1)
    ns = SLAB_STATES

    @pl.when(ic == 0)
    def _():
        zc_ref[...] = jnp.zeros_like(zc_ref)

    n_chunks = R // SSM_MM_CHUNK
    n_tiles = SSM_TI // 8
    wb = wb_ref[0]
    wc = wc_ref[0]
    dsk = dsk_ref[...]
    arb = jnp.broadcast_to(pw_ref[0, 0:1, :ns], (8, ns))
    aib = jnp.broadcast_to(pw_ref[0, 0:1, ns:], (8, ns))


    def bu_chunk(c):
        lo = c * SSM_MM_CHUNK
        uc = u_ref[lo:lo + SSM_MM_CHUNK].reshape(SSM_MM_CHUNK * SSM_TI, SSM_SLAB)
        bu = jnp.dot(uc.astype(BF16), wb, preferred_element_type=F32)
        s_ref[lo:lo + SSM_MM_CHUNK] = bu.reshape(SSM_MM_CHUNK, SSM_TI, 2 * ns)

    def local_chunk(c):
        lo = c * SSM_MM_CHUNK
        first = max(lo, 1)
        for t in range(n_tiles):
            rows = slice(t * 8, (t + 1) * 8)
            pr = s_ref[first - 1, rows, :ns]
            pi = s_ref[first - 1, rows, ns:]
            for r in range(first, lo + SSM_MM_CHUNK):
                nr = s_ref[r, rows, :ns] + (arb * pr - aib * pi)
                ni = s_ref[r, rows, ns:] + (arb * pi + aib * pr)
                s_ref[r, rows, :ns] = nr
                s_ref[r, rows, ns:] = ni
                pr, pi = nr, ni

    bu_chunk(0)
    for c in range(n_chunks):
        if c + 1 < n_chunks:
            bu_chunk(c + 1)
        local_chunk(c)

    n_ch = ns // LANES
    for c in range(n_ch):
        tr_ref[0, c * 8:c * 8 + 1, :] = pw_ref[0, R - 1:R, c * LANES:(c + 1) * LANES]
        tr_ref[1, c * 8:c * 8 + 1, :] = pw_ref[0, R - 1:R, ns + c * LANES:ns + (c + 1) * LANES]
    a16r = tr_ref[0, pl.ds(0, n_ch, stride=8), :]
    a16i = tr_ref[1, pl.ds(0, n_ch, stride=8), :]

    def ztile(t, carry):
        zr, zi = carry
        rows = pl.ds(pl.multiple_of(t * 8, 8), 8)
        for c in range(n_ch):
            tr_ref[0, c * 8:(c + 1) * 8, :] = s_ref[R - 1, rows, c * LANES:(c + 1) * LANES]
            tr_ref[1, c * 8:(c + 1) * 8, :] = s_ref[R - 1, rows, ns + c * LANES:ns + (c + 1) * LANES]
        for i in range(8):
            zt_ref[0, pl.ds(i, n_ch, stride=8), :] = zr
            zt_ref[1, pl.ds(i, n_ch, stride=8), :] = zi
            er = tr_ref[0, pl.ds(i, n_ch, stride=8), :]
            ei = tr_ref[1, pl.ds(i, n_ch, stride=8), :]
            zr, zi = a16r * zr - a16i * zi + er, a16r * zi + a16i * zr + ei
        for c in range(n_ch):
            zs_ref[rows, c * LANES:(c + 1) * LANES] = zt_ref[0, c * 8:(c + 1) * 8, :]
            zs_ref[rows, ns + c * LANES:ns + (c + 1) * LANES] = zt_ref[1, c * 8:(c + 1) * 8, :]
        return zr, zi

    zr_end, zi_end = lax.fori_loop(0, SSM_TI // 8, ztile, (zc_ref[0], zc_ref[1]))
    zc_ref[0] = zr_end
    zc_ref[1] = zi_end

    def fix_chunk(c):
        lo = c * SSM_MM_CHUNK
        for r in range(lo, lo + SSM_MM_CHUNK):
            prb = jnp.broadcast_to(pw_ref[0, r:r + 1, :ns], (8, ns))
            pib = jnp.broadcast_to(pw_ref[0, r:r + 1, ns:], (8, ns))
            for t in range(n_tiles):
                rows = slice(t * 8, (t + 1) * 8)
                zr = zs_ref[rows, :ns]
                zi = zs_ref[rows, ns:]
                s_ref[r, rows, :ns] = s_ref[r, rows, :ns] + (prb * zr - pib * zi)
                s_ref[r, rows, ns:] = s_ref[r, rows, ns:] + (prb * zi + pib * zr)

    def out_chunk(c):
        lo = c * SSM_MM_CHUNK
        xs = s_ref[lo:lo + SSM_MM_CHUNK].reshape(SSM_MM_CHUNK * SSM_TI, 2 * ns)
        y = jnp.dot(xs.astype(BF16), wc, preferred_element_type=F32)
        y = y.reshape(SSM_MM_CHUNK, SSM_TI, SSM_SLAB) + dsk * u_ref[lo:lo + SSM_MM_CHUNK]
        o_ref[lo:lo + SSM_MM_CHUNK] = jax.nn.gelu(y)

    fix_chunk(0)
    for c in range(n_chunks):
        if c + 1 < n_chunks:
            fix_chunk(c + 1)
        out_chunk(c)


def ssm_scan(u3, wb, wc, pows, dskip):
    ns2 = 2 * SLAB_STATES
    return pl.pallas_call(
        _ssm_kernel,
        out_shape=jax.ShapeDtypeStruct((R, NI, SSM_W), F32),
        grid=(SSM_SLABS, NI // SSM_TI),
        in_specs=[pl.BlockSpec((R, SSM_TI, SSM_SLAB), lambda kb, ic: (0, ic, kb)),
                  pl.BlockSpec((1, SSM_SLAB, ns2), lambda kb, ic: (kb, 0, 0)),
                  pl.BlockSpec((1, ns2, SSM_SLAB), lambda kb, ic: (kb, 0, 0)),
                  pl.BlockSpec((1, R, ns2), lambda kb, ic: (kb, 0, 0)),
                  pl.BlockSpec((1, SSM_SLAB), lambda kb, ic: (0, kb))],
        out_specs=pl.BlockSpec((R, SSM_TI, SSM_SLAB), lambda kb, ic: (0, ic, kb)),
        scratch_shapes=[pltpu.VMEM((R, SSM_TI, ns2), F32),
                        pltpu.VMEM((SSM_TI, ns2), F32),
                        pltpu.VMEM((2, SLAB_STATES // LANES, LANES), F32),
                        pltpu.VMEM((2, 8 * SLAB_STATES // LANES, LANES), F32),
                        pltpu.VMEM((2, 8 * SLAB_STATES // LANES, LANES), F32)],
        compiler_params=_cparams(("arbitrary", "arbitrary")),
        name="s5_ssm",
    )(u3, wb, wc, pows, dskip)


def ssm_params(a_re, a_im, log_step, b_re, b_im, c_re, c_im):
    G, P, H = SSM_W // SSM_CH, SSM_STATE, SSM_CH
    gs = SSM_SLAB // SSM_CH
    step = jnp.exp(log_step.astype(F32))[:, None]
    kk = jnp.arange(1, R + 1, dtype=F32)[:, None, None]
    mag = jnp.exp(kk * (a_re * step))
    ang = kk * (a_im * step)
    pw_re, pw_im = mag * jnp.cos(ang), mag * jnp.sin(ang)
    abar_re, abar_im = pw_re[0], pw_im[0]
    nr, ni = abar_re - 1.0, abar_im
    den = a_re * a_re + a_im * a_im
    f_re = (nr * a_re + ni * a_im) / den
    f_im = (ni * a_re - nr * a_im) / den
    bbar_re = f_re[..., None] * b_re - f_im[..., None] * b_im
    bbar_im = f_re[..., None] * b_im + f_im[..., None] * b_re
    rows_g = jnp.arange(gs * H, dtype=jnp.int32) // H
    cols_g = jnp.arange(gs * P, dtype=jnp.int32) // P
    b_t = jnp.concatenate([jnp.swapaxes(bbar_re, 1, 2).reshape(SSM_SLABS, gs * H, P),
                           jnp.swapaxes(bbar_im, 1, 2).reshape(SSM_SLABS, gs * H, P)], axis=-1)
    tile_p = (jnp.arange(2 * P, dtype=jnp.int32)[:, None] ==
              (jnp.arange(2 * gs * P, dtype=jnp.int32) // (gs * P) * P + jnp.arange(2 * gs * P, dtype=jnp.int32) % P)[None, :])
    mask_b = rows_g[:, None] == jnp.tile(cols_g, 2)[None, :]
    wb = jnp.where(mask_b[None], jnp.einsum('kap,pc->kac', b_t.astype(BF16), tile_p.astype(BF16),
                                            preferred_element_type=F32), 0.0).astype(BF16)
    c_t = jnp.concatenate([jnp.swapaxes(c_re.astype(F32), 1, 2).reshape(SSM_SLABS, gs * P, H),
                           -jnp.swapaxes(c_im.astype(F32), 1, 2).reshape(SSM_SLABS, gs * P, H)], axis=1)
    tile_h = jnp.arange(H, dtype=jnp.int32)[:, None] == (jnp.arange(gs * H, dtype=jnp.int32) % H)[None, :]
    mask_c = jnp.tile(cols_g, 2)[:, None] == rows_g[None, :]
    wc = jnp.where(mask_c[None], jnp.einsum('kah,hc->kac', c_t.astype(BF16), tile_h.astype(BF16),
                                            preferred_element_type=F32), 0.0).astype(BF16)
    pows = jnp.concatenate([pw_re.reshape(R, SSM_SLABS, gs * P), pw_im.reshape(R, SSM_SLABS, gs * P)], axis=-1)
    return wb, wc, jnp.transpose(pows, (1, 0, 2))


def _glu_kernel(a_ref, w_ref, b_ref, o_ref, wbf_ref):
    @pl.when(pl.program_id(0) == 0)
    def _():
        wbf_ref[...] = w_ref[...].astype(BF16)

    a = a_ref[...].astype(BF16)
    for c in range(SSM_W // MXU_N):
        cols = slice(c * MXU_N, (c + 1) * MXU_N)
        acc = jnp.dot(a, wbf_ref[:, cols], preferred_element_type=F32)
        o_ref[:, cols] = (a_ref[:, cols] * jax.nn.sigmoid(acc + b_ref[:, cols])).astype(BF16)


def glu(yg, w_glu, b_glu):
    tm = 1024
    return pl.pallas_call(
        _glu_kernel,
        out_shape=jax.ShapeDtypeStruct((T, SSM_W), BF16),
        grid=(T // tm,),
        in_specs=[pl.BlockSpec((tm, SSM_W), lambda i: (i, 0)),
                  pl.BlockSpec((SSM_W, SSM_W), lambda i: (0, 0)),
                  pl.BlockSpec((1, SSM_W), lambda i: (0, 0))],
        out_specs=pl.BlockSpec((tm, SSM_W), lambda i: (i, 0)),
        scratch_shapes=[pltpu.VMEM((SSM_W, SSM_W), BF16)],
        compiler_params=_cparams(("arbitrary",)),
        name="ssm_glu",
    )(yg, w_glu, b_glu)


def _merge_kernel(h_ref, at_ref, ss_ref, win_hbm, ba_ref, bs_ref, wa_hbm, ws_hbm, o_ref,
                  wga_st, wgs_st, wa_st, ws_st, wga_bf, wgs_bf, wa_bf, ws_bf, sem):
    gate_col = 3 * QKV_COLS + SSM_W
    _stationary_weight_tile(win_hbm, gate_col, wga_st, wga_bf, sem.at[0])
    _stationary_weight_tile(win_hbm, gate_col + D, wgs_st, wgs_bf, sem.at[1])
    _stationary_weight_tile(wa_hbm, 0, wa_st, wa_bf, sem.at[2])
    _stationary_weight_tile(ws_hbm, 0, ws_st, ws_bf, sem.at[3])
    h = h_ref[...]
    at = at_ref[...]
    ss = ss_ref[...]
    for c in range(o_ref.shape[1] // MXU_N):
        cols = slice(c * MXU_N, (c + 1) * MXU_N)
        ga = jax.nn.sigmoid(jnp.dot(h, wga_bf[:, cols], preferred_element_type=F32) + ba_ref[:, cols])
        a = jnp.dot(at, wa_bf[:, cols], preferred_element_type=F32)
        gs = jax.nn.sigmoid(jnp.dot(h, wgs_bf[:, cols], preferred_element_type=F32) + bs_ref[:, cols])
        s = jnp.dot(ss, ws_bf[:, cols], preferred_element_type=F32)
        o_ref[:, cols] = (ga * a + gs * s).astype(BF16)


def merge(h, attn, ssm, w_in, b_gate, w_up_attn, w_up_ssm):
    tm, tn = 1024, 512
    nj = D // tn
    any_space = pl.BlockSpec(memory_space=pl.ANY)
    return pl.pallas_call(
        _merge_kernel,
        out_shape=jax.ShapeDtypeStruct((T, D), BF16),
        grid=(nj, T // tm),
        in_specs=[pl.BlockSpec((tm, D), lambda j, i: (i, 0)),
                  pl.BlockSpec((tm, ATTN_OUT), lambda j, i: (i, 0)),
                  pl.BlockSpec((tm, SSM_W), lambda j, i: (i, 0)),
                  any_space,
                  pl.BlockSpec((1, tn), lambda j, i: (0, j)),
                  pl.BlockSpec((1, tn), lambda j, i: (0, j + nj)),
                  any_space,
                  any_space],
        out_specs=pl.BlockSpec((tm, tn), lambda j, i: (i, j)),
        scratch_shapes=[pltpu.VMEM((D, tn), F32), pltpu.VMEM((D, tn), F32),
                        pltpu.VMEM((ATTN_OUT, tn), F32), pltpu.VMEM((SSM_W, tn), F32),
                        pltpu.VMEM((D, tn), BF16), pltpu.VMEM((D, tn), BF16),
                        pltpu.VMEM((ATTN_OUT, tn), BF16), pltpu.VMEM((SSM_W, tn), BF16),
                        pltpu.SemaphoreType.DMA((4,))],
        compiler_params=_cparams(("arbitrary", "arbitrary")),
        name="gates_branch_merge",
    )(h, attn, ssm, w_in, b_gate, b_gate, w_up_attn, w_up_ssm)


OUTPROJ_TB = 64
OUTPROJ_PITCH = OUTPROJ_TB + 8


def _outproj_kernel(m_ref, w_hbm, x_ref, o_ref, slab, stage_ref, wbf_ref, sem):
    _stationary_weight_tile(w_hbm, 0, stage_ref, wbf_ref, sem.at[0])
    tb = OUTPROJ_TB
    a = m_ref[...].reshape(R * tb, D)
    per = MXU_N // LANES
    for c in range(o_ref.shape[1] // MXU_N):
        acc = jnp.dot(a, wbf_ref[:, c * MXU_N:(c + 1) * MXU_N], preferred_element_type=F32)
        for s_ in range(per):
            lanes = slice(s_ * LANES, (s_ + 1) * LANES)
            for r in range(R):
                slab[c * per + s_, r * OUTPROJ_PITCH:r * OUTPROJ_PITCH + tb, :] = acc[r * tb:(r + 1) * tb, lanes]
        for s_ in range(per):
            lanes = slice((c * per + s_) * LANES, (c * per + s_ + 1) * LANES)
            for i in range(tb):
                rows = slice(i * R, (i + 1) * R)
                o_ref[rows, lanes] = slab[c * per + s_, pl.ds(i, R, stride=OUTPROJ_PITCH), :] + x_ref[rows, lanes]


def out_proj(merged3, w_out, x2):
    tb, tn = OUTPROJ_TB, 1024
    return pl.pallas_call(
        _outproj_kernel,
        out_shape=jax.ShapeDtypeStruct((T, D), F32),
        grid=(D // tn, NI // tb),
        in_specs=[pl.BlockSpec((R, tb, D), lambda j, i: (0, i, 0)),
                  pl.BlockSpec(memory_space=pl.ANY),
                  pl.BlockSpec((tb * R, tn), lambda j, i: (i, j))],
        out_specs=pl.BlockSpec((tb * R, tn), lambda j, i: (i, j)),
        scratch_shapes=[pltpu.VMEM((tn // LANES, OUTPROJ_PITCH * R, LANES), F32),
                        pltpu.VMEM((D, tn), F32), pltpu.VMEM((D, tn), BF16),
                        pltpu.SemaphoreType.DMA((1,))],
        compiler_params=_cparams(("arbitrary", "arbitrary")),
        name="out_proj_residual",
    )(merged3, w_out, x2)


PACK_ROWS = 8
HALF_D = D // 2


def _pack_bf16_pairs(zf):
    top = lax.bitcast_convert_type(zf, jnp.uint32)
    return top[:, HALF_D:] | (top[:, :HALF_D] >> 16)


def _unpack_bf16_pairs(x_ref, first, n):
    lo, hi = [], []
    for c in range(PACK_ROWS):
        w = x_ref[pl.ds(first * PACK_ROWS + c, n, stride=PACK_ROWS), :]
        lo.append(lax.bitcast_convert_type(w << 16, F32))
        hi.append(lax.bitcast_convert_type(w & jnp.uint32(0xFFFF0000), F32))
    return jnp.concatenate(lo + hi, axis=1)


def _router_kernel(x_ref, g_ref, w_ref, b_ref, h_ref, id_ref, wt_ref, cnt_ref, carry_ref):
    step = pl.program_id(0)

    @pl.when(step == 0)
    def _():
        carry_ref[...] = jnp.zeros_like(carry_ref)

    x = x_ref[...]
    ms = jnp.mean(x * x, axis=-1, keepdims=True)
    z = x * lax.rsqrt(ms + NORM_EPS) * g_ref[...]
    zh = z.astype(BF16)
    zf = zh.astype(F32)
    word = _pack_bf16_pairs(zf)
    for c in range(PACK_ROWS):
        h_ref[pl.ds(c, x.shape[0], stride=PACK_ROWS), :] = word[:, c * LANES:(c + 1) * LANES]
    zl = (z - zf).astype(BF16)
    w = w_ref[...]
    wh = w.astype(BF16)
    wl = (w - wh.astype(F32)).astype(BF16)
    logits = (jnp.dot(zh, wh, preferred_element_type=F32) + jnp.dot(zl, wh, preferred_element_type=F32)
              + jnp.dot(zh, wl, preferred_element_type=F32)) + b_ref[...]
    lane = lax.broadcasted_iota(jnp.int32, logits.shape, 1)
    lanef = lane.astype(F32)
    neg = jnp.float32(-jnp.inf)
    big = jnp.float32(1e9)
    gl = jnp.where(lane < N_EGROUPS, logits, neg)
    gmax = jnp.max(gl, axis=-1, keepdims=True)
    gidx = jnp.min(jnp.where(gl == gmax, lanef, big), axis=-1, keepdims=True)
    pg = 1.0 / jnp.sum(jnp.exp(gl - gmax), axis=-1, keepdims=True)
    lo = N_EGROUPS + EXPERTS_PER_GROUP * gidx
    el = jnp.where((lanef >= lo) & (lanef < lo + EXPERTS_PER_GROUP), logits, neg)
    t1 = jnp.max(el, axis=-1, keepdims=True)
    j1 = jnp.min(jnp.where(el == t1, lanef, big), axis=-1, keepdims=True)
    el2 = jnp.where(lanef == j1, neg, el)
    t2 = jnp.max(el2, axis=-1, keepdims=True)
    j2 = jnp.min(jnp.where(el2 == t2, lanef, big), axis=-1, keepdims=True)
    e21 = jnp.exp(t2 - t1)
    w1 = pg / (1.0 + e21)
    w2 = pg * e21 / (1.0 + e21)
    e1f = j1 - N_EGROUPS
    e2f = j2 - N_EGROUPS

    tm = x.shape[0]
    oh1 = (lanef == e1f).astype(F32)
    oh2 = (lanef == e2f).astype(F32)
    ri = lax.broadcasted_iota(jnp.int32, (tm, tm), 0)
    ci = lax.broadcasted_iota(jnp.int32, (tm, tm), 1)
    before = (ci < ri).astype(BF16)
    p1 = jnp.dot(before, oh1.astype(BF16), preferred_element_type=F32)
    p2 = jnp.dot(before, oh2.astype(BF16), preferred_element_type=F32)
    carry = carry_ref[...]
    c1 = jnp.sum(oh1, axis=0, keepdims=True)
    c2 = jnp.sum(oh2, axis=0, keepdims=True)
    rank1 = jnp.sum(oh1 * (carry + p1), axis=-1, keepdims=True)
    rank2 = jnp.sum(oh2 * (carry + c1 + p2), axis=-1, keepdims=True)
    carry = carry + c1 + c2
    carry_ref[...] = carry
    cnt_ref[...] = jnp.broadcast_to(carry, cnt_ref.shape).astype(jnp.int32)

    ids = jnp.where(lane == 0, e1f, jnp.where(lane == 1, e2f, jnp.where(lane == 2, rank1, jnp.where(lane == 3, rank2, 0.0))))
    id_ref[...] = ids.astype(jnp.int32)
    wt_ref[...] = jnp.where(lane == 0, w1, jnp.where(lane == 1, w2, 0.0))


def router(x1, gain, w_r, b_r):
    tm = 512
    return pl.pallas_call(
        _router_kernel,
        out_shape=(jax.ShapeDtypeStruct((T * PACK_ROWS, LANES), jnp.uint32),
                   jax.ShapeDtypeStruct((T, LANES), jnp.int32),
                   jax.ShapeDtypeStruct((T, LANES), F32),
                   jax.ShapeDtypeStruct((8, LANES), jnp.int32)),
        grid=(T // tm,),
        in_specs=[pl.BlockSpec((tm, D), lambda i: (i, 0)),
                  pl.BlockSpec((1, D), lambda i: (0, 0)),
                  pl.BlockSpec((D, LANES), lambda i: (0, 0)),
                  pl.BlockSpec((1, LANES), lambda i: (0, 0))],
        out_specs=(pl.BlockSpec((tm * PACK_ROWS, LANES), lambda i: (i, 0)),
                   pl.BlockSpec((tm, LANES), lambda i: (i, 0)),
                   pl.BlockSpec((tm, LANES), lambda i: (i, 0)),
                   pl.BlockSpec((8, LANES), lambda i: (0, 0))),
        scratch_shapes=[pltpu.VMEM((1, LANES), F32)],
        compiler_params=_cparams(("arbitrary",)),
        name="ffn_norm_router",
    )(x1, gain, w_r, b_r)


DISPATCH_TB = 1024
N_ZERO_FILLS = 2 * N_EXPERTS


def _dispatch_kernel(dest_ref, zs_ref, h_ref, xs_hbm, zbuf, zsem, sem):
    step = pl.program_id(0)

    @pl.when(step == 0)
    def _():
        zbuf[...] = jnp.zeros_like(zbuf)

        def zero_copy(e):
            start = pl.multiple_of(jnp.maximum(zs_ref[e], 0) * PACK_ROWS, PACK_ROWS)
            return pltpu.make_async_copy(zbuf, xs_hbm.at[pl.ds(start, MOE_BLOCK * PACK_ROWS)], zsem.at[0])

        def zstart(e, c):
            @pl.when(zs_ref[e] >= 0)
            def _():
                zero_copy(e).start()
            return c

        def zwait(e, c):
            @pl.when(zs_ref[e] >= 0)
            def _():
                zero_copy(e).wait()
            return c

        lax.fori_loop(0, N_ZERO_FILLS, zstart, 0)
        lax.fori_loop(0, N_ZERO_FILLS, zwait, 0)

    def row_copy(n, k):
        a = (step * DISPATCH_TB + n) * TOP_K + k
        src = h_ref.at[pl.ds(pl.multiple_of(n * PACK_ROWS, PACK_ROWS), PACK_ROWS)]
        dst = xs_hbm.at[pl.ds(pl.multiple_of(dest_ref[a] * PACK_ROWS, PACK_ROWS), PACK_ROWS)]
        return pltpu.make_async_copy(src, dst, sem.at[0])

    def issue(n, c):
        for k in range(TOP_K):
            row_copy(n, k).start(priority=k)
        return c

    def drain(n, c):
        for k in range(TOP_K):
            row_copy(n, k).wait()
        return c

    lax.fori_loop(0, DISPATCH_TB, issue, 0, unroll=8)
    lax.fori_loop(0, DISPATCH_TB, drain, 0, unroll=8)


def dispatch(dest, zero_start, hpk):
    grid_spec = pltpu.PrefetchScalarGridSpec(
        num_scalar_prefetch=2,
        grid=(T // DISPATCH_TB,),
        in_specs=[pl.BlockSpec((DISPATCH_TB * PACK_ROWS, LANES), lambda i, dst, zs: (i, 0))],
        out_specs=pl.BlockSpec(memory_space=pl.ANY),
        scratch_shapes=[pltpu.VMEM((MOE_BLOCK * PACK_ROWS, LANES), jnp.uint32),
                        pltpu.SemaphoreType.DMA((1,)),
                        pltpu.SemaphoreType.DMA((1,))],
    )
    return pl.pallas_call(
        _dispatch_kernel,
        out_shape=jax.ShapeDtypeStruct((MOE_ROWS * PACK_ROWS, LANES), jnp.uint32),
        grid_spec=grid_spec,
        compiler_params=_cparams(("arbitrary",)),
        name="moe_dispatch",
    )(dest, zero_start, hpk)


def _expert_kernel(be_ref, nu_ref, ord_ref, seq_ref, x_ref, wg_hbm, wu_hbm, wd_hbm, y_ref,
                   wg_st, wu_st, wd_st, wg_bf, wu_bf, wd_bf, sem):
    b = pl.program_id(0)

    def weight_copies(e, slot):
        return (pltpu.make_async_copy(wg_hbm.at[e], wg_st.at[slot], sem.at[slot, 0]),
                pltpu.make_async_copy(wu_hbm.at[e], wu_st.at[slot], sem.at[slot, 1]),
                pltpu.make_async_copy(wd_hbm.at[e], wd_st.at[slot], sem.at[slot, 2]))

    def start_fetch(n, slot):
        @pl.when(seq_ref[n] >= 0)
        def _():
            for cp in weight_copies(seq_ref[n], slot):
                cp.start(priority=1)

    @pl.when(b < nu_ref[0])
    def _():
        n = ord_ref[b]
        slot = n % 2
        changed = jnp.logical_or(b == 0, be_ref[b] != be_ref[jnp.maximum(b - 1, 0)])

        @pl.when(b == 0)
        def _():
            start_fetch(0, 0)
            start_fetch(1, 1)

        @pl.when(changed)
        def _():
            cg, cu, cd = weight_copies(be_ref[b], slot)
            cg.wait()
            wg_bf[...] = wg_st[slot].astype(BF16)
            cu.wait()
            wu_bf[...] = wu_st[slot].astype(BF16)
            cd.wait()
            wd_bf[...] = wd_st[slot].astype(BF16)
            start_fetch(n + 2, slot)

        x = _unpack_bf16_pairs(x_ref, 0, MOE_BLOCK).astype(BF16)
        gate = jnp.dot(x, wg_bf[...], preferred_element_type=F32)
        up = jnp.dot(x, wu_bf[...], preferred_element_type=F32)
        hid = (jax.nn.silu(gate) * up).astype(BF16)
        y = jnp.dot(hid, wd_bf[...], preferred_element_type=F32)
        word = _pack_bf16_pairs(y.astype(BF16).astype(F32))
        for c in range(PACK_ROWS):
            y_ref[pl.ds(c, MOE_BLOCK, stride=PACK_ROWS), :] = word[:, c * LANES:(c + 1) * LANES]

    @pl.when(b >= nu_ref[0])
    def _():
        y_ref[...] = jnp.zeros_like(y_ref)


def experts(block_expert, n_used, block_ord, expert_seq, xs, w_gate, w_up, w_down):
    def blk(b, be, nu, od, sq):
        return jnp.minimum(b, nu[0] - 1)

    grid_spec = pltpu.PrefetchScalarGridSpec(
        num_scalar_prefetch=4,
        grid=(MOE_BLOCKS,),
        in_specs=[pl.BlockSpec((MOE_BLOCK * PACK_ROWS, LANES), lambda b, be, nu, od, sq: (blk(b, be, nu, od, sq), 0)),
                  pl.BlockSpec(memory_space=pl.ANY),
                  pl.BlockSpec(memory_space=pl.ANY),
                  pl.BlockSpec(memory_space=pl.ANY)],
        out_specs=pl.BlockSpec((MOE_BLOCK * PACK_ROWS, LANES), lambda b, be, nu, od, sq: (b, 0)),
        scratch_shapes=[pltpu.VMEM((2, D, EXPERT_FF), F32),
                        pltpu.VMEM((2, D, EXPERT_FF), F32),
                        pltpu.VMEM((2, EXPERT_FF, D), F32),
                        pltpu.VMEM((D, EXPERT_FF), BF16),
                        pltpu.VMEM((D, EXPERT_FF), BF16),
                        pltpu.VMEM((EXPERT_FF, D), BF16),
                        pltpu.SemaphoreType.DMA((2, 3))],
    )
    return pl.pallas_call(
        _expert_kernel,
        out_shape=jax.ShapeDtypeStruct((MOE_ROWS * PACK_ROWS, LANES), jnp.uint32),
        grid_spec=grid_spec,
        compiler_params=_cparams(("arbitrary",)),
        name="moe_experts",
    )(block_expert, n_used, block_ord, expert_seq, xs, w_gate, w_up, w_down)


COMBINE_ROWS = 512
COMBINE_SUB = 16


def _combine_kernel(dest_ref, ys_hbm, x_ref, wt_ref, g_ref, o_ref, ybuf, sem):
    s = pl.program_id(0)
    ns = pl.num_programs(0)
    slot = s % 2
    tb = COMBINE_SUB
    rows = COMBINE_ROWS
    nxt = jnp.minimum(s + 1, ns - 1)

    def row_copy(step, n, k, sl):
        tok = step * rows + n
        src = ys_hbm.at[pl.ds(pl.multiple_of(dest_ref[tok * TOP_K + k] * PACK_ROWS, PACK_ROWS), PACK_ROWS)]
        dst = ybuf.at[sl * TOP_K + k, pl.ds(pl.multiple_of(n * PACK_ROWS, PACK_ROWS), PACK_ROWS)]
        return pltpu.make_async_copy(src, dst, sem.at[sl])

    def wait_all(step, sl):
        def body(n, c):
            for k in range(TOP_K):
                row_copy(step, n, k, sl).wait()
            return c
        lax.fori_loop(0, rows, body, 0, unroll=8)

    @pl.when(s == 0)
    def _():
        def body(n, c):
            for k in range(TOP_K):
                row_copy(0, n, k, 0).start(priority=k)
            return c
        lax.fori_loop(0, rows, body, 0, unroll=8)

    wait_all(s, slot)

    g = g_ref[...]
    y0_ref = ybuf.at[slot * TOP_K]
    y1_ref = ybuf.at[slot * TOP_K + 1]
    for b in range(rows // tb):
        for n in range(b * tb, (b + 1) * tb):
            for k in range(TOP_K):
                row_copy(nxt, n, k, 1 - slot).start(priority=k)
        sub = slice(b * tb, (b + 1) * tb)
        w = wt_ref[sub, :]
        y0 = _unpack_bf16_pairs(y0_ref, b * tb, tb)
        y1 = _unpack_bf16_pairs(y1_ref, b * tb, tb)
        z = x_ref[sub, :] + (w[:, 0:1] * y0 + w[:, 1:2] * y1)
        ms = jnp.mean(z * z, axis=-1, keepdims=True)
        o_ref[sub, :] = z * lax.rsqrt(ms + NORM_EPS) * g

    @pl.when(s == ns - 1)
    def _():
        wait_all(nxt, 1 - slot)


def combine(dest, ys, x1, wts, gain):
    rows = COMBINE_ROWS
    grid_spec = pltpu.PrefetchScalarGridSpec(
        num_scalar_prefetch=1,
        grid=(T // rows,),
        in_specs=[pl.BlockSpec(memory_space=pl.ANY),
                  pl.BlockSpec((rows, D), lambda s, dst: (s, 0)),
                  pl.BlockSpec((rows, LANES), lambda s, dst: (s, 0)),
                  pl.BlockSpec((1, D), lambda s, dst: (0, 0))],
        out_specs=pl.BlockSpec((rows, D), lambda s, dst: (s, 0)),
        scratch_shapes=[pltpu.VMEM((2 * TOP_K, rows * PACK_ROWS, LANES), jnp.uint32),
                        pltpu.SemaphoreType.DMA((2,))],
    )
    return pl.pallas_call(
        _combine_kernel,
        out_shape=jax.ShapeDtypeStruct((T, D), F32),
        grid_spec=grid_spec,
        compiler_params=_cparams(("arbitrary",)),
        name="moe_combine_final_norm",
    )(dest, ys, x1, wts, gain)


def dispatch_plan(ids, counts):
    experts_ = jnp.arange(N_EXPERTS, dtype=jnp.int32)
    padded = (counts + MOE_BLOCK - 1) // MOE_BLOCK * MOE_BLOCK
    pad_end = jnp.cumsum(padded)
    pad_start = pad_end - padded
    e = ids[:, :TOP_K]
    start_of = jnp.sum(jnp.where(e[:, :, None] == experts_[None, None, :], pad_start[None, None, :], 0), axis=-1)
    dest = (start_of + ids[:, TOP_K:2 * TOP_K]).reshape(N_ASSIGN).astype(jnp.int32)
    n_used = pad_end[-1] // MOE_BLOCK
    block_start = jnp.minimum(jnp.arange(MOE_BLOCKS, dtype=jnp.int32), n_used - 1) * MOE_BLOCK
    block_expert = jnp.sum((block_start[:, None] >= pad_end[None, :]).astype(jnp.int32), axis=1)
    block_expert = jnp.minimum(block_expert, N_EXPERTS - 1).astype(jnp.int32)
    tail = n_used + experts_
    zero_start = jnp.concatenate([jnp.where(counts > 0, pad_end - MOE_BLOCK, -1),
                                  jnp.where(tail < MOE_BLOCKS, tail * MOE_BLOCK, -1)]).astype(jnp.int32)
    present = counts > 0
    expert_ord = jnp.cumsum(present.astype(jnp.int32)) - 1
    slots = jnp.arange(N_EXPERTS + 2, dtype=jnp.int32)
    hit = present[None, :] & (expert_ord[None, :] == slots[:, None])
    expert_seq = jnp.where(jnp.any(hit, axis=1), jnp.sum(jnp.where(hit, experts_[None, :], 0), axis=1), -1)
    block_ord = jnp.sum(jnp.where(block_expert[:, None] == experts_[None, :], expert_ord[None, :], 0), axis=1)
    return (block_expert, n_used.astype(jnp.int32).reshape(1), dest, zero_start,
            block_ord.astype(jnp.int32), expert_seq.astype(jnp.int32))


def kernel(x, norm_mix, w_in, b_gate, ssm_a_re, ssm_a_im, ssm_log_step, ssm_b_re, ssm_b_im, ssm_c_re, ssm_c_im, ssm_d, w_glu, b_glu, w_up_attn, w_up_ssm, w_out, norm_ffn, w_router_group, b_router_group, w_router_expert, b_router_expert, w_expert_gate, w_expert_up, w_expert_down, norm_final):
    x2 = x.reshape(T, D)
    h = norm_permute(x2, norm_mix.reshape(1, D)).reshape(T, D)
    w_in_l = w_in.reshape(D, IN_COLS)
    qkv = proj(h, w_in_l, 0, 3 * QKV_COLS, tn=QKV_COLS, name="proj_qkv")
    u = proj(h, w_in_l, 3 * QKV_COLS, SSM_W, tn=SSM_W, name="proj_ssm_in")

    attn = attention(qkv.reshape(R, NI, 3 * QKV_COLS)).reshape(T, ATTN_OUT)

    G = SSM_W // SSM_CH
    wb, wc, pows = ssm_params(
        ssm_a_re.reshape(G, SSM_STATE).astype(F32), ssm_a_im.reshape(G, SSM_STATE).astype(F32),
        ssm_log_step.reshape(G),
        ssm_b_re.reshape(G, SSM_STATE, SSM_CH).astype(F32), ssm_b_im.reshape(G, SSM_STATE, SSM_CH).astype(F32),
        ssm_c_re.reshape(G, SSM_CH, SSM_STATE), ssm_c_im.reshape(G, SSM_CH, SSM_STATE))
    yg = ssm_scan(u.reshape(R, NI, SSM_W), wb, wc, pows, ssm_d.reshape(1, SSM_W).astype(F32))
    ssm = glu(yg.reshape(T, SSM_W), w_glu.reshape(SSM_W, SSM_W), b_glu.reshape(1, SSM_W))

    merged = merge(h, attn, ssm, w_in_l, b_gate.reshape(1, 2 * D),
                   w_up_attn.reshape(ATTN_OUT, D), w_up_ssm.reshape(SSM_W, D))
    x1 = out_proj(merged.reshape(R, NI, D), w_out.reshape(D, D), x2)

    w_r = jnp.concatenate([w_router_group.reshape(D, N_EGROUPS), w_router_expert.reshape(D, N_EXPERTS),
                           jnp.zeros((D, LANES - N_EGROUPS - N_EXPERTS), F32)], axis=1)
    b_r = jnp.concatenate([b_router_group.reshape(1, N_EGROUPS), b_router_expert.reshape(1, N_EXPERTS),
                           jnp.zeros((1, LANES - N_EGROUPS - N_EXPERTS), F32)], axis=1)
    hpk, ids, wts, counts = router(x1, norm_ffn.reshape(1, D), w_r, b_r)

    block_expert, n_used, dest, zero_start, block_ord, expert_seq = dispatch_plan(ids[:, :2 * TOP_K], counts[0, :N_EXPERTS])
    xs = dispatch(dest, zero_start, hpk)
    ys = experts(block_expert, n_used, block_ord, expert_seq, xs,
                 w_expert_gate.reshape(N_EXPERTS, D, EXPERT_FF), w_expert_up.reshape(N_EXPERTS, D, EXPERT_FF),
                 w_expert_down.reshape(N_EXPERTS, EXPERT_FF, D))
    out = combine(dest, ys, x1, wts, norm_final.reshape(1, D))
    return out.reshape(1, T, D)
```

```python
import functools

import jax
import jax.numpy as jnp
from jax import lax
from jax.experimental import pallas as pl
from jax.experimental.pallas import tpu as pltpu

F32 = jnp.float32
BF16 = jnp.bfloat16

T = 8192
D = 2048
R = 16
NI = T // R
HEAD_DIM = 64
N_HEAD_SLOTS = 8
DILATIONS = (1, 4, 16)
ATTN_BLOCK = 128
QKV_COLS = 1536
ATTN_OUT = 512
SSM_W = 1024
SSM_STATE = 64
SSM_CH = 16
IN_COLS = 3 * QKV_COLS + SSM_W + 2 * D
N_EXPERTS = 32
N_EGROUPS = 4
EXPERTS_PER_GROUP = 8
TOP_K = 2
EXPERT_FF = 512
NORM_EPS = 1e-6
LANES = 128
VMEM_LIMIT = 48 * 1024 * 1024

MOE_BLOCK = 256
N_ASSIGN = T * TOP_K
MOE_BLOCKS = N_ASSIGN // MOE_BLOCK + N_EXPERTS
MOE_ROWS = MOE_BLOCKS * MOE_BLOCK


def _cparams(sem):
    return pltpu.CompilerParams(dimension_semantics=sem, vmem_limit_bytes=VMEM_LIMIT)


N_SLABS = D // LANES
NORM_TB = 64
NORM_CHUNK = 64
NORM_PITCH = R + 8


def _norm_permute_kernel(x_ref, g_ref, h_ref, slab):
    g = g_ref[...]

    def chunk(t, c):
        rows = pl.ds(pl.multiple_of(t * NORM_CHUNK, NORM_CHUNK), NORM_CHUNK)
        x = x_ref[rows, :]
        ms = jnp.mean(x * x, axis=-1, keepdims=True)
        hn = x * lax.rsqrt(ms + NORM_EPS) * g
        for k in range(NORM_CHUNK // R):
            dst = pl.ds(pl.multiple_of((t * (NORM_CHUNK // R) + k) * NORM_PITCH, 8), R)
            for s_ in range(N_SLABS):
                slab[s_, dst, :] = hn[k * R:(k + 1) * R, s_ * LANES:(s_ + 1) * LANES]
        return c

    lax.fori_loop(0, NORM_TB * R // NORM_CHUNK, chunk, 0)
    for r in range(R):
        pieces = [slab[s_, pl.ds(r, NORM_TB, stride=NORM_PITCH), :] for s_ in range(N_SLABS)]
        h_ref[r] = jnp.concatenate(pieces, axis=1).astype(BF16)


def norm_permute(x2, gain):
    return pl.pallas_call(
        _norm_permute_kernel,
        out_shape=jax.ShapeDtypeStruct((R, NI, D), BF16),
        grid=(NI // NORM_TB,),
        in_specs=[pl.BlockSpec((NORM_TB * R, D), lambda i: (i, 0)),
                  pl.BlockSpec((1, D), lambda i: (0, 0))],
        out_specs=pl.BlockSpec((R, NORM_TB, D), lambda i: (0, i, 0)),
        scratch_shapes=[pltpu.VMEM((N_SLABS, NORM_TB * NORM_PITCH, LANES), F32)],
        compiler_params=_cparams(("arbitrary",)),
        name="norm_permute",
    )(x2, gain)


MXU_N = 256


def _stationary_weight_tile(w_hbm, col0, stage_ref, wbf_ref, sem):
    j = pl.program_id(0)
    tn = stage_ref.shape[1]

    def copy(jj):
        c0 = pl.multiple_of(col0 + jj * tn, LANES)
        return pltpu.make_async_copy(w_hbm.at[:, pl.ds(c0, tn)], stage_ref, sem)

    @pl.when(pl.program_id(1) == 0)
    def _():
        @pl.when(j == 0)
        def _():
            copy(0).start()

        copy(j).wait()
        wbf_ref[...] = stage_ref[...].astype(BF16)

        @pl.when(j + 1 < pl.num_programs(0))
        def _():
            copy(j + 1).start()


def _proj_kernel(a_ref, w_hbm, o_ref, stage_ref, wbf_ref, sem, *, col_off):
    _stationary_weight_tile(w_hbm, col_off, stage_ref, wbf_ref, sem.at[0])
    a = a_ref[...]
    for c in range(o_ref.shape[1] // MXU_N):
        cols = slice(c * MXU_N, (c + 1) * MXU_N)
        o_ref[:, cols] = jnp.dot(a, wbf_ref[:, cols], preferred_element_type=F32)


def proj(h, w_in, col_off, n_cols, tn, name="proj"):
    tm = 1024
    return pl.pallas_call(
        functools.partial(_proj_kernel, col_off=col_off),
        out_shape=jax.ShapeDtypeStruct((T, n_cols), F32),
        grid=(n_cols // tn, T // tm),
        in_specs=[pl.BlockSpec((tm, D), lambda j, i: (i, 0)),
                  pl.BlockSpec(memory_space=pl.ANY)],
        out_specs=pl.BlockSpec((tm, tn), lambda j, i: (i, j)),
        scratch_shapes=[pltpu.VMEM((D, tn), F32), pltpu.VMEM((D, tn), BF16), pltpu.SemaphoreType.DMA((1,))],
        compiler_params=_cparams(("arbitrary", "arbitrary")),
        name=name,
    )(h, w_in)


def _seq_index_maps(d):
    nseg = R // d
    qlen = ATTN_BLOCK // nseg
    return nseg, qlen


def _bias_matrices(d, hp):
    nseg, qlen = _seq_index_maps(d)
    klen = 2 * qlen
    row = lax.broadcasted_iota(jnp.int32, (2 * ATTN_BLOCK, 2 * ATTN_BLOCK), 0)
    col = lax.broadcasted_iota(jnp.int32, (2 * ATTN_BLOCK, 2 * ATTN_BLOCK), 1)
    rho = row % ATTN_BLOCK
    jq = (rho % qlen) * nseg + rho // qlen
    jk = ((col % klen) - qlen) * nseg + col // klen
    steps = jq - jk
    valid = (steps >= 0) & (steps <= ATTN_BLOCK)
    head = 2 * hp + row // ATTN_BLOCK
    slope = lax.bitcast_convert_type((127 - (head + 1)) << 23, F32)
    bias = -slope * (d * steps).astype(F32)
    neg = jnp.float32(-jnp.inf)
    return jnp.where(valid, bias, neg), jnp.where(valid & (jk >= 0), bias, neg)


def _attend_pair(q, k, v, bias):
    lane = lax.broadcasted_iota(jnp.int32, (ATTN_BLOCK, LANES), 1)
    first = lane < HEAD_DIM
    zero = jnp.zeros_like(q)
    q2 = jnp.concatenate([jnp.where(first, q, zero), jnp.where(first, zero, q)], axis=0).astype(BF16)
    s = lax.dot_general(q2, k.astype(BF16), (((1,), (1,)), ((), ())), preferred_element_type=F32)
    s = s + bias
    m = jnp.max(s, axis=-1, keepdims=True)
    p = jnp.exp(s - m)
    l = jnp.sum(p, axis=-1, keepdims=True)
    o2 = jnp.dot(p.astype(BF16), v.astype(BF16), preferred_element_type=F32)
    o = jnp.where(first, o2[:ATTN_BLOCK], o2[ATTN_BLOCK:])
    m_b = jnp.where(first, m[:ATTN_BLOCK], m[ATTN_BLOCK:])
    l_b = jnp.where(first, l[:ATTN_BLOCK], l[ATTN_BLOCK:])
    return o, m_b, l_b


def _attn_kernel(q_ref, kp_ref, kc_ref, vp_ref, vc_ref, o_ref, kbuf, vbuf, obuf, mbuf, lbuf, bias_ref):
    hp = pl.program_id(0)
    it = pl.program_id(1)
    g = pl.program_id(2)
    scale = HEAD_DIM ** -0.5

    for gi, d in enumerate(DILATIONS):
        nseg, qlen = _seq_index_maps(d)
        klen = 2 * qlen
        nblk = ATTN_BLOCK // qlen

        @pl.when(g == gi)
        def _(gi=gi, d=d, nseg=nseg, qlen=qlen, klen=klen, nblk=nblk):
            if nseg > 1:
                kbuf[:, :ATTN_BLOCK, :] = kp_ref[...]
                kbuf[:, ATTN_BLOCK:, :] = kc_ref[...]
                vbuf[:, :ATTN_BLOCK, :] = vp_ref[...]
                vbuf[:, ATTN_BLOCK:, :] = vc_ref[...]

            @pl.when(it == 0)
            def _():
                b_reg, b_first = _bias_matrices(d, hp)
                bias_ref[gi, 0] = b_reg
                bias_ref[gi, 1] = b_first

            def block(idx, carry):
                rd = idx // nblk
                bb = idx % nblk
                q0 = pl.multiple_of(bb * qlen, qlen)
                k0 = pl.multiple_of(ATTN_BLOCK + bb * qlen - qlen, qlen)
                qs, ks, vs = [], [], []
                for m_ in range(nseg):
                    rr = rd + d * m_
                    qs.append(q_ref[rr, pl.ds(q0, qlen), :])
                    if nseg > 1:
                        ks.append(kbuf[rr, pl.ds(k0, klen), :])
                        vs.append(vbuf[rr, pl.ds(k0, klen), :])
                    else:
                        ks += [kp_ref[rr], kc_ref[rr]]
                        vs += [vp_ref[rr], vc_ref[rr]]
                q = jnp.concatenate(qs, axis=0) * scale
                k = jnp.concatenate(ks, axis=0)
                v = jnp.concatenate(vs, axis=0)
                is_first = jnp.logical_and(it == 0, bb == 0)
                bias = bias_ref[gi, jnp.where(is_first, 1, 0)]
                o, mx, den = _attend_pair(q, k, v, bias)
                for m_ in range(nseg):
                    rr = rd + d * m_
                    seg = slice(m_ * qlen, (m_ + 1) * qlen)
                    obuf[gi, rr, pl.ds(q0, qlen), :] = o[seg]
                    mbuf[gi, rr, pl.ds(q0, qlen), :] = mx[seg]
                    lbuf[gi, rr, pl.ds(q0, qlen), :] = den[seg]
                return carry

            lax.fori_loop(0, d * nblk, block, 0, unroll=True)

    @pl.when(g == len(DILATIONS) - 1)
    def _():
        for r in range(R):
            m0, m1, m2 = mbuf[0, r], mbuf[1, r], mbuf[2, r]
            mx = jnp.maximum(jnp.maximum(m0, m1), m2)
            e0, e1, e2 = jnp.exp(m0 - mx), jnp.exp(m1 - mx), jnp.exp(m2 - mx)
            den = e0 * lbuf[0, r] + e1 * lbuf[1, r] + e2 * lbuf[2, r]
            num = e0 * obuf[0, r] + e1 * obuf[1, r] + e2 * obuf[2, r]
            o_ref[r] = (num / den).astype(BF16)


def attention(qkv3):
    n_hp = N_HEAD_SLOTS // 2
    n_it = NI // ATTN_BLOCK
    ng = len(DILATIONS)
    cb = QKV_COLS // LANES

    def cur(base):
        return pl.BlockSpec((R, ATTN_BLOCK, LANES), lambda hp, it, g: (0, it, base + g * n_hp + hp))

    def prev(base):
        return pl.BlockSpec((R, ATTN_BLOCK, LANES),
                            lambda hp, it, g: (0, jnp.maximum(it - 1, 0), base + g * n_hp + hp))

    return pl.pallas_call(
        _attn_kernel,
        out_shape=jax.ShapeDtypeStruct((R, NI, ATTN_OUT), BF16),
        grid=(n_hp, n_it, ng),
        in_specs=[cur(0), prev(cb), cur(cb), prev(2 * cb), cur(2 * cb)],
        out_specs=pl.BlockSpec((R, ATTN_BLOCK, LANES), lambda hp, it, g: (0, it, hp)),
        scratch_shapes=[pltpu.VMEM((R, 2 * ATTN_BLOCK, LANES), F32),
                        pltpu.VMEM((R, 2 * ATTN_BLOCK, LANES), F32),
                        pltpu.VMEM((ng, R, ATTN_BLOCK, LANES), F32),
                        pltpu.VMEM((ng, R, ATTN_BLOCK, LANES), F32),
                        pltpu.VMEM((ng, R, ATTN_BLOCK, LANES), F32),
                        pltpu.VMEM((ng, 2, 2 * ATTN_BLOCK, 2 * ATTN_BLOCK), F32)],
        compiler_params=_cparams(("arbitrary", "arbitrary", "arbitrary")),
        name="dilated_attention",
    )(qkv3, qkv3, qkv3, qkv3, qkv3)


SSM_SLAB = 256
SSM_SLABS = SSM_W // SSM_SLAB
SLAB_STATES = SSM_SLAB // SSM_CH * SSM_STATE
SSM_TI = 128
SSM_MM_CHUNK = 4


def _ssm_kernel(u_ref, wb_ref, wc_ref, pw_ref, dsk_ref, o_ref, s_ref, zs_ref, zc_ref, tr_ref, zt_ref):
    ic = pl.program_id(1)
    ns = SLAB_STATES

    @pl.when(ic == 0)
    def _():
        zc_ref[...] = jnp.zeros_like(zc_ref)

    n_chunks = R // SSM_MM_CHUNK
    n_tiles = SSM_TI // 8
    wb = wb_ref[0]
    wc = wc_ref[0]
    dsk = dsk_ref[...]
    arb = jnp.broadcast_to(pw_ref[0, 0:1, :ns], (8, ns))
    aib = jnp.broadcast_to(pw_ref[0, 0:1, ns:], (8, ns))


    def bu_chunk(c):
        lo = c * SSM_MM_CHUNK
        uc = u_ref[lo:lo + SSM_MM_CHUNK].reshape(SSM_MM_CHUNK * SSM_TI, SSM_SLAB)
        bu = jnp.dot(uc.astype(BF16), wb, preferred_element_type=F32)
        s_ref[lo:lo + SSM_MM_CHUNK] = bu.reshape(SSM_MM_CHUNK, SSM_TI, 2 * ns)

    def local_chunk(c):
        lo = c * SSM_MM_CHUNK
        first = max(lo, 1)
        for t in range(n_tiles):
            rows = slice(t * 8, (t + 1) * 8)
            pr = s_ref[first - 1, rows, :ns]
            pi = s_ref[first - 1, rows, ns:]
            for r in range(first, lo + SSM_MM_CHUNK):
                nr = s_ref[r, rows, :ns] + (arb * pr - aib * pi)
                ni = s_ref[r, rows, ns:] + (arb * pi + aib * pr)
                s_ref[r, rows, :ns] = nr
                s_ref[r, rows, ns:] = ni
                pr, pi = nr, ni

    bu_chunk(0)
    for c in range(n_chunks):
        if c + 1 < n_chunks:
            bu_chunk(c + 1)
        local_chunk(c)

    n_ch = ns // LANES
    for c in range(n_ch):
        tr_ref[0, c * 8:c * 8 + 1, :] = pw_ref[0, R - 1:R, c * LANES:(c + 1) * LANES]
        tr_ref[1, c * 8:c * 8 + 1, :] = pw_ref[0, R - 1:R, ns + c * LANES:ns + (c + 1) * LANES]
    a16r = tr_ref[0, pl.ds(0, n_ch, stride=8), :]
    a16i = tr_ref[1, pl.ds(0, n_ch, stride=8), :]

    def ztile(t, carry):
        zr, zi = carry
        rows = pl.ds(pl.multiple_of(t * 8, 8), 8)
        for c in range(n_ch):
            tr_ref[0, c * 8:(c + 1) * 8, :] = s_ref[R - 1, rows, c * LANES:(c + 1) * LANES]
            tr_ref[1, c * 8:(c + 1) * 8, :] = s_ref[R - 1, rows, ns + c * LANES:ns + (c + 1) * LANES]
        for i in range(8):
            zt_ref[0, pl.ds(i, n_ch, stride=8), :] = zr
            zt_ref[1, pl.ds(i, n_ch, stride=8), :] = zi
            er = tr_ref[0, pl.ds(i, n_ch, stride=8), :]
            ei = tr_ref[1, pl.ds(i, n_ch, stride=8), :]
            zr, zi = a16r * zr - a16i * zi + er, a16r * zi + a16i * zr + ei
        for c in range(n_ch):
            zs_ref[rows, c * LANES:(c + 1) * LANES] = zt_ref[0, c * 8:(c + 1) * 8, :]
            zs_ref[rows, ns + c * LANES:ns + (c + 1) * LANES] = zt_ref[1, c * 8:(c + 1) * 8, :]
        return zr, zi

    zr_end, zi_end = lax.fori_loop(0, SSM_TI // 8, ztile, (zc_ref[0], zc_ref[1]))
    zc_ref[0] = zr_end
    zc_ref[1] = zi_end

    def fix_chunk(c):
        lo = c * SSM_MM_CHUNK
        for r in range(lo, lo + SSM_MM_CHUNK):
            prb = jnp.broadcast_to(pw_ref[0, r:r + 1, :ns], (8, ns))
            pib = jnp.broadcast_to(pw_ref[0, r:r + 1, ns:], (8, ns))
            for t in range(n_tiles):
                rows = slice(t * 8, (t + 1) * 8)
                zr = zs_ref[rows, :ns]
                zi = zs_ref[rows, ns:]
                s_ref[r, rows, :ns] = s_ref[r, rows, :ns] + (prb * zr - pib * zi)
                s_ref[r, rows, ns:] = s_ref[r, rows, ns:] + (prb * zi + pib * zr)

    def out_chunk(c):
        lo = c * SSM_MM_CHUNK
        xs = s_ref[lo:lo + SSM_MM_CHUNK].reshape(SSM_MM_CHUNK * SSM_TI, 2 * ns)
        y = jnp.dot(xs.astype(BF16), wc, preferred_element_type=F32)
        y = y.reshape(SSM_MM_CHUNK, SSM_TI, SSM_SLAB) + dsk * u_ref[lo:lo + SSM_MM_CHUNK]
        o_ref[lo:lo + SSM_MM_CHUNK] = jax.nn.gelu(y)

    fix_chunk(0)
    for c in range(n_chunks):
        if c + 1 < n_chunks:
            fix_chunk(c + 1)
        out_chunk(c)


def ssm_scan(u3, wb, wc, pows, dskip):
    ns2 = 2 * SLAB_STATES
    return pl.pallas_call(
        _ssm_kernel,
        out_shape=jax.ShapeDtypeStruct((R, NI, SSM_W), F32),
        grid=(SSM_SLABS, NI // SSM_TI),
        in_specs=[pl.BlockSpec((R, SSM_TI, SSM_SLAB), lambda kb, ic: (0, ic, kb)),
                  pl.BlockSpec((1, SSM_SLAB, ns2), lambda kb, ic: (kb, 0, 0)),
                  pl.BlockSpec((1, ns2, SSM_SLAB), lambda kb, ic: (kb, 0, 0)),
                  pl.BlockSpec((1, R, ns2), lambda kb, ic: (kb, 0, 0)),
                  pl.BlockSpec((1, SSM_SLAB), lambda kb, ic: (0, kb))],
        out_specs=pl.BlockSpec((R, SSM_TI, SSM_SLAB), lambda kb, ic: (0, ic, kb)),
        scratch_shapes=[pltpu.VMEM((R, SSM_TI, ns2), F32),
                        pltpu.VMEM((SSM_TI, ns2), F32),
                        pltpu.VMEM((2, SLAB_STATES // LANES, LANES), F32),
                        pltpu.VMEM((2, 8 * SLAB_STATES // LANES, LANES), F32),
                        pltpu.VMEM((2, 8 * SLAB_STATES // LANES, LANES), F32)],
        compiler_params=_cparams(("arbitrary", "arbitrary")),
        name="s5_ssm",
    )(u3, wb, wc, pows, dskip)


def ssm_params(a_re, a_im, log_step, b_re, b_im, c_re, c_im):
    G, P, H = SSM_W // SSM_CH, SSM_STATE, SSM_CH
    gs = SSM_SLAB // SSM_CH
    step = jnp.exp(log_step.astype(F32))[:, None]
    kk = jnp.arange(1, R + 1, dtype=F32)[:, None, None]
    mag = jnp.exp(kk * (a_re * step))
    ang = kk * (a_im * step)
    pw_re, pw_im = mag * jnp.cos(ang), mag * jnp.sin(ang)
    abar_re, abar_im = pw_re[0], pw_im[0]
    nr, ni = abar_re - 1.0, abar_im
    den = a_re * a_re + a_im * a_im
    f_re = (nr * a_re + ni * a_im) / den
    f_im = (ni * a_re - nr * a_im) / den
    bbar_re = f_re[..., None] * b_re - f_im[..., None] * b_im
    bbar_im = f_re[..., None] * b_im + f_im[..., None] * b_re
    rows_g = jnp.arange(gs * H, dtype=jnp.int32) // H
    cols_g = jnp.arange(gs * P, dtype=jnp.int32) // P
    b_t = jnp.concatenate([jnp.swapaxes(bbar_re, 1, 2).reshape(SSM_SLABS, gs * H, P),
                           jnp.swapaxes(bbar_im, 1, 2).reshape(SSM_SLABS, gs * H, P)], axis=-1)
    tile_p = (jnp.arange(2 * P, dtype=jnp.int32)[:, None] ==
              (jnp.arange(2 * gs * P, dtype=jnp.int32) // (gs * P) * P + jnp.arange(2 * gs * P, dtype=jnp.int32) % P)[None, :])
    mask_b = rows_g[:, None] == jnp.tile(cols_g, 2)[None, :]
    wb = jnp.where(mask_b[None], jnp.einsum('kap,pc->kac', b_t.astype(BF16), tile_p.astype(BF16),
                                            preferred_element_type=F32), 0.0).astype(BF16)
    c_t = jnp.concatenate([jnp.swapaxes(c_re.astype(F32), 1, 2).reshape(SSM_SLABS, gs * P, H),
                           -jnp.swapaxes(c_im.astype(F32), 1, 2).reshape(SSM_SLABS, gs * P, H)], axis=1)
    tile_h = jnp.arange(H, dtype=jnp.int32)[:, None] == (jnp.arange(gs * H, dtype=jnp.int32) % H)[None, :]
    mask_c = jnp.tile(cols_g, 2)[:, None] == rows_g[None, :]
    wc = jnp.where(mask_c[None], jnp.einsum('kah,hc->kac', c_t.astype(BF16), tile_h.astype(BF16),
                                            preferred_element_type=F32), 0.0).astype(BF16)
    pows = jnp.concatenate([pw_re.reshape(R, SSM_SLABS, gs * P), pw_im.reshape(R, SSM_SLABS, gs * P)], axis=-1)
    return wb, wc, jnp.transpose(pows, (1, 0, 2))


def _glu_kernel(a_ref, w_ref, b_ref, o_ref, wbf_ref):
    @pl.when(pl.program_id(0) == 0)
    def _():
        wbf_ref[...] = w_ref[...].astype(BF16)

    a = a_ref[...].astype(BF16)
    for c in range(SSM_W // MXU_N):
        cols = slice(c * MXU_N, (c + 1) * MXU_N)
        acc = jnp.dot(a, wbf_ref[:, cols], preferred_element_type=F32)
        o_ref[:, cols] = (a_ref[:, cols] * jax.nn.sigmoid(acc + b_ref[:, cols])).astype(BF16)


def glu(yg, w_glu, b_glu):
    tm = 1024
    return pl.pallas_call(
        _glu_kernel,
        out_shape=jax.ShapeDtypeStruct((T, SSM_W), BF16),
        grid=(T // tm,),
        in_specs=[pl.BlockSpec((tm, SSM_W), lambda i: (i, 0)),
                  pl.BlockSpec((SSM_W, SSM_W), lambda i: (0, 0)),
                  pl.BlockSpec((1, SSM_W), lambda i: (0, 0))],
        out_specs=pl.BlockSpec((tm, SSM_W), lambda i: (i, 0)),
        scratch_shapes=[pltpu.VMEM((SSM_W, SSM_W), BF16)],
        compiler_params=_cparams(("arbitrary",)),
        name="ssm_glu",
    )(yg, w_glu, b_glu)


def _merge_kernel(h_ref, at_ref, ss_ref, win_hbm, ba_ref, bs_ref, wa_hbm, ws_hbm, o_ref,
                  wga_st, wgs_st, wa_st, ws_st, wga_bf, wgs_bf, wa_bf, ws_bf, sem):
    gate_col = 3 * QKV_COLS + SSM_W
    _stationary_weight_tile(win_hbm, gate_col, wga_st, wga_bf, sem.at[0])
    _stationary_weight_tile(win_hbm, gate_col + D, wgs_st, wgs_bf, sem.at[1])
    _stationary_weight_tile(wa_hbm, 0, wa_st, wa_bf, sem.at[2])
    _stationary_weight_tile(ws_hbm, 0, ws_st, ws_bf, sem.at[3])
    h = h_ref[...]
    at = at_ref[...]
    ss = ss_ref[...]
    for c in range(o_ref.shape[1] // MXU_N):
        cols = slice(c * MXU_N, (c + 1) * MXU_N)
        ga = jax.nn.sigmoid(jnp.dot(h, wga_bf[:, cols], preferred_element_type=F32) + ba_ref[:, cols])
        a = jnp.dot(at, wa_bf[:, cols], preferred_element_type=F32)
        gs = jax.nn.sigmoid(jnp.dot(h, wgs_bf[:, cols], preferred_element_type=F32) + bs_ref[:, cols])
        s = jnp.dot(ss, ws_bf[:, cols], preferred_element_type=F32)
        o_ref[:, cols] = (ga * a + gs * s).astype(BF16)


def merge(h, attn, ssm, w_in, b_gate, w_up_attn, w_up_ssm):
    tm, tn = 1024, 512
    nj = D // tn
    any_space = pl.BlockSpec(memory_space=pl.ANY)
    return pl.pallas_call(
        _merge_kernel,
        out_shape=jax.ShapeDtypeStruct((T, D), BF16),
        grid=(nj, T // tm),
        in_specs=[pl.BlockSpec((tm, D), lambda j, i: (i, 0)),
                  pl.BlockSpec((tm, ATTN_OUT), lambda j, i: (i, 0)),
                  pl.BlockSpec((tm, SSM_W), lambda j, i: (i, 0)),
                  any_space,
                  pl.BlockSpec((1, tn), lambda j, i: (0, j)),
                  pl.BlockSpec((1, tn), lambda j, i: (0, j + nj)),
                  any_space,
                  any_space],
        out_specs=pl.BlockSpec((tm, tn), lambda j, i: (i, j)),
        scratch_shapes=[pltpu.VMEM((D, tn), F32), pltpu.VMEM((D, tn), F32),
                        pltpu.VMEM((ATTN_OUT, tn), F32), pltpu.VMEM((SSM_W, tn), F32),
                        pltpu.VMEM((D, tn), BF16), pltpu.VMEM((D, tn), BF16),
                        pltpu.VMEM((ATTN_OUT, tn), BF16), pltpu.VMEM((SSM_W, tn), BF16),
                        pltpu.SemaphoreType.DMA((4,))],
        compiler_params=_cparams(("arbitrary", "arbitrary")),
        name="gates_branch_merge",
    )(h, attn, ssm, w_in, b_gate, b_gate, w_up_attn, w_up_ssm)


OUTPROJ_TB = 64
OUTPROJ_PITCH = OUTPROJ_TB + 8


def _outproj_kernel(m_ref, w_hbm, x_ref, o_ref, slab, stage_ref, wbf_ref, sem):
    _stationary_weight_tile(w_hbm, 0, stage_ref, wbf_ref, sem.at[0])
    tb = OUTPROJ_TB
    a = m_ref[...].reshape(R * tb, D)
    per = MXU_N // LANES
    for c in range(o_ref.shape[1] // MXU_N):
        acc = jnp.dot(a, wbf_ref[:, c * MXU_N:(c + 1) * MXU_N], preferred_element_type=F32)
        for s_ in range(per):
            lanes = slice(s_ * LANES, (s_ + 1) * LANES)
            for r in range(R):
                slab[c * per + s_, r * OUTPROJ_PITCH:r * OUTPROJ_PITCH + tb, :] = acc[r * tb:(r + 1) * tb, lanes]
        for s_ in range(per):
            lanes = slice((c * per + s_) * LANES, (c * per + s_ + 1) * LANES)
            for i in range(tb):
                rows = slice(i * R, (i + 1) * R)
                o_ref[rows, lanes] = slab[c * per + s_, pl.ds(i, R, stride=OUTPROJ_PITCH), :] + x_ref[rows, lanes]


def out_proj(merged3, w_out, x2):
    tb, tn = OUTPROJ_TB, 1024
    return pl.pallas_call(
        _outproj_kernel,
        out_shape=jax.ShapeDtypeStruct((T, D), F32),
        grid=(D // tn, NI // tb),
        in_specs=[pl.BlockSpec((R, tb, D), lambda j, i: (0, i, 0)),
                  pl.BlockSpec(memory_space=pl.ANY),
                  pl.BlockSpec((tb * R, tn), lambda j, i: (i, j))],
        out_specs=pl.BlockSpec((tb * R, tn), lambda j, i: (i, j)),
        scratch_shapes=[pltpu.VMEM((tn // LANES, OUTPROJ_PITCH * R, LANES), F32),
                        pltpu.VMEM((D, tn), F32), pltpu.VMEM((D, tn), BF16),
                        pltpu.SemaphoreType.DMA((1,))],
        compiler_params=_cparams(("arbitrary", "arbitrary")),
        name="out_proj_residual",
    )(merged3, w_out, x2)


PACK_ROWS = 8
HALF_D = D // 2


def _pack_bf16_pairs(zf):
    top = lax.bitcast_convert_type(zf, jnp.uint32)
    return top[:, HALF_D:] | (top[:, :HALF_D] >> 16)


def _unpack_bf16_pairs(x_ref, first, n):
    lo, hi = [], []
    for c in range(PACK_ROWS):
        w = x_ref[pl.ds(first * PACK_ROWS + c, n, stride=PACK_ROWS), :]
        lo.append(lax.bitcast_convert_type(w << 16, F32))
        hi.append(lax.bitcast_convert_type(w & jnp.uint32(0xFFFF0000), F32))
    return jnp.concatenate(lo + hi, axis=1)


def _router_kernel(x_ref, g_ref, w_ref, b_ref, h_ref, id_ref, wt_ref, cnt_ref, carry_ref):
    step = pl.program_id(0)

    @pl.when(step == 0)
    def _():
        carry_ref[...] = jnp.zeros_like(carry_ref)

    x = x_ref[...]
    ms = jnp.mean(x * x, axis=-1, keepdims=True)
    z = x * lax.rsqrt(ms + NORM_EPS) * g_ref[...]
    zh = z.astype(BF16)
    zf = zh.astype(F32)
    word = _pack_bf16_pairs(zf)
    for c in range(PACK_ROWS):
        h_ref[pl.ds(c, x.shape[0], stride=PACK_ROWS), :] = word[:, c * LANES:(c + 1) * LANES]
    zl = (z - zf).astype(BF16)
    w = w_ref[...]
    wh = w.astype(BF16)
    wl = (w - wh.astype(F32)).astype(BF16)
    logits = (jnp.dot(zh, wh, preferred_element_type=F32) + jnp.dot(zl, wh, preferred_element_type=F32)
              + jnp.dot(zh, wl, preferred_element_type=F32)) + b_ref[...]
    lane = lax.broadcasted_iota(jnp.int32, logits.shape, 1)
    lanef = lane.astype(F32)
    neg = jnp.float32(-jnp.inf)
    big = jnp.float32(1e9)
    gl = jnp.where(lane < N_EGROUPS, logits, neg)
    gmax = jnp.max(gl, axis=-1, keepdims=True)
    gidx = jnp.min(jnp.where(gl == gmax, lanef, big), axis=-1, keepdims=True)
    pg = 1.0 / jnp.sum(jnp.exp(gl - gmax), axis=-1, keepdims=True)
    lo = N_EGROUPS + EXPERTS_PER_GROUP * gidx
    el = jnp.where((lanef >= lo) & (lanef < lo + EXPERTS_PER_GROUP), logits, neg)
    t1 = jnp.max(el, axis=-1, keepdims=True)
    j1 = jnp.min(jnp.where(el == t1, lanef, big), axis=-1, keepdims=True)
    el2 = jnp.where(lanef == j1, neg, el)
    t2 = jnp.max(el2, axis=-1, keepdims=True)
    j2 = jnp.min(jnp.where(el2 == t2, lanef, big), axis=-1, keepdims=True)
    e21 = jnp.exp(t2 - t1)
    w1 = pg / (1.0 + e21)
    w2 = pg * e21 / (1.0 + e21)
    e1f = j1 - N_EGROUPS
    e2f = j2 - N_EGROUPS

    tm = x.shape[0]
    oh1 = (lanef == e1f).astype(F32)
    oh2 = (lanef == e2f).astype(F32)
    ri = lax.broadcasted_iota(jnp.int32, (tm, tm), 0)
    ci = lax.broadcasted_iota(jnp.int32, (tm, tm), 1)
    before = (ci < ri).astype(BF16)
    p1 = jnp.dot(before, oh1.astype(BF16), preferred_element_type=F32)
    p2 = jnp.dot(before, oh2.astype(BF16), preferred_element_type=F32)
    carry = carry_ref[...]
    c1 = jnp.sum(oh1, axis=0, keepdims=True)
    c2 = jnp.sum(oh2, axis=0, keepdims=True)
    rank1 = jnp.sum(oh1 * (carry + p1), axis=-1, keepdims=True)
    rank2 = jnp.sum(oh2 * (carry + c1 + p2), axis=-1, keepdims=True)
    carry = carry + c1 + c2
    carry_ref[...] = carry
    cnt_ref[...] = jnp.broadcast_to(carry, cnt_ref.shape).astype(jnp.int32)

    ids = jnp.where(lane == 0, e1f, jnp.where(lane == 1, e2f, jnp.where(lane == 2, rank1, jnp.where(lane == 3, rank2, 0.0))))
    id_ref[...] = ids.astype(jnp.int32)
    wt_ref[...] = jnp.where(lane == 0, w1, jnp.where(lane == 1, w2, 0.0))


def router(x1, gain, w_r, b_r):
    tm = 512
    return pl.pallas_call(
        _router_kernel,
        out_shape=(jax.ShapeDtypeStruct((T * PACK_ROWS, LANES), jnp.uint32),
                   jax.ShapeDtypeStruct((T, LANES), jnp.int32),
                   jax.ShapeDtypeStruct((T, LANES), F32),
                   jax.ShapeDtypeStruct((8, LANES), jnp.int32)),
        grid=(T // tm,),
        in_specs=[pl.BlockSpec((tm, D), lambda i: (i, 0)),
                  pl.BlockSpec((1, D), lambda i: (0, 0)),
                  pl.BlockSpec((D, LANES), lambda i: (0, 0)),
                  pl.BlockSpec((1, LANES), lambda i: (0, 0))],
        out_specs=(pl.BlockSpec((tm * PACK_ROWS, LANES), lambda i: (i, 0)),
                   pl.BlockSpec((tm, LANES), lambda i: (i, 0)),
                   pl.BlockSpec((tm, LANES), lambda i: (i, 0)),
                   pl.BlockSpec((8, LANES), lambda i: (0, 0))),
        scratch_shapes=[pltpu.VMEM((1, LANES), F32)],
        compiler_params=_cparams(("arbitrary",)),
        name="ffn_norm_router",
    )(x1, gain, w_r, b_r)


DISPATCH_TB = 1024
N_ZERO_FILLS = 2 * N_EXPERTS


def _dispatch_kernel(dest_ref, zs_ref, h_ref, xs_hbm, zbuf, zsem, sem):
    step = pl.program_id(0)

    @pl.when(step == 0)
    def _():
        zbuf[...] = jnp.zeros_like(zbuf)

        def zero_copy(e):
            start = pl.multiple_of(jnp.maximum(zs_ref[e], 0) * PACK_ROWS, PACK_ROWS)
            return pltpu.make_async_copy(zbuf, xs_hbm.at[pl.ds(start, MOE_BLOCK * PACK_ROWS)], zsem.at[0])

        def zstart(e, c):
            @pl.when(zs_ref[e] >= 0)
            def _():
                zero_copy(e).start()
            return c

        def zwait(e, c):
            @pl.when(zs_ref[e] >= 0)
            def _():
                zero_copy(e).wait()
            return c

        lax.fori_loop(0, N_ZERO_FILLS, zstart, 0)
        lax.fori_loop(0, N_ZERO_FILLS, zwait, 0)

    def row_copy(n, k):
        a = (step * DISPATCH_TB + n) * TOP_K + k
        src = h_ref.at[pl.ds(pl.multiple_of(n * PACK_ROWS, PACK_ROWS), PACK_ROWS)]
        dst = xs_hbm.at[pl.ds(pl.multiple_of(dest_ref[a] * PACK_ROWS, PACK_ROWS), PACK_ROWS)]
        return pltpu.make_async_copy(src, dst, sem.at[0])

    def issue(n, c):
        for k in range(TOP_K):
            row_copy(n, k).start(priority=k)
        return c

    def drain(n, c):
        for k in range(TOP_K):
            row_copy(n, k).wait()
        return c

    lax.fori_loop(0, DISPATCH_TB, issue, 0, unroll=8)
    lax.fori_loop(0, DISPATCH_TB, drain, 0, unroll=8)


def dispatch(dest, zero_start, hpk):
    grid_spec = pltpu.PrefetchScalarGridSpec(
        num_scalar_prefetch=2,
        grid=(T // DISPATCH_TB,),
        in_specs=[pl.BlockSpec((DISPATCH_TB * PACK_ROWS, LANES), lambda i, dst, zs: (i, 0))],
        out_specs=pl.BlockSpec(memory_space=pl.ANY),
        scratch_shapes=[pltpu.VMEM((MOE_BLOCK * PACK_ROWS, LANES), jnp.uint32),
                        pltpu.SemaphoreType.DMA((1,)),
                        pltpu.SemaphoreType.DMA((1,))],
    )
    return pl.pallas_call(
        _dispatch_kernel,
        out_shape=jax.ShapeDtypeStruct((MOE_ROWS * PACK_ROWS, LANES), jnp.uint32),
        grid_spec=grid_spec,
        compiler_params=_cparams(("arbitrary",)),
        name="moe_dispatch",
    )(dest, zero_start, hpk)


def _expert_kernel(be_ref, nu_ref, ord_ref, seq_ref, x_ref, wg_hbm, wu_hbm, wd_hbm, y_ref,
                   wg_st, wu_st, wd_st, wg_bf, wu_bf, wd_bf, sem):
    b = pl.program_id(0)

    def weight_copies(e, slot):
        return (pltpu.make_async_copy(wg_hbm.at[e], wg_st.at[slot], sem.at[slot, 0]),
                pltpu.make_async_copy(wu_hbm.at[e], wu_st.at[slot], sem.at[slot, 1]),
                pltpu.make_async_copy(wd_hbm.at[e], wd_st.at[slot], sem.at[slot, 2]))

    def start_fetch(n, slot):
        @pl.when(seq_ref[n] >= 0)
        def _():
            for cp in weight_copies(seq_ref[n], slot):
                cp.start(priority=1)

    @pl.when(b < nu_ref[0])
    def _():
        n = ord_ref[b]
        slot = n % 2
        changed = jnp.logical_or(b == 0, be_ref[b] != be_ref[jnp.maximum(b - 1, 0)])

        @pl.when(b == 0)
        def _():
            start_fetch(0, 0)
            start_fetch(1, 1)

        @pl.when(changed)
        def _():
            cg, cu, cd = weight_copies(be_ref[b], slot)
            cg.wait()
            wg_bf[...] = wg_st[slot].astype(BF16)
            cu.wait()
            wu_bf[...] = wu_st[slot].astype(BF16)
            cd.wait()
            wd_bf[...] = wd_st[slot].astype(BF16)
            start_fetch(n + 2, slot)

        x = _unpack_bf16_pairs(x_ref, 0, MOE_BLOCK).astype(BF16)
        gate = jnp.dot(x, wg_bf[...], preferred_element_type=F32)
        up = jnp.dot(x, wu_bf[...], preferred_element_type=F32)
        hid = (jax.nn.silu(gate) * up).astype(BF16)
        y = jnp.dot(hid, wd_bf[...], preferred_element_type=F32)
        word = _pack_bf16_pairs(y.astype(BF16).astype(F32))
        for c in range(PACK_ROWS):
            y_ref[pl.ds(c, MOE_BLOCK, stride=PACK_ROWS), :] = word[:, c * LANES:(c + 1) * LANES]

    @pl.when(b >= nu_ref[0])
    def _():
        y_ref[...] = jnp.zeros_like(y_ref)


def experts(block_expert, n_used, block_ord, expert_seq, xs, w_gate, w_up, w_down):
    def blk(b, be, nu, od, sq):
        return jnp.minimum(b, nu[0] - 1)

    grid_spec = pltpu.PrefetchScalarGridSpec(
        num_scalar_prefetch=4,
        grid=(MOE_BLOCKS,),
        in_specs=[pl.BlockSpec((MOE_BLOCK * PACK_ROWS, LANES), lambda b, be, nu, od, sq: (blk(b, be, nu, od, sq), 0)),
                  pl.BlockSpec(memory_space=pl.ANY),
                  pl.BlockSpec(memory_space=pl.ANY),
                  pl.BlockSpec(memory_space=pl.ANY)],
        out_specs=pl.BlockSpec((MOE_BLOCK * PACK_ROWS, LANES), lambda b, be, nu, od, sq: (b, 0)),
        scratch_shapes=[pltpu.VMEM((2, D, EXPERT_FF), F32),
                        pltpu.VMEM((2, D, EXPERT_FF), F32),
                        pltpu.VMEM((2, EXPERT_FF, D), F32),
                        pltpu.VMEM((D, EXPERT_FF), BF16),
                        pltpu.VMEM((D, EXPERT_FF), BF16),
                        pltpu.VMEM((EXPERT_FF, D), BF16),
                        pltpu.SemaphoreType.DMA((2, 3))],
    )
    return pl.pallas_call(
        _expert_kernel,
        out_shape=jax.ShapeDtypeStruct((MOE_ROWS * PACK_ROWS, LANES), jnp.uint32),
        grid_spec=grid_spec,
        compiler_params=_cparams(("arbitrary",)),
        name="moe_experts",
    )(block_expert, n_used, block_ord, expert_seq, xs, w_gate, w_up, w_down)


COMBINE_ROWS = 512
COMBINE_SUB = 16


def _combine_kernel(dest_ref, ys_hbm, x_ref, wt_ref, g_ref, o_ref, ybuf, sem):
    s = pl.program_id(0)
    ns = pl.num_programs(0)
    slot = s % 2
    tb = COMBINE_SUB
    rows = COMBINE_ROWS
    nxt = jnp.minimum(s + 1, ns - 1)

    def row_copy(step, n, k, sl):
        tok = step * rows + n
        src = ys_hbm.at[pl.ds(pl.multiple_of(dest_ref[tok * TOP_K + k] * PACK_ROWS, PACK_ROWS), PACK_ROWS)]
        dst = ybuf.at[sl * TOP_K + k, pl.ds(pl.multiple_of(n * PACK_ROWS, PACK_ROWS), PACK_ROWS)]
        return pltpu.make_async_copy(src, dst, sem.at[sl])

    def wait_all(step, sl):
        def body(n, c):
            for k in range(TOP_K):
                row_copy(step, n, k, sl).wait()
            return c
        lax.fori_loop(0, rows, body, 0, unroll=8)

    @pl.when(s == 0)
    def _():
        def body(n, c):
            for k in range(TOP_K):
                row_copy(0, n, k, 0).start(priority=1)
            return c
        lax.fori_loop(0, rows, body, 0, unroll=8)

    wait_all(s, slot)

    g = g_ref[...]
    y0_ref = ybuf.at[slot * TOP_K]
    y1_ref = ybuf.at[slot * TOP_K + 1]
    for b in range(rows // tb):
        for n in range(b * tb, (b + 1) * tb):
            for k in range(TOP_K):
                row_copy(nxt, n, k, 1 - slot).start(priority=1)
        sub = slice(b * tb, (b + 1) * tb)
        w = wt_ref[sub, :]
        y0 = _unpack_bf16_pairs(y0_ref, b * tb, tb)
        y1 = _unpack_bf16_pairs(y1_ref, b * tb, tb)
        z = x_ref[sub, :] + (w[:, 0:1] * y0 + w[:, 1:2] * y1)
        ms = jnp.mean(z * z, axis=-1, keepdims=True)
        o_ref[sub, :] = z * lax.rsqrt(ms + NORM_EPS) * g

    @pl.when(s == ns - 1)
    def _():
        wait_all(nxt, 1 - slot)


def combine(dest, ys, x1, wts, gain):
    rows = COMBINE_ROWS
    grid_spec = pltpu.PrefetchScalarGridSpec(
        num_scalar_prefetch=1,
        grid=(T // rows,),
        in_specs=[pl.BlockSpec(memory_space=pl.ANY),
                  pl.BlockSpec((rows, D), lambda s, dst: (s, 0)),
                  pl.BlockSpec((rows, LANES), lambda s, dst: (s, 0)),
                  pl.BlockSpec((1, D), lambda s, dst: (0, 0))],
        out_specs=pl.BlockSpec((rows, D), lambda s, dst: (s, 0)),
        scratch_shapes=[pltpu.VMEM((2 * TOP_K, rows * PACK_ROWS, LANES), jnp.uint32),
                        pltpu.SemaphoreType.DMA((2,))],
    )
    return pl.pallas_call(
        _combine_kernel,
        out_shape=jax.ShapeDtypeStruct((T, D), F32),
        grid_spec=grid_spec,
        compiler_params=_cparams(("arbitrary",)),
        name="moe_combine_final_norm",
    )(dest, ys, x1, wts, gain)


def dispatch_plan(ids, counts):
    experts_ = jnp.arange(N_EXPERTS, dtype=jnp.int32)
    padded = (counts + MOE_BLOCK - 1) // MOE_BLOCK * MOE_BLOCK
    pad_end = jnp.cumsum(padded)
    pad_start = pad_end - padded
    e = ids[:, :TOP_K]
    start_of = jnp.sum(jnp.where(e[:, :, None] == experts_[None, None, :], pad_start[None, None, :], 0), axis=-1)
    dest = (start_of + ids[:, TOP_K:2 * TOP_K]).reshape(N_ASSIGN).astype(jnp.int32)
    n_used = pad_end[-1] // MOE_BLOCK
    block_start = jnp.minimum(jnp.arange(MOE_BLOCKS, dtype=jnp.int32), n_used - 1) * MOE_BLOCK
    block_expert = jnp.sum((block_start[:, None] >= pad_end[None, :]).astype(jnp.int32), axis=1)
    block_expert = jnp.minimum(block_expert, N_EXPERTS - 1).astype(jnp.int32)
    tail = n_used + experts_
    zero_start = jnp.concatenate([jnp.where(counts > 0, pad_end - MOE_BLOCK, -1),
                                  jnp.where(tail < MOE_BLOCKS, tail * MOE_BLOCK, -1)]).astype(jnp.int32)
    present = counts > 0
    expert_ord = jnp.cumsum(present.astype(jnp.int32)) - 1
    slots = jnp.arange(N_EXPERTS + 2, dtype=jnp.int32)
    hit = present[None, :] & (expert_ord[None, :] == slots[:, None])
    expert_seq = jnp.where(jnp.any(hit, axis=1), jnp.sum(jnp.where(hit, experts_[None, :], 0), axis=1), -1)
    block_ord = jnp.sum(jnp.where(block_expert[:, None] == experts_[None, :], expert_ord[None, :], 0), axis=1)
    return (block_expert, n_used.astype(jnp.int32).reshape(1), dest, zero_start,
            block_ord.astype(jnp.int32), expert_seq.astype(jnp.int32))


def kernel(x, norm_mix, w_in, b_gate, ssm_a_re, ssm_a_im, ssm_log_step, ssm_b_re, ssm_b_im, ssm_c_re, ssm_c_im, ssm_d, w_glu, b_glu, w_up_attn, w_up_ssm, w_out, norm_ffn, w_router_group, b_router_group, w_router_expert, b_router_expert, w_expert_gate, w_expert_up, w_expert_down, norm_final):
    x2 = x.reshape(T, D)
    h = norm_permute(x2, norm_mix.reshape(1, D)).reshape(T, D)
    w_in_l = w_in.reshape(D, IN_COLS)
    qkv = proj(h, w_in_l, 0, 3 * QKV_COLS, tn=QKV_COLS, name="proj_qkv")
    u = proj(h, w_in_l, 3 * QKV_COLS, SSM_W, tn=SSM_W, name="proj_ssm_in")

    attn = attention(qkv.reshape(R, NI, 3 * QKV_COLS)).reshape(T, ATTN_OUT)

    G = SSM_W // SSM_CH
    wb, wc, pows = ssm_params(
        ssm_a_re.reshape(G, SSM_STATE).astype(F32), ssm_a_im.reshape(G, SSM_STATE).astype(F32),
        ssm_log_step.reshape(G),
        ssm_b_re.reshape(G, SSM_STATE, SSM_CH).astype(F32), ssm_b_im.reshape(G, SSM_STATE, SSM_CH).astype(F32),
        ssm_c_re.reshape(G, SSM_CH, SSM_STATE), ssm_c_im.reshape(G, SSM_CH, SSM_STATE))
    yg = ssm_scan(u.reshape(R, NI, SSM_W), wb, wc, pows, ssm_d.reshape(1, SSM_W).astype(F32))
    ssm = glu(yg.reshape(T, SSM_W), w_glu.reshape(SSM_W, SSM_W), b_glu.reshape(1, SSM_W))

    merged = merge(h, attn, ssm, w_in_l, b_gate.reshape(1, 2 * D),
                   w_up_attn.reshape(ATTN_OUT, D), w_up_ssm.reshape(SSM_W, D))
    x1 = out_proj(merged.reshape(R, NI, D), w_out.reshape(D, D), x2)

    w_r = jnp.concatenate([w_router_group.reshape(D, N_EGROUPS), w_router_expert.reshape(D, N_EXPERTS),
                           jnp.zeros((D, LANES - N_EGROUPS - N_EXPERTS), F32)], axis=1)
    b_r = jnp.concatenate([b_router_group.reshape(1, N_EGROUPS), b_router_expert.reshape(1, N_EXPERTS),
                           jnp.zeros((1, LANES - N_EGROUPS - N_EXPERTS), F32)], axis=1)
    hpk, ids, wts, counts = router(x1, norm_ffn.reshape(1, D), w_r, b_r)

    block_expert, n_used, dest, zero_start, block_ord, expert_seq = dispatch_plan(ids[:, :2 * TOP_K], counts[0, :N_EXPERTS])
    xs = dispatch(dest, zero_start, hpk)
    ys = experts(block_expert, n_used, block_ord, expert_seq, xs,
                 w_expert_gate.reshape(N_EXPERTS, D, EXPERT_FF), w_expert_up.reshape(N_EXPERTS, D, EXPERT_FF),
                 w_expert_down.reshape(N_EXPERTS, EXPERT_FF, D))
    out = combine(dest, ys, x1, wts, norm_final.reshape(1, D))
    return out.reshape(1, T, D)
```
